```python
import numpy as np
import jax, jax.numpy as jnp
from jax import lax

D_MODEL = 1024
BATCH = 8
SEQ = 2048
DEPTH = 1

HEAD_DIM = 64
NSA_HEADS = 8
NSA_KV = 2
NSA_REP = NSA_HEADS // NSA_KV
RWKV_HEADS = 8
NSA_WIDTH = NSA_HEADS * HEAD_DIM
RWKV_WIDTH = RWKV_HEADS * HEAD_DIM
MIX_WIDTH = NSA_WIDTH + RWKV_WIDTH
KV_WIDTH = NSA_KV * HEAD_DIM
ROPE_DIM = HEAD_DIM // 4
ROPE_THETA = 500000.0
CMP_BLOCK = 32
CMP_STRIDE = 16
CMP_HIDDEN = 256
SEL_BLOCK = 64
SEL_TOPK = 8
WINDOW = 512
Q_BLOCK = 128
DECAY_RANK = 64
ICLR_RANK = 64
RWKV_SHIFT_WIDTH = 3 * RWKV_WIDTH + DECAY_RANK + ICLR_RANK
IN_SIZES = (NSA_WIDTH, KV_WIDTH, KV_WIDTH, KV_WIDTH, KV_WIDTH, KV_WIDTH, KV_WIDTH,
            3 * NSA_HEADS, NSA_WIDTH, RWKV_SHIFT_WIDTH, RWKV_WIDTH)
IN_WIDTH = sum(IN_SIZES)
SCALE = HEAD_DIM ** -0.5
RMS_EPS = 1e-6
GN_EPS = 64e-5
NEG_INF = -1e30
FORCE_BONUS = 1e3

kernel_name = "hymba_nsa_rwkv7_hybrid"


def _split(h, sizes):
    idx = np.cumsum(sizes)[:-1].tolist()
    return jnp.split(h, idx, axis=-1)


def rms_norm(x, g):
    xf = x.astype(jnp.float32)
    y = xf * lax.rsqrt(jnp.mean(xf * xf, axis=-1, keepdims=True) + RMS_EPS)
    return (y * g.astype(jnp.float32)).astype(x.dtype)


def partial_rope(t, pos):
    half = ROPE_DIM // 2
    inv = ROPE_THETA ** (-jnp.arange(half, dtype=jnp.float32) / half)
    ang = pos.astype(jnp.float32)[:, None] * inv[None, :]
    cos, sin = jnp.cos(ang), jnp.sin(ang)
    tr = t[..., :ROPE_DIM].astype(jnp.float32)
    t1, t2 = tr[..., :half], tr[..., half:]
    rot = jnp.concatenate([t1 * cos - t2 * sin, t2 * cos + t1 * sin], axis=-1).astype(t.dtype)
    return jnp.concatenate([rot, t[..., ROPE_DIM:]], axis=-1)


def masked_softmax(s, mask):
    s = jnp.where(mask, s.astype(jnp.float32), NEG_INF)
    p = jax.nn.softmax(s, axis=-1)
    return jnp.where(mask, p, 0.0)


def compress_blocks(kv, pos_emb, w1, w2):
    B, G, S, D = kv.shape
    ch = kv.reshape(B, G, S // CMP_STRIDE, CMP_STRIDE, D)
    blk = jnp.concatenate([ch[:, :, :-1], ch[:, :, 1:]], axis=3) + pos_emb
    flat = blk.reshape(B, G, -1, CMP_BLOCK * D)
    return jax.nn.silu(flat @ w1) @ w2


def cmp_to_sel_matrix(n_cmp, n_sel):
    c0 = np.arange(n_cmp)[:, None] * CMP_STRIDE
    s0 = np.arange(n_sel)[None, :] * SEL_BLOCK
    ov = np.clip(np.minimum(c0 + CMP_BLOCK, s0 + SEL_BLOCK) - np.maximum(c0, s0), 0, None)
    return jnp.asarray(ov / CMP_BLOCK, dtype=jnp.float32)


def nsa_mixer(q, kc, vc, ks, vs, kw, vw, gate_logits,
              pos_k, w1_k, w2_k, pos_v, w1_v, w2_v):
    B, S, _ = q.shape
    n_cmp = S // CMP_STRIDE - 1
    n_sel = S // SEL_BLOCK
    top_n = min(SEL_TOPK, n_sel)
    n_qb = S // Q_BLOCK
    pos = jnp.arange(S)
    f32 = jnp.float32
    qh = q.reshape(B, S, NSA_KV, NSA_REP, HEAD_DIM).transpose(0, 2, 3, 1, 4)

    def kvh(t):
        return t.reshape(B, S, NSA_KV, HEAD_DIM).transpose(0, 2, 1, 3)

    k_cmp = compress_blocks(kvh(kc), pos_k, w1_k, w2_k)
    v_cmp = compress_blocks(kvh(vc), pos_v, w1_v, w2_v)
    s_cmp = jnp.einsum('bgrsd,bgcd->bgrsc', qh, k_cmp) * SCALE
    cmp_end = jnp.arange(n_cmp) * CMP_STRIDE + CMP_BLOCK - 1
    p_cmp = masked_softmax(s_cmp, cmp_end[None, :] <= pos[:, None])
    o_cmp = jnp.einsum('bgrsc,bgcd->bgrsd', p_cmp, v_cmp.astype(f32))

    imp = jnp.einsum('bgrsc,cj->bgsj', p_cmp, cmp_to_sel_matrix(n_cmp, n_sel))
    t_blk = (pos // SEL_BLOCK)[:, None]
    j = jnp.arange(n_sel)[None, :]
    forced = ((j == 0) | (j == t_blk) | (j == t_blk - 1)).astype(f32)
    imp = jnp.where(j <= t_blk, imp + FORCE_BONUS * forced, -1.0)
    _, sel_idx = lax.top_k(imp, top_n)

    q_rot = partial_rope(qh, pos)
    ks_blk = partial_rope(kvh(ks), pos).reshape(B, NSA_KV, n_sel, SEL_BLOCK, HEAD_DIM)
    vs_blk = kvh(vs).reshape(B, NSA_KV, n_sel, SEL_BLOCK, HEAD_DIM)
    pad = ((0, 0), (0, 0), (WINDOW, 0), (0, 0))
    kw_pad = jnp.pad(partial_rope(kvh(kw), pos), pad)
    vw_pad = jnp.pad(kvh(vw), pad)
    bi = jnp.arange(B)[:, None, None, None]
    gi = jnp.arange(NSA_KV)[None, :, None, None]
    n_tok = top_n * SEL_BLOCK

    def block_fn(qb):
        q0 = qb * Q_BLOCK
        tq = q0 + jnp.arange(Q_BLOCK)
        qc = lax.dynamic_slice_in_dim(q_rot, q0, Q_BLOCK, axis=3)
        idx = lax.dynamic_slice_in_dim(sel_idx, q0, Q_BLOCK, axis=2)
        k_g = ks_blk[bi, gi, idx].reshape(B, NSA_KV, Q_BLOCK, n_tok, HEAD_DIM)
        v_g = vs_blk[bi, gi, idx].reshape(B, NSA_KV, Q_BLOCK, n_tok, HEAD_DIM)
        k_pos = (idx[..., None] * SEL_BLOCK + jnp.arange(SEL_BLOCK)).reshape(B, NSA_KV, Q_BLOCK, n_tok)
        s_sel = jnp.einsum('bgrqd,bgqmd->bgrqm', qc, k_g) * SCALE
        p_sel = masked_softmax(s_sel, (k_pos <= tq[:, None])[:, :, None])
        o_sel = jnp.einsum('bgrqm,bgqmd->bgrqd', p_sel, v_g.astype(f32))
        kwc = lax.dynamic_slice_in_dim(kw_pad, q0, WINDOW + Q_BLOCK, axis=2)
        vwc = lax.dynamic_slice_in_dim(vw_pad, q0, WINDOW + Q_BLOCK, axis=2)
        w_pos = q0 - WINDOW + jnp.arange(WINDOW + Q_BLOCK)
        diff = tq[:, None] - w_pos[None, :]
        w_mask = (diff >= 0) & (diff < WINDOW) & (w_pos[None, :] >= 0)
        s_win = jnp.einsum('bgrqd,bgkd->bgrqk', qc, kwc) * SCALE
        p_win = masked_softmax(s_win, w_mask)
        o_win = jnp.einsum('bgrqk,bgkd->bgrqd', p_win, vwc.astype(f32))
        return o_sel, o_win

    o_sel, o_win = lax.map(block_fn, jnp.arange(n_qb))

    def unblock(o):
        return o.transpose(1, 2, 3, 0, 4, 5).reshape(B, NSA_KV, NSA_REP, S, HEAD_DIM)

    g = jax.nn.sigmoid(gate_logits.astype(f32)).reshape(B, S, NSA_KV, NSA_REP, 3).transpose(0, 2, 3, 1, 4)
    o = g[..., 0:1] * o_cmp + g[..., 1:2] * unblock(o_sel) + g[..., 2:3] * unblock(o_win)
    return o.transpose(0, 3, 1, 2, 4).reshape(B, S, NSA_WIDTH)


def rwkv7_step(state, inp):
    r_t, w_t, k_t, v_t, kk_t, a_t = inp
    sa = jnp.einsum('bhvk,bhk->bhv', state, -kk_t)
    state = (state * w_t[:, :, None, :] + sa[..., None] * (kk_t * a_t)[:, :, None, :]
             + v_t[..., None] * k_t[:, :, None, :])
    y = jnp.einsum('bhvk,bhk->bhv', state, r_t)
    return state, y


def rwkv7_mixer(p, shift_mu, w0, w_up, a0, a_up, k_k, k_a, r_k, gn_w, gn_b):
    B, S, _ = p.shape
    f32 = jnp.float32
    prev = jnp.pad(p, ((0, 0), (1, 0), (0, 0)))[:, :-1]
    p = p + shift_mu * (prev - p)
    r, k, v, wd, ad = _split(p, (RWKV_WIDTH, RWKV_WIDTH, RWKV_WIDTH, DECAY_RANK, ICLR_RANK))
    w = (w0 + jnp.tanh(wd) @ w_up).astype(f32)
    decay = jnp.exp(-jnp.exp(-jax.nn.softplus(-w) - 0.5))
    a = jax.nn.sigmoid((a0 + ad @ a_up).astype(f32))

    def heads(t):
        return t.astype(f32).reshape(B, S, RWKV_HEADS, HEAD_DIM)

    kk = heads(k * k_k)
    kk = kk / jnp.maximum(jnp.sqrt(jnp.sum(kk * kk, axis=-1, keepdims=True)), 1e-12)
    k = k.astype(f32) * (1.0 + (a - 1.0) * k_a)
    rh, kh, vh, wh, ah = heads(r), heads(k), heads(v), heads(decay), heads(a)

    def tm(t):
        return t.transpose(1, 0, 2, 3)

    s0 = jnp.zeros((B, RWKV_HEADS, HEAD_DIM, HEAD_DIM), f32)
    _, y = lax.scan(rwkv7_step, s0, (tm(rh), tm(wh), tm(kh), tm(vh), tm(kk), tm(ah)))
    y = y.transpose(1, 0, 2, 3)
    mean = jnp.mean(y, axis=-1, keepdims=True)
    var = jnp.mean(jnp.square(y - mean), axis=-1, keepdims=True)
    y = (y - mean) * lax.rsqrt(var + GN_EPS)
    y = y * gn_w.reshape(RWKV_HEADS, HEAD_DIM) + gn_b.reshape(RWKV_HEADS, HEAD_DIM)
    y = y + jnp.sum(rh * kh * r_k.reshape(RWKV_HEADS, HEAD_DIM), axis=-1, keepdims=True) * vh
    return y.reshape(B, S, RWKV_WIDTH)


def hybrid_layer(x, norm_g, w_in, cmp_pos_k, cmp_w1_k, cmp_w2_k, cmp_pos_v, cmp_w1_v, cmp_w2_v,
                 shift_mu, decay_w0, decay_up, iclr_a0, iclr_up, k_k, k_a, r_k, gn_w, gn_b, w_out):
    h = rms_norm(x, norm_g)
    proj = h @ w_in
    (q, kc, vc, ks, vs, kw, vw, gl, g_nsa, rwkv_in, g_rwkv) = _split(proj, IN_SIZES)
    o_nsa = nsa_mixer(q, kc, vc, ks, vs, kw, vw, gl,
                      cmp_pos_k, cmp_w1_k, cmp_w2_k, cmp_pos_v, cmp_w1_v, cmp_w2_v)
    o_nsa = o_nsa * jax.nn.silu(g_nsa.astype(jnp.float32))
    o_rwkv = rwkv7_mixer(rwkv_in, shift_mu, decay_w0, decay_up, iclr_a0, iclr_up,
                         k_k, k_a, r_k, gn_w, gn_b)
    o_rwkv = o_rwkv * jax.nn.silu(g_rwkv.astype(jnp.float32))
    mix = jnp.concatenate([o_nsa, o_rwkv], axis=-1).astype(x.dtype)
    return x + mix @ w_out


def setup_inputs(seed: int = 0) -> dict:
    key = jax.random.key(seed)
    ks = jax.random.split(key, 24)
    f32 = jnp.float32

    def nrm(k, shape, scale):
        return jax.random.normal(k, shape, f32) * scale

    L = DEPTH
    return {
        "x": nrm(ks[0], (BATCH, SEQ, D_MODEL), 1.0),
        "norm_g": 1.0 + nrm(ks[1], (L, D_MODEL), 0.05),
        "w_in": nrm(ks[2], (L, D_MODEL, IN_WIDTH), D_MODEL ** -0.5),
        "cmp_pos_k": nrm(ks[3], (L, CMP_BLOCK, HEAD_DIM), 0.1),
        "cmp_w1_k": nrm(ks[4], (L, CMP_BLOCK * HEAD_DIM, CMP_HIDDEN), (CMP_BLOCK * HEAD_DIM) ** -0.5),
        "cmp_w2_k": nrm(ks[5], (L, CMP_HIDDEN, HEAD_DIM), CMP_HIDDEN ** -0.5),
        "cmp_pos_v": nrm(ks[6], (L, CMP_BLOCK, HEAD_DIM), 0.1),
        "cmp_w1_v": nrm(ks[7], (L, CMP_BLOCK * HEAD_DIM, CMP_HIDDEN), (CMP_BLOCK * HEAD_DIM) ** -0.5),
        "cmp_w2_v": nrm(ks[8], (L, CMP_HIDDEN, HEAD_DIM), CMP_HIDDEN ** -0.5),
        "shift_mu": jax.random.uniform(ks[9], (L, RWKV_SHIFT_WIDTH), f32),
        "decay_w0": jax.random.uniform(ks[10], (L, RWKV_WIDTH), f32, -4.0, 1.0),
        "decay_up": nrm(ks[11], (L, DECAY_RANK, RWKV_WIDTH), 0.1 * DECAY_RANK ** -0.5),
        "iclr_a0": nrm(ks[12], (L, RWKV_WIDTH), 0.5),
        "iclr_up": nrm(ks[13], (L, ICLR_RANK, RWKV_WIDTH), 0.5 * ICLR_RANK ** -0.5),
        "k_k": 0.85 + nrm(ks[14], (L, RWKV_WIDTH), 0.05),
        "k_a": 1.0 + nrm(ks[15], (L, RWKV_WIDTH), 0.05),
        "r_k": nrm(ks[16], (L, RWKV_WIDTH), 0.1),
        "gn_w": 1.0 + nrm(ks[17], (L, RWKV_WIDTH), 0.05),
        "gn_b": nrm(ks[18], (L, RWKV_WIDTH), 0.02),
        "w_out": nrm(ks[19], (L, MIX_WIDTH, D_MODEL), MIX_WIDTH ** -0.5),
        "final_g": 1.0 + nrm(ks[20], (D_MODEL,), 0.05),
    }


def reference(x, norm_g, w_in, cmp_pos_k, cmp_w1_k, cmp_w2_k, cmp_pos_v, cmp_w1_v, cmp_w2_v,
              shift_mu, decay_w0, decay_up, iclr_a0, iclr_up, k_k, k_a, r_k, gn_w, gn_b,
              w_out, final_g):
    h = x
    for l in range(DEPTH):
        h = hybrid_layer(h, norm_g[l], w_in[l], cmp_pos_k[l], cmp_w1_k[l], cmp_w2_k[l],
                         cmp_pos_v[l], cmp_w1_v[l], cmp_w2_v[l], shift_mu[l], decay_w0[l],
                         decay_up[l], iclr_a0[l], iclr_up[l], k_k[l], k_a[l], r_k[l],
                         gn_w[l], gn_b[l], w_out[l])
    return rms_norm(h, final_g)
```

```python
import functools

import numpy as np
import jax
import jax.numpy as jnp
from jax import lax
from jax.experimental import pallas as pl
from jax.experimental.pallas import tpu as pltpu

F32 = jnp.float32
BF16 = jnp.bfloat16

D_MODEL = 1024
HEAD_DIM = 64
NSA_HEADS = 8
NSA_KV = 2
NSA_REP = NSA_HEADS // NSA_KV
RWKV_HEADS = 8
NSA_WIDTH = NSA_HEADS * HEAD_DIM
RWKV_WIDTH = RWKV_HEADS * HEAD_DIM
KV_WIDTH = NSA_KV * HEAD_DIM
ROPE_DIM = HEAD_DIM // 4
ROPE_THETA = 500000.0
CMP_BLOCK = 32
CMP_STRIDE = 16
CMP_HIDDEN = 256
SEL_BLOCK = 64
SEL_TOPK = 8
WINDOW = 512
DECAY_RANK = 64
ICLR_RANK = 64
RWKV_SHIFT_WIDTH = 3 * RWKV_WIDTH + DECAY_RANK + ICLR_RANK
IN_SIZES = (NSA_WIDTH, KV_WIDTH, KV_WIDTH, KV_WIDTH, KV_WIDTH, KV_WIDTH, KV_WIDTH,
            3 * NSA_HEADS, NSA_WIDTH, RWKV_SHIFT_WIDTH, RWKV_WIDTH)
SCALE = HEAD_DIM ** -0.5
RMS_EPS = 1e-6
GN_EPS = 64e-5
NEG_INF = -1e30
FORCE_BONUS = 1e3

LANE = 128
GATE_PAD = LANE
C_Q = 0
C_KV = C_Q + NSA_WIDTH
C_GL = C_KV + 6 * KV_WIDTH
C_GN = C_GL + NSA_KV * GATE_PAD
C_RW = C_GN + NSA_WIDTH
C_GR = C_RW + RWKV_SHIFT_WIDTH
C_END = C_GR + RWKV_WIDTH

RWKV_CHUNK = 64
VMEM_LIMIT = 48 * 1024 * 1024


def _dot(a, b):
    return jnp.dot(a, b, preferred_element_type=F32)


def _dot_nt(a, b):
    return lax.dot_general(a, b, (((1,), (1,)), ((), ())), preferred_element_type=F32)


def _dot_tn(a, b):
    return lax.dot_general(a, b, (((0,), (0,)), ((), ())), preferred_element_type=F32)


def _split3(x):
    hi = x.astype(BF16)
    r1 = x - hi.astype(F32)
    mid = r1.astype(BF16)
    lo = (r1 - mid.astype(F32)).astype(BF16)
    return hi, mid, lo


def _dot_x3(x, w_bf16):
    hi, mid, lo = _split3(x)
    return _dot(hi, w_bf16) + _dot(mid, w_bf16) + _dot(lo, w_bf16)


def _sigmoid(x):
    return 1.0 / (1.0 + jnp.exp(-x))


def _rope128(t, ra, rm, rp):
    return t * ra + pltpu.roll(t, LANE - ROPE_DIM // 2, 1) * rm + pltpu.roll(t, ROPE_DIM // 2, 1) * rp


def _in_proj_kernel(x_ref, g_ref, w_ref, ra_ref, rm_ref, rp_ref,
                    q_ref, qr_ref, kc_ref, vc_ref, ks_ref, vs_ref, kw_ref, vw_ref,
                    gate_ref, gn_ref, rw_ref, gr_ref):
    x = x_ref[...]
    ms = jnp.mean(x * x, axis=-1, keepdims=True)
    y = (x * lax.rsqrt(ms + RMS_EPS) * g_ref[...]).astype(BF16)
    ra, rm, rp = ra_ref[...], rm_ref[...], rp_ref[...]

    def proj(c0, c1):
        return _dot(y, w_ref[:, c0:c1])

    q = proj(C_Q, C_KV) * SCALE
    for i in range(NSA_WIDTH // LANE):
        t = q[:, i * LANE:(i + 1) * LANE]
        tr = _rope128(t, ra, rm, rp)
        for j in range(LANE // HEAD_DIM):
            h = i * (LANE // HEAD_DIM) + j
            q_ref[h] = t[:, j * HEAD_DIM:(j + 1) * HEAD_DIM].astype(BF16)
            qr_ref[h] = tr[:, j * HEAD_DIM:(j + 1) * HEAD_DIM].astype(BF16)

    kv = proj(C_KV, C_GL)
    for i, (ref, rot) in enumerate(((kc_ref, False), (vc_ref, False), (ks_ref, True),
                                    (vs_ref, False), (kw_ref, True), (vw_ref, False))):
        t = kv[:, i * LANE:(i + 1) * LANE]
        if rot:
            t = _rope128(t, ra, rm, rp)
        for g in range(NSA_KV):
            ref[g] = t[:, g * HEAD_DIM:(g + 1) * HEAD_DIM].astype(BF16)

    gate_ref[...] = _sigmoid(proj(C_GL, C_GN))
    gn = proj(C_GN, C_RW)
    gn_ref[...] = gn * _sigmoid(gn)
    rw_ref[...] = proj(C_RW, C_GR)
    gr = proj(C_GR, C_END)
    gr_ref[...] = gr * _sigmoid(gr)


def _in_proj(x2, norm_g, w_p, ra, rm, rp, *, seq, tm):
    n = x2.shape[0]
    nt = n // tm
    spt = seq // tm
    hm = lambda d: jax.ShapeDtypeStruct((d, n, HEAD_DIM), BF16)
    hspec = lambda d: pl.BlockSpec((d, tm, HEAD_DIM), lambda i: (0, i, 0))
    row = lambda w: pl.BlockSpec((tm, w), lambda i: (i, 0))
    tab = pl.BlockSpec((tm, LANE), lambda i: (i % spt, 0))
    return pl.pallas_call(
        _in_proj_kernel,
        grid=(nt,),
        in_specs=[row(D_MODEL),
                  pl.BlockSpec((1, D_MODEL), lambda i: (0, 0)),
                  pl.BlockSpec((D_MODEL, C_END), lambda i: (0, 0)),
                  tab, tab, tab],
        out_specs=[hspec(NSA_HEADS), hspec(NSA_HEADS)] + [hspec(NSA_KV)] * 6
                  + [row(NSA_KV * GATE_PAD), row(NSA_WIDTH), row(RWKV_SHIFT_WIDTH), row(RWKV_WIDTH)],
        out_shape=[hm(NSA_HEADS), hm(NSA_HEADS)] + [hm(NSA_KV)] * 6
                  + [jax.ShapeDtypeStruct((n, NSA_KV * GATE_PAD), F32),
                     jax.ShapeDtypeStruct((n, NSA_WIDTH), F32),
                     jax.ShapeDtypeStruct((n, RWKV_SHIFT_WIDTH), F32),
                     jax.ShapeDtypeStruct((n, RWKV_WIDTH), F32)],
        compiler_params=pltpu.CompilerParams(dimension_semantics=("arbitrary",),
                                             vmem_limit_bytes=VMEM_LIMIT),
        name="in_proj",
    )(x2, norm_g, w_p, ra, rm, rp)


def _compress_kernel(kc_ref, vc_ref, pk_ref, w1k_ref, w2k_ref, pv_ref, w1v_ref, w2v_ref,
                     ko_ref, vo_ref):
    half = CMP_STRIDE * HEAD_DIM

    def one(c_ref, pos_ref, w1_ref, w2_ref, o_ref):
        c = c_ref[0, 0]
        za = _dot(c, w1_ref[0:half, :])
        zb = _dot(c, w1_ref[half:2 * half, :])
        pos = jnp.broadcast_to(pos_ref[...], (8, 2 * half)).astype(BF16)
        pv = _dot(pos, w1_ref[...])[0:1, :]
        hid = za + pltpu.roll(zb, c.shape[0] - 1, 0) + pv
        act = (hid * _sigmoid(hid)).astype(BF16)
        o_ref[0, 0] = _dot(act, w2_ref[...]).astype(BF16)

    one(kc_ref, pk_ref, w1k_ref, w2k_ref, ko_ref)
    one(vc_ref, pv_ref, w1v_ref, w2v_ref, vo_ref)


def _compress(kc_r, vc_r, pk, w1k, w2k, pv, w1v, w2v):
    g, b, nch, width = kc_r.shape
    blk = pl.BlockSpec((1, 1, nch, width), lambda i, j: (i, j, 0, 0))
    full = lambda a: pl.BlockSpec(a.shape, lambda i, j: (0,) * a.ndim)
    oblk = pl.BlockSpec((1, 1, nch, HEAD_DIM), lambda i, j: (i, j, 0, 0))
    osd = jax.ShapeDtypeStruct((g, b, nch, HEAD_DIM), BF16)
    return pl.pallas_call(
        _compress_kernel,
        grid=(g, b),
        in_specs=[blk, blk, full(pk), full(w1k), full(w2k), full(pv), full(w1v), full(w2v)],
        out_specs=[oblk, oblk],
        out_shape=[osd, osd],
        compiler_params=pltpu.CompilerParams(dimension_semantics=("arbitrary", "arbitrary"),
                                             vmem_limit_bytes=VMEM_LIMIT),
        name="compress",
    )(kc_r, vc_r, pk, w1k, w2k, pv, w1v, w2v)


def _nsa_kernel(q_ref, qr_ref, kc_ref, vc_ref, ks_ref, vs_ref, kw_ref, vw_ref,
                gate_ref, gn_ref, mt_ref, ex_ref, o_ref, bias_ref, *, tq, seq):
    qi = pl.program_id(2)
    q0 = qi * tq
    r4 = NSA_REP * tq
    n_sel = seq // SEL_BLOCK
    ncp = kc_ref.shape[2]

    q = q_ref[...].reshape(r4, HEAD_DIM)
    s = _dot_nt(q, kc_ref[0, 0]).reshape(NSA_REP, tq, ncp)
    t_c = q0 + lax.broadcasted_iota(jnp.int32, (1, tq, ncp), 1)
    c_c = lax.broadcasted_iota(jnp.int32, (1, tq, ncp), 2)
    cmask = (c_c * CMP_STRIDE + (CMP_BLOCK - 1)) <= t_c
    s = jnp.where(cmask, s, NEG_INF)
    m = jnp.max(s, axis=-1, keepdims=True)
    e = jnp.where(cmask, jnp.exp(s - m), 0.0)
    den = jnp.sum(e, axis=-1, keepdims=True)
    p = e * (1.0 / jnp.where(den > 0.0, den, 1.0))
    o_cmp = _dot(p.reshape(r4, ncp).astype(BF16), vc_ref[0, 0]).reshape(NSA_REP, tq, HEAD_DIM)

    psum = jnp.sum(p, axis=0)
    mt = mt_ref[...]
    hi, mid, lo = _split3(psum)
    imp = _dot_nt(mt, hi) + _dot_nt(mt, mid) + _dot_nt(mt, lo)
    j = lax.broadcasted_iota(jnp.int32, (LANE, tq), 0)
    t = q0 + lax.broadcasted_iota(jnp.int32, (LANE, tq), 1)
    tb = t // SEL_BLOCK
    forced = (j == 0) | (j == tb) | (j == tb - 1)
    val = jnp.where(j <= tb, imp + jnp.where(forced, FORCE_BONUS, 0.0), -1.0)
    val = jnp.where(j < n_sel, val, -2.0)
    cnt = jnp.zeros((LANE, tq), F32)
    for i in range(n_sel):
        vi = val[i:i + 1, :]
        ge = jnp.where(vi >= val, 1.0, 0.0)
        gt = jnp.where(vi > val, 1.0, 0.0)
        cnt = cnt + jnp.where(j > i, ge, gt)
    sel = jnp.where((cnt < float(SEL_TOPK)) & (j < n_sel), 1.0, 0.0)
    sel_tok = _dot(sel.T.astype(BF16), ex_ref[...])
    sel_bias = (sel_tok - 1.0) * (-NEG_INF)
    n_kt = seq // tq
    for kt in range(n_kt):
        bias_ref[kt] = sel_bias[:, kt * tq:(kt + 1) * tq]

    qr = qr_ref[...].reshape(r4, HEAD_DIM)
    t_q = q0 + lax.broadcasted_iota(jnp.int32, (tq, tq), 0)
    k_l = lax.broadcasted_iota(jnp.int32, (tq, tq), 1)

    def attend(k_ref, v_ref, kt_lo, bias_fn):
        def body(kt, carry):
            m_i, l_i, acc = carry
            k0 = pl.multiple_of(kt * tq, tq)
            kb = k_ref[0, pl.ds(k0, tq), :]
            vb = v_ref[0, pl.ds(k0, tq), :]
            sc = _dot_nt(qr, kb).reshape(NSA_REP, tq, tq) + bias_fn(kt, k0)[None]
            m_n = jnp.maximum(m_i, jnp.max(sc, axis=-1, keepdims=True))
            alpha = jnp.exp(m_i - m_n)
            pe = jnp.exp(sc - m_n)
            l_n = alpha * l_i + jnp.sum(pe, axis=-1, keepdims=True)
            pv = _dot(pe.reshape(r4, tq).astype(BF16), vb).reshape(NSA_REP, tq, HEAD_DIM)
            return m_n, l_n, alpha * acc + pv

        init = (jnp.full((NSA_REP, tq, 1), NEG_INF, F32), jnp.zeros((NSA_REP, tq, 1), F32),
                jnp.zeros((NSA_REP, tq, HEAD_DIM), F32))
        m_f, l_f, acc = lax.fori_loop(kt_lo, qi + 1, body, init)
        return acc * (1.0 / l_f)

    def sel_bias_fn(kt, k0):
        return jnp.where(k0 + k_l <= t_q, bias_ref[kt], NEG_INF)

    def win_bias_fn(kt, k0):
        d = t_q - (k0 + k_l)
        return jnp.where((d >= 0) & (d < WINDOW), 0.0, NEG_INF)

    o_sel = attend(ks_ref, vs_ref, 0, sel_bias_fn)
    o_win = attend(kw_ref, vw_ref, jnp.maximum(qi - WINDOW // tq, 0), win_bias_fn)

    gates = gate_ref[...]
    outs = []
    for r in range(NSA_REP):
        gc = gates[:, 3 * r:3 * r + 1]
        gs = gates[:, 3 * r + 1:3 * r + 2]
        gw = gates[:, 3 * r + 2:3 * r + 3]
        outs.append(gc * o_cmp[r] + gs * o_sel[r] + gw * o_win[r])
    o = jnp.concatenate(outs, axis=-1) * gn_ref[...]
    o_ref[...] = o.astype(BF16)


def _nsa(qh, qrh, kcmp, vcmp, ksh, vsh, kwh, vwh, gates, gn, mt, ex, *, batch, seq, tq):
    n = batch * seq
    nq = seq // tq
    ncp = kcmp.shape[2]
    qspec = pl.BlockSpec((NSA_REP, tq, HEAD_DIM), lambda b, g, i: (g, b * nq + i, 0))
    cspec = pl.BlockSpec((1, 1, ncp, HEAD_DIM), lambda b, g, i: (g, b, 0, 0))
    kspec = pl.BlockSpec((1, seq, HEAD_DIM), lambda b, g, i: (g, b, 0))
    full = lambda a: pl.BlockSpec(a.shape, lambda b, g, i: (0,) * a.ndim)
    kern = functools.partial(_nsa_kernel, tq=tq, seq=seq)
    return pl.pallas_call(
        kern,
        grid=(batch, NSA_KV, nq),
        in_specs=[qspec, qspec, cspec, cspec, kspec, kspec, kspec, kspec,
                  pl.BlockSpec((tq, GATE_PAD), lambda b, g, i: (b * nq + i, g)),
                  pl.BlockSpec((tq, NSA_REP * HEAD_DIM), lambda b, g, i: (b * nq + i, g)),
                  full(mt), full(ex)],
        out_specs=pl.BlockSpec((tq, NSA_REP * HEAD_DIM), lambda b, g, i: (b * nq + i, g)),
        out_shape=jax.ShapeDtypeStruct((n, NSA_WIDTH), BF16),
        scratch_shapes=[pltpu.VMEM((seq // tq, tq, tq), F32)],
        compiler_params=pltpu.CompilerParams(
            dimension_semantics=("arbitrary", "arbitrary", "arbitrary"),
            vmem_limit_bytes=VMEM_LIMIT),
        name="nsa",
    )(qh, qrh, kcmp, vcmp, ksh, vsh, kwh, vwh, gates, gn, mt, ex)


def _rwkv_kernel(p_ref, gr_ref, mu_ref, w0_ref, a0_ref, wab_ref, kk_ref, ka_ref, rk_ref,
                 gw_ref, gb_ref, ones_ref, tril_ref, o_ref, st_ref, carry_ref, *, tt):
    c = RWKV_CHUNK
    hd = HEAD_DIM
    step = pl.program_id(1)

    @pl.when(step == 0)
    def _():
        st_ref[...] = jnp.zeros_like(st_ref)
        carry_ref[...] = jnp.zeros_like(carry_ref)

    p = p_ref[...]
    row = lax.broadcasted_iota(jnp.int32, p.shape, 0)
    prev = jnp.where(row == 0, carry_ref[...], pltpu.roll(p, 1, 0))
    carry_ref[...] = p[tt - 1:tt, :]
    ps = p + mu_ref[...] * (prev - p)
    r = ps[:, 0:RWKV_WIDTH]
    k = ps[:, RWKV_WIDTH:2 * RWKV_WIDTH]
    v = ps[:, 2 * RWKV_WIDTH:3 * RWKV_WIDTH]
    lora = ps[:, 3 * RWKV_WIDTH:]
    lane = lax.broadcasted_iota(jnp.int32, lora.shape, 1)
    feat = jnp.where(lane < DECAY_RANK, jnp.tanh(lora), lora).astype(BF16)
    up = _dot(feat, wab_ref[...])
    w = w0_ref[...] + up[:, 0:RWKV_WIDTH]
    lw = _sigmoid(w) * (-float(np.exp(-0.5)))
    a = _sigmoid(a0_ref[...] + up[:, RWKV_WIDTH:])
    ones_bd = ones_ref[...]
    kk = k * kk_ref[...]
    kkn = kk * lax.rsqrt(jnp.maximum(_dot_x3(kk * kk, ones_bd), 1e-24))
    k2 = k * (1.0 + (a - 1.0) * ka_ref[...])
    alpha = -kkn
    beta = kkn * a
    bonus = _dot_x3(r * k2 * rk_ref[...], ones_bd) * v

    tril_inc = tril_ref[...]
    ri = lax.broadcasted_iota(jnp.int32, (c, c), 0)
    ci = lax.broadcasted_iota(jnp.int32, (c, c), 1)
    low_s = ri > ci
    low_i = ri >= ci
    eye = ri == ci
    eye_f = jnp.where(eye, 1.0, 0.0)

    ys = []
    for ch in range(tt // c):
        sl = slice(ch * c, (ch + 1) * c)
        lwc = lw[sl]
        hi, mid, lo = _split3(lwc)
        cum = _dot(tril_inc, hi) + _dot(tril_inc, mid) + _dot(tril_inc, lo)
        cend = cum[c - 1:c, :]
        e_neg = jnp.exp(-cum)
        e_end = jnp.exp(cend - cum)
        at = alpha[sl] * jnp.exp(cum - lwc)
        bt = beta[sl] * e_neg
        kt = k2[sl] * e_neg
        rt = r[sl] * jnp.exp(cum)
        bh = beta[sl] * e_end
        kh = k2[sl] * e_end
        pc = jnp.exp(cend)
        vc = v[sl]
        yh = []
        for h in range(RWKV_HEADS):
            hs = slice(h * hd, (h + 1) * hd)
            at_h, rt_h, v_h = at[:, hs], rt[:, hs], vc[:, hs].astype(BF16)
            la = jnp.concatenate([at_h, rt_h], axis=0).astype(BF16)
            rb = jnp.concatenate([bt[:, hs], kt[:, hs]], axis=0).astype(BF16)
            a4 = _dot_nt(la, rb)
            nab = jnp.where(low_s, a4[0:c, 0:c], 0.0)
            aak = jnp.where(low_s, a4[0:c, c:], 0.0).astype(BF16)
            arb = jnp.where(low_i, a4[c:, 0:c], 0.0).astype(BF16)
            ark = jnp.where(low_i, a4[c:, c:], 0.0).astype(BF16)
            tinv = eye_f + nab
            npow = nab
            for _ in range(5):
                nb = npow.astype(BF16)
                npow = _dot(nb, nb)
                tinv = tinv + _dot(tinv.astype(BF16), npow.astype(BF16))
            tb = tinv.astype(BF16)
            w2 = _dot(aak, v_h)
            tx = _dot(tb, jnp.concatenate([w2, at_h], axis=1).astype(BF16))
            u0, ta = tx[:, 0:hd], tx[:, hd:]
            tu = jnp.concatenate([ta, u0], axis=1).astype(BF16)
            ax = _dot(arb, tu)
            rq = rt_h + ax[:, 0:hd]
            y0 = ax[:, hd:] + _dot(ark, v_h)
            gh = _dot_tn(bh[:, hs].astype(BF16), tu)
            g_m = jnp.where(eye, pc[:, hs], 0.0) + gh[:, 0:hd]
            h_m = gh[:, hd:] + _dot_tn(kh[:, hs].astype(BF16), v_h)
            st = st_ref[h]
            stb = st.astype(BF16)
            yh.append(_dot(rq.astype(BF16), stb) + y0)
            st_ref[h] = _dot(g_m.astype(BF16), stb) + h_m
        ys.append(jnp.concatenate(yh, axis=1))
    y = jnp.concatenate(ys, axis=0) if len(ys) > 1 else ys[0]

    inv_hd = 1.0 / hd
    mean = _dot_x3(y, ones_bd) * inv_hd
    yc = y - mean
    var = _dot_x3(yc * yc, ones_bd) * inv_hd
    yn = yc * lax.rsqrt(var + GN_EPS) * gw_ref[...] + gb_ref[...]
    o_ref[...] = ((yn + bonus) * gr_ref[...]).astype(BF16)


def _rwkv(rw, gr, mu, w0, a0, wab, kk, ka, rk, gw, gb, ones_bd, tril, *, batch, seq, tt):
    n = batch * seq
    ns = seq // tt
    row = lambda w: pl.BlockSpec((tt, w), lambda b, i: (b * ns + i, 0))
    full = lambda a: pl.BlockSpec(a.shape, lambda b, i: (0,) * a.ndim)
    kern = functools.partial(_rwkv_kernel, tt=tt)
    consts = (mu, w0, a0, wab, kk, ka, rk, gw, gb, ones_bd, tril)
    return pl.pallas_call(
        kern,
        grid=(batch, ns),
        in_specs=[row(RWKV_SHIFT_WIDTH), row(RWKV_WIDTH)] + [full(a) for a in consts],
        out_specs=row(RWKV_WIDTH),
        out_shape=jax.ShapeDtypeStruct((n, RWKV_WIDTH), BF16),
        scratch_shapes=[pltpu.VMEM((RWKV_HEADS, HEAD_DIM, HEAD_DIM), F32),
                        pltpu.VMEM((1, RWKV_SHIFT_WIDTH), F32)],
        compiler_params=pltpu.CompilerParams(dimension_semantics=("arbitrary", "arbitrary"),
                                             vmem_limit_bytes=VMEM_LIMIT),
        name="rwkv",
    )(rw, gr, *consts)


def _out_kernel(x_ref, on_ref, or_ref, wn_ref, wr_ref, g_ref, o_ref):
    h = x_ref[...] + _dot(on_ref[...], wn_ref[...]) + _dot(or_ref[...], wr_ref[...])
    ms = jnp.mean(h * h, axis=-1, keepdims=True)
    o_ref[...] = h * lax.rsqrt(ms + RMS_EPS) * g_ref[...]


def _out_proj(x2, o_nsa, o_rwkv, wn, wr, final_g, *, tm):
    n = x2.shape[0]
    row = lambda w: pl.BlockSpec((tm, w), lambda i: (i, 0))
    full = lambda a: pl.BlockSpec(a.shape, lambda i: (0,) * a.ndim)
    return pl.pallas_call(
        _out_kernel,
        grid=(n // tm,),
        in_specs=[row(D_MODEL), row(NSA_WIDTH), row(RWKV_WIDTH), full(wn), full(wr), full(final_g)],
        out_specs=row(D_MODEL),
        out_shape=jax.ShapeDtypeStruct((n, D_MODEL), F32),
        compiler_params=pltpu.CompilerParams(dimension_semantics=("arbitrary",),
                                             vmem_limit_bytes=VMEM_LIMIT),
        name="out_proj",
    )(x2, o_nsa, o_rwkv, wn, wr, final_g)


def _rope_tables(seq):
    half = ROPE_DIM // 2
    inv = ROPE_THETA ** (-np.arange(half, dtype=np.float64) / half)
    ang = np.arange(seq, dtype=np.float64)[:, None] * inv[None, :]
    cos, sin = np.cos(ang), np.sin(ang)
    ra = np.ones((seq, HEAD_DIM)); rm = np.zeros((seq, HEAD_DIM)); rp = np.zeros((seq, HEAD_DIM))
    ra[:, :half] = cos; ra[:, half:ROPE_DIM] = cos
    rm[:, :half] = -sin
    rp[:, half:ROPE_DIM] = sin
    rep = lambda t: jnp.asarray(np.tile(t, (1, LANE // HEAD_DIM)), F32)
    return rep(ra), rep(rm), rep(rp)


def _cmp_to_sel_t(n_cmp_pad, n_sel):
    n_cmp = n_cmp_pad - 1
    c0 = np.arange(n_cmp)[:, None] * CMP_STRIDE
    s0 = np.arange(n_sel)[None, :] * SEL_BLOCK
    ov = np.clip(np.minimum(c0 + CMP_BLOCK, s0 + SEL_BLOCK) - np.maximum(c0, s0), 0, None) / CMP_BLOCK
    mt = np.zeros((LANE, n_cmp_pad))
    mt[:n_sel, :n_cmp] = ov.T
    return jnp.asarray(mt, BF16)


def _block_expand(n_sel, seq):
    ex = np.zeros((LANE, seq))
    ex[np.arange(seq) // SEL_BLOCK, np.arange(seq)] = 1.0
    return jnp.asarray(ex, BF16)


def _prep_w_in(w_in):
    idx = np.cumsum(IN_SIZES)[:-1].tolist()
    q, kc, vc, ks, vs, kw, vw, gl, gn, rw, gr = jnp.split(w_in, idx, axis=1)
    per_g = 3 * NSA_REP
    pad = jnp.zeros((D_MODEL, GATE_PAD - per_g), w_in.dtype)
    gl_p = jnp.concatenate([t for g in range(NSA_KV) for t in (gl[:, g * per_g:(g + 1) * per_g], pad)], axis=1)
    return jnp.concatenate([q, kc, vc, ks, vs, kw, vw, gl_p, gn, rw, gr], axis=1).astype(BF16)


def _layer(x2, norm_g, w_in, cmp_pos_k, cmp_w1_k, cmp_w2_k, cmp_pos_v, cmp_w1_v, cmp_w2_v,
           shift_mu, decay_w0, decay_up, iclr_a0, iclr_up, k_k, k_a, r_k, gn_w, gn_b, w_out,
           final_g, *, batch, seq):
    n = batch * seq
    tm = 256
    tq = 128
    tt = 128
    nch = seq // CMP_STRIDE
    n_sel = seq // SEL_BLOCK
    ra, rm, rp = _rope_tables(seq)
    rowv = lambda t: t.reshape(1, -1).astype(F32)

    (qh, qrh, kch, vch, ksh, vsh, kwh, vwh, gates, gn, rw, gr) = _in_proj(
        x2, rowv(norm_g), _prep_w_in(w_in), ra, rm, rp, seq=seq, tm=tm)

    chunks = lambda t: t.reshape(NSA_KV, batch, nch, CMP_STRIDE * HEAD_DIM)
    kcmp, vcmp = _compress(chunks(kch), chunks(vch),
                           rowv(cmp_pos_k), cmp_w1_k.astype(BF16), cmp_w2_k.astype(BF16),
                           rowv(cmp_pos_v), cmp_w1_v.astype(BF16), cmp_w2_v.astype(BF16))

    o_nsa = _nsa(qh, qrh, kcmp, vcmp, ksh, vsh, kwh, vwh, gates, gn,
                 _cmp_to_sel_t(nch, n_sel), _block_expand(n_sel, seq), batch=batch, seq=seq, tq=tq)

    z = jnp.zeros((DECAY_RANK, RWKV_WIDTH), F32)
    wab = jnp.concatenate([jnp.concatenate([decay_up, z], axis=1),
                           jnp.concatenate([z, iclr_up], axis=1)], axis=0).astype(BF16)
    hid = np.arange(RWKV_WIDTH) // HEAD_DIM
    ones_bd = jnp.asarray(hid[:, None] == hid[None, :], BF16)
    tril = jnp.asarray(np.tril(np.ones((RWKV_CHUNK, RWKV_CHUNK))), BF16)
    o_rwkv = _rwkv(rw, gr, rowv(shift_mu), rowv(decay_w0), rowv(iclr_a0), wab, rowv(k_k), rowv(k_a),
                   rowv(r_k), rowv(gn_w), rowv(gn_b), ones_bd, tril, batch=batch, seq=seq, tt=tt)

    w_o = w_out.astype(BF16)
    return _out_proj(x2, o_nsa, o_rwkv, w_o[:NSA_WIDTH], w_o[NSA_WIDTH:], rowv(final_g), tm=tm)


def kernel(x, norm_g, w_in, cmp_pos_k, cmp_w1_k, cmp_w2_k, cmp_pos_v, cmp_w1_v, cmp_w2_v, shift_mu, decay_w0, decay_up, iclr_a0, iclr_up, k_k, k_a, r_k, gn_w, gn_b, w_out, final_g):
    batch, seq, d = x.shape
    assert d == D_MODEL and norm_g.shape[0] == 1, "single-layer trunk"
    out = _layer(x.reshape(batch * seq, d), norm_g[0], w_in[0], cmp_pos_k[0], cmp_w1_k[0], cmp_w2_k[0],
                 cmp_pos_v[0], cmp_w1_v[0], cmp_w2_v[0], shift_mu[0], decay_w0[0], decay_up[0],
                 iclr_a0[0], iclr_up[0], k_k[0], k_a[0], r_k[0], gn_w[0], gn_b[0], w_out[0],
                 final_g, batch=batch, seq=seq)
    return out.reshape(batch, seq, d)
```

```python
import functools

import numpy as np
import jax
import jax.numpy as jnp
from jax import lax
from jax.experimental import pallas as pl
from jax.experimental.pallas import tpu as pltpu

F32 = jnp.float32
BF16 = jnp.bfloat16

D_MODEL = 1024
HEAD_DIM = 64
NSA_HEADS = 8
NSA_KV = 2
NSA_REP = NSA_HEADS // NSA_KV
RWKV_HEADS = 8
NSA_WIDTH = NSA_HEADS * HEAD_DIM
RWKV_WIDTH = RWKV_HEADS * HEAD_DIM
KV_WIDTH = NSA_KV * HEAD_DIM
ROPE_DIM = HEAD_DIM // 4
ROPE_HALF = ROPE_DIM // 2
ROPE_THETA = 500000.0
CMP_BLOCK = 32
CMP_STRIDE = 16
CMP_HIDDEN = 256
SEL_BLOCK = 64
SEL_TOPK = 8
WINDOW = 512
DECAY_RANK = 64
ICLR_RANK = 64
RWKV_SHIFT_WIDTH = 3 * RWKV_WIDTH + DECAY_RANK + ICLR_RANK
IN_SIZES = (NSA_WIDTH, KV_WIDTH, KV_WIDTH, KV_WIDTH, KV_WIDTH, KV_WIDTH, KV_WIDTH,
            3 * NSA_HEADS, NSA_WIDTH, RWKV_SHIFT_WIDTH, RWKV_WIDTH)
SCALE = HEAD_DIM ** -0.5
RMS_EPS = 1e-6
GN_EPS = 64e-5
NEG_INF = -1e30
FORCE_BONUS = 1e3

LANE = 128
SUBLANE = 8
GATE_ROWS = 16

T_KV = 0
T_RW = T_KV + 4 * KV_WIDTH
T_GR = T_RW + RWKV_SHIFT_WIDTH
T_END = T_GR + RWKV_WIDTH
R_Q = 0
R_VS = R_Q + NSA_WIDTH
R_VW = R_VS + KV_WIDTH
R_GL = R_VW + KV_WIDTH
R_GN = R_GL + NSA_KV * GATE_ROWS
R_END = R_GN + NSA_WIDTH

KEY_TILE = 256
RWKV_CHUNK = 64
VMEM_LIMIT = 48 * 1024 * 1024


def _dot(a, b):
    return jnp.dot(a, b, preferred_element_type=F32)


def _dot_nt(a, b):
    return lax.dot_general(a, b, (((1,), (1,)), ((), ())), preferred_element_type=F32)


def _dot_tn(a, b):
    return lax.dot_general(a, b, (((0,), (0,)), ((), ())), preferred_element_type=F32)


def _bmm(a, b):
    return lax.dot_general(a, b, (((2,), (1,)), ((0,), (0,))), preferred_element_type=F32)


def _bmm_nt(a, b):
    return lax.dot_general(a, b, (((2,), (2,)), ((0,), (0,))), preferred_element_type=F32)


def _split3(x):
    hi = x.astype(BF16)
    r1 = x - hi.astype(F32)
    mid = r1.astype(BF16)
    lo = (r1 - mid.astype(F32)).astype(BF16)
    return hi, mid, lo


def _dot_x3(x, w_bf16):
    hi, mid, lo = _split3(x)
    return _dot(hi, w_bf16) + _dot(mid, w_bf16) + _dot(lo, w_bf16)


def _x3_dot(w_bf16, x):
    hi, mid, lo = _split3(x)
    return _dot(w_bf16, hi) + _dot(w_bf16, mid) + _dot(w_bf16, lo)


def _sigmoid(x):
    return 1.0 / (1.0 + jnp.exp(-x))


def _rope128(t, ra, rm, rp):
    return t * ra + pltpu.roll(t, LANE - ROPE_HALF, 1) * rm + pltpu.roll(t, ROPE_HALF, 1) * rp


def _in_proj_kernel(x_ref, g_ref, wt_ref, wf_ref, ra_ref, rm_ref, rp_ref, cos_ref, sin_ref,
                    kc_ref, vc_ref, ks_ref, kw_ref, rw_ref, gr_ref,
                    qt_ref, qrt_ref, vst_ref, vwt_ref, gate_ref, gn_ref):
    x = x_ref[...]
    ms = jnp.mean(x * x, axis=-1, keepdims=True)
    y = (x * lax.rsqrt(ms + RMS_EPS) * g_ref[...]).astype(BF16)

    kv = _dot(y, wt_ref[:, T_KV:T_RW])
    ra, rm, rp = ra_ref[...], rm_ref[...], rp_ref[...]
    for i, (ref, rot) in enumerate(((kc_ref, False), (vc_ref, False), (ks_ref, True), (kw_ref, True))):
        t = kv[:, i * LANE:(i + 1) * LANE]
        if rot:
            t = _rope128(t, ra, rm, rp)
        for g in range(NSA_KV):
            ref[g] = t[:, g * HEAD_DIM:(g + 1) * HEAD_DIM].astype(BF16)
    rw_ref[...] = _dot(y, wt_ref[:, T_RW:T_GR])
    gr = _dot(y, wt_ref[:, T_GR:T_END])
    gr_ref[...] = gr * _sigmoid(gr)

    def proj_t(r0, r1):
        return _dot_nt(wf_ref[r0:r1, :], y)

    qt = proj_t(R_Q, R_VS) * SCALE
    cos, sin = cos_ref[...], sin_ref[...]
    qt_ref[...] = qt.astype(BF16)
    for h in range(NSA_HEADS):
        r0 = h * HEAD_DIM
        t1 = qt[r0:r0 + ROPE_HALF]
        t2 = qt[r0 + ROPE_HALF:r0 + ROPE_DIM]
        qrt_ref[r0:r0 + ROPE_DIM, :] = jnp.concatenate(
            [t1 * cos - t2 * sin, t2 * cos + t1 * sin], axis=0).astype(BF16)
        qrt_ref[r0 + ROPE_DIM:r0 + HEAD_DIM, :] = qt[r0 + ROPE_DIM:r0 + HEAD_DIM].astype(BF16)
    vt = proj_t(R_VS, R_GL).astype(BF16)
    for j in range(vt.shape[1] // LANE):
        vst_ref[j] = vt[0:KV_WIDTH, j * LANE:(j + 1) * LANE]
        vwt_ref[j] = vt[KV_WIDTH:, j * LANE:(j + 1) * LANE]
    gate_ref[...] = _sigmoid(proj_t(R_GL, R_GN))
    gn = proj_t(R_GN, R_END)
    gn_ref[...] = gn * _sigmoid(gn)


def _in_proj(x2, norm_g, w_t, w_f, tabs, *, seq, tm):
    n = x2.shape[0]
    spt = seq // tm
    ra, rm, rp, cos, sin = tabs
    hm = jax.ShapeDtypeStruct((NSA_KV, n, HEAD_DIM), BF16)
    hspec = pl.BlockSpec((NSA_KV, tm, HEAD_DIM), lambda i: (0, i, 0))
    row = lambda w: pl.BlockSpec((tm, w), lambda i: (i, 0))
    col = lambda r: pl.BlockSpec((r, tm), lambda i: (0, i))
    full = lambda a: pl.BlockSpec(a.shape, lambda i: (0,) * a.ndim)
    tab = pl.BlockSpec((tm, LANE), lambda i: (i % spt, 0))
    tabt = pl.BlockSpec((ROPE_HALF, tm), lambda i: (0, i % spt))
    vtile = pl.BlockSpec((tm // LANE, KV_WIDTH, LANE), lambda i: (i, 0, 0))
    return pl.pallas_call(
        _in_proj_kernel,
        grid=(n // tm,),
        in_specs=[row(D_MODEL), full(norm_g), full(w_t), full(w_f), tab, tab, tab, tabt, tabt],
        out_specs=[hspec] * 4 + [row(RWKV_SHIFT_WIDTH), row(RWKV_WIDTH),
                                 col(NSA_WIDTH), col(NSA_WIDTH), vtile, vtile,
                                 col(NSA_KV * GATE_ROWS), col(NSA_WIDTH)],
        out_shape=[hm] * 4 + [jax.ShapeDtypeStruct((n, RWKV_SHIFT_WIDTH), F32),
                              jax.ShapeDtypeStruct((n, RWKV_WIDTH), F32),
                              jax.ShapeDtypeStruct((NSA_WIDTH, n), BF16),
                              jax.ShapeDtypeStruct((NSA_WIDTH, n), BF16),
                              jax.ShapeDtypeStruct((n // LANE, KV_WIDTH, LANE), BF16),
                              jax.ShapeDtypeStruct((n // LANE, KV_WIDTH, LANE), BF16),
                              jax.ShapeDtypeStruct((NSA_KV * GATE_ROWS, n), F32),
                              jax.ShapeDtypeStruct((NSA_WIDTH, n), F32)],
        compiler_params=pltpu.CompilerParams(dimension_semantics=("arbitrary",),
                                             vmem_limit_bytes=VMEM_LIMIT),
        name="in_proj",
    )(x2, norm_g, w_t, w_f, ra, rm, rp, cos, sin)


def _compress_kernel(kc_ref, vc_ref, pk_ref, w1k_ref, w2k_ref, pv_ref, w1v_ref, w2vt_ref,
                     ko_ref, vo_ref):
    half = CMP_STRIDE * HEAD_DIM

    def hidden(c_ref, pos_ref, w1_ref):
        c = c_ref[0, 0]
        za = _dot(c, w1_ref[0:half, :])
        zb = _dot(c, w1_ref[half:2 * half, :])
        pos = jnp.broadcast_to(pos_ref[...], (SUBLANE, 2 * half)).astype(BF16)
        pv = _dot(pos, w1_ref[...])[0:1, :]
        hid = za + pltpu.roll(zb, c.shape[0] - 1, 0) + pv
        return (hid * _sigmoid(hid)).astype(BF16)

    ko_ref[0, 0] = _dot(hidden(kc_ref, pk_ref, w1k_ref), w2k_ref[...]).astype(BF16)
    vo_ref[0, 0] = _dot_nt(w2vt_ref[...], hidden(vc_ref, pv_ref, w1v_ref)).astype(BF16)


def _compress(kc_r, vc_r, pk, w1k, w2k, pv, w1v, w2vt):
    g, b, nch, width = kc_r.shape
    blk = pl.BlockSpec((1, 1, nch, width), lambda i, j: (i, j, 0, 0))
    full = lambda a: pl.BlockSpec(a.shape, lambda i, j: (0,) * a.ndim)
    return pl.pallas_call(
        _compress_kernel,
        grid=(g, b),
        in_specs=[blk, blk, full(pk), full(w1k), full(w2k), full(pv), full(w1v), full(w2vt)],
        out_specs=[pl.BlockSpec((1, 1, nch, HEAD_DIM), lambda i, j: (i, j, 0, 0)),
                   pl.BlockSpec((1, 1, HEAD_DIM, nch), lambda i, j: (i, j, 0, 0))],
        out_shape=[jax.ShapeDtypeStruct((g, b, nch, HEAD_DIM), BF16),
                   jax.ShapeDtypeStruct((g, b, HEAD_DIM, nch), BF16)],
        compiler_params=pltpu.CompilerParams(dimension_semantics=("arbitrary", "arbitrary"),
                                             vmem_limit_bytes=VMEM_LIMIT),
        name="compress",
    )(kc_r, vc_r, pk, w1k, w2k, pv, w1v, w2vt)


def _nsa_kernel(qt_ref, qrt_ref, kc_ref, vct_ref, ks_ref, vst_ref, kw_ref, vwt_ref,
                gate_ref, gn_ref, mt_ref, o_ref, sb_ref, *, tq, seq):
    tk = KEY_TILE
    qi = pl.program_id(2)
    q0 = qi * tq
    nl = NSA_REP * tq
    n_sel = seq // SEL_BLOCK
    ncp = kc_ref.shape[2]

    def heads_on_lanes(ref):
        return jnp.concatenate([ref[r * HEAD_DIM:(r + 1) * HEAD_DIM, :] for r in range(NSA_REP)], axis=1)

    def tile4(a):
        return jnp.concatenate([a] * NSA_REP, axis=1)

    s = _dot(kc_ref[0, 0], heads_on_lanes(qt_ref))
    t_c = q0 + lax.broadcasted_iota(jnp.int32, (ncp, tq), 1)
    c_c = lax.broadcasted_iota(jnp.int32, (ncp, tq), 0)
    cmask = tile4((c_c * CMP_STRIDE + (CMP_BLOCK - 1)) <= t_c)
    s = jnp.where(cmask, s, NEG_INF)
    m = jnp.max(s, axis=0, keepdims=True)
    e = jnp.where(cmask, jnp.exp(s - m), 0.0)
    den = jnp.sum(e, axis=0, keepdims=True)
    p = e * (1.0 / jnp.where(den > 0.0, den, 1.0))
    o_cmp = _dot(vct_ref[0, 0], p.astype(BF16))

    psum = p[:, 0:tq]
    for r in range(1, NSA_REP):
        psum = psum + p[:, r * tq:(r + 1) * tq]
    imp = _x3_dot(mt_ref[...], psum)
    j = lax.broadcasted_iota(jnp.int32, (n_sel, tq), 0)
    t = q0 + lax.broadcasted_iota(jnp.int32, (n_sel, tq), 1)
    tb = t // SEL_BLOCK
    forced = (j == 0) | (j == tb) | (j == tb - 1)
    val = jnp.where(j <= tb, imp + jnp.where(forced, FORCE_BONUS, 0.0), -1.0)
    cnt = jnp.zeros((n_sel, tq), F32)
    for i in range(n_sel):
        vi = val[i:i + 1, :]
        ge = jnp.where(vi >= val, 1.0, 0.0)
        gt = jnp.where(vi > val, 1.0, 0.0)
        cnt = cnt + jnp.where(j > i, ge, gt)
    sb_ref[...] = jnp.where(cnt < float(SEL_TOPK), 0.0, NEG_INF)

    qr = heads_on_lanes(qrt_ref)
    k_s = lax.broadcasted_iota(jnp.int32, (tk, tq), 0)
    t_l = q0 + lax.broadcasted_iota(jnp.int32, (tk, tq), 1)
    bpt = tk // SEL_BLOCK

    def update(carry, k_ref, vt_ref, kt, bias):
        m_i, l_i, acc = carry
        kb = k_ref[0, pl.ds(pl.multiple_of(kt * tk, tk), tk), :]
        vtb = jnp.concatenate([vt_ref[kt * (tk // LANE) + jj] for jj in range(tk // LANE)], axis=1)
        sc = _dot(kb, qr)
        if bias is not None:
            sc = sc + tile4(bias)
        m_n = jnp.maximum(m_i, jnp.max(sc, axis=0, keepdims=True))
        alpha = jnp.exp(m_i - m_n)
        pe = jnp.exp(sc - m_n)
        l_n = alpha * l_i + jnp.sum(pe, axis=0, keepdims=True)
        return m_n, l_n, alpha * acc + _dot(vtb, pe.astype(BF16))

    def sel_bias(kt):
        rows = [jnp.broadcast_to(sb_ref[pl.ds(kt * bpt + jj, 1), :], (SEL_BLOCK, tq)) for jj in range(bpt)]
        return jnp.concatenate(rows, axis=0)

    def both(kt, carry, causal):
        d = t_l - (kt * tk + k_s)
        sb = sel_bias(kt)
        if causal:
            sb = jnp.where(d >= 0, sb, NEG_INF)
            wb = jnp.where(d >= 0, 0.0, NEG_INF)
        else:
            wb = jnp.where(d < WINDOW, 0.0, NEG_INF)
        return (update(carry[0], ks_ref, vst_ref, kt, sb), update(carry[1], kw_ref, vwt_ref, kt, wb))

    init = (jnp.full((1, nl), NEG_INF, F32), jnp.zeros((1, nl), F32), jnp.zeros((HEAD_DIM, nl), F32))
    a = (q0 + tq - 1) // tk
    lo = jnp.maximum(a - WINDOW // tk, 0)
    c_sel = lax.fori_loop(0, lo, lambda kt, c: update(c, ks_ref, vst_ref, kt, sel_bias(kt)), init)
    c_sel, c_win = lax.fori_loop(lo, a, lambda kt, c: both(kt, c, False), (c_sel, init))
    c_sel, c_win = both(a, (c_sel, c_win), True)
    o_sel = c_sel[2] * (1.0 / c_sel[1])
    o_win = c_win[2] * (1.0 / c_win[1])

    gates = gate_ref[...]
    for r in range(NSA_REP):
        ls = slice(r * tq, (r + 1) * tq)
        o = (gates[3 * r:3 * r + 1, :] * o_cmp[:, ls] + gates[3 * r + 1:3 * r + 2, :] * o_sel[:, ls]
             + gates[3 * r + 2:3 * r + 3, :] * o_win[:, ls])
        rs = slice(r * HEAD_DIM, (r + 1) * HEAD_DIM)
        o_ref[rs, :] = (o * gn_ref[rs, :]).astype(BF16)


def _nsa(qt, qrt, kcmp, vcmpt, ksh, vst, kwh, vwt, gates, gnt, mt, *, batch, seq, tq):
    n = batch * seq
    nq = seq // tq
    ncp = kcmp.shape[2]
    grp = NSA_REP * HEAD_DIM
    qspec = pl.BlockSpec((grp, tq), lambda b, g, i: (g, b * nq + i))
    kspec = pl.BlockSpec((1, seq, HEAD_DIM), lambda b, g, i: (g, b, 0))
    vspec = pl.BlockSpec((seq // LANE, HEAD_DIM, LANE), lambda b, g, i: (b, g, 0))
    kern = functools.partial(_nsa_kernel, tq=tq, seq=seq)
    return pl.pallas_call(
        kern,
        grid=(batch, NSA_KV, nq),
        in_specs=[qspec, qspec,
                  pl.BlockSpec((1, 1, ncp, HEAD_DIM), lambda b, g, i: (g, b, 0, 0)),
                  pl.BlockSpec((1, 1, HEAD_DIM, ncp), lambda b, g, i: (g, b, 0, 0)),
                  kspec, vspec, kspec, vspec,
                  pl.BlockSpec((GATE_ROWS, tq), lambda b, g, i: (g, b * nq + i)),
                  qspec,
                  pl.BlockSpec(mt.shape, lambda b, g, i: (0, 0))],
        out_specs=qspec,
        out_shape=jax.ShapeDtypeStruct((NSA_WIDTH, n), BF16),
        scratch_shapes=[pltpu.VMEM((seq // SEL_BLOCK, tq), F32)],
        compiler_params=pltpu.CompilerParams(
            dimension_semantics=("arbitrary", "arbitrary", "arbitrary"),
            vmem_limit_bytes=VMEM_LIMIT),
        name="nsa",
    )(qt, qrt, kcmp, vcmpt, ksh, vst, kwh, vwt, gates, gnt, mt)


def _rwkv_kernel(p_ref, gr_ref, mu_ref, w0_ref, a0_ref, wab_ref, kk_ref, ka_ref, rk_ref,
                 gw_ref, gb_ref, ones_ref, tril_ref, o_ref, st_ref, carry_ref, *, tt):
    c = RWKV_CHUNK
    hd = HEAD_DIM
    nh = RWKV_HEADS
    nc = tt // c
    step = pl.program_id(1)

    @pl.when(step == 0)
    def _():
        st_ref[...] = jnp.zeros_like(st_ref)
        carry_ref[...] = jnp.zeros_like(carry_ref)

    p = p_ref[...]
    row = lax.broadcasted_iota(jnp.int32, p.shape, 0)
    prev = jnp.where(row == 0, carry_ref[...], pltpu.roll(p, 1, 0))
    carry_ref[...] = p[tt - 1:tt, :]
    ps = p + mu_ref[...] * (prev - p)
    r = ps[:, 0:RWKV_WIDTH]
    k = ps[:, RWKV_WIDTH:2 * RWKV_WIDTH]
    v = ps[:, 2 * RWKV_WIDTH:3 * RWKV_WIDTH]
    lora = ps[:, 3 * RWKV_WIDTH:]
    lane = lax.broadcasted_iota(jnp.int32, lora.shape, 1)
    feat = jnp.where(lane < DECAY_RANK, jnp.tanh(lora), lora).astype(BF16)
    up = _dot(feat, wab_ref[...])
    w = w0_ref[...] + up[:, 0:RWKV_WIDTH]
    lw = _sigmoid(w) * (-float(np.exp(-0.5)))
    a = _sigmoid(a0_ref[...] + up[:, RWKV_WIDTH:])
    ones_bd = ones_ref[...]
    kk = k * kk_ref[...]
    kkn = kk * lax.rsqrt(jnp.maximum(_dot_x3(kk * kk, ones_bd), 1e-24))
    k2 = k * (1.0 + (a - 1.0) * ka_ref[...])
    alpha = -kkn
    beta = kkn * a
    bonus = _dot_x3(r * k2 * rk_ref[...], ones_bd) * v

    cum = _x3_dot(tril_ref[...], lw)
    cend = jnp.concatenate(
        [jnp.broadcast_to(cum[(ch + 1) * c - 1:(ch + 1) * c, :], (c, RWKV_WIDTH)) for ch in range(nc)], axis=0)
    e_neg = jnp.exp(-cum)
    e_end = jnp.exp(cend - cum)
    at = alpha * jnp.exp(cum - lw)
    bt = beta * e_neg
    kt = k2 * e_neg
    rt = r * jnp.exp(cum)
    bh = beta * e_end
    kh = k2 * e_end
    pc = jnp.exp(cend)

    def items(x, dtype=BF16):
        return jnp.stack([x[ch * c:(ch + 1) * c, h * hd:(h + 1) * hd]
                          for ch in range(nc) for h in range(nh)], axis=0).astype(dtype)

    at_i, rt_i, bt_i, kt_i, v_i = items(at), items(rt, F32), items(bt), items(kt), items(v)
    bh_i, kh_i = items(bh), items(kh)
    ri = lax.broadcasted_iota(jnp.int32, (1, c, c), 1)
    ci = lax.broadcasted_iota(jnp.int32, (1, c, c), 2)
    low_s = ri > ci
    low_i = ri >= ci
    eye = ri == ci

    a4 = _bmm_nt(jnp.concatenate([at_i, rt_i.astype(BF16)], axis=1),
                 jnp.concatenate([bt_i, kt_i], axis=1))
    nab = jnp.where(low_s, a4[:, 0:c, 0:c], 0.0)
    aak = jnp.where(low_s, a4[:, 0:c, c:], 0.0).astype(BF16)
    arbk = jnp.concatenate([jnp.where(low_i, a4[:, c:, 0:c], 0.0),
                            jnp.where(low_i, a4[:, c:, c:], 0.0)], axis=2).astype(BF16)
    tinv = jnp.where(eye, 1.0, 0.0) + nab
    npow = nab
    for _ in range(5):
        nb = npow.astype(BF16)
        npow = _bmm(nb, nb)
        tinv = tinv + _bmm(tinv.astype(BF16), npow.astype(BF16))
    w2 = _bmm(aak, v_i)
    tx = _bmm(tinv.astype(BF16), jnp.concatenate([w2.astype(BF16), at_i], axis=2))
    u0, ta = tx[:, :, 0:hd], tx[:, :, hd:]
    rhs = jnp.concatenate([jnp.concatenate([ta, u0], axis=2).astype(BF16),
                           jnp.concatenate([jnp.zeros_like(v_i), v_i], axis=2)], axis=1)
    ax = _bmm(arbk, rhs)
    bkt = jnp.concatenate([jnp.swapaxes(items(bh, F32), 1, 2), jnp.swapaxes(items(kh, F32), 1, 2)],
                          axis=2).astype(BF16)
    gh = _bmm(bkt, rhs)
    rq = rt_i + ax[:, :, 0:hd]
    y0 = ax[:, :, hd:]
    pc_i = items(pc, F32)
    g_m = jnp.where(eye, pc_i, 0.0) + gh[:, :, 0:hd]
    h_m = gh[:, :, hd:]
    lhs = jnp.concatenate([rq, g_m], axis=1).astype(BF16)

    st = st_ref[...]
    ys = []
    for ch in range(nc):
        sl = slice(ch * nh, (ch + 1) * nh)
        res = _bmm(lhs[sl], st.astype(BF16))
        yc = res[:, 0:c, :] + y0[sl]
        st = res[:, c:, :] + h_m[sl]
        ys.append(jnp.concatenate([yc[h] for h in range(nh)], axis=1))
    st_ref[...] = st
    y = jnp.concatenate(ys, axis=0) if nc > 1 else ys[0]

    inv_hd = 1.0 / hd
    mean = _dot_x3(y, ones_bd) * inv_hd
    ycen = y - mean
    var = _dot_x3(ycen * ycen, ones_bd) * inv_hd
    yn = ycen * lax.rsqrt(var + GN_EPS) * gw_ref[...] + gb_ref[...]
    o_ref[...] = ((yn + bonus) * gr_ref[...]).astype(BF16)


def _rwkv(rw, gr, mu, w0, a0, wab, kk, ka, rk, gw, gb, ones_bd, tril, *, batch, seq, tt):
    n = batch * seq
    ns = seq // tt
    row = lambda w: pl.BlockSpec((tt, w), lambda b, i: (b * ns + i, 0))
    full = lambda a: pl.BlockSpec(a.shape, lambda b, i: (0,) * a.ndim)
    kern = functools.partial(_rwkv_kernel, tt=tt)
    consts = (mu, w0, a0, wab, kk, ka, rk, gw, gb, ones_bd, tril)
    return pl.pallas_call(
        kern,
        grid=(batch, ns),
        in_specs=[row(RWKV_SHIFT_WIDTH), row(RWKV_WIDTH)] + [full(a) for a in consts],
        out_specs=row(RWKV_WIDTH),
        out_shape=jax.ShapeDtypeStruct((n, RWKV_WIDTH), BF16),
        scratch_shapes=[pltpu.VMEM((RWKV_HEADS, HEAD_DIM, HEAD_DIM), F32),
                        pltpu.VMEM((1, RWKV_SHIFT_WIDTH), F32)],
        compiler_params=pltpu.CompilerParams(dimension_semantics=("arbitrary", "arbitrary"),
                                             vmem_limit_bytes=VMEM_LIMIT),
        name="rwkv",
    )(rw, gr, *consts)


def _out_kernel(x_ref, ont_ref, or_ref, wn_ref, wr_ref, g_ref, o_ref):
    h = x_ref[...] + _dot_tn(ont_ref[...], wn_ref[...]) + _dot(or_ref[...], wr_ref[...])
    ms = jnp.mean(h * h, axis=-1, keepdims=True)
    o_ref[...] = h * lax.rsqrt(ms + RMS_EPS) * g_ref[...]


def _out_proj(x2, o_nsa_t, o_rwkv, wn, wr, final_g, *, tm):
    n = x2.shape[0]
    row = lambda w: pl.BlockSpec((tm, w), lambda i: (i, 0))
    full = lambda a: pl.BlockSpec(a.shape, lambda i: (0,) * a.ndim)
    return pl.pallas_call(
        _out_kernel,
        grid=(n // tm,),
        in_specs=[row(D_MODEL), pl.BlockSpec((NSA_WIDTH, tm), lambda i: (0, i)), row(RWKV_WIDTH),
                  full(wn), full(wr), full(final_g)],
        out_specs=row(D_MODEL),
        out_shape=jax.ShapeDtypeStruct((n, D_MODEL), F32),
        compiler_params=pltpu.CompilerParams(dimension_semantics=("arbitrary",),
                                             vmem_limit_bytes=VMEM_LIMIT),
        name="out_proj",
    )(x2, o_nsa_t, o_rwkv, wn, wr, final_g)


def _rope_tables(seq):
    inv = ROPE_THETA ** (-np.arange(ROPE_HALF, dtype=np.float64) / ROPE_HALF)
    ang = np.arange(seq, dtype=np.float64)[:, None] * inv[None, :]
    cos, sin = np.cos(ang), np.sin(ang)
    ra = np.ones((seq, HEAD_DIM)); rm = np.zeros((seq, HEAD_DIM)); rp = np.zeros((seq, HEAD_DIM))
    ra[:, :ROPE_HALF] = cos; ra[:, ROPE_HALF:ROPE_DIM] = cos
    rm[:, :ROPE_HALF] = -sin
    rp[:, ROPE_HALF:ROPE_DIM] = sin
    rep = lambda t: jnp.asarray(np.tile(t, (1, LANE // HEAD_DIM)), F32)
    return rep(ra), rep(rm), rep(rp), jnp.asarray(cos.T, F32), jnp.asarray(sin.T, F32)


def _cmp_to_sel_t(n_cmp_pad, n_sel):
    n_cmp = n_cmp_pad - 1
    c0 = np.arange(n_cmp)[:, None] * CMP_STRIDE
    s0 = np.arange(n_sel)[None, :] * SEL_BLOCK
    ov = np.clip(np.minimum(c0 + CMP_BLOCK, s0 + SEL_BLOCK) - np.maximum(c0, s0), 0, None) / CMP_BLOCK
    mt = np.zeros((n_sel, n_cmp_pad))
    mt[:, :n_cmp] = ov.T
    return jnp.asarray(mt, BF16)


def _prep_w_in(w_in):
    idx = np.cumsum(IN_SIZES)[:-1].tolist()
    q, kc, vc, ks, vs, kw, vw, gl, gn, rw, gr = jnp.split(w_in, idx, axis=1)
    w_t = jnp.concatenate([kc, vc, ks, kw, rw, gr], axis=1).astype(BF16)
    per_g = 3 * NSA_REP
    pad = jnp.zeros((D_MODEL, GATE_ROWS - per_g), w_in.dtype)
    gl_p = [t for g in range(NSA_KV) for t in (gl[:, g * per_g:(g + 1) * per_g], pad)]
    w_f = jnp.concatenate([q, vs, vw] + gl_p + [gn], axis=1).T.astype(BF16)
    return w_t, w_f


def _layer(x2, norm_g, w_in, cmp_pos_k, cmp_w1_k, cmp_w2_k, cmp_pos_v, cmp_w1_v, cmp_w2_v,
           shift_mu, decay_w0, decay_up, iclr_a0, iclr_up, k_k, k_a, r_k, gn_w, gn_b, w_out,
           final_g, *, batch, seq):
    tm = 256
    tq = 128
    tt = 256
    assert WINDOW == 2 * KEY_TILE and seq % KEY_TILE == 0 and KEY_TILE % tq == 0
    nch = seq // CMP_STRIDE
    n_sel = seq // SEL_BLOCK
    rowv = lambda t: t.reshape(1, -1).astype(F32)

    w_t, w_f = _prep_w_in(w_in)
    (kch, vch, ksh, kwh, rw, gr, qt, qrt, vst, vwt, gates, gnt) = _in_proj(
        x2, rowv(norm_g), w_t, w_f, _rope_tables(seq), seq=seq, tm=tm)

    chunks = lambda t: t.reshape(NSA_KV, batch, nch, CMP_STRIDE * HEAD_DIM)
    kcmp, vcmpt = _compress(chunks(kch), chunks(vch),
                            rowv(cmp_pos_k), cmp_w1_k.astype(BF16), cmp_w2_k.astype(BF16),
                            rowv(cmp_pos_v), cmp_w1_v.astype(BF16), cmp_w2_v.T.astype(BF16))

    o_nsa_t = _nsa(qt, qrt, kcmp, vcmpt, ksh, vst, kwh, vwt, gates, gnt,
                   _cmp_to_sel_t(nch, n_sel), batch=batch, seq=seq, tq=tq)

    z = jnp.zeros((DECAY_RANK, RWKV_WIDTH), F32)
    wab = jnp.concatenate([jnp.concatenate([decay_up, z], axis=1),
                           jnp.concatenate([z, iclr_up], axis=1)], axis=0).astype(BF16)
    hid = np.arange(RWKV_WIDTH) // HEAD_DIM
    ones_bd = jnp.asarray(hid[:, None] == hid[None, :], BF16)
    ti = np.arange(tt)
    tril = jnp.asarray((ti[:, None] >= ti[None, :]) & (ti[:, None] // RWKV_CHUNK == ti[None, :] // RWKV_CHUNK), BF16)
    o_rwkv = _rwkv(rw, gr, rowv(shift_mu), rowv(decay_w0), rowv(iclr_a0), wab, rowv(k_k), rowv(k_a),
                   rowv(r_k), rowv(gn_w), rowv(gn_b), ones_bd, tril, batch=batch, seq=seq, tt=tt)

    w_o = w_out.astype(BF16)
    return _out_proj(x2, o_nsa_t, o_rwkv, w_o[:NSA_WIDTH], w_o[NSA_WIDTH:], rowv(final_g), tm=tm)


def kernel(x, norm_g, w_in, cmp_pos_k, cmp_w1_k, cmp_w2_k, cmp_pos_v, cmp_w1_v, cmp_w2_v, shift_mu, decay_w0, decay_up, iclr_a0, iclr_up, k_k, k_a, r_k, gn_w, gn_b, w_out, final_g):
    batch, seq, d = x.shape
    assert d == D_MODEL and norm_g.shape[0] == 1, "single-layer trunk"
    out = _layer(x.reshape(batch * seq, d), norm_g[0], w_in[0], cmp_pos_k[0], cmp_w1_k[0], cmp_w2_k[0],
                 cmp_pos_v[0], cmp_w1_v[0], cmp_w2_v[0], shift_mu[0], decay_w0[0], decay_up[0],
                 iclr_a0[0], iclr_up[0], k_k[0], k_a[0], r_k[0], gn_w[0], gn_b[0], w_out[0],
                 final_g, batch=batch, seq=seq)
    return out.reshape(batch, seq, d)
```

```python
import functools

import numpy as np
import jax
import jax.numpy as jnp
from jax import lax
from jax.experimental import pallas as pl
from jax.experimental.pallas import tpu as pltpu

F32 = jnp.float32
BF16 = jnp.bfloat16

D_MODEL = 1024
HEAD_DIM = 64
NSA_HEADS = 8
NSA_KV = 2
NSA_REP = NSA_HEADS // NSA_KV
RWKV_HEADS = 8
NSA_WIDTH = NSA_HEADS * HEAD_DIM
RWKV_WIDTH = RWKV_HEADS * HEAD_DIM
KV_WIDTH = NSA_KV * HEAD_DIM
ROPE_DIM = HEAD_DIM // 4
ROPE_HALF = ROPE_DIM // 2
ROPE_THETA = 500000.0
CMP_BLOCK = 32
CMP_STRIDE = 16
CMP_HIDDEN = 256
SEL_BLOCK = 64
SEL_TOPK = 8
WINDOW = 512
DECAY_RANK = 64
ICLR_RANK = 64
RWKV_SHIFT_WIDTH = 3 * RWKV_WIDTH + DECAY_RANK + ICLR_RANK
IN_SIZES = (NSA_WIDTH, KV_WIDTH, KV_WIDTH, KV_WIDTH, KV_WIDTH, KV_WIDTH, KV_WIDTH,
            3 * NSA_HEADS, NSA_WIDTH, RWKV_SHIFT_WIDTH, RWKV_WIDTH)
SCALE = HEAD_DIM ** -0.5
RMS_EPS = 1e-6
GN_EPS = 64e-5
NEG_INF = -1e30
FORCE_BONUS = 1e3

LANE = 128
SUBLANE = 8
GATE_ROWS = 16

T_KV = 0
T_RW = T_KV + 4 * KV_WIDTH
T_GR = T_RW + RWKV_SHIFT_WIDTH
T_END = T_GR + RWKV_WIDTH
R_Q = 0
R_VS = R_Q + NSA_WIDTH
R_VW = R_VS + KV_WIDTH
R_GL = R_VW + KV_WIDTH
R_GN = R_GL + NSA_KV * GATE_ROWS
R_END = R_GN + NSA_WIDTH

LOG2E = float(np.log2(np.e))
K_AUG = LANE
V_ROWS = HEAD_DIM + 16
KEY_TILE = 256
RWKV_CHUNK = 64
VMEM_LIMIT = 48 * 1024 * 1024


def _dot(a, b):
    return jnp.dot(a, b, preferred_element_type=F32)


def _dot_nt(a, b):
    return lax.dot_general(a, b, (((1,), (1,)), ((), ())), preferred_element_type=F32)


def _dot_tn(a, b):
    return lax.dot_general(a, b, (((0,), (0,)), ((), ())), preferred_element_type=F32)


def _bmm(a, b):
    return lax.dot_general(a, b, (((2,), (1,)), ((0,), (0,))), preferred_element_type=F32)


def _bmm_nt(a, b):
    return lax.dot_general(a, b, (((2,), (2,)), ((0,), (0,))), preferred_element_type=F32)


def _split3(x):
    hi = x.astype(BF16)
    r1 = x - hi.astype(F32)
    mid = r1.astype(BF16)
    lo = (r1 - mid.astype(F32)).astype(BF16)
    return hi, mid, lo


def _dot_x3(x, w_bf16):
    hi, mid, lo = _split3(x)
    return _dot(hi, w_bf16) + _dot(mid, w_bf16) + _dot(lo, w_bf16)


def _x3_dot(w_bf16, x):
    hi, mid, lo = _split3(x)
    return _dot(w_bf16, hi) + _dot(w_bf16, mid) + _dot(w_bf16, lo)


def _sigmoid(x):
    return 1.0 / (1.0 + jnp.exp(-x))


def _rope128(t, ra, rm, rp):
    return t * ra + pltpu.roll(t, LANE - ROPE_HALF, 1) * rm + pltpu.roll(t, ROPE_HALF, 1) * rp


def _in_proj_kernel(x_ref, g_ref, wt_ref, wf_ref, ra_ref, rm_ref, rp_ref, cos_ref, sin_ref, oh_ref,
                    kc_ref, vc_ref, ks_ref, kw_ref, rw_ref, gr_ref,
                    qt_ref, qrt_ref, vst_ref, vwt_ref, gate_ref, gn_ref):
    x = x_ref[...]
    ms = jnp.mean(x * x, axis=-1, keepdims=True)
    y = (x * lax.rsqrt(ms + RMS_EPS) * g_ref[...]).astype(BF16)

    kv = _dot(y, wt_ref[:, T_KV:T_RW])
    ra, rm, rp = ra_ref[...], rm_ref[...], rp_ref[...]
    for i, (ref, rot) in enumerate(((kc_ref, False), (vc_ref, False), (ks_ref, True), (kw_ref, True))):
        t = kv[:, i * LANE:(i + 1) * LANE]
        if rot:
            t = _rope128(t, ra, rm, rp)
        for g in range(NSA_KV):
            tg = t[:, g * HEAD_DIM:(g + 1) * HEAD_DIM].astype(BF16)
            if ref is ks_ref:
                ref[g] = jnp.concatenate([tg, oh_ref[...]], axis=1)
            else:
                ref[g] = tg
    rw_ref[...] = _dot(y, wt_ref[:, T_RW:T_GR])
    gr = _dot(y, wt_ref[:, T_GR:T_END])
    gr_ref[...] = gr * _sigmoid(gr)

    def proj_t(r0, r1):
        return _dot_nt(wf_ref[r0:r1, :], y)

    qt = proj_t(R_Q, R_VS) * (SCALE * LOG2E)
    cos, sin = cos_ref[...], sin_ref[...]
    qt_ref[...] = qt.astype(BF16)
    for h in range(NSA_HEADS):
        r0 = h * HEAD_DIM
        t1 = qt[r0:r0 + ROPE_HALF]
        t2 = qt[r0 + ROPE_HALF:r0 + ROPE_DIM]
        qrt_ref[r0:r0 + ROPE_DIM, :] = jnp.concatenate(
            [t1 * cos - t2 * sin, t2 * cos + t1 * sin], axis=0).astype(BF16)
        qrt_ref[r0 + ROPE_DIM:r0 + HEAD_DIM, :] = qt[r0 + ROPE_DIM:r0 + HEAD_DIM].astype(BF16)
    vt = proj_t(R_VS, R_GL).astype(BF16)
    ones = jnp.ones((V_ROWS - HEAD_DIM, LANE), BF16)
    for j in range(vt.shape[1] // LANE):
        for i, ref in enumerate((vst_ref, vwt_ref)):
            for g in range(NSA_KV):
                r0 = i * KV_WIDTH + g * HEAD_DIM
                ref[j, g * V_ROWS:(g + 1) * V_ROWS, :] = jnp.concatenate(
                    [vt[r0:r0 + HEAD_DIM, j * LANE:(j + 1) * LANE], ones], axis=0)
    gate_ref[...] = _sigmoid(proj_t(R_GL, R_GN))
    gn = proj_t(R_GN, R_END)
    gn_ref[...] = gn * _sigmoid(gn)


def _in_proj(x2, norm_g, w_t, w_f, tabs, *, seq, tm):
    n = x2.shape[0]
    spt = seq // tm
    ra, rm, rp, cos, sin, onehot = tabs
    hm = lambda w: jax.ShapeDtypeStruct((NSA_KV, n, w), BF16)
    hspec = lambda w: pl.BlockSpec((NSA_KV, tm, w), lambda i: (0, i, 0))
    row = lambda w: pl.BlockSpec((tm, w), lambda i: (i, 0))
    col = lambda r: pl.BlockSpec((r, tm), lambda i: (0, i))
    full = lambda a: pl.BlockSpec(a.shape, lambda i: (0,) * a.ndim)
    tab = lambda w: pl.BlockSpec((tm, w), lambda i: (i % spt, 0))
    tabt = pl.BlockSpec((ROPE_HALF, tm), lambda i: (0, i % spt))
    vtile = pl.BlockSpec((tm // LANE, NSA_KV * V_ROWS, LANE), lambda i: (i, 0, 0))
    vsd = jax.ShapeDtypeStruct((n // LANE, NSA_KV * V_ROWS, LANE), BF16)
    kw = (HEAD_DIM, HEAD_DIM, K_AUG, HEAD_DIM)
    return pl.pallas_call(
        _in_proj_kernel,
        grid=(n // tm,),
        in_specs=[row(D_MODEL), full(norm_g), full(w_t), full(w_f), tab(LANE), tab(LANE), tab(LANE),
                  tabt, tabt, tab(K_AUG - HEAD_DIM)],
        out_specs=[hspec(w) for w in kw] + [row(RWKV_SHIFT_WIDTH), row(RWKV_WIDTH),
                                            col(NSA_WIDTH), col(NSA_WIDTH), vtile, vtile,
                                            col(NSA_KV * GATE_ROWS), col(NSA_WIDTH)],
        out_shape=[hm(w) for w in kw] + [jax.ShapeDtypeStruct((n, RWKV_SHIFT_WIDTH), F32),
                                         jax.ShapeDtypeStruct((n, RWKV_WIDTH), F32),
                                         jax.ShapeDtypeStruct((NSA_WIDTH, n), BF16),
                                         jax.ShapeDtypeStruct((NSA_WIDTH, n), BF16),
                                         vsd, vsd,
                                         jax.ShapeDtypeStruct((NSA_KV * GATE_ROWS, n), F32),
                                         jax.ShapeDtypeStruct((NSA_WIDTH, n), F32)],
        compiler_params=pltpu.CompilerParams(dimension_semantics=("arbitrary",),
                                             vmem_limit_bytes=VMEM_LIMIT),
        name="in_proj",
    )(x2, norm_g, w_t, w_f, ra, rm, rp, cos, sin, onehot)


def _compress_kernel(kc_ref, vc_ref, pk_ref, w1k_ref, w2k_ref, pv_ref, w1v_ref, w2vt_ref,
                     ko_ref, vo_ref):
    half = CMP_STRIDE * HEAD_DIM

    def hidden(c_ref, pos_ref, w1_ref):
        c = c_ref[0, 0]
        za = _dot(c, w1_ref[0:half, :])
        zb = _dot(c, w1_ref[half:2 * half, :])
        pos = jnp.broadcast_to(pos_ref[...], (SUBLANE, 2 * half)).astype(BF16)
        pv = _dot(pos, w1_ref[...])[0:1, :]
        hid = za + pltpu.roll(zb, c.shape[0] - 1, 0) + pv
        return (hid * _sigmoid(hid)).astype(BF16)

    ko_ref[0, 0] = _dot(hidden(kc_ref, pk_ref, w1k_ref), w2k_ref[...]).astype(BF16)
    vo_ref[0, 0] = _dot_nt(w2vt_ref[...], hidden(vc_ref, pv_ref, w1v_ref)).astype(BF16)


def _compress(kc_r, vc_r, pk, w1k, w2k, pv, w1v, w2vt):
    g, b, nch, width = kc_r.shape
    blk = pl.BlockSpec((1, 1, nch, width), lambda i, j: (i, j, 0, 0))
    full = lambda a: pl.BlockSpec(a.shape, lambda i, j: (0,) * a.ndim)
    return pl.pallas_call(
        _compress_kernel,
        grid=(g, b),
        in_specs=[blk, blk, full(pk), full(w1k), full(w2k), full(pv), full(w1v), full(w2vt)],
        out_specs=[pl.BlockSpec((1, 1, nch, HEAD_DIM), lambda i, j: (i, j, 0, 0)),
                   pl.BlockSpec((1, 1, HEAD_DIM, nch), lambda i, j: (i, j, 0, 0))],
        out_shape=[jax.ShapeDtypeStruct((g, b, nch, HEAD_DIM), BF16),
                   jax.ShapeDtypeStruct((g, b, HEAD_DIM, nch), BF16)],
        compiler_params=pltpu.CompilerParams(dimension_semantics=("arbitrary", "arbitrary"),
                                             vmem_limit_bytes=VMEM_LIMIT),
        name="compress",
    )(kc_r, vc_r, pk, w1k, w2k, pv, w1v, w2vt)


def _nsa_kernel(qt_ref, qrt_ref, kc_ref, vct_ref, ks_ref, vst_ref, kw_ref, vwt_ref,
                gate_ref, gn_ref, mt_ref, o_ref, *, tq, seq):
    tk = KEY_TILE
    qi = pl.program_id(2)
    q0 = qi * tq
    nl = NSA_REP * tq
    n_sel = seq // SEL_BLOCK
    ncp = kc_ref.shape[2]

    def heads_on_lanes(ref):
        return jnp.concatenate([ref[r * HEAD_DIM:(r + 1) * HEAD_DIM, :] for r in range(NSA_REP)], axis=1)

    def tile4(a):
        return jnp.concatenate([a] * NSA_REP, axis=1)

    s = _dot(kc_ref[0, 0], heads_on_lanes(qt_ref))
    t_c = q0 + lax.broadcasted_iota(jnp.int32, (ncp, tq), 1)
    c_c = lax.broadcasted_iota(jnp.int32, (ncp, tq), 0)
    cmask = tile4((c_c * CMP_STRIDE + (CMP_BLOCK - 1)) <= t_c)
    s = jnp.where(cmask, s, NEG_INF)
    m = jnp.max(s, axis=0, keepdims=True)
    e = jnp.where(cmask, jnp.exp2(s - m), 0.0)
    den = jnp.sum(e, axis=0, keepdims=True)
    p = e * (1.0 / jnp.where(den > 0.0, den, 1.0))
    o_cmp = _dot(vct_ref[0, 0], p.astype(BF16))

    psum = p[:, 0:tq]
    for r in range(1, NSA_REP):
        psum = psum + p[:, r * tq:(r + 1) * tq]
    imp = _x3_dot(mt_ref[...], psum)
    j = lax.broadcasted_iota(jnp.int32, (n_sel, tq), 0)
    t = q0 + lax.broadcasted_iota(jnp.int32, (n_sel, tq), 1)
    tb = t // SEL_BLOCK
    forced = (j == 0) | (j == tb) | (j == tb - 1)
    val = jnp.where(j <= tb, imp + jnp.where(forced, FORCE_BONUS, 0.0), -1.0)
    cnt = jnp.zeros((n_sel, tq), F32)
    for i in range(n_sel):
        vi = val[i:i + 1, :]
        ge = jnp.where(vi >= val, 1.0, 0.0)
        gt = jnp.where(vi > val, 1.0, 0.0)
        cnt = cnt + jnp.where(j > i, ge, gt)
    sel_bias = jnp.where(cnt < float(SEL_TOPK), 0.0, NEG_INF).astype(BF16)

    qr = heads_on_lanes(qrt_ref)
    qr_sel = jnp.concatenate([qr, tile4(sel_bias), jnp.zeros((K_AUG - HEAD_DIM - n_sel, nl), BF16)], axis=0)
    k_s = lax.broadcasted_iota(jnp.int32, (tk, tq), 0)
    t_l = q0 + lax.broadcasted_iota(jnp.int32, (tk, tq), 1)

    def update(carry, k_ref, vt_ref, q_op, kt, keep):
        m_i, acc = carry
        kb = k_ref[0, pl.ds(pl.multiple_of(kt * tk, tk), tk), :]
        vtb = jnp.concatenate([vt_ref[kt * (tk // LANE) + jj] for jj in range(tk // LANE)], axis=1)
        sc = _dot(kb, q_op)
        if keep is not None:
            sc = jnp.concatenate([jnp.where(keep, sc[:, r * tq:(r + 1) * tq], NEG_INF)
                                  for r in range(NSA_REP)], axis=1)
        m_n = jnp.maximum(m_i, jnp.max(sc, axis=0, keepdims=True))
        pe = jnp.exp2(sc - m_n).astype(BF16)
        return m_n, jnp.exp2(m_i - m_n) * acc + _dot(vtb, pe)

    def sel_only(kt, c):
        return update(c, ks_ref, vst_ref, qr_sel, kt, None)

    def both(kt, carry, sel_keep, win_keep):
        d = t_l - (kt * tk + k_s)
        return (update(carry[0], ks_ref, vst_ref, qr_sel, kt, None if sel_keep is None else sel_keep(d)),
                update(carry[1], kw_ref, vwt_ref, qr, kt, None if win_keep is None else win_keep(d)))

    init = (jnp.full((1, nl), NEG_INF, F32), jnp.zeros((V_ROWS, nl), F32))
    a = (q0 + tq - 1) // tk
    lo = jnp.maximum(a - WINDOW // tk, 0)
    c_sel = lax.fori_loop(0, lo, sel_only, init)
    cs = lax.fori_loop(lo, lo + (a >= 2).astype(jnp.int32),
                       lambda kt, c: both(kt, c, None, lambda d: d < WINDOW), (c_sel, init))
    cs = lax.fori_loop(jnp.maximum(a - 1, 0), a, lambda kt, c: both(kt, c, None, None), cs)
    c_sel, c_win = both(a, cs, lambda d: d >= 0, lambda d: d >= 0)
    o_sel = c_sel[1][0:HEAD_DIM] * (1.0 / c_sel[1][HEAD_DIM:HEAD_DIM + 1])
    o_win = c_win[1][0:HEAD_DIM] * (1.0 / c_win[1][HEAD_DIM:HEAD_DIM + 1])

    gates = gate_ref[...]
    for r in range(NSA_REP):
        ls = slice(r * tq, (r + 1) * tq)
        o = (gates[3 * r:3 * r + 1, :] * o_cmp[:, ls] + gates[3 * r + 1:3 * r + 2, :] * o_sel[:, ls]
             + gates[3 * r + 2:3 * r + 3, :] * o_win[:, ls])
        rs = slice(r * HEAD_DIM, (r + 1) * HEAD_DIM)
        o_ref[rs, :] = (o * gn_ref[rs, :]).astype(BF16)


def _nsa(qt, qrt, kcmp, vcmpt, ksh, vst, kwh, vwt, gates, gnt, mt, *, batch, seq, tq):
    n = batch * seq
    nq = seq // tq
    ncp = kcmp.shape[2]
    grp = NSA_REP * HEAD_DIM
    qspec = pl.BlockSpec((grp, tq), lambda b, g, i: (g, b * nq + i))
    kspec = lambda w: pl.BlockSpec((1, seq, w), lambda b, g, i: (g, b, 0))
    vspec = pl.BlockSpec((seq // LANE, V_ROWS, LANE), lambda b, g, i: (b, g, 0))
    kern = functools.partial(_nsa_kernel, tq=tq, seq=seq)
    return pl.pallas_call(
        kern,
        grid=(batch, NSA_KV, nq),
        in_specs=[qspec, qspec,
                  pl.BlockSpec((1, 1, ncp, HEAD_DIM), lambda b, g, i: (g, b, 0, 0)),
                  pl.BlockSpec((1, 1, HEAD_DIM, ncp), lambda b, g, i: (g, b, 0, 0)),
                  kspec(K_AUG), vspec, kspec(HEAD_DIM), vspec,
                  pl.BlockSpec((GATE_ROWS, tq), lambda b, g, i: (g, b * nq + i)),
                  qspec,
                  pl.BlockSpec(mt.shape, lambda b, g, i: (0, 0))],
        out_specs=qspec,
        out_shape=jax.ShapeDtypeStruct((NSA_WIDTH, n), BF16),
        compiler_params=pltpu.CompilerParams(
            dimension_semantics=("arbitrary", "arbitrary", "arbitrary"),
            vmem_limit_bytes=VMEM_LIMIT),
        name="nsa",
    )(qt, qrt, kcmp, vcmpt, ksh, vst, kwh, vwt, gates, gnt, mt)


def _rwkv_kernel(p_ref, gr_ref, mu_ref, w0_ref, a0_ref, wab_ref, kk_ref, ka_ref, rk_ref,
                 gw_ref, gb_ref, ones_ref, tril_ref, o_ref, st_ref, carry_ref, *, tt):
    c = RWKV_CHUNK
    hd = HEAD_DIM
    nh = RWKV_HEADS
    nc = tt // c
    step = pl.program_id(1)

    @pl.when(step == 0)
    def _():
        st_ref[...] = jnp.zeros_like(st_ref)
        carry_ref[...] = jnp.zeros_like(carry_ref)

    p = p_ref[...]
    row = lax.broadcasted_iota(jnp.int32, p.shape, 0)
    prev = jnp.where(row == 0, carry_ref[...], pltpu.roll(p, 1, 0))
    carry_ref[...] = p[tt - 1:tt, :]
    ps = p + mu_ref[...] * (prev - p)
    r = ps[:, 0:RWKV_WIDTH]
    k = ps[:, RWKV_WIDTH:2 * RWKV_WIDTH]
    v = ps[:, 2 * RWKV_WIDTH:3 * RWKV_WIDTH]
    lora = ps[:, 3 * RWKV_WIDTH:]
    lane = lax.broadcasted_iota(jnp.int32, lora.shape, 1)
    feat = jnp.where(lane < DECAY_RANK, jnp.tanh(lora), lora).astype(BF16)
    up = _dot(feat, wab_ref[...])
    w = w0_ref[...] + up[:, 0:RWKV_WIDTH]
    lw = _sigmoid(w) * (-float(np.exp(-0.5)))
    a = _sigmoid(a0_ref[...] + up[:, RWKV_WIDTH:])
    ones_bd = ones_ref[...]
    kk = k * kk_ref[...]
    kkn = kk * lax.rsqrt(jnp.maximum(_dot_x3(kk * kk, ones_bd), 1e-24))
    k2 = k * (1.0 + (a - 1.0) * ka_ref[...])
    alpha = -kkn
    beta = kkn * a
    bonus = _dot_x3(r * k2 * rk_ref[...], ones_bd) * v

    cum = _x3_dot(tril_ref[...], lw)
    cend = jnp.concatenate(
        [jnp.broadcast_to(cum[(ch + 1) * c - 1:(ch + 1) * c, :], (c, RWKV_WIDTH)) for ch in range(nc)], axis=0)
    e_neg = jnp.exp(-cum)
    e_end = jnp.exp(cend - cum)
    at = alpha * jnp.exp(cum - lw)
    bt = beta * e_neg
    kt = k2 * e_neg
    rt = r * jnp.exp(cum)
    bh = beta * e_end
    kh = k2 * e_end
    pc = jnp.exp(cend)

    def items(x, dtype=BF16):
        return jnp.stack([x[ch * c:(ch + 1) * c, h * hd:(h + 1) * hd]
                          for ch in range(nc) for h in range(nh)], axis=0).astype(dtype)

    at_i, rt_i, bt_i, kt_i, v_i = items(at), items(rt, F32), items(bt), items(kt), items(v)
    bh_i, kh_i = items(bh), items(kh)
    ri = lax.broadcasted_iota(jnp.int32, (1, c, c), 1)
    ci = lax.broadcasted_iota(jnp.int32, (1, c, c), 2)
    low_s = ri > ci
    low_i = ri >= ci
    eye = ri == ci

    a4 = _bmm_nt(jnp.concatenate([at_i, rt_i.astype(BF16)], axis=1),
                 jnp.concatenate([bt_i, kt_i], axis=1))
    nab = jnp.where(low_s, a4[:, 0:c, 0:c], 0.0)
    aak = jnp.where(low_s, a4[:, 0:c, c:], 0.0).astype(BF16)
    arbk = jnp.concatenate([jnp.where(low_i, a4[:, c:, 0:c], 0.0),
                            jnp.where(low_i, a4[:, c:, c:], 0.0)], axis=2).astype(BF16)
    tinv = jnp.where(eye, 1.0, 0.0) + nab
    npow = nab
    for _ in range(5):
        nb = npow.astype(BF16)
        npow = _bmm(nb, nb)
        tinv = tinv + _bmm(tinv.astype(BF16), npow.astype(BF16))
    w2 = _bmm(aak, v_i)
    tx = _bmm(tinv.astype(BF16), jnp.concatenate([w2.astype(BF16), at_i], axis=2))
    u0, ta = tx[:, :, 0:hd], tx[:, :, hd:]
    rhs = jnp.concatenate([jnp.concatenate([ta, u0], axis=2).astype(BF16),
                           jnp.concatenate([jnp.zeros_like(v_i), v_i], axis=2)], axis=1)
    ax = _bmm(arbk, rhs)
    bkt = jnp.concatenate([jnp.swapaxes(items(bh, F32), 1, 2), jnp.swapaxes(items(kh, F32), 1, 2)],
                          axis=2).astype(BF16)
    gh = _bmm(bkt, rhs)
    rq = rt_i + ax[:, :, 0:hd]
    y0 = ax[:, :, hd:]
    pc_i = items(pc, F32)
    g_m = jnp.where(eye, pc_i, 0.0) + gh[:, :, 0:hd]
    h_m = gh[:, :, hd:]
    lhs = jnp.concatenate([rq, g_m], axis=1).astype(BF16)

    st = st_ref[...]
    ys = []
    for ch in range(nc):
        sl = slice(ch * nh, (ch + 1) * nh)
        res = _bmm(lhs[sl], st.astype(BF16))
        yc = res[:, 0:c, :] + y0[sl]
        st = res[:, c:, :] + h_m[sl]
        ys.append(jnp.concatenate([yc[h] for h in range(nh)], axis=1))
    st_ref[...] = st
    y = jnp.concatenate(ys, axis=0) if nc > 1 else ys[0]

    inv_hd = 1.0 / hd
    mean = _dot_x3(y, ones_bd) * inv_hd
    ycen = y - mean
    var = _dot_x3(ycen * ycen, ones_bd) * inv_hd
    yn = ycen * lax.rsqrt(var + GN_EPS) * gw_ref[...] + gb_ref[...]
    o_ref[...] = ((yn + bonus) * gr_ref[...]).astype(BF16)


def _rwkv(rw, gr, mu, w0, a0, wab, kk, ka, rk, gw, gb, ones_bd, tril, *, batch, seq, tt):
    n = batch * seq
    ns = seq // tt
    row = lambda w: pl.BlockSpec((tt, w), lambda b, i: (b * ns + i, 0))
    full = lambda a: pl.BlockSpec(a.shape, lambda b, i: (0,) * a.ndim)
    kern = functools.partial(_rwkv_kernel, tt=tt)
    consts = (mu, w0, a0, wab, kk, ka, rk, gw, gb, ones_bd, tril)
    return pl.pallas_call(
        kern,
        grid=(batch, ns),
        in_specs=[row(RWKV_SHIFT_WIDTH), row(RWKV_WIDTH)] + [full(a) for a in consts],
        out_specs=row(RWKV_WIDTH),
        out_shape=jax.ShapeDtypeStruct((n, RWKV_WIDTH), BF16),
        scratch_shapes=[pltpu.VMEM((RWKV_HEADS, HEAD_DIM, HEAD_DIM), F32),
                        pltpu.VMEM((1, RWKV_SHIFT_WIDTH), F32)],
        compiler_params=pltpu.CompilerParams(dimension_semantics=("arbitrary", "arbitrary"),
                                             vmem_limit_bytes=VMEM_LIMIT),
        name="rwkv",
    )(rw, gr, *consts)


def _out_kernel(x_ref, ont_ref, or_ref, wn_ref, wr_ref, g_ref, o_ref):
    h = x_ref[...] + _dot_tn(ont_ref[...], wn_ref[...]) + _dot(or_ref[...], wr_ref[...])
    ms = jnp.mean(h * h, axis=-1, keepdims=True)
    o_ref[...] = h * lax.rsqrt(ms + RMS_EPS) * g_ref[...]


def _out_proj(x2, o_nsa_t, o_rwkv, wn, wr, final_g, *, tm):
    n = x2.shape[0]
    row = lambda w: pl.BlockSpec((tm, w), lambda i: (i, 0))
    full = lambda a: pl.BlockSpec(a.shape, lambda i: (0,) * a.ndim)
    return pl.pallas_call(
        _out_kernel,
        grid=(n // tm,),
        in_specs=[row(D_MODEL), pl.BlockSpec((NSA_WIDTH, tm), lambda i: (0, i)), row(RWKV_WIDTH),
                  full(wn), full(wr), full(final_g)],
        out_specs=row(D_MODEL),
        out_shape=jax.ShapeDtypeStruct((n, D_MODEL), F32),
        compiler_params=pltpu.CompilerParams(dimension_semantics=("arbitrary",),
                                             vmem_limit_bytes=VMEM_LIMIT),
        name="out_proj",
    )(x2, o_nsa_t, o_rwkv, wn, wr, final_g)


def _rope_tables(seq):
    inv = ROPE_THETA ** (-np.arange(ROPE_HALF, dtype=np.float64) / ROPE_HALF)
    ang = np.arange(seq, dtype=np.float64)[:, None] * inv[None, :]
    cos, sin = np.cos(ang), np.sin(ang)
    ra = np.ones((seq, HEAD_DIM)); rm = np.zeros((seq, HEAD_DIM)); rp = np.zeros((seq, HEAD_DIM))
    ra[:, :ROPE_HALF] = cos; ra[:, ROPE_HALF:ROPE_DIM] = cos
    rm[:, :ROPE_HALF] = -sin
    rp[:, ROPE_HALF:ROPE_DIM] = sin
    rep = lambda t: jnp.asarray(np.tile(t, (1, LANE // HEAD_DIM)), F32)
    assert seq // SEL_BLOCK <= K_AUG - HEAD_DIM
    onehot = np.zeros((seq, K_AUG - HEAD_DIM))
    onehot[np.arange(seq), np.arange(seq) // SEL_BLOCK] = 1.0
    return (rep(ra), rep(rm), rep(rp), jnp.asarray(cos.T, F32), jnp.asarray(sin.T, F32),
            jnp.asarray(onehot, BF16))


def _cmp_to_sel_t(n_cmp_pad, n_sel):
    n_cmp = n_cmp_pad - 1
    c0 = np.arange(n_cmp)[:, None] * CMP_STRIDE
    s0 = np.arange(n_sel)[None, :] * SEL_BLOCK
    ov = np.clip(np.minimum(c0 + CMP_BLOCK, s0 + SEL_BLOCK) - np.maximum(c0, s0), 0, None) / CMP_BLOCK
    mt = np.zeros((n_sel, n_cmp_pad))
    mt[:, :n_cmp] = ov.T
    return jnp.asarray(mt, BF16)


def _prep_w_in(w_in):
    idx = np.cumsum(IN_SIZES)[:-1].tolist()
    q, kc, vc, ks, vs, kw, vw, gl, gn, rw, gr = jnp.split(w_in, idx, axis=1)
    w_t = jnp.concatenate([kc, vc, ks, kw, rw, gr], axis=1).astype(BF16)
    per_g = 3 * NSA_REP
    pad = jnp.zeros((D_MODEL, GATE_ROWS - per_g), w_in.dtype)
    gl_p = [t for g in range(NSA_KV) for t in (gl[:, g * per_g:(g + 1) * per_g], pad)]
    w_f = jnp.concatenate([q, vs, vw] + gl_p + [gn], axis=1).T.astype(BF16)
    return w_t, w_f


def _layer(x2, norm_g, w_in, cmp_pos_k, cmp_w1_k, cmp_w2_k, cmp_pos_v, cmp_w1_v, cmp_w2_v,
           shift_mu, decay_w0, decay_up, iclr_a0, iclr_up, k_k, k_a, r_k, gn_w, gn_b, w_out,
           final_g, *, batch, seq):
    tm = 256
    tq = 256
    tt = 256
    assert WINDOW == 2 * KEY_TILE and seq % KEY_TILE == 0 and KEY_TILE % tq == 0
    nch = seq // CMP_STRIDE
    n_sel = seq // SEL_BLOCK
    rowv = lambda t: t.reshape(1, -1).astype(F32)

    w_t, w_f = _prep_w_in(w_in)
    (kch, vch, ksh, kwh, rw, gr, qt, qrt, vst, vwt, gates, gnt) = _in_proj(
        x2, rowv(norm_g), w_t, w_f, _rope_tables(seq), seq=seq, tm=tm)

    chunks = lambda t: t.reshape(NSA_KV, batch, nch, CMP_STRIDE * HEAD_DIM)
    kcmp, vcmpt = _compress(chunks(kch), chunks(vch),
                            rowv(cmp_pos_k), cmp_w1_k.astype(BF16), cmp_w2_k.astype(BF16),
                            rowv(cmp_pos_v), cmp_w1_v.astype(BF16), cmp_w2_v.T.astype(BF16))

    o_nsa_t = _nsa(qt, qrt, kcmp, vcmpt, ksh, vst, kwh, vwt, gates, gnt,
                   _cmp_to_sel_t(nch, n_sel), batch=batch, seq=seq, tq=tq)

    z = jnp.zeros((DECAY_RANK, RWKV_WIDTH), F32)
    wab = jnp.concatenate([jnp.concatenate([decay_up, z], axis=1),
                           jnp.concatenate([z, iclr_up], axis=1)], axis=0).astype(BF16)
    hid = np.arange(RWKV_WIDTH) // HEAD_DIM
    ones_bd = jnp.asarray(hid[:, None] == hid[None, :], BF16)
    ti = np.arange(tt)
    tril = jnp.asarray((ti[:, None] >= ti[None, :]) & (ti[:, None] // RWKV_CHUNK == ti[None, :] // RWKV_CHUNK), BF16)
    o_rwkv = _rwkv(rw, gr, rowv(shift_mu), rowv(decay_w0), rowv(iclr_a0), wab, rowv(k_k), rowv(k_a),
                   rowv(r_k), rowv(gn_w), rowv(gn_b), ones_bd, tril, batch=batch, seq=seq, tt=tt)

    w_o = w_out.astype(BF16)
    return _out_proj(x2, o_nsa_t, o_rwkv, w_o[:NSA_WIDTH], w_o[NSA_WIDTH:], rowv(final_g), tm=tm)


def kernel(x, norm_g, w_in, cmp_pos_k, cmp_w1_k, cmp_w2_k, cmp_pos_v, cmp_w1_v, cmp_w2_v, shift_mu, decay_w0, decay_up, iclr_a0, iclr_up, k_k, k_a, r_k, gn_w, gn_b, w_out, final_g):
    batch, seq, d = x.shape
    assert d == D_MODEL and norm_g.shape[0] == 1, "single-layer trunk"
    out = _layer(x.reshape(batch * seq, d), norm_g[0], w_in[0], cmp_pos_k[0], cmp_w1_k[0], cmp_w2_k[0],
                 cmp_pos_v[0], cmp_w1_v[0], cmp_w2_v[0], shift_mu[0], decay_w0[0], decay_up[0],
                 iclr_a0[0], iclr_up[0], k_k[0], k_a[0], r_k[0], gn_w[0], gn_b[0], w_out[0],
                 final_g, batch=batch, seq=seq)
    return out.reshape(batch, seq, d)
```

```python
import functools

import numpy as np
import jax
import jax.numpy as jnp
from jax import lax
from jax.experimental import pallas as pl
from jax.experimental.pallas import tpu as pltpu

F32 = jnp.float32
BF16 = jnp.bfloat16

D_MODEL = 1024
HEAD_DIM = 64
NSA_HEADS = 8
NSA_KV = 2
NSA_REP = NSA_HEADS // NSA_KV
RWKV_HEADS = 8
NSA_WIDTH = NSA_HEADS * HEAD_DIM
RWKV_WIDTH = RWKV_HEADS * HEAD_DIM
KV_WIDTH = NSA_KV * HEAD_DIM
ROPE_DIM = HEAD_DIM // 4
ROPE_HALF = ROPE_DIM // 2
ROPE_THETA = 500000.0
CMP_BLOCK = 32
CMP_STRIDE = 16
CMP_HIDDEN = 256
SEL_BLOCK = 64
SEL_TOPK = 8
WINDOW = 512
DECAY_RANK = 64
ICLR_RANK = 64
RWKV_SHIFT_WIDTH = 3 * RWKV_WIDTH + DECAY_RANK + ICLR_RANK
IN_SIZES = (NSA_WIDTH, KV_WIDTH, KV_WIDTH, KV_WIDTH, KV_WIDTH, KV_WIDTH, KV_WIDTH,
            3 * NSA_HEADS, NSA_WIDTH, RWKV_SHIFT_WIDTH, RWKV_WIDTH)
SCALE = HEAD_DIM ** -0.5
RMS_EPS = 1e-6
GN_EPS = 64e-5
NEG_INF = -1e30
FORCE_BONUS = 1e3

LANE = 128
SUBLANE = 8
GATE_ROWS = 16

T_KV = 0
T_RW = T_KV + 4 * KV_WIDTH
T_GR = T_RW + RWKV_SHIFT_WIDTH
T_END = T_GR + RWKV_WIDTH
R_Q = 0
R_VS = R_Q + NSA_WIDTH
R_VW = R_VS + KV_WIDTH
R_GL = R_VW + KV_WIDTH
R_GN = R_GL + NSA_KV * GATE_ROWS
R_END = R_GN + NSA_WIDTH

LOG2E = float(np.log2(np.e))
K_AUG = LANE
V_ROWS = HEAD_DIM + 16
KEY_TILE = 256
RWKV_CHUNK = 64
VMEM_LIMIT = 48 * 1024 * 1024


def _dot(a, b):
    return jnp.dot(a, b, preferred_element_type=F32)


def _dot_nt(a, b):
    return lax.dot_general(a, b, (((1,), (1,)), ((), ())), preferred_element_type=F32)


def _dot_tn(a, b):
    return lax.dot_general(a, b, (((0,), (0,)), ((), ())), preferred_element_type=F32)


def _bmm(a, b):
    return lax.dot_general(a, b, (((2,), (1,)), ((0,), (0,))), preferred_element_type=F32)


def _bmm_nt(a, b):
    return lax.dot_general(a, b, (((2,), (2,)), ((0,), (0,))), preferred_element_type=F32)


def _split3(x):
    hi = x.astype(BF16)
    r1 = x - hi.astype(F32)
    mid = r1.astype(BF16)
    lo = (r1 - mid.astype(F32)).astype(BF16)
    return hi, mid, lo


def _head_sums(x, ones_blk):
    w = ones_blk.shape[0]
    hi = x.astype(BF16)
    lo = (x - hi.astype(F32)).astype(BF16)
    return jnp.concatenate(
        [_dot(hi[:, i:i + w], ones_blk) + _dot(lo[:, i:i + w], ones_blk) for i in range(0, x.shape[1], w)],
        axis=1)


def _x3_dot(w_bf16, x):
    hi, mid, lo = _split3(x)
    return _dot(w_bf16, hi) + _dot(w_bf16, mid) + _dot(w_bf16, lo)


def _sigmoid(x):
    return 1.0 / (1.0 + jnp.exp(-x))


def _rope128(t, ra, rm, rp):
    return t * ra + pltpu.roll(t, LANE - ROPE_HALF, 1) * rm + pltpu.roll(t, ROPE_HALF, 1) * rp


def _in_proj_kernel(x_ref, g_ref, wt_ref, wf_ref, ra_ref, rm_ref, rp_ref, cos_ref, sin_ref, oh_ref,
                    kc_ref, vc_ref, ks_ref, kw_ref, rw_ref, gr_ref,
                    qt_ref, qrt_ref, vst_ref, vwt_ref, gate_ref, gn_ref):
    x = x_ref[...]
    ms = jnp.mean(x * x, axis=-1, keepdims=True)
    y = (x * lax.rsqrt(ms + RMS_EPS) * g_ref[...]).astype(BF16)

    kv = _dot(y, wt_ref[:, T_KV:T_RW])
    ra, rm, rp = ra_ref[...], rm_ref[...], rp_ref[...]
    for i, (ref, rot) in enumerate(((kc_ref, False), (vc_ref, False), (ks_ref, True), (kw_ref, True))):
        t = kv[:, i * LANE:(i + 1) * LANE]
        if rot:
            t = _rope128(t, ra, rm, rp)
        for g in range(NSA_KV):
            tg = t[:, g * HEAD_DIM:(g + 1) * HEAD_DIM].astype(BF16)
            if ref is ks_ref:
                ref[g] = jnp.concatenate([tg, oh_ref[...]], axis=1)
            else:
                ref[g] = tg
    rw_ref[...] = _dot(y, wt_ref[:, T_RW:T_GR])
    gr = _dot(y, wt_ref[:, T_GR:T_END])
    gr_ref[...] = gr * _sigmoid(gr)

    def proj_t(r0, r1):
        return _dot_nt(wf_ref[r0:r1, :], y)

    qt = proj_t(R_Q, R_VS) * (SCALE * LOG2E)
    cos, sin = cos_ref[...], sin_ref[...]
    qt_ref[...] = qt.astype(BF16)
    for h in range(NSA_HEADS):
        r0 = h * HEAD_DIM
        t1 = qt[r0:r0 + ROPE_HALF]
        t2 = qt[r0 + ROPE_HALF:r0 + ROPE_DIM]
        qrt_ref[r0:r0 + ROPE_DIM, :] = jnp.concatenate(
            [t1 * cos - t2 * sin, t2 * cos + t1 * sin], axis=0).astype(BF16)
        qrt_ref[r0 + ROPE_DIM:r0 + HEAD_DIM, :] = qt[r0 + ROPE_DIM:r0 + HEAD_DIM].astype(BF16)
    vt = proj_t(R_VS, R_GL).astype(BF16)
    ones = jnp.ones((V_ROWS - HEAD_DIM, LANE), BF16)
    for j in range(vt.shape[1] // LANE):
        for i, ref in enumerate((vst_ref, vwt_ref)):
            for g in range(NSA_KV):
                r0 = i * KV_WIDTH + g * HEAD_DIM
                ref[j, g * V_ROWS:(g + 1) * V_ROWS, :] = jnp.concatenate(
                    [vt[r0:r0 + HEAD_DIM, j * LANE:(j + 1) * LANE], ones], axis=0)
    gate_ref[...] = _sigmoid(proj_t(R_GL, R_GN))
    gn = proj_t(R_GN, R_END)
    gn_ref[...] = gn * _sigmoid(gn)


def _in_proj(x2, norm_g, w_t, w_f, tabs, *, seq, tm):
    n = x2.shape[0]
    spt = seq // tm
    ra, rm, rp, cos, sin, onehot = tabs
    hm = lambda w: jax.ShapeDtypeStruct((NSA_KV, n, w), BF16)
    hspec = lambda w: pl.BlockSpec((NSA_KV, tm, w), lambda i: (0, i, 0))
    row = lambda w: pl.BlockSpec((tm, w), lambda i: (i, 0))
    col = lambda r: pl.BlockSpec((r, tm), lambda i: (0, i))
    full = lambda a: pl.BlockSpec(a.shape, lambda i: (0,) * a.ndim)
    tab = lambda w: pl.BlockSpec((tm, w), lambda i: (i % spt, 0))
    tabt = pl.BlockSpec((ROPE_HALF, tm), lambda i: (0, i % spt))
    vtile = pl.BlockSpec((tm // LANE, NSA_KV * V_ROWS, LANE), lambda i: (i, 0, 0))
    vsd = jax.ShapeDtypeStruct((n // LANE, NSA_KV * V_ROWS, LANE), BF16)
    kw = (HEAD_DIM, HEAD_DIM, K_AUG, HEAD_DIM)
    return pl.pallas_call(
        _in_proj_kernel,
        grid=(n // tm,),
        in_specs=[row(D_MODEL), full(norm_g), full(w_t), full(w_f), tab(LANE), tab(LANE), tab(LANE),
                  tabt, tabt, tab(K_AUG - HEAD_DIM)],
        out_specs=[hspec(w) for w in kw] + [row(RWKV_SHIFT_WIDTH), row(RWKV_WIDTH),
                                            col(NSA_WIDTH), col(NSA_WIDTH), vtile, vtile,
                                            col(NSA_KV * GATE_ROWS), col(NSA_WIDTH)],
        out_shape=[hm(w) for w in kw] + [jax.ShapeDtypeStruct((n, RWKV_SHIFT_WIDTH), F32),
                                         jax.ShapeDtypeStruct((n, RWKV_WIDTH), F32),
                                         jax.ShapeDtypeStruct((NSA_WIDTH, n), BF16),
                                         jax.ShapeDtypeStruct((NSA_WIDTH, n), BF16),
                                         vsd, vsd,
                                         jax.ShapeDtypeStruct((NSA_KV * GATE_ROWS, n), F32),
                                         jax.ShapeDtypeStruct((NSA_WIDTH, n), F32)],
        compiler_params=pltpu.CompilerParams(dimension_semantics=("arbitrary",),
                                             vmem_limit_bytes=VMEM_LIMIT),
        name="in_proj",
    )(x2, norm_g, w_t, w_f, ra, rm, rp, cos, sin, onehot)


def _compress_kernel(kc_ref, vc_ref, pk_ref, w1k_ref, w2k_ref, pv_ref, w1v_ref, w2vt_ref,
                     ko_ref, vo_ref):
    half = CMP_STRIDE * HEAD_DIM

    def hidden(c_ref, pos_ref, w1_ref):
        c = c_ref[0, 0]
        za = _dot(c, w1_ref[0:half, :])
        zb = _dot(c, w1_ref[half:2 * half, :])
        pos = jnp.broadcast_to(pos_ref[...], (SUBLANE, 2 * half)).astype(BF16)
        pv = _dot(pos, w1_ref[...])[0:1, :]
        hid = za + pltpu.roll(zb, c.shape[0] - 1, 0) + pv
        return (hid * _sigmoid(hid)).astype(BF16)

    ko_ref[0, 0] = _dot(hidden(kc_ref, pk_ref, w1k_ref), w2k_ref[...]).astype(BF16)
    vo_ref[0, 0] = _dot_nt(w2vt_ref[...], hidden(vc_ref, pv_ref, w1v_ref)).astype(BF16)


def _compress(kc_r, vc_r, pk, w1k, w2k, pv, w1v, w2vt):
    g, b, nch, width = kc_r.shape
    blk = pl.BlockSpec((1, 1, nch, width), lambda i, j: (i, j, 0, 0))
    full = lambda a: pl.BlockSpec(a.shape, lambda i, j: (0,) * a.ndim)
    return pl.pallas_call(
        _compress_kernel,
        grid=(g, b),
        in_specs=[blk, blk, full(pk), full(w1k), full(w2k), full(pv), full(w1v), full(w2vt)],
        out_specs=[pl.BlockSpec((1, 1, nch, HEAD_DIM), lambda i, j: (i, j, 0, 0)),
                   pl.BlockSpec((1, 1, HEAD_DIM, nch), lambda i, j: (i, j, 0, 0))],
        out_shape=[jax.ShapeDtypeStruct((g, b, nch, HEAD_DIM), BF16),
                   jax.ShapeDtypeStruct((g, b, HEAD_DIM, nch), BF16)],
        compiler_params=pltpu.CompilerParams(dimension_semantics=("arbitrary", "arbitrary"),
                                             vmem_limit_bytes=VMEM_LIMIT),
        name="compress",
    )(kc_r, vc_r, pk, w1k, w2k, pv, w1v, w2vt)


def _nsa_kernel(qt_ref, qrt_ref, kc_ref, vct_ref, ks_ref, vst_ref, kw_ref, vwt_ref,
                gate_ref, gn_ref, mt_ref, o_ref, *, tq, seq):
    tk = KEY_TILE
    qi = pl.program_id(2)
    q0 = qi * tq
    nl = NSA_REP * tq
    n_sel = seq // SEL_BLOCK
    ncp = kc_ref.shape[2]

    def heads_on_lanes(ref):
        return jnp.concatenate([ref[r * HEAD_DIM:(r + 1) * HEAD_DIM, :] for r in range(NSA_REP)], axis=1)

    def tile4(a):
        return jnp.concatenate([a] * NSA_REP, axis=1)

    s = _dot(kc_ref[0, 0], heads_on_lanes(qt_ref))
    t_c = q0 + lax.broadcasted_iota(jnp.int32, (ncp, tq), 1)
    c_c = lax.broadcasted_iota(jnp.int32, (ncp, tq), 0)
    cmask = tile4((c_c * CMP_STRIDE + (CMP_BLOCK - 1)) <= t_c)
    s = jnp.where(cmask, s, NEG_INF)
    m = jnp.max(s, axis=0, keepdims=True)
    e = jnp.where(cmask, jnp.exp2(s - m), 0.0)
    den = jnp.sum(e, axis=0, keepdims=True)
    p = e * (1.0 / jnp.where(den > 0.0, den, 1.0))
    o_cmp = _dot(vct_ref[0, 0], p.astype(BF16))

    psum = p[:, 0:tq]
    for r in range(1, NSA_REP):
        psum = psum + p[:, r * tq:(r + 1) * tq]
    imp = _x3_dot(mt_ref[...], psum)
    j = lax.broadcasted_iota(jnp.int32, (n_sel, tq), 0)
    t = q0 + lax.broadcasted_iota(jnp.int32, (n_sel, tq), 1)
    tb = t // SEL_BLOCK
    forced = (j == 0) | (j == tb) | (j == tb - 1)
    val = jnp.where(j <= tb, imp + jnp.where(forced, FORCE_BONUS, 0.0), -1.0)
    cnt = jnp.zeros((n_sel, tq), F32)
    for i in range(n_sel):
        vi = val[i:i + 1, :]
        ge = jnp.where(vi >= val, 1.0, 0.0)
        gt = jnp.where(vi > val, 1.0, 0.0)
        cnt = cnt + jnp.where(j > i, ge, gt)
    sel_bias = jnp.where(cnt < float(SEL_TOPK), 0.0, NEG_INF).astype(BF16)

    qr = heads_on_lanes(qrt_ref)
    qr_sel = jnp.concatenate([qr, tile4(sel_bias), jnp.zeros((K_AUG - HEAD_DIM - n_sel, nl), BF16)], axis=0)
    k_s = lax.broadcasted_iota(jnp.int32, (tk, tq), 0)
    t_l = q0 + lax.broadcasted_iota(jnp.int32, (tk, tq), 1)

    def update(carry, k_ref, vt_ref, q_op, kt, keep):
        m_i, acc = carry
        kb = k_ref[0, pl.ds(pl.multiple_of(kt * tk, tk), tk), :]
        vtb = jnp.concatenate([vt_ref[kt * (tk // LANE) + jj] for jj in range(tk // LANE)], axis=1)
        sc = _dot(kb, q_op)
        if keep is not None:
            sc = jnp.concatenate([jnp.where(keep, sc[:, r * tq:(r + 1) * tq], NEG_INF)
                                  for r in range(NSA_REP)], axis=1)
        m_n = jnp.maximum(m_i, jnp.max(sc, axis=0, keepdims=True))
        pe = jnp.exp2(sc - m_n).astype(BF16)
        return m_n, jnp.exp2(m_i - m_n) * acc + _dot(vtb, pe)

    def sel_only(kt, c):
        return update(c, ks_ref, vst_ref, qr_sel, kt, None)

    def both(kt, carry, sel_keep, win_keep):
        d = t_l - (kt * tk + k_s)
        return (update(carry[0], ks_ref, vst_ref, qr_sel, kt, None if sel_keep is None else sel_keep(d)),
                update(carry[1], kw_ref, vwt_ref, qr, kt, None if win_keep is None else win_keep(d)))

    init = (jnp.full((1, nl), NEG_INF, F32), jnp.zeros((V_ROWS, nl), F32))
    a = (q0 + tq - 1) // tk
    lo = jnp.maximum(a - WINDOW // tk, 0)
    c_sel = lax.fori_loop(0, lo, sel_only, init)
    cs = lax.fori_loop(lo, lo + (a >= 2).astype(jnp.int32),
                       lambda kt, c: both(kt, c, None, lambda d: d < WINDOW), (c_sel, init))
    cs = lax.fori_loop(jnp.maximum(a - 1, 0), a, lambda kt, c: both(kt, c, None, None), cs)
    c_sel, c_win = both(a, cs, lambda d: d >= 0, lambda d: d >= 0)
    o_sel = c_sel[1][0:HEAD_DIM] * (1.0 / c_sel[1][HEAD_DIM:HEAD_DIM + 1])
    o_win = c_win[1][0:HEAD_DIM] * (1.0 / c_win[1][HEAD_DIM:HEAD_DIM + 1])

    gates = gate_ref[...]
    for r in range(NSA_REP):
        ls = slice(r * tq, (r + 1) * tq)
        o = (gates[3 * r:3 * r + 1, :] * o_cmp[:, ls] + gates[3 * r + 1:3 * r + 2, :] * o_sel[:, ls]
             + gates[3 * r + 2:3 * r + 3, :] * o_win[:, ls])
        rs = slice(r * HEAD_DIM, (r + 1) * HEAD_DIM)
        o_ref[rs, :] = (o * gn_ref[rs, :]).astype(BF16)


def _nsa(qt, qrt, kcmp, vcmpt, ksh, vst, kwh, vwt, gates, gnt, mt, *, batch, seq, tq):
    n = batch * seq
    nq = seq // tq
    ncp = kcmp.shape[2]
    grp = NSA_REP * HEAD_DIM
    qspec = pl.BlockSpec((grp, tq), lambda b, g, i: (g, b * nq + i))
    kspec = lambda w: pl.BlockSpec((1, seq, w), lambda b, g, i: (g, b, 0))
    vspec = pl.BlockSpec((seq // LANE, V_ROWS, LANE), lambda b, g, i: (b, g, 0))
    kern = functools.partial(_nsa_kernel, tq=tq, seq=seq)
    return pl.pallas_call(
        kern,
        grid=(batch, NSA_KV, nq),
        in_specs=[qspec, qspec,
                  pl.BlockSpec((1, 1, ncp, HEAD_DIM), lambda b, g, i: (g, b, 0, 0)),
                  pl.BlockSpec((1, 1, HEAD_DIM, ncp), lambda b, g, i: (g, b, 0, 0)),
                  kspec(K_AUG), vspec, kspec(HEAD_DIM), vspec,
                  pl.BlockSpec((GATE_ROWS, tq), lambda b, g, i: (g, b * nq + i)),
                  qspec,
                  pl.BlockSpec(mt.shape, lambda b, g, i: (0, 0))],
        out_specs=qspec,
        out_shape=jax.ShapeDtypeStruct((NSA_WIDTH, n), BF16),
        compiler_params=pltpu.CompilerParams(
            dimension_semantics=("arbitrary", "arbitrary", "arbitrary"),
            vmem_limit_bytes=VMEM_LIMIT),
        name="nsa",
    )(qt, qrt, kcmp, vcmpt, ksh, vst, kwh, vwt, gates, gnt, mt)


def _rwkv_kernel(p_ref, gr_ref, mu_ref, w0_ref, a0_ref, wab_ref, kk_ref, ka_ref, rk_ref,
                 gw_ref, gb_ref, ones_ref, tril_ref, o_ref, st_ref, carry_ref, *, tt):
    c = RWKV_CHUNK
    hd = HEAD_DIM
    nh = RWKV_HEADS
    nc = tt // c
    step = pl.program_id(1)

    @pl.when(step == 0)
    def _():
        st_ref[...] = jnp.zeros_like(st_ref)
        carry_ref[...] = jnp.zeros_like(carry_ref)

    p = p_ref[...]
    row = lax.broadcasted_iota(jnp.int32, p.shape, 0)
    prev = jnp.where(row == 0, carry_ref[...], pltpu.roll(p, 1, 0))
    carry_ref[...] = p[tt - 1:tt, :]
    ps = p + mu_ref[...] * (prev - p)
    r = ps[:, 0:RWKV_WIDTH]
    k = ps[:, RWKV_WIDTH:2 * RWKV_WIDTH]
    v = ps[:, 2 * RWKV_WIDTH:3 * RWKV_WIDTH]
    lora = ps[:, 3 * RWKV_WIDTH:]
    lane = lax.broadcasted_iota(jnp.int32, lora.shape, 1)
    feat = jnp.where(lane < DECAY_RANK, jnp.tanh(lora), lora).astype(BF16)
    up = _dot(feat, wab_ref[...])
    w = w0_ref[...] + up[:, 0:RWKV_WIDTH]
    lw = _sigmoid(w) * (-float(np.exp(-0.5)))
    a = _sigmoid(a0_ref[...] + up[:, RWKV_WIDTH:])
    ones_bd = ones_ref[...]
    kk = k * kk_ref[...]
    kkn = kk * lax.rsqrt(jnp.maximum(_head_sums(kk * kk, ones_bd), 1e-24))
    k2 = k * (1.0 + (a - 1.0) * ka_ref[...])
    alpha = -kkn
    beta = kkn * a
    bonus = _head_sums(r * k2 * rk_ref[...], ones_bd) * v

    cum = _x3_dot(tril_ref[...], lw)
    cend = jnp.concatenate(
        [jnp.broadcast_to(cum[(ch + 1) * c - 1:(ch + 1) * c, :], (c, RWKV_WIDTH)) for ch in range(nc)], axis=0)
    e_neg = jnp.exp(-cum)
    e_end = jnp.exp(cend - cum)
    at = alpha * jnp.exp(cum - lw)
    bt = beta * e_neg
    kt = k2 * e_neg
    rt = r * jnp.exp(cum)
    bh = beta * e_end
    kh = k2 * e_end
    pc = jnp.exp(cend)

    npair = RWKV_WIDTH // LANE

    def pairs(x):
        return jnp.stack([x[ch * c:(ch + 1) * c, j * LANE:(j + 1) * LANE]
                          for ch in range(nc) for j in range(npair)], axis=0)

    at_p, rt_p, bt_p, kt_p, v_p, bh_p, kh_p = (pairs(t) for t in (at, rt, bt, kt, v, bh, kh))
    pc_p = jnp.stack([pc[ch * c:ch * c + 1, j * LANE:(j + 1) * LANE]
                      for ch in range(nc) for j in range(npair)], axis=0)
    lane_c = lax.broadcasted_iota(jnp.int32, (1, c, LANE), 2)
    row_c = lax.broadcasted_iota(jnp.int32, (1, c, LANE), 1)
    even_c = lane_c < hd
    col_c = jnp.where(even_c, lane_c, lane_c - hd)
    low_s = row_c > col_c
    low_i = row_c >= col_c
    lane_2c = lax.broadcasted_iota(jnp.int32, (1, 2 * c, LANE), 2)
    row_2c = lax.broadcasted_iota(jnp.int32, (1, 2 * c, LANE), 1)
    even_2c = lane_2c < hd
    on_bd = (row_2c < hd) == even_2c
    zero_c = jnp.zeros((1, c, LANE), BF16)

    def bd(x):
        xb = x.astype(BF16)
        return jnp.concatenate([jnp.where(even_c, xb, zero_c), jnp.where(even_c, zero_c, xb)], axis=1)

    def abd(x):
        xb = x.astype(BF16)
        return jnp.concatenate([jnp.where(even_c, zero_c, xb), jnp.where(even_c, xb, zero_c)], axis=1)

    la = jnp.concatenate([at_p, rt_p], axis=1).astype(BF16)
    zero_2c = jnp.zeros((1, 2 * c, LANE), BF16)
    r_e = _bmm_nt(jnp.where(even_2c, la, zero_2c), jnp.concatenate([bt_p, kt_p], axis=1).astype(BF16))
    r_o = _bmm_nt(jnp.where(even_2c, zero_2c, la), jnp.concatenate([kt_p, bt_p], axis=1).astype(BF16))
    nab = jnp.where(low_s, jnp.where(even_c, r_e[:, 0:c], r_o[:, 0:c]), 0.0)
    aak_sw = jnp.where(low_s, jnp.where(even_c, r_o[:, 0:c], r_e[:, 0:c]), 0.0).astype(BF16)
    arb = jnp.where(low_i, jnp.where(even_c, r_e[:, c:], r_o[:, c:]), 0.0).astype(BF16)
    ark_sw = jnp.where(low_i, jnp.where(even_c, r_o[:, c:], r_e[:, c:]), 0.0).astype(BF16)
    tinv = jnp.where(row_c == col_c, 1.0, 0.0) + nab
    npow = nab
    nbd = bd(npow)
    for _ in range(5):
        npow = _bmm(npow.astype(BF16), nbd)
        nbd = bd(npow)
        tinv = tinv + _bmm(tinv.astype(BF16), nbd)
    v_abd = abd(v_p)
    w2 = _bmm(aak_sw, v_abd)
    tx = _bmm(tinv.astype(BF16), jnp.concatenate([bd(w2), bd(at_p)], axis=2))
    u0, ta = tx[:, :, 0:LANE], tx[:, :, LANE:]
    w_e = jnp.concatenate([jnp.concatenate([bd(ta), bd(u0)], axis=2),
                           jnp.concatenate([jnp.zeros_like(v_abd), v_abd], axis=2)], axis=1)
    ax = _bmm(jnp.concatenate([arb, ark_sw], axis=2), w_e)
    rq = rt_p + ax[:, :, 0:LANE]
    y0 = ax[:, :, LANE:]
    v_b = v_p.astype(BF16)
    w_f = jnp.concatenate([jnp.concatenate([ta, u0], axis=2).astype(BF16),
                           jnp.concatenate([jnp.zeros_like(v_b), v_b], axis=2)], axis=1)
    gh = _bmm(jnp.concatenate([jnp.swapaxes(bh_p, 1, 2), jnp.swapaxes(kh_p, 1, 2)], axis=2).astype(BF16), w_f)
    g_bd = jnp.where(on_bd, gh[:, :, 0:LANE], 0.0) + jnp.where(row_2c == lane_2c, pc_p, 0.0)
    h_bd = jnp.where(on_bd, gh[:, :, LANE:], 0.0)
    lhs = jnp.concatenate([rq, g_bd], axis=1).astype(BF16)

    st = st_ref[...]
    ys = []
    for ch in range(nc):
        sl = slice(ch * npair, (ch + 1) * npair)
        res = _bmm(lhs[sl], st.astype(BF16))
        yc = res[:, 0:c, :] + y0[sl]
        st = res[:, c:, :] + h_bd[sl]
        ys.append(jnp.concatenate([yc[j] for j in range(npair)], axis=1))
    st_ref[...] = st
    y = jnp.concatenate(ys, axis=0) if nc > 1 else ys[0]

    inv_hd = 1.0 / hd
    mean = _head_sums(y, ones_bd) * inv_hd
    ycen = y - mean
    var = _head_sums(ycen * ycen, ones_bd) * inv_hd
    yn = ycen * lax.rsqrt(var + GN_EPS) * gw_ref[...] + gb_ref[...]
    o_ref[...] = ((yn + bonus) * gr_ref[...]).astype(BF16)


def _rwkv(rw, gr, mu, w0, a0, wab, kk, ka, rk, gw, gb, ones_bd, tril, *, batch, seq, tt):
    n = batch * seq
    ns = seq // tt
    row = lambda w: pl.BlockSpec((tt, w), lambda b, i: (b * ns + i, 0))
    full = lambda a: pl.BlockSpec(a.shape, lambda b, i: (0,) * a.ndim)
    kern = functools.partial(_rwkv_kernel, tt=tt)
    consts = (mu, w0, a0, wab, kk, ka, rk, gw, gb, ones_bd, tril)
    return pl.pallas_call(
        kern,
        grid=(batch, ns),
        in_specs=[row(RWKV_SHIFT_WIDTH), row(RWKV_WIDTH)] + [full(a) for a in consts],
        out_specs=row(RWKV_WIDTH),
        out_shape=jax.ShapeDtypeStruct((n, RWKV_WIDTH), BF16),
        scratch_shapes=[pltpu.VMEM((RWKV_WIDTH // LANE, LANE, LANE), F32),
                        pltpu.VMEM((1, RWKV_SHIFT_WIDTH), F32)],
        compiler_params=pltpu.CompilerParams(dimension_semantics=("arbitrary", "arbitrary"),
                                             vmem_limit_bytes=VMEM_LIMIT),
        name="rwkv",
    )(rw, gr, *consts)


def _out_kernel(x_ref, ont_ref, or_ref, wn_ref, wr_ref, g_ref, o_ref):
    h = x_ref[...] + _dot_tn(ont_ref[...], wn_ref[...]) + _dot(or_ref[...], wr_ref[...])
    ms = jnp.mean(h * h, axis=-1, keepdims=True)
    o_ref[...] = h * lax.rsqrt(ms + RMS_EPS) * g_ref[...]


def _out_proj(x2, o_nsa_t, o_rwkv, wn, wr, final_g, *, tm):
    n = x2.shape[0]
    row = lambda w: pl.BlockSpec((tm, w), lambda i: (i, 0))
    full = lambda a: pl.BlockSpec(a.shape, lambda i: (0,) * a.ndim)
    return pl.pallas_call(
        _out_kernel,
        grid=(n // tm,),
        in_specs=[row(D_MODEL), pl.BlockSpec((NSA_WIDTH, tm), lambda i: (0, i)), row(RWKV_WIDTH),
                  full(wn), full(wr), full(final_g)],
        out_specs=row(D_MODEL),
        out_shape=jax.ShapeDtypeStruct((n, D_MODEL), F32),
        compiler_params=pltpu.CompilerParams(dimension_semantics=("arbitrary",),
                                             vmem_limit_bytes=VMEM_LIMIT),
        name="out_proj",
    )(x2, o_nsa_t, o_rwkv, wn, wr, final_g)


def _rope_tables(seq):
    inv = ROPE_THETA ** (-np.arange(ROPE_HALF, dtype=np.float64) / ROPE_HALF)
    ang = np.arange(seq, dtype=np.float64)[:, None] * inv[None, :]
    cos, sin = np.cos(ang), np.sin(ang)
    ra = np.ones((seq, HEAD_DIM)); rm = np.zeros((seq, HEAD_DIM)); rp = np.zeros((seq, HEAD_DIM))
    ra[:, :ROPE_HALF] = cos; ra[:, ROPE_HALF:ROPE_DIM] = cos
    rm[:, :ROPE_HALF] = -sin
    rp[:, ROPE_HALF:ROPE_DIM] = sin
    rep = lambda t: jnp.asarray(np.tile(t, (1, LANE // HEAD_DIM)), F32)
    assert seq // SEL_BLOCK <= K_AUG - HEAD_DIM
    onehot = np.zeros((seq, K_AUG - HEAD_DIM))
    onehot[np.arange(seq), np.arange(seq) // SEL_BLOCK] = 1.0
    return (rep(ra), rep(rm), rep(rp), jnp.asarray(cos.T, F32), jnp.asarray(sin.T, F32),
            jnp.asarray(onehot, BF16))


def _cmp_to_sel_t(n_cmp_pad, n_sel):
    n_cmp = n_cmp_pad - 1
    c0 = np.arange(n_cmp)[:, None] * CMP_STRIDE
    s0 = np.arange(n_sel)[None, :] * SEL_BLOCK
    ov = np.clip(np.minimum(c0 + CMP_BLOCK, s0 + SEL_BLOCK) - np.maximum(c0, s0), 0, None) / CMP_BLOCK
    mt = np.zeros((n_sel, n_cmp_pad))
    mt[:, :n_cmp] = ov.T
    return jnp.asarray(mt, BF16)


def _prep_w_in(w_in):
    idx = np.cumsum(IN_SIZES)[:-1].tolist()
    q, kc, vc, ks, vs, kw, vw, gl, gn, rw, gr = jnp.split(w_in, idx, axis=1)
    w_t = jnp.concatenate([kc, vc, ks, kw, rw, gr], axis=1).astype(BF16)
    per_g = 3 * NSA_REP
    pad = jnp.zeros((D_MODEL, GATE_ROWS - per_g), w_in.dtype)
    gl_p = [t for g in range(NSA_KV) for t in (gl[:, g * per_g:(g + 1) * per_g], pad)]
    w_f = jnp.concatenate([q, vs, vw] + gl_p + [gn], axis=1).T.astype(BF16)
    return w_t, w_f


def _layer(x2, norm_g, w_in, cmp_pos_k, cmp_w1_k, cmp_w2_k, cmp_pos_v, cmp_w1_v, cmp_w2_v,
           shift_mu, decay_w0, decay_up, iclr_a0, iclr_up, k_k, k_a, r_k, gn_w, gn_b, w_out,
           final_g, *, batch, seq):
    tm = 256
    tq = 256
    tt = 256
    assert WINDOW == 2 * KEY_TILE and seq % KEY_TILE == 0 and KEY_TILE % tq == 0
    nch = seq // CMP_STRIDE
    n_sel = seq // SEL_BLOCK
    rowv = lambda t: t.reshape(1, -1).astype(F32)

    w_t, w_f = _prep_w_in(w_in)
    (kch, vch, ksh, kwh, rw, gr, qt, qrt, vst, vwt, gates, gnt) = _in_proj(
        x2, rowv(norm_g), w_t, w_f, _rope_tables(seq), seq=seq, tm=tm)

    chunks = lambda t: t.reshape(NSA_KV, batch, nch, CMP_STRIDE * HEAD_DIM)
    kcmp, vcmpt = _compress(chunks(kch), chunks(vch),
                            rowv(cmp_pos_k), cmp_w1_k.astype(BF16), cmp_w2_k.astype(BF16),
                            rowv(cmp_pos_v), cmp_w1_v.astype(BF16), cmp_w2_v.T.astype(BF16))

    o_nsa_t = _nsa(qt, qrt, kcmp, vcmpt, ksh, vst, kwh, vwt, gates, gnt,
                   _cmp_to_sel_t(nch, n_sel), batch=batch, seq=seq, tq=tq)

    z = jnp.zeros((DECAY_RANK, RWKV_WIDTH), F32)
    wab = jnp.concatenate([jnp.concatenate([decay_up, z], axis=1),
                           jnp.concatenate([z, iclr_up], axis=1)], axis=0).astype(BF16)
    hid = np.arange(2 * LANE) // HEAD_DIM
    ones_bd = jnp.asarray(hid[:, None] == hid[None, :], BF16)
    ti = np.arange(tt)
    tril = jnp.asarray((ti[:, None] >= ti[None, :]) & (ti[:, None] // RWKV_CHUNK == ti[None, :] // RWKV_CHUNK), BF16)
    o_rwkv = _rwkv(rw, gr, rowv(shift_mu), rowv(decay_w0), rowv(iclr_a0), wab, rowv(k_k), rowv(k_a),
                   rowv(r_k), rowv(gn_w), rowv(gn_b), ones_bd, tril, batch=batch, seq=seq, tt=tt)

    w_o = w_out.astype(BF16)
    return _out_proj(x2, o_nsa_t, o_rwkv, w_o[:NSA_WIDTH], w_o[NSA_WIDTH:], rowv(final_g), tm=tm)


def kernel(x, norm_g, w_in, cmp_pos_k, cmp_w1_k, cmp_w2_k, cmp_pos_v, cmp_w1_v, cmp_w2_v, shift_mu, decay_w0, decay_up, iclr_a0, iclr_up, k_k, k_a, r_k, gn_w, gn_b, w_out, final_g):
    batch, seq, d = x.shape
    assert d == D_MODEL and norm_g.shape[0] == 1, "single-layer trunk"
    out = _layer(x.reshape(batch * seq, d), norm_g[0], w_in[0], cmp_pos_k[0], cmp_w1_k[0], cmp_w2_k[0],
                 cmp_pos_v[0], cmp_w1_v[0], cmp_w2_v[0], shift_mu[0], decay_w0[0], decay_up[0],
                 iclr_a0[0], iclr_up[0], k_k[0], k_a[0], r_k[0], gn_w[0], gn_b[0], w_out[0],
                 final_g, batch=batch, seq=seq)
    return out.reshape(batch, seq, d)
```

```python
import functools

import numpy as np
import jax
import jax.numpy as jnp
from jax import lax
from jax.experimental import pallas as pl
from jax.experimental.pallas import tpu as pltpu

F32 = jnp.float32
BF16 = jnp.bfloat16

D_MODEL = 1024
HEAD_DIM = 64
NSA_HEADS = 8
NSA_KV = 2
NSA_REP = NSA_HEADS // NSA_KV
RWKV_HEADS = 8
NSA_WIDTH = NSA_HEADS * HEAD_DIM
RWKV_WIDTH = RWKV_HEADS * HEAD_DIM
KV_WIDTH = NSA_KV * HEAD_DIM
ROPE_DIM = HEAD_DIM // 4
ROPE_HALF = ROPE_DIM // 2
ROPE_THETA = 500000.0
CMP_BLOCK = 32
CMP_STRIDE = 16
CMP_HIDDEN = 256
SEL_BLOCK = 64
SEL_TOPK = 8
WINDOW = 512
DECAY_RANK = 64
ICLR_RANK = 64
RWKV_SHIFT_WIDTH = 3 * RWKV_WIDTH + DECAY_RANK + ICLR_RANK
IN_SIZES = (NSA_WIDTH, KV_WIDTH, KV_WIDTH, KV_WIDTH, KV_WIDTH, KV_WIDTH, KV_WIDTH,
            3 * NSA_HEADS, NSA_WIDTH, RWKV_SHIFT_WIDTH, RWKV_WIDTH)
SCALE = HEAD_DIM ** -0.5
RMS_EPS = 1e-6
GN_EPS = 64e-5
NEG_INF = -1e30
FORCE_BONUS = 1e3

LANE = 128
SUBLANE = 8
GATE_ROWS = 16

T_KV = 0
T_RW = T_KV + 4 * KV_WIDTH
T_GR = T_RW + RWKV_SHIFT_WIDTH
T_END = T_GR + RWKV_WIDTH
R_Q = 0
R_VS = R_Q + NSA_WIDTH
R_VW = R_VS + KV_WIDTH
R_GL = R_VW + KV_WIDTH
R_GN = R_GL + NSA_KV * GATE_ROWS
R_END = R_GN + NSA_WIDTH

LOG2E = float(np.log2(np.e))
K_AUG = LANE
V_ROWS = HEAD_DIM + 16
KEY_TILE = 256
RWKV_CHUNK = 64
VMEM_LIMIT = 48 * 1024 * 1024


def _dot(a, b):
    return jnp.dot(a, b, preferred_element_type=F32)


def _dot_nt(a, b):
    return lax.dot_general(a, b, (((1,), (1,)), ((), ())), preferred_element_type=F32)


def _dot_tn(a, b):
    return lax.dot_general(a, b, (((0,), (0,)), ((), ())), preferred_element_type=F32)


def _bmm(a, b):
    return lax.dot_general(a, b, (((2,), (1,)), ((0,), (0,))), preferred_element_type=F32)


def _bmm_nt(a, b):
    return lax.dot_general(a, b, (((2,), (2,)), ((0,), (0,))), preferred_element_type=F32)


def _split3(x):
    hi = x.astype(BF16)
    r1 = x - hi.astype(F32)
    mid = r1.astype(BF16)
    lo = (r1 - mid.astype(F32)).astype(BF16)
    return hi, mid, lo


def _head_sums(x, ones_blk):
    w = ones_blk.shape[0]
    hi = x.astype(BF16)
    lo = (x - hi.astype(F32)).astype(BF16)
    return jnp.concatenate(
        [_dot(hi[:, i:i + w], ones_blk) + _dot(lo[:, i:i + w], ones_blk) for i in range(0, x.shape[1], w)],
        axis=1)


def _x3_dot(w_bf16, x):
    hi, mid, lo = _split3(x)
    return _dot(w_bf16, hi) + _dot(w_bf16, mid) + _dot(w_bf16, lo)


def _sigmoid(x):
    return 1.0 / (1.0 + jnp.exp(-x))


def _rope128(t, ra, rm, rp):
    return t * ra + pltpu.roll(t, LANE - ROPE_HALF, 1) * rm + pltpu.roll(t, ROPE_HALF, 1) * rp


def _in_proj_kernel(x_ref, g_ref, wt_ref, wf_ref, ra_ref, rm_ref, rp_ref, cos_ref, sin_ref, oh_ref,
                    kc_ref, vc_ref, ks_ref, kw_ref, rw_ref, gr_ref,
                    qt_ref, qrt_ref, vst_ref, vwt_ref, gate_ref, gn_ref):
    x = x_ref[...]
    ms = jnp.mean(x * x, axis=-1, keepdims=True)
    y = (x * lax.rsqrt(ms + RMS_EPS) * g_ref[...]).astype(BF16)

    kv = _dot(y, wt_ref[:, T_KV:T_RW])
    ra, rm, rp = ra_ref[...], rm_ref[...], rp_ref[...]
    for i, (ref, rot) in enumerate(((kc_ref, False), (vc_ref, False), (ks_ref, True), (kw_ref, True))):
        t = kv[:, i * LANE:(i + 1) * LANE]
        if rot:
            t = _rope128(t, ra, rm, rp)
        for g in range(NSA_KV):
            tg = t[:, g * HEAD_DIM:(g + 1) * HEAD_DIM].astype(BF16)
            if ref is ks_ref:
                ref[g] = jnp.concatenate([tg, oh_ref[...]], axis=1)
            else:
                ref[g] = tg
    rw_ref[...] = _dot(y, wt_ref[:, T_RW:T_GR])
    gr = _dot(y, wt_ref[:, T_GR:T_END])
    gr_ref[...] = gr * _sigmoid(gr)

    def proj_t(r0, r1):
        return _dot_nt(wf_ref[r0:r1, :], y)

    qt = proj_t(R_Q, R_VS) * (SCALE * LOG2E)
    cos, sin = cos_ref[...], sin_ref[...]
    qt_ref[...] = qt.astype(BF16)
    for h in range(NSA_HEADS):
        r0 = h * HEAD_DIM
        t1 = qt[r0:r0 + ROPE_HALF]
        t2 = qt[r0 + ROPE_HALF:r0 + ROPE_DIM]
        qrt_ref[r0:r0 + ROPE_DIM, :] = jnp.concatenate(
            [t1 * cos - t2 * sin, t2 * cos + t1 * sin], axis=0).astype(BF16)
        qrt_ref[r0 + ROPE_DIM:r0 + HEAD_DIM, :] = qt[r0 + ROPE_DIM:r0 + HEAD_DIM].astype(BF16)
    vt = proj_t(R_VS, R_GL).astype(BF16)
    ones = jnp.ones((V_ROWS - HEAD_DIM, LANE), BF16)
    for j in range(vt.shape[1] // LANE):
        for i, ref in enumerate((vst_ref, vwt_ref)):
            for g in range(NSA_KV):
                r0 = i * KV_WIDTH + g * HEAD_DIM
                ref[j, g * V_ROWS:(g + 1) * V_ROWS, :] = jnp.concatenate(
                    [vt[r0:r0 + HEAD_DIM, j * LANE:(j + 1) * LANE], ones], axis=0)
    gate_ref[...] = _sigmoid(proj_t(R_GL, R_GN))
    gn = proj_t(R_GN, R_END)
    gn_ref[...] = gn * _sigmoid(gn)


def _in_proj(x2, norm_g, w_t, w_f, tabs, *, seq, tm):
    n = x2.shape[0]
    spt = seq // tm
    ra, rm, rp, cos, sin, onehot = tabs
    hm = lambda w: jax.ShapeDtypeStruct((NSA_KV, n, w), BF16)
    hspec = lambda w: pl.BlockSpec((NSA_KV, tm, w), lambda i: (0, i, 0))
    row = lambda w: pl.BlockSpec((tm, w), lambda i: (i, 0))
    col = lambda r: pl.BlockSpec((r, tm), lambda i: (0, i))
    full = lambda a: pl.BlockSpec(a.shape, lambda i: (0,) * a.ndim)
    tab = lambda w: pl.BlockSpec((tm, w), lambda i: (i % spt, 0))
    tabt = pl.BlockSpec((ROPE_HALF, tm), lambda i: (0, i % spt))
    vtile = pl.BlockSpec((tm // LANE, NSA_KV * V_ROWS, LANE), lambda i: (i, 0, 0))
    vsd = jax.ShapeDtypeStruct((n // LANE, NSA_KV * V_ROWS, LANE), BF16)
    kw = (HEAD_DIM, HEAD_DIM, K_AUG, HEAD_DIM)
    return pl.pallas_call(
        _in_proj_kernel,
        grid=(n // tm,),
        in_specs=[row(D_MODEL), full(norm_g), full(w_t), full(w_f), tab(LANE), tab(LANE), tab(LANE),
                  tabt, tabt, tab(K_AUG - HEAD_DIM)],
        out_specs=[hspec(w) for w in kw] + [row(RWKV_SHIFT_WIDTH), row(RWKV_WIDTH),
                                            col(NSA_WIDTH), col(NSA_WIDTH), vtile, vtile,
                                            col(NSA_KV * GATE_ROWS), col(NSA_WIDTH)],
        out_shape=[hm(w) for w in kw] + [jax.ShapeDtypeStruct((n, RWKV_SHIFT_WIDTH), F32),
                                         jax.ShapeDtypeStruct((n, RWKV_WIDTH), F32),
                                         jax.ShapeDtypeStruct((NSA_WIDTH, n), BF16),
                                         jax.ShapeDtypeStruct((NSA_WIDTH, n), BF16),
                                         vsd, vsd,
                                         jax.ShapeDtypeStruct((NSA_KV * GATE_ROWS, n), F32),
                                         jax.ShapeDtypeStruct((NSA_WIDTH, n), F32)],
        compiler_params=pltpu.CompilerParams(dimension_semantics=("arbitrary",),
                                             vmem_limit_bytes=VMEM_LIMIT),
        name="in_proj",
    )(x2, norm_g, w_t, w_f, ra, rm, rp, cos, sin, onehot)


def _compress_kernel(kc_ref, vc_ref, pk_ref, w1k_ref, w2k_ref, pv_ref, w1v_ref, w2vt_ref,
                     ko_ref, vo_ref):
    half = CMP_STRIDE * HEAD_DIM

    def hidden(c_ref, pos_ref, w1_ref):
        c = c_ref[0, 0]
        za = _dot(c, w1_ref[0:half, :])
        zb = _dot(c, w1_ref[half:2 * half, :])
        pos = jnp.broadcast_to(pos_ref[...], (SUBLANE, 2 * half)).astype(BF16)
        pv = _dot(pos, w1_ref[...])[0:1, :]
        hid = za + pltpu.roll(zb, c.shape[0] - 1, 0) + pv
        return (hid * _sigmoid(hid)).astype(BF16)

    ko_ref[0, 0] = _dot(hidden(kc_ref, pk_ref, w1k_ref), w2k_ref[...]).astype(BF16)
    vo_ref[0, 0] = _dot_nt(w2vt_ref[...], hidden(vc_ref, pv_ref, w1v_ref)).astype(BF16)


def _compress(kc_r, vc_r, pk, w1k, w2k, pv, w1v, w2vt):
    g, b, nch, width = kc_r.shape
    blk = pl.BlockSpec((1, 1, nch, width), lambda i, j: (i, j, 0, 0))
    full = lambda a: pl.BlockSpec(a.shape, lambda i, j: (0,) * a.ndim)
    return pl.pallas_call(
        _compress_kernel,
        grid=(g, b),
        in_specs=[blk, blk, full(pk), full(w1k), full(w2k), full(pv), full(w1v), full(w2vt)],
        out_specs=[pl.BlockSpec((1, 1, nch, HEAD_DIM), lambda i, j: (i, j, 0, 0)),
                   pl.BlockSpec((1, 1, HEAD_DIM, nch), lambda i, j: (i, j, 0, 0))],
        out_shape=[jax.ShapeDtypeStruct((g, b, nch, HEAD_DIM), BF16),
                   jax.ShapeDtypeStruct((g, b, HEAD_DIM, nch), BF16)],
        compiler_params=pltpu.CompilerParams(dimension_semantics=("arbitrary", "arbitrary"),
                                             vmem_limit_bytes=VMEM_LIMIT),
        name="compress",
    )(kc_r, vc_r, pk, w1k, w2k, pv, w1v, w2vt)


def _nsa_kernel(qt_ref, qrt_ref, kc_ref, vct_ref, ks_ref, vst_ref, kw_ref, vwt_ref,
                gate_ref, gn_ref, mt_ref, o_ref, *, tq, seq):
    tk = KEY_TILE
    qi = pl.program_id(2)
    q0 = qi * tq
    nl = NSA_REP * tq
    n_sel = seq // SEL_BLOCK
    ncp = kc_ref.shape[2]

    def heads_on_lanes(ref):
        return jnp.concatenate([ref[r * HEAD_DIM:(r + 1) * HEAD_DIM, :] for r in range(NSA_REP)], axis=1)

    def tile4(a):
        return jnp.concatenate([a] * NSA_REP, axis=1)

    s = _dot(kc_ref[0, 0], heads_on_lanes(qt_ref))
    t_c = q0 + lax.broadcasted_iota(jnp.int32, (ncp, tq), 1)
    c_c = lax.broadcasted_iota(jnp.int32, (ncp, tq), 0)
    cmask = tile4((c_c * CMP_STRIDE + (CMP_BLOCK - 1)) <= t_c)
    s = jnp.where(cmask, s, NEG_INF)
    m = jnp.max(s, axis=0, keepdims=True)
    e = jnp.where(cmask, jnp.exp2(s - m), 0.0)
    den = jnp.sum(e, axis=0, keepdims=True)
    p = e * (1.0 / jnp.where(den > 0.0, den, 1.0))
    o_cmp = _dot(vct_ref[0, 0], p.astype(BF16))

    psum = p[:, 0:tq]
    for r in range(1, NSA_REP):
        psum = psum + p[:, r * tq:(r + 1) * tq]
    imp = _x3_dot(mt_ref[...], psum)
    j = lax.broadcasted_iota(jnp.int32, (n_sel, tq), 0)
    t = q0 + lax.broadcasted_iota(jnp.int32, (n_sel, tq), 1)
    tb = t // SEL_BLOCK
    forced = (j == 0) | (j == tb) | (j == tb - 1)
    val = jnp.where(j <= tb, imp + jnp.where(forced, FORCE_BONUS, 0.0), -1.0)
    cnt = jnp.zeros((n_sel, tq), F32)
    for i in range(n_sel):
        vi = val[i:i + 1, :]
        ge = jnp.where(vi >= val, 1.0, 0.0)
        gt = jnp.where(vi > val, 1.0, 0.0)
        cnt = cnt + jnp.where(j > i, ge, gt)
    sel_bias = jnp.where(cnt < float(SEL_TOPK), 0.0, NEG_INF).astype(BF16)

    qr = heads_on_lanes(qrt_ref)
    qr_sel = jnp.concatenate([qr, tile4(sel_bias), jnp.zeros((K_AUG - HEAD_DIM - n_sel, nl), BF16)], axis=0)
    k_s = lax.broadcasted_iota(jnp.int32, (tk, tq), 0)
    t_l = q0 + lax.broadcasted_iota(jnp.int32, (tk, tq), 1)

    def update(carry, k_ref, vt_ref, q_op, kts, keeps):
        m_i, acc = carry

        def scores(i):
            sc = _dot(k_ref[0, pl.ds(pl.multiple_of(kts[i] * tk, tk), tk), :], q_op)
            if keeps[i] is None:
                return sc
            return jnp.concatenate([jnp.where(keeps[i], sc[:, r * tq:(r + 1) * tq], NEG_INF)
                                    for r in range(NSA_REP)], axis=1)

        sc_next = scores(0)
        for i, kt in enumerate(kts):
            sc = sc_next
            if i + 1 < len(kts):
                sc_next = scores(i + 1)
            m_n = jnp.maximum(m_i, jnp.max(sc, axis=0, keepdims=True))
            vtb = jnp.concatenate([vt_ref[kt * (tk // LANE) + jj] for jj in range(tk // LANE)], axis=1)
            acc = jnp.exp2(m_i - m_n) * acc + _dot(vtb, jnp.exp2(sc - m_n).astype(BF16))
            m_i = m_n
        return m_i, acc

    def last_tiles(carry, k_ref, vt_ref, q_op, low_keep, spare_tile):
        kts, keeps = [], []
        for back, keep_fn in ((0, lambda d: d >= 0), (1, None), (2, low_keep)):
            exists = a >= back
            kt = jnp.where(exists, a - back, 0 if spare_tile is None else spare_tile)
            d = t_l - (kt * tk + k_s)
            kts.append(kt)
            if keep_fn is not None:
                keeps.append(keep_fn(d) & exists)
            else:
                keeps.append(exists if spare_tile is None else None)
        return update(carry, k_ref, vt_ref, q_op, kts, keeps)

    init = (jnp.full((1, nl), NEG_INF, F32), jnp.zeros((V_ROWS, nl), F32))
    a = (q0 + tq - 1) // tk
    c_win = last_tiles(init, kw_ref, vwt_ref, qr, lambda d: d < WINDOW, None)
    n_old = jnp.maximum(a - WINDOW // tk, 0)
    c_sel = lax.fori_loop(0, n_old % 2, lambda kt, c: update(c, ks_ref, vst_ref, qr_sel, [kt], [None]), init)
    c_sel = lax.fori_loop(0, n_old // 2,
                          lambda i, c: update(c, ks_ref, vst_ref, qr_sel,
                                              [n_old % 2 + 2 * i, n_old % 2 + 2 * i + 1], [None, None]), c_sel)
    assert (seq // tk - 1) * (tk // SEL_BLOCK) >= SEL_TOPK and WINDOW // SEL_BLOCK <= SEL_TOPK
    c_sel = last_tiles(c_sel, ks_ref, vst_ref, qr_sel, None, seq // tk - 1)
    o_sel = c_sel[1][0:HEAD_DIM] * (1.0 / c_sel[1][HEAD_DIM:HEAD_DIM + 1])
    o_win = c_win[1][0:HEAD_DIM] * (1.0 / c_win[1][HEAD_DIM:HEAD_DIM + 1])

    gates = gate_ref[...]
    for r in range(NSA_REP):
        ls = slice(r * tq, (r + 1) * tq)
        o = (gates[3 * r:3 * r + 1, :] * o_cmp[:, ls] + gates[3 * r + 1:3 * r + 2, :] * o_sel[:, ls]
             + gates[3 * r + 2:3 * r + 3, :] * o_win[:, ls])
        rs = slice(r * HEAD_DIM, (r + 1) * HEAD_DIM)
        o_ref[rs, :] = (o * gn_ref[rs, :]).astype(BF16)


def _nsa(qt, qrt, kcmp, vcmpt, ksh, vst, kwh, vwt, gates, gnt, mt, *, batch, seq, tq):
    n = batch * seq
    nq = seq // tq
    ncp = kcmp.shape[2]
    grp = NSA_REP * HEAD_DIM
    qspec = pl.BlockSpec((grp, tq), lambda b, g, i: (g, b * nq + i))
    kspec = lambda w: pl.BlockSpec((1, seq, w), lambda b, g, i: (g, b, 0))
    vspec = pl.BlockSpec((seq // LANE, V_ROWS, LANE), lambda b, g, i: (b, g, 0))
    kern = functools.partial(_nsa_kernel, tq=tq, seq=seq)
    return pl.pallas_call(
        kern,
        grid=(batch, NSA_KV, nq),
        in_specs=[qspec, qspec,
                  pl.BlockSpec((1, 1, ncp, HEAD_DIM), lambda b, g, i: (g, b, 0, 0)),
                  pl.BlockSpec((1, 1, HEAD_DIM, ncp), lambda b, g, i: (g, b, 0, 0)),
                  kspec(K_AUG), vspec, kspec(HEAD_DIM), vspec,
                  pl.BlockSpec((GATE_ROWS, tq), lambda b, g, i: (g, b * nq + i)),
                  qspec,
                  pl.BlockSpec(mt.shape, lambda b, g, i: (0, 0))],
        out_specs=qspec,
        out_shape=jax.ShapeDtypeStruct((NSA_WIDTH, n), BF16),
        compiler_params=pltpu.CompilerParams(
            dimension_semantics=("arbitrary", "arbitrary", "arbitrary"),
            vmem_limit_bytes=VMEM_LIMIT),
        name="nsa",
    )(qt, qrt, kcmp, vcmpt, ksh, vst, kwh, vwt, gates, gnt, mt)


def _rwkv_kernel(p_ref, gr_ref, mu_ref, w0_ref, a0_ref, wab_ref, kk_ref, ka_ref, rk_ref,
                 gw_ref, gb_ref, ones_ref, tril_ref, o_ref, st_ref, carry_ref, *, tt):
    c = RWKV_CHUNK
    hd = HEAD_DIM
    nh = RWKV_HEADS
    nc = tt // c
    step = pl.program_id(1)

    @pl.when(step == 0)
    def _():
        st_ref[...] = jnp.zeros_like(st_ref)
        carry_ref[...] = jnp.zeros_like(carry_ref)

    p = p_ref[...]
    row = lax.broadcasted_iota(jnp.int32, p.shape, 0)
    prev = jnp.where(row == 0, carry_ref[...], pltpu.roll(p, 1, 0))
    carry_ref[...] = p[tt - 1:tt, :]
    ps = p + mu_ref[...] * (prev - p)
    r = ps[:, 0:RWKV_WIDTH]
    k = ps[:, RWKV_WIDTH:2 * RWKV_WIDTH]
    v = ps[:, 2 * RWKV_WIDTH:3 * RWKV_WIDTH]
    lora = ps[:, 3 * RWKV_WIDTH:]
    lane = lax.broadcasted_iota(jnp.int32, lora.shape, 1)
    feat = jnp.where(lane < DECAY_RANK, jnp.tanh(lora), lora).astype(BF16)
    up = _dot(feat, wab_ref[...])
    w = w0_ref[...] + up[:, 0:RWKV_WIDTH]
    lw = _sigmoid(w) * (-float(np.exp(-0.5)))
    a = _sigmoid(a0_ref[...] + up[:, RWKV_WIDTH:])
    ones_bd = ones_ref[...]
    kk = k * kk_ref[...]
    kkn = kk * lax.rsqrt(jnp.maximum(_head_sums(kk * kk, ones_bd), 1e-24))
    k2 = k * (1.0 + (a - 1.0) * ka_ref[...])
    alpha = -kkn
    beta = kkn * a
    bonus = _head_sums(r * k2 * rk_ref[...], ones_bd) * v

    cum = _x3_dot(tril_ref[...], lw)
    cend = jnp.concatenate(
        [jnp.broadcast_to(cum[(ch + 1) * c - 1:(ch + 1) * c, :], (c, RWKV_WIDTH)) for ch in range(nc)], axis=0)
    e_neg = jnp.exp(-cum)
    e_end = jnp.exp(cend - cum)
    at = alpha * jnp.exp(cum - lw)
    bt = beta * e_neg
    kt = k2 * e_neg
    rt = r * jnp.exp(cum)
    bh = beta * e_end
    kh = k2 * e_end
    pc = jnp.exp(cend)

    npair = RWKV_WIDTH // LANE

    def pairs(x):
        return jnp.stack([x[ch * c:(ch + 1) * c, j * LANE:(j + 1) * LANE]
                          for ch in range(nc) for j in range(npair)], axis=0)

    at_p, rt_p, bt_p, kt_p, v_p, bh_p, kh_p = (pairs(t) for t in (at, rt, bt, kt, v, bh, kh))
    pc_p = jnp.stack([pc[ch * c:ch * c + 1, j * LANE:(j + 1) * LANE]
                      for ch in range(nc) for j in range(npair)], axis=0)
    lane_c = lax.broadcasted_iota(jnp.int32, (1, c, LANE), 2)
    row_c = lax.broadcasted_iota(jnp.int32, (1, c, LANE), 1)
    even_c = lane_c < hd
    col_c = jnp.where(even_c, lane_c, lane_c - hd)
    low_s = row_c > col_c
    low_i = row_c >= col_c
    lane_2c = lax.broadcasted_iota(jnp.int32, (1, 2 * c, LANE), 2)
    row_2c = lax.broadcasted_iota(jnp.int32, (1, 2 * c, LANE), 1)
    even_2c = lane_2c < hd
    on_bd = (row_2c < hd) == even_2c
    zero_c = jnp.zeros((1, c, LANE), BF16)

    def bd(x):
        xb = x.astype(BF16)
        return jnp.concatenate([jnp.where(even_c, xb, zero_c), jnp.where(even_c, zero_c, xb)], axis=1)

    def abd(x):
        xb = x.astype(BF16)
        return jnp.concatenate([jnp.where(even_c, zero_c, xb), jnp.where(even_c, xb, zero_c)], axis=1)

    la = jnp.concatenate([at_p, rt_p], axis=1).astype(BF16)
    zero_2c = jnp.zeros((1, 2 * c, LANE), BF16)
    r_e = _bmm_nt(jnp.where(even_2c, la, zero_2c), jnp.concatenate([bt_p, kt_p], axis=1).astype(BF16))
    r_o = _bmm_nt(jnp.where(even_2c, zero_2c, la), jnp.concatenate([kt_p, bt_p], axis=1).astype(BF16))
    nab = jnp.where(low_s, jnp.where(even_c, r_e[:, 0:c], r_o[:, 0:c]), 0.0)
    aak_sw = jnp.where(low_s, jnp.where(even_c, r_o[:, 0:c], r_e[:, 0:c]), 0.0).astype(BF16)
    arb = jnp.where(low_i, jnp.where(even_c, r_e[:, c:], r_o[:, c:]), 0.0).astype(BF16)
    ark_sw = jnp.where(low_i, jnp.where(even_c, r_o[:, c:], r_e[:, c:]), 0.0).astype(BF16)
    tinv = jnp.where(row_c == col_c, 1.0, 0.0) + nab
    npow = nab
    nbd = bd(npow)
    for _ in range(5):
        npow = _bmm(npow.astype(BF16), nbd)
        nbd = bd(npow)
        tinv = tinv + _bmm(tinv.astype(BF16), nbd)
    v_abd = abd(v_p)
    w2 = _bmm(aak_sw, v_abd)
    tx = _bmm(tinv.astype(BF16), jnp.concatenate([bd(w2), bd(at_p)], axis=2))
    u0, ta = tx[:, :, 0:LANE], tx[:, :, LANE:]
    w_e = jnp.concatenate([jnp.concatenate([bd(ta), bd(u0)], axis=2),
                           jnp.concatenate([jnp.zeros_like(v_abd), v_abd], axis=2)], axis=1)
    ax = _bmm(jnp.concatenate([arb, ark_sw], axis=2), w_e)
    rq = rt_p + ax[:, :, 0:LANE]
    y0 = ax[:, :, LANE:]
    v_b = v_p.astype(BF16)
    w_f = jnp.concatenate([jnp.concatenate([ta, u0], axis=2).astype(BF16),
                           jnp.concatenate([jnp.zeros_like(v_b), v_b], axis=2)], axis=1)
    gh = _bmm(jnp.concatenate([jnp.swapaxes(bh_p, 1, 2), jnp.swapaxes(kh_p, 1, 2)], axis=2).astype(BF16), w_f)
    g_bd = jnp.where(on_bd, gh[:, :, 0:LANE], 0.0) + jnp.where(row_2c == lane_2c, pc_p, 0.0)
    h_bd = jnp.where(on_bd, gh[:, :, LANE:], 0.0)
    lhs = jnp.concatenate([rq, g_bd], axis=1).astype(BF16)

    st = st_ref[...]
    ys = []
    for ch in range(nc):
        sl = slice(ch * npair, (ch + 1) * npair)
        res = _bmm(lhs[sl], st.astype(BF16))
        yc = res[:, 0:c, :] + y0[sl]
        st = res[:, c:, :] + h_bd[sl]
        ys.append(jnp.concatenate([yc[j] for j in range(npair)], axis=1))
    st_ref[...] = st
    y = jnp.concatenate(ys, axis=0) if nc > 1 else ys[0]

    inv_hd = 1.0 / hd
    mean = _head_sums(y, ones_bd) * inv_hd
    ycen = y - mean
    var = _head_sums(ycen * ycen, ones_bd) * inv_hd
    yn = ycen * lax.rsqrt(var + GN_EPS) * gw_ref[...] + gb_ref[...]
    o_ref[...] = ((yn + bonus) * gr_ref[...]).astype(BF16)


def _rwkv(rw, gr, mu, w0, a0, wab, kk, ka, rk, gw, gb, ones_bd, tril, *, batch, seq, tt):
    n = batch * seq
    ns = seq // tt
    row = lambda w: pl.BlockSpec((tt, w), lambda b, i: (b * ns + i, 0))
    full = lambda a: pl.BlockSpec(a.shape, lambda b, i: (0,) * a.ndim)
    kern = functools.partial(_rwkv_kernel, tt=tt)
    consts = (mu, w0, a0, wab, kk, ka, rk, gw, gb, ones_bd, tril)
    return pl.pallas_call(
        kern,
        grid=(batch, ns),
        in_specs=[row(RWKV_SHIFT_WIDTH), row(RWKV_WIDTH)] + [full(a) for a in consts],
        out_specs=row(RWKV_WIDTH),
        out_shape=jax.ShapeDtypeStruct((n, RWKV_WIDTH), BF16),
        scratch_shapes=[pltpu.VMEM((RWKV_WIDTH // LANE, LANE, LANE), F32),
                        pltpu.VMEM((1, RWKV_SHIFT_WIDTH), F32)],
        compiler_params=pltpu.CompilerParams(dimension_semantics=("arbitrary", "arbitrary"),
                                             vmem_limit_bytes=VMEM_LIMIT),
        name="rwkv",
    )(rw, gr, *consts)


def _out_kernel(x_ref, ont_ref, or_ref, wn_ref, wr_ref, g_ref, o_ref):
    h = x_ref[...] + _dot_tn(ont_ref[...], wn_ref[...]) + _dot(or_ref[...], wr_ref[...])
    ms = jnp.mean(h * h, axis=-1, keepdims=True)
    o_ref[...] = h * lax.rsqrt(ms + RMS_EPS) * g_ref[...]


def _out_proj(x2, o_nsa_t, o_rwkv, wn, wr, final_g, *, tm):
    n = x2.shape[0]
    row = lambda w: pl.BlockSpec((tm, w), lambda i: (i, 0))
    full = lambda a: pl.BlockSpec(a.shape, lambda i: (0,) * a.ndim)
    return pl.pallas_call(
        _out_kernel,
        grid=(n // tm,),
        in_specs=[row(D_MODEL), pl.BlockSpec((NSA_WIDTH, tm), lambda i: (0, i)), row(RWKV_WIDTH),
                  full(wn), full(wr), full(final_g)],
        out_specs=row(D_MODEL),
        out_shape=jax.ShapeDtypeStruct((n, D_MODEL), F32),
        compiler_params=pltpu.CompilerParams(dimension_semantics=("arbitrary",),
                                             vmem_limit_bytes=VMEM_LIMIT),
        name="out_proj",
    )(x2, o_nsa_t, o_rwkv, wn, wr, final_g)


def _rope_tables(seq):
    inv = ROPE_THETA ** (-np.arange(ROPE_HALF, dtype=np.float64) / ROPE_HALF)
    ang = np.arange(seq, dtype=np.float64)[:, None] * inv[None, :]
    cos, sin = np.cos(ang), np.sin(ang)
    ra = np.ones((seq, HEAD_DIM)); rm = np.zeros((seq, HEAD_DIM)); rp = np.zeros((seq, HEAD_DIM))
    ra[:, :ROPE_HALF] = cos; ra[:, ROPE_HALF:ROPE_DIM] = cos
    rm[:, :ROPE_HALF] = -sin
    rp[:, ROPE_HALF:ROPE_DIM] = sin
    rep = lambda t: jnp.asarray(np.tile(t, (1, LANE // HEAD_DIM)), F32)
    assert seq // SEL_BLOCK <= K_AUG - HEAD_DIM
    onehot = np.zeros((seq, K_AUG - HEAD_DIM))
    onehot[np.arange(seq), np.arange(seq) // SEL_BLOCK] = 1.0
    return (rep(ra), rep(rm), rep(rp), jnp.asarray(cos.T, F32), jnp.asarray(sin.T, F32),
            jnp.asarray(onehot, BF16))


def _cmp_to_sel_t(n_cmp_pad, n_sel):
    n_cmp = n_cmp_pad - 1
    c0 = np.arange(n_cmp)[:, None] * CMP_STRIDE
    s0 = np.arange(n_sel)[None, :] * SEL_BLOCK
    ov = np.clip(np.minimum(c0 + CMP_BLOCK, s0 + SEL_BLOCK) - np.maximum(c0, s0), 0, None) / CMP_BLOCK
    mt = np.zeros((n_sel, n_cmp_pad))
    mt[:, :n_cmp] = ov.T
    return jnp.asarray(mt, BF16)


def _prep_w_in(w_in):
    idx = np.cumsum(IN_SIZES)[:-1].tolist()
    q, kc, vc, ks, vs, kw, vw, gl, gn, rw, gr = jnp.split(w_in, idx, axis=1)
    w_t = jnp.concatenate([kc, vc, ks, kw, rw, gr], axis=1).astype(BF16)
    per_g = 3 * NSA_REP
    pad = jnp.zeros((D_MODEL, GATE_ROWS - per_g), w_in.dtype)
    gl_p = [t for g in range(NSA_KV) for t in (gl[:, g * per_g:(g + 1) * per_g], pad)]
    w_f = jnp.concatenate([q, vs, vw] + gl_p + [gn], axis=1).T.astype(BF16)
    return w_t, w_f


def _layer(x2, norm_g, w_in, cmp_pos_k, cmp_w1_k, cmp_w2_k, cmp_pos_v, cmp_w1_v, cmp_w2_v,
           shift_mu, decay_w0, decay_up, iclr_a0, iclr_up, k_k, k_a, r_k, gn_w, gn_b, w_out,
           final_g, *, batch, seq):
    tm = 256
    tq = 256
    tt = 256
    assert WINDOW == 2 * KEY_TILE and seq % KEY_TILE == 0 and KEY_TILE % tq == 0
    nch = seq // CMP_STRIDE
    n_sel = seq // SEL_BLOCK
    rowv = lambda t: t.reshape(1, -1).astype(F32)

    w_t, w_f = _prep_w_in(w_in)
    (kch, vch, ksh, kwh, rw, gr, qt, qrt, vst, vwt, gates, gnt) = _in_proj(
        x2, rowv(norm_g), w_t, w_f, _rope_tables(seq), seq=seq, tm=tm)

    chunks = lambda t: t.reshape(NSA_KV, batch, nch, CMP_STRIDE * HEAD_DIM)
    kcmp, vcmpt = _compress(chunks(kch), chunks(vch),
                            rowv(cmp_pos_k), cmp_w1_k.astype(BF16), cmp_w2_k.astype(BF16),
                            rowv(cmp_pos_v), cmp_w1_v.astype(BF16), cmp_w2_v.T.astype(BF16))

    o_nsa_t = _nsa(qt, qrt, kcmp, vcmpt, ksh, vst, kwh, vwt, gates, gnt,
                   _cmp_to_sel_t(nch, n_sel), batch=batch, seq=seq, tq=tq)

    z = jnp.zeros((DECAY_RANK, RWKV_WIDTH), F32)
    wab = jnp.concatenate([jnp.concatenate([decay_up, z], axis=1),
                           jnp.concatenate([z, iclr_up], axis=1)], axis=0).astype(BF16)
    hid = np.arange(2 * LANE) // HEAD_DIM
    ones_bd = jnp.asarray(hid[:, None] == hid[None, :], BF16)
    ti = np.arange(tt)
    tril = jnp.asarray((ti[:, None] >= ti[None, :]) & (ti[:, None] // RWKV_CHUNK == ti[None, :] // RWKV_CHUNK), BF16)
    o_rwkv = _rwkv(rw, gr, rowv(shift_mu), rowv(decay_w0), rowv(iclr_a0), wab, rowv(k_k), rowv(k_a),
                   rowv(r_k), rowv(gn_w), rowv(gn_b), ones_bd, tril, batch=batch, seq=seq, tt=tt)

    w_o = w_out.astype(BF16)
    return _out_proj(x2, o_nsa_t, o_rwkv, w_o[:NSA_WIDTH], w_o[NSA_WIDTH:], rowv(final_g), tm=tm)


def kernel(x, norm_g, w_in, cmp_pos_k, cmp_w1_k, cmp_w2_k, cmp_pos_v, cmp_w1_v, cmp_w2_v, shift_mu, decay_w0, decay_up, iclr_a0, iclr_up, k_k, k_a, r_k, gn_w, gn_b, w_out, final_g):
    batch, seq, d = x.shape
    assert d == D_MODEL and norm_g.shape[0] == 1, "single-layer trunk"
    out = _layer(x.reshape(batch * seq, d), norm_g[0], w_in[0], cmp_pos_k[0], cmp_w1_k[0], cmp_w2_k[0],
                 cmp_pos_v[0], cmp_w1_v[0], cmp_w2_v[0], shift_mu[0], decay_w0[0], decay_up[0],
                 iclr_a0[0], iclr_up[0], k_k[0], k_a[0], r_k[0], gn_w[0], gn_b[0], w_out[0],
                 final_g, batch=batch, seq=seq)
    return out.reshape(batch, seq, d)
```

```python
import functools

import numpy as np
import jax
import jax.numpy as jnp
from jax import lax
from jax.experimental import pallas as pl
from jax.experimental.pallas import tpu as pltpu

F32 = jnp.float32
BF16 = jnp.bfloat16

D_MODEL = 1024
HEAD_DIM = 64
NSA_HEADS = 8
NSA_KV = 2
NSA_REP = NSA_HEADS // NSA_KV
RWKV_HEADS = 8
NSA_WIDTH = NSA_HEADS * HEAD_DIM
RWKV_WIDTH = RWKV_HEADS * HEAD_DIM
KV_WIDTH = NSA_KV * HEAD_DIM
ROPE_DIM = HEAD_DIM // 4
ROPE_HALF = ROPE_DIM // 2
ROPE_THETA = 500000.0
CMP_BLOCK = 32
CMP_STRIDE = 16
CMP_HIDDEN = 256
SEL_BLOCK = 64
SEL_TOPK = 8
WINDOW = 512
DECAY_RANK = 64
ICLR_RANK = 64
RWKV_SHIFT_WIDTH = 3 * RWKV_WIDTH + DECAY_RANK + ICLR_RANK
IN_SIZES = (NSA_WIDTH, KV_WIDTH, KV_WIDTH, KV_WIDTH, KV_WIDTH, KV_WIDTH, KV_WIDTH,
            3 * NSA_HEADS, NSA_WIDTH, RWKV_SHIFT_WIDTH, RWKV_WIDTH)
SCALE = HEAD_DIM ** -0.5
RMS_EPS = 1e-6
GN_EPS = 64e-5
NEG_INF = -1e30
FORCE_BONUS = 1e3

LANE = 128
SUBLANE = 8
GATE_ROWS = 16

T_KV = 0
T_RW = T_KV + 4 * KV_WIDTH
T_GR = T_RW + RWKV_SHIFT_WIDTH
T_END = T_GR + RWKV_WIDTH
R_Q = 0
R_VS = R_Q + NSA_WIDTH
R_VW = R_VS + KV_WIDTH
R_GL = R_VW + KV_WIDTH
R_GN = R_GL + NSA_KV * GATE_ROWS
R_END = R_GN + NSA_WIDTH

LOG2E = float(np.log2(np.e))
K_AUG = LANE
V_ROWS = HEAD_DIM + 16
KEY_TILE = 256
RWKV_CHUNK = 64
VMEM_LIMIT = 48 * 1024 * 1024


def _dot(a, b):
    return jnp.dot(a, b, preferred_element_type=F32)


def _dot_nt(a, b):
    return lax.dot_general(a, b, (((1,), (1,)), ((), ())), preferred_element_type=F32)


def _dot_tn(a, b):
    return lax.dot_general(a, b, (((0,), (0,)), ((), ())), preferred_element_type=F32)


def _bmm(a, b):
    return lax.dot_general(a, b, (((2,), (1,)), ((0,), (0,))), preferred_element_type=F32)


def _bmm_nt(a, b):
    return lax.dot_general(a, b, (((2,), (2,)), ((0,), (0,))), preferred_element_type=F32)


def _split3(x):
    hi = x.astype(BF16)
    r1 = x - hi.astype(F32)
    mid = r1.astype(BF16)
    lo = (r1 - mid.astype(F32)).astype(BF16)
    return hi, mid, lo


def _head_sums(x, ones_blk):
    w = ones_blk.shape[0]
    hi = x.astype(BF16)
    lo = (x - hi.astype(F32)).astype(BF16)
    return jnp.concatenate(
        [_dot(hi[:, i:i + w], ones_blk) + _dot(lo[:, i:i + w], ones_blk) for i in range(0, x.shape[1], w)],
        axis=1)


def _x3_dot(w_bf16, x):
    hi, mid, lo = _split3(x)
    return _dot(w_bf16, hi) + _dot(w_bf16, mid) + _dot(w_bf16, lo)


def _sigmoid(x):
    return 1.0 / (1.0 + jnp.exp(-x))


def _rope128(t, ra, rm, rp):
    return t * ra + pltpu.roll(t, LANE - ROPE_HALF, 1) * rm + pltpu.roll(t, ROPE_HALF, 1) * rp


def _in_proj_kernel(x_ref, g_ref, wt_ref, wf_ref, ra_ref, rm_ref, rp_ref, cos_ref, sin_ref, oh_ref,
                    kc_ref, vc_ref, ks_ref, kw_ref, rw_ref, gr_ref,
                    qt_ref, qrt_ref, vst_ref, vwt_ref, gate_ref, gn_ref, cmp_scr):
    x = x_ref[...]
    ms = jnp.mean(x * x, axis=-1, keepdims=True)
    y = (x * lax.rsqrt(ms + RMS_EPS) * g_ref[...]).astype(BF16)
    tm = x.shape[0]

    kv = _dot(y, wt_ref[:, T_KV:T_RW])
    for i, ref in enumerate((kc_ref, vc_ref)):
        cmp_scr[i] = kv[:, i * KV_WIDTH:(i + 1) * KV_WIDTH]
        for tau in range(CMP_STRIDE):
            piece = cmp_scr[i, pl.ds(tau, tm // CMP_STRIDE, stride=CMP_STRIDE), :].astype(BF16)
            for g in range(NSA_KV):
                ref[g, :, tau * HEAD_DIM:(tau + 1) * HEAD_DIM] = piece[:, g * HEAD_DIM:(g + 1) * HEAD_DIM]
    ra, rm, rp = ra_ref[...], rm_ref[...], rp_ref[...]
    for i, ref in ((2, ks_ref), (3, kw_ref)):
        t = _rope128(kv[:, i * LANE:(i + 1) * LANE], ra, rm, rp)
        for g in range(NSA_KV):
            tg = t[:, g * HEAD_DIM:(g + 1) * HEAD_DIM].astype(BF16)
            if ref is ks_ref:
                ref[g] = jnp.concatenate([tg, oh_ref[...]], axis=1)
            else:
                ref[g] = tg
    rw_ref[...] = _dot(y, wt_ref[:, T_RW:T_GR])
    gr = _dot(y, wt_ref[:, T_GR:T_END])
    gr_ref[...] = gr * _sigmoid(gr)

    def proj_t(r0, r1):
        return _dot_nt(wf_ref[r0:r1, :], y)

    qt = proj_t(R_Q, R_VS) * (SCALE * LOG2E)
    cos, sin = cos_ref[...], sin_ref[...]
    qt_ref[...] = qt.astype(BF16)
    for h in range(NSA_HEADS):
        r0 = h * HEAD_DIM
        t1 = qt[r0:r0 + ROPE_HALF]
        t2 = qt[r0 + ROPE_HALF:r0 + ROPE_DIM]
        qrt_ref[r0:r0 + ROPE_DIM, :] = jnp.concatenate(
            [t1 * cos - t2 * sin, t2 * cos + t1 * sin], axis=0).astype(BF16)
        qrt_ref[r0 + ROPE_DIM:r0 + HEAD_DIM, :] = qt[r0 + ROPE_DIM:r0 + HEAD_DIM].astype(BF16)
    vt = proj_t(R_VS, R_GL).astype(BF16)
    ones = jnp.ones((V_ROWS - HEAD_DIM, LANE), BF16)
    for j in range(vt.shape[1] // LANE):
        for i, ref in enumerate((vst_ref, vwt_ref)):
            for g in range(NSA_KV):
                r0 = i * KV_WIDTH + g * HEAD_DIM
                ref[j, g * V_ROWS:(g + 1) * V_ROWS, :] = jnp.concatenate(
                    [vt[r0:r0 + HEAD_DIM, j * LANE:(j + 1) * LANE], ones], axis=0)
    gate_ref[...] = _sigmoid(proj_t(R_GL, R_GN))
    gn = proj_t(R_GN, R_END)
    gn_ref[...] = gn * _sigmoid(gn)


def _in_proj(x2, norm_g, w_t, w_f, tabs, *, seq, tm):
    n = x2.shape[0]
    spt = seq // tm
    ra, rm, rp, cos, sin, onehot = tabs
    hm = lambda w: jax.ShapeDtypeStruct((NSA_KV, n, w), BF16)
    hspec = lambda w: pl.BlockSpec((NSA_KV, tm, w), lambda i: (0, i, 0))
    row = lambda w: pl.BlockSpec((tm, w), lambda i: (i, 0))
    col = lambda r: pl.BlockSpec((r, tm), lambda i: (0, i))
    full = lambda a: pl.BlockSpec(a.shape, lambda i: (0,) * a.ndim)
    tab = lambda w: pl.BlockSpec((tm, w), lambda i: (i % spt, 0))
    tabt = pl.BlockSpec((ROPE_HALF, tm), lambda i: (0, i % spt))
    vtile = pl.BlockSpec((tm // LANE, NSA_KV * V_ROWS, LANE), lambda i: (i, 0, 0))
    vsd = jax.ShapeDtypeStruct((n // LANE, NSA_KV * V_ROWS, LANE), BF16)
    cw = CMP_STRIDE * HEAD_DIM
    cspec = pl.BlockSpec((NSA_KV, tm // CMP_STRIDE, cw), lambda i: (0, i, 0))
    csd = jax.ShapeDtypeStruct((NSA_KV, n // CMP_STRIDE, cw), BF16)
    return pl.pallas_call(
        _in_proj_kernel,
        grid=(n // tm,),
        in_specs=[row(D_MODEL), full(norm_g), full(w_t), full(w_f), tab(LANE), tab(LANE), tab(LANE),
                  tabt, tabt, tab(K_AUG - HEAD_DIM)],
        out_specs=[cspec, cspec, hspec(K_AUG), hspec(HEAD_DIM), row(RWKV_SHIFT_WIDTH), row(RWKV_WIDTH),
                   col(NSA_WIDTH), col(NSA_WIDTH), vtile, vtile,
                   col(NSA_KV * GATE_ROWS), col(NSA_WIDTH)],
        out_shape=[csd, csd, hm(K_AUG), hm(HEAD_DIM),
                   jax.ShapeDtypeStruct((n, RWKV_SHIFT_WIDTH), F32),
                   jax.ShapeDtypeStruct((n, RWKV_WIDTH), F32),
                   jax.ShapeDtypeStruct((NSA_WIDTH, n), BF16),
                   jax.ShapeDtypeStruct((NSA_WIDTH, n), BF16),
                   vsd, vsd,
                   jax.ShapeDtypeStruct((NSA_KV * GATE_ROWS, n), F32),
                   jax.ShapeDtypeStruct((NSA_WIDTH, n), F32)],
        scratch_shapes=[pltpu.VMEM((2, tm, KV_WIDTH), F32)],
        compiler_params=pltpu.CompilerParams(dimension_semantics=("arbitrary",),
                                             vmem_limit_bytes=VMEM_LIMIT),
        name="in_proj",
    )(x2, norm_g, w_t, w_f, ra, rm, rp, cos, sin, onehot)


def _compress_kernel(kc_ref, vc_ref, pk_ref, w1k_ref, w2k_ref, pv_ref, w1v_ref, w2vt_ref,
                     ko_ref, vo_ref):
    half = CMP_STRIDE * HEAD_DIM

    def hidden(c_ref, pos_ref, w1_ref):
        c = c_ref[0, 0]
        za = _dot(c, w1_ref[0:half, :])
        zb = _dot(c, w1_ref[half:2 * half, :])
        pos = jnp.broadcast_to(pos_ref[...], (SUBLANE, 2 * half)).astype(BF16)
        pv = _dot(pos, w1_ref[...])[0:1, :]
        hid = za + pltpu.roll(zb, c.shape[0] - 1, 0) + pv
        return (hid * _sigmoid(hid)).astype(BF16)

    ko_ref[0, 0] = _dot(hidden(kc_ref, pk_ref, w1k_ref), w2k_ref[...]).astype(BF16)
    vo_ref[0, 0] = _dot_nt(w2vt_ref[...], hidden(vc_ref, pv_ref, w1v_ref)).astype(BF16)


def _compress(kc_r, vc_r, pk, w1k, w2k, pv, w1v, w2vt):
    g, b, nch, width = kc_r.shape
    blk = pl.BlockSpec((1, 1, nch, width), lambda i, j: (i, j, 0, 0))
    full = lambda a: pl.BlockSpec(a.shape, lambda i, j: (0,) * a.ndim)
    return pl.pallas_call(
        _compress_kernel,
        grid=(g, b),
        in_specs=[blk, blk, full(pk), full(w1k), full(w2k), full(pv), full(w1v), full(w2vt)],
        out_specs=[pl.BlockSpec((1, 1, nch, HEAD_DIM), lambda i, j: (i, j, 0, 0)),
                   pl.BlockSpec((1, 1, HEAD_DIM, nch), lambda i, j: (i, j, 0, 0))],
        out_shape=[jax.ShapeDtypeStruct((g, b, nch, HEAD_DIM), BF16),
                   jax.ShapeDtypeStruct((g, b, HEAD_DIM, nch), BF16)],
        compiler_params=pltpu.CompilerParams(dimension_semantics=("arbitrary", "arbitrary"),
                                             vmem_limit_bytes=VMEM_LIMIT),
        name="compress",
    )(kc_r, vc_r, pk, w1k, w2k, pv, w1v, w2vt)


def _nsa_kernel(qt_ref, qrt_ref, kc_ref, vct_ref, ks_ref, vst_ref, kw_ref, vwt_ref,
                gate_ref, gn_ref, mt_ref, o_ref, *, tq, seq):
    tk = KEY_TILE
    qi = pl.program_id(2)
    q0 = qi * tq
    nl = NSA_REP * tq
    n_sel = seq // SEL_BLOCK
    ncp = kc_ref.shape[2]

    def heads_on_lanes(ref):
        return jnp.concatenate([ref[r * HEAD_DIM:(r + 1) * HEAD_DIM, :] for r in range(NSA_REP)], axis=1)

    def tile4(a):
        return jnp.concatenate([a] * NSA_REP, axis=1)

    s = _dot(kc_ref[0, 0], heads_on_lanes(qt_ref))
    t_c = q0 + lax.broadcasted_iota(jnp.int32, (ncp, tq), 1)
    c_c = lax.broadcasted_iota(jnp.int32, (ncp, tq), 0)
    cmask = tile4((c_c * CMP_STRIDE + (CMP_BLOCK - 1)) <= t_c)
    s = jnp.where(cmask, s, NEG_INF)
    m = jnp.max(s, axis=0, keepdims=True)
    e = jnp.where(cmask, jnp.exp2(s - m), 0.0)
    den = jnp.sum(e, axis=0, keepdims=True)
    p = e * (1.0 / jnp.where(den > 0.0, den, 1.0))
    o_cmp = _dot(vct_ref[0, 0], p.astype(BF16))

    psum = p[:, 0:tq]
    for r in range(1, NSA_REP):
        psum = psum + p[:, r * tq:(r + 1) * tq]
    imp = _x3_dot(mt_ref[...], psum)
    j = lax.broadcasted_iota(jnp.int32, (n_sel, tq), 0)
    t = q0 + lax.broadcasted_iota(jnp.int32, (n_sel, tq), 1)
    tb = t // SEL_BLOCK
    forced = (j == 0) | (j == tb) | (j == tb - 1)
    val = jnp.where(j <= tb, imp + jnp.where(forced, FORCE_BONUS, 0.0), -1.0)
    cnt = jnp.zeros((n_sel, tq), F32)
    for i in range(n_sel):
        vi = val[i:i + 1, :]
        ge = jnp.where(vi >= val, 1.0, 0.0)
        gt = jnp.where(vi > val, 1.0, 0.0)
        cnt = cnt + jnp.where(j > i, ge, gt)
    sel_bias = jnp.where(cnt < float(SEL_TOPK), 0.0, NEG_INF).astype(BF16)

    qr = heads_on_lanes(qrt_ref)
    qr_sel = jnp.concatenate([qr, tile4(sel_bias), jnp.zeros((K_AUG - HEAD_DIM - n_sel, nl), BF16)], axis=0)
    k_s = lax.broadcasted_iota(jnp.int32, (tk, tq), 0)
    t_l = q0 + lax.broadcasted_iota(jnp.int32, (tk, tq), 1)

    def update(carry, k_ref, vt_ref, q_op, kts, keeps):
        m_i, acc = carry

        def scores(i):
            sc = _dot(k_ref[0, pl.ds(pl.multiple_of(kts[i] * tk, tk), tk), :], q_op)
            if keeps[i] is None:
                return sc
            return jnp.concatenate([jnp.where(keeps[i], sc[:, r * tq:(r + 1) * tq], NEG_INF)
                                    for r in range(NSA_REP)], axis=1)

        sc_next = scores(0)
        for i, kt in enumerate(kts):
            sc = sc_next
            if i + 1 < len(kts):
                sc_next = scores(i + 1)
            m_n = jnp.maximum(m_i, jnp.max(sc, axis=0, keepdims=True))
            vtb = jnp.concatenate([vt_ref[kt * (tk // LANE) + jj] for jj in range(tk // LANE)], axis=1)
            acc = jnp.exp2(m_i - m_n) * acc + _dot(vtb, jnp.exp2(sc - m_n).astype(BF16))
            m_i = m_n
        return m_i, acc

    def last_tiles(carry, k_ref, vt_ref, q_op, low_keep, spare_tile):
        kts, keeps = [], []
        for back, keep_fn in ((0, lambda d: d >= 0), (1, None), (2, low_keep)):
            exists = a >= back
            kt = jnp.where(exists, a - back, 0 if spare_tile is None else spare_tile)
            d = t_l - (kt * tk + k_s)
            kts.append(kt)
            if keep_fn is not None:
                keeps.append(keep_fn(d) & exists)
            else:
                keeps.append(exists if spare_tile is None else None)
        return update(carry, k_ref, vt_ref, q_op, kts, keeps)

    init = (jnp.full((1, nl), NEG_INF, F32), jnp.zeros((V_ROWS, nl), F32))
    a = (q0 + tq - 1) // tk
    c_win = last_tiles(init, kw_ref, vwt_ref, qr, lambda d: d < WINDOW, None)
    n_old = jnp.maximum(a - WINDOW // tk, 0)
    c_sel = lax.fori_loop(0, n_old % 2, lambda kt, c: update(c, ks_ref, vst_ref, qr_sel, [kt], [None]), init)
    c_sel = lax.fori_loop(0, n_old // 2,
                          lambda i, c: update(c, ks_ref, vst_ref, qr_sel,
                                              [n_old % 2 + 2 * i, n_old % 2 + 2 * i + 1], [None, None]), c_sel)
    assert (seq // tk - 1) * (tk // SEL_BLOCK) >= SEL_TOPK and WINDOW // SEL_BLOCK <= SEL_TOPK
    c_sel = last_tiles(c_sel, ks_ref, vst_ref, qr_sel, None, seq // tk - 1)
    o_sel = c_sel[1][0:HEAD_DIM] * (1.0 / c_sel[1][HEAD_DIM:HEAD_DIM + 1])
    o_win = c_win[1][0:HEAD_DIM] * (1.0 / c_win[1][HEAD_DIM:HEAD_DIM + 1])

    gates = gate_ref[...]
    for r in range(NSA_REP):
        ls = slice(r * tq, (r + 1) * tq)
        o = (gates[3 * r:3 * r + 1, :] * o_cmp[:, ls] + gates[3 * r + 1:3 * r + 2, :] * o_sel[:, ls]
             + gates[3 * r + 2:3 * r + 3, :] * o_win[:, ls])
        rs = slice(r * HEAD_DIM, (r + 1) * HEAD_DIM)
        o_ref[rs, :] = (o * gn_ref[rs, :]).astype(BF16)


def _nsa(qt, qrt, kcmp, vcmpt, ksh, vst, kwh, vwt, gates, gnt, mt, *, batch, seq, tq):
    n = batch * seq
    nq = seq // tq
    ncp = kcmp.shape[2]
    grp = NSA_REP * HEAD_DIM
    qspec = pl.BlockSpec((grp, tq), lambda b, g, i: (g, b * nq + i))
    kspec = lambda w: pl.BlockSpec((1, seq, w), lambda b, g, i: (g, b, 0))
    vspec = pl.BlockSpec((seq // LANE, V_ROWS, LANE), lambda b, g, i: (b, g, 0))
    kern = functools.partial(_nsa_kernel, tq=tq, seq=seq)
    return pl.pallas_call(
        kern,
        grid=(batch, NSA_KV, nq),
        in_specs=[qspec, qspec,
                  pl.BlockSpec((1, 1, ncp, HEAD_DIM), lambda b, g, i: (g, b, 0, 0)),
                  pl.BlockSpec((1, 1, HEAD_DIM, ncp), lambda b, g, i: (g, b, 0, 0)),
                  kspec(K_AUG), vspec, kspec(HEAD_DIM), vspec,
                  pl.BlockSpec((GATE_ROWS, tq), lambda b, g, i: (g, b * nq + i)),
                  qspec,
                  pl.BlockSpec(mt.shape, lambda b, g, i: (0, 0))],
        out_specs=qspec,
        out_shape=jax.ShapeDtypeStruct((NSA_WIDTH, n), BF16),
        compiler_params=pltpu.CompilerParams(
            dimension_semantics=("arbitrary", "arbitrary", "arbitrary"),
            vmem_limit_bytes=VMEM_LIMIT),
        name="nsa",
    )(qt, qrt, kcmp, vcmpt, ksh, vst, kwh, vwt, gates, gnt, mt)


def _rwkv_kernel(p_ref, gr_ref, mu_ref, w0_ref, a0_ref, wab_ref, kk_ref, ka_ref, rk_ref,
                 gw_ref, gb_ref, ones_ref, tril_ref, o_ref, st_ref, carry_ref, *, tt):
    c = RWKV_CHUNK
    hd = HEAD_DIM
    nh = RWKV_HEADS
    nc = tt // c
    step = pl.program_id(1)

    @pl.when(step == 0)
    def _():
        st_ref[...] = jnp.zeros_like(st_ref)
        carry_ref[...] = jnp.zeros_like(carry_ref)

    p = p_ref[...]
    row = lax.broadcasted_iota(jnp.int32, p.shape, 0)
    prev = jnp.where(row == 0, carry_ref[...], pltpu.roll(p, 1, 0))
    carry_ref[...] = p[tt - 1:tt, :]
    ps = p + mu_ref[...] * (prev - p)
    r = ps[:, 0:RWKV_WIDTH]
    k = ps[:, RWKV_WIDTH:2 * RWKV_WIDTH]
    v = ps[:, 2 * RWKV_WIDTH:3 * RWKV_WIDTH]
    lora = ps[:, 3 * RWKV_WIDTH:]
    lane = lax.broadcasted_iota(jnp.int32, lora.shape, 1)
    feat = jnp.where(lane < DECAY_RANK, jnp.tanh(lora), lora).astype(BF16)
    up = _dot(feat, wab_ref[...])
    w = w0_ref[...] + up[:, 0:RWKV_WIDTH]
    lw = _sigmoid(w) * (-float(np.exp(-0.5)))
    a = _sigmoid(a0_ref[...] + up[:, RWKV_WIDTH:])
    ones_bd = ones_ref[...]
    kk = k * kk_ref[...]
    kkn = kk * lax.rsqrt(jnp.maximum(_head_sums(kk * kk, ones_bd), 1e-24))
    k2 = k * (1.0 + (a - 1.0) * ka_ref[...])
    alpha = -kkn
    beta = kkn * a
    bonus = _head_sums(r * k2 * rk_ref[...], ones_bd) * v

    cum = _x3_dot(tril_ref[...], lw)
    cend = jnp.concatenate(
        [jnp.broadcast_to(cum[(ch + 1) * c - 1:(ch + 1) * c, :], (c, RWKV_WIDTH)) for ch in range(nc)], axis=0)
    e_neg = jnp.exp(-cum)
    e_end = jnp.exp(cend - cum)
    at = alpha * jnp.exp(cum - lw)
    bt = beta * e_neg
    kt = k2 * e_neg
    rt = r * jnp.exp(cum)
    bh = beta * e_end
    kh = k2 * e_end
    pc = jnp.exp(cend)

    npair = RWKV_WIDTH // LANE

    def pairs(x):
        return jnp.stack([x[ch * c:(ch + 1) * c, j * LANE:(j + 1) * LANE]
                          for ch in range(nc) for j in range(npair)], axis=0)

    at_p, rt_p, bt_p, kt_p, v_p, bh_p, kh_p = (pairs(t) for t in (at, rt, bt, kt, v, bh, kh))
    pc_p = jnp.stack([pc[ch * c:ch * c + 1, j * LANE:(j + 1) * LANE]
                      for ch in range(nc) for j in range(npair)], axis=0)
    lane_c = lax.broadcasted_iota(jnp.int32, (1, c, LANE), 2)
    row_c = lax.broadcasted_iota(jnp.int32, (1, c, LANE), 1)
    even_c = lane_c < hd
    col_c = jnp.where(even_c, lane_c, lane_c - hd)
    low_s = row_c > col_c
    low_i = row_c >= col_c
    lane_2c = lax.broadcasted_iota(jnp.int32, (1, 2 * c, LANE), 2)
    row_2c = lax.broadcasted_iota(jnp.int32, (1, 2 * c, LANE), 1)
    even_2c = lane_2c < hd
    on_bd = (row_2c < hd) == even_2c
    zero_c = jnp.zeros((1, c, LANE), BF16)

    def bd(x):
        xb = x.astype(BF16)
        return jnp.concatenate([jnp.where(even_c, xb, zero_c), jnp.where(even_c, zero_c, xb)], axis=1)

    def abd(x):
        xb = x.astype(BF16)
        return jnp.concatenate([jnp.where(even_c, zero_c, xb), jnp.where(even_c, xb, zero_c)], axis=1)

    la = jnp.concatenate([at_p, rt_p], axis=1).astype(BF16)
    zero_2c = jnp.zeros((1, 2 * c, LANE), BF16)
    r_e = _bmm_nt(jnp.where(even_2c, la, zero_2c), jnp.concatenate([bt_p, kt_p], axis=1).astype(BF16))
    r_o = _bmm_nt(jnp.where(even_2c, zero_2c, la), jnp.concatenate([kt_p, bt_p], axis=1).astype(BF16))
    nab = jnp.where(low_s, jnp.where(even_c, r_e[:, 0:c], r_o[:, 0:c]), 0.0)
    aak_sw = jnp.where(low_s, jnp.where(even_c, r_o[:, 0:c], r_e[:, 0:c]), 0.0).astype(BF16)
    arb = jnp.where(low_i, jnp.where(even_c, r_e[:, c:], r_o[:, c:]), 0.0).astype(BF16)
    ark_sw = jnp.where(low_i, jnp.where(even_c, r_o[:, c:], r_e[:, c:]), 0.0).astype(BF16)
    tinv = jnp.where(row_c == col_c, 1.0, 0.0) + nab
    npow = _bmm(nab.astype(BF16), bd(nab))
    n_dbl = 5
    for it in range(n_dbl):
        nbd = bd(npow)
        if it + 1 < n_dbl:
            res = _bmm(jnp.concatenate([tinv, npow], axis=1).astype(BF16), nbd)
            tinv = tinv + res[:, 0:c]
            npow = res[:, c:]
        else:
            tinv = tinv + _bmm(tinv.astype(BF16), nbd)
    av = _bmm(jnp.concatenate([aak_sw, ark_sw], axis=1), abd(v_p))
    tx = _bmm(tinv.astype(BF16), jnp.concatenate([bd(av[:, 0:c]), bd(at_p)], axis=2))
    u0, ta = tx[:, :, 0:LANE], tx[:, :, LANE:]
    ax = _bmm(arb, jnp.concatenate([bd(ta), bd(u0)], axis=2))
    rq = rt_p + ax[:, :, 0:LANE]
    y0 = ax[:, :, LANE:] + av[:, c:]
    v_b = v_p.astype(BF16)
    w_f = jnp.concatenate([jnp.concatenate([ta, u0], axis=2).astype(BF16),
                           jnp.concatenate([jnp.zeros_like(v_b), v_b], axis=2)], axis=1)
    gh = _bmm(jnp.concatenate([jnp.swapaxes(bh_p, 1, 2), jnp.swapaxes(kh_p, 1, 2)], axis=2).astype(BF16), w_f)
    g_bd = jnp.where(on_bd, gh[:, :, 0:LANE], 0.0) + jnp.where(row_2c == lane_2c, pc_p, 0.0)
    h_bd = jnp.where(on_bd, gh[:, :, LANE:], 0.0)
    lhs = jnp.concatenate([rq, g_bd], axis=1).astype(BF16)

    st = st_ref[...]
    ys = []
    for ch in range(nc):
        sl = slice(ch * npair, (ch + 1) * npair)
        res = _bmm(lhs[sl], st.astype(BF16))
        yc = res[:, 0:c, :] + y0[sl]
        st = res[:, c:, :] + h_bd[sl]
        ys.append(jnp.concatenate([yc[j] for j in range(npair)], axis=1))
    st_ref[...] = st
    y = jnp.concatenate(ys, axis=0) if nc > 1 else ys[0]

    inv_hd = 1.0 / hd
    mean = _head_sums(y, ones_bd) * inv_hd
    ycen = y - mean
    var = _head_sums(ycen * ycen, ones_bd) * inv_hd
    yn = ycen * lax.rsqrt(var + GN_EPS) * gw_ref[...] + gb_ref[...]
    o_ref[...] = ((yn + bonus) * gr_ref[...]).astype(BF16)


def _rwkv(rw, gr, mu, w0, a0, wab, kk, ka, rk, gw, gb, ones_bd, tril, *, batch, seq, tt):
    n = batch * seq
    ns = seq // tt
    row = lambda w: pl.BlockSpec((tt, w), lambda b, i: (b * ns + i, 0))
    full = lambda a: pl.BlockSpec(a.shape, lambda b, i: (0,) * a.ndim)
    kern = functools.partial(_rwkv_kernel, tt=tt)
    consts = (mu, w0, a0, wab, kk, ka, rk, gw, gb, ones_bd, tril)
    return pl.pallas_call(
        kern,
        grid=(batch, ns),
        in_specs=[row(RWKV_SHIFT_WIDTH), row(RWKV_WIDTH)] + [full(a) for a in consts],
        out_specs=row(RWKV_WIDTH),
        out_shape=jax.ShapeDtypeStruct((n, RWKV_WIDTH), BF16),
        scratch_shapes=[pltpu.VMEM((RWKV_WIDTH // LANE, LANE, LANE), F32),
                        pltpu.VMEM((1, RWKV_SHIFT_WIDTH), F32)],
        compiler_params=pltpu.CompilerParams(dimension_semantics=("arbitrary", "arbitrary"),
                                             vmem_limit_bytes=VMEM_LIMIT),
        name="rwkv",
    )(rw, gr, *consts)


def _out_kernel(x_ref, ont_ref, or_ref, wn_ref, wr_ref, g_ref, o_ref):
    h = x_ref[...] + _dot_tn(ont_ref[...], wn_ref[...]) + _dot(or_ref[...], wr_ref[...])
    ms = jnp.mean(h * h, axis=-1, keepdims=True)
    o_ref[...] = h * lax.rsqrt(ms + RMS_EPS) * g_ref[...]


def _out_proj(x2, o_nsa_t, o_rwkv, wn, wr, final_g, *, tm):
    n = x2.shape[0]
    row = lambda w: pl.BlockSpec((tm, w), lambda i: (i, 0))
    full = lambda a: pl.BlockSpec(a.shape, lambda i: (0,) * a.ndim)
    return pl.pallas_call(
        _out_kernel,
        grid=(n // tm,),
        in_specs=[row(D_MODEL), pl.BlockSpec((NSA_WIDTH, tm), lambda i: (0, i)), row(RWKV_WIDTH),
                  full(wn), full(wr), full(final_g)],
        out_specs=row(D_MODEL),
        out_shape=jax.ShapeDtypeStruct((n, D_MODEL), F32),
        compiler_params=pltpu.CompilerParams(dimension_semantics=("arbitrary",),
                                             vmem_limit_bytes=VMEM_LIMIT),
        name="out_proj",
    )(x2, o_nsa_t, o_rwkv, wn, wr, final_g)


def _rope_tables(seq):
    inv = ROPE_THETA ** (-np.arange(ROPE_HALF, dtype=np.float64) / ROPE_HALF)
    ang = np.arange(seq, dtype=np.float64)[:, None] * inv[None, :]
    cos, sin = np.cos(ang), np.sin(ang)
    ra = np.ones((seq, HEAD_DIM)); rm = np.zeros((seq, HEAD_DIM)); rp = np.zeros((seq, HEAD_DIM))
    ra[:, :ROPE_HALF] = cos; ra[:, ROPE_HALF:ROPE_DIM] = cos
    rm[:, :ROPE_HALF] = -sin
    rp[:, ROPE_HALF:ROPE_DIM] = sin
    rep = lambda t: jnp.asarray(np.tile(t, (1, LANE // HEAD_DIM)), F32)
    assert seq // SEL_BLOCK <= K_AUG - HEAD_DIM
    onehot = np.zeros((seq, K_AUG - HEAD_DIM))
    onehot[np.arange(seq), np.arange(seq) // SEL_BLOCK] = 1.0
    return (rep(ra), rep(rm), rep(rp), jnp.asarray(cos.T, F32), jnp.asarray(sin.T, F32),
            jnp.asarray(onehot, BF16))


def _cmp_to_sel_t(n_cmp_pad, n_sel):
    n_cmp = n_cmp_pad - 1
    c0 = np.arange(n_cmp)[:, None] * CMP_STRIDE
    s0 = np.arange(n_sel)[None, :] * SEL_BLOCK
    ov = np.clip(np.minimum(c0 + CMP_BLOCK, s0 + SEL_BLOCK) - np.maximum(c0, s0), 0, None) / CMP_BLOCK
    mt = np.zeros((n_sel, n_cmp_pad))
    mt[:, :n_cmp] = ov.T
    return jnp.asarray(mt, BF16)


def _prep_w_in(w_in):
    idx = np.cumsum(IN_SIZES)[:-1].tolist()
    q, kc, vc, ks, vs, kw, vw, gl, gn, rw, gr = jnp.split(w_in, idx, axis=1)
    w_t = jnp.concatenate([kc, vc, ks, kw, rw, gr], axis=1).astype(BF16)
    per_g = 3 * NSA_REP
    pad = jnp.zeros((D_MODEL, GATE_ROWS - per_g), w_in.dtype)
    gl_p = [t for g in range(NSA_KV) for t in (gl[:, g * per_g:(g + 1) * per_g], pad)]
    w_f = jnp.concatenate([q, vs, vw] + gl_p + [gn], axis=1).T.astype(BF16)
    return w_t, w_f


def _layer(x2, norm_g, w_in, cmp_pos_k, cmp_w1_k, cmp_w2_k, cmp_pos_v, cmp_w1_v, cmp_w2_v,
           shift_mu, decay_w0, decay_up, iclr_a0, iclr_up, k_k, k_a, r_k, gn_w, gn_b, w_out,
           final_g, *, batch, seq):
    tm = 256
    tq = 256
    tt = 256
    assert WINDOW == 2 * KEY_TILE and seq % KEY_TILE == 0 and KEY_TILE % tq == 0
    nch = seq // CMP_STRIDE
    n_sel = seq // SEL_BLOCK
    rowv = lambda t: t.reshape(1, -1).astype(F32)

    w_t, w_f = _prep_w_in(w_in)
    (kch, vch, ksh, kwh, rw, gr, qt, qrt, vst, vwt, gates, gnt) = _in_proj(
        x2, rowv(norm_g), w_t, w_f, _rope_tables(seq), seq=seq, tm=tm)

    chunks = lambda t: t.reshape(NSA_KV, batch, nch, CMP_STRIDE * HEAD_DIM)
    kcmp, vcmpt = _compress(chunks(kch), chunks(vch),
                            rowv(cmp_pos_k), cmp_w1_k.astype(BF16), cmp_w2_k.astype(BF16),
                            rowv(cmp_pos_v), cmp_w1_v.astype(BF16), cmp_w2_v.T.astype(BF16))

    o_nsa_t = _nsa(qt, qrt, kcmp, vcmpt, ksh, vst, kwh, vwt, gates, gnt,
                   _cmp_to_sel_t(nch, n_sel), batch=batch, seq=seq, tq=tq)

    z = jnp.zeros((DECAY_RANK, RWKV_WIDTH), F32)
    wab = jnp.concatenate([jnp.concatenate([decay_up, z], axis=1),
                           jnp.concatenate([z, iclr_up], axis=1)], axis=0).astype(BF16)
    hid = np.arange(2 * LANE) // HEAD_DIM
    ones_bd = jnp.asarray(hid[:, None] == hid[None, :], BF16)
    ti = np.arange(tt)
    tril = jnp.asarray((ti[:, None] >= ti[None, :]) & (ti[:, None] // RWKV_CHUNK == ti[None, :] // RWKV_CHUNK), BF16)
    o_rwkv = _rwkv(rw, gr, rowv(shift_mu), rowv(decay_w0), rowv(iclr_a0), wab, rowv(k_k), rowv(k_a),
                   rowv(r_k), rowv(gn_w), rowv(gn_b), ones_bd, tril, batch=batch, seq=seq, tt=tt)

    w_o = w_out.astype(BF16)
    return _out_proj(x2, o_nsa_t, o_rwkv, w_o[:NSA_WIDTH], w_o[NSA_WIDTH:], rowv(final_g), tm=2 * tm)


def kernel(x, norm_g, w_in, cmp_pos_k, cmp_w1_k, cmp_w2_k, cmp_pos_v, cmp_w1_v, cmp_w2_v, shift_mu, decay_w0, decay_up, iclr_a0, iclr_up, k_k, k_a, r_k, gn_w, gn_b, w_out, final_g):
    batch, seq, d = x.shape
    assert d == D_MODEL and norm_g.shape[0] == 1, "single-layer trunk"
    out = _layer(x.reshape(batch * seq, d), norm_g[0], w_in[0], cmp_pos_k[0], cmp_w1_k[0], cmp_w2_k[0],
                 cmp_pos_v[0], cmp_w1_v[0], cmp_w2_v[0], shift_mu[0], decay_w0[0], decay_up[0],
                 iclr_a0[0], iclr_up[0], k_k[0], k_a[0], r_k[0], gn_w[0], gn_b[0], w_out[0],
                 final_g, batch=batch, seq=seq)
    return out.reshape(batch, seq, d)
```

```python
import functools

import numpy as np
import jax
import jax.numpy as jnp
from jax import lax
from jax.experimental import pallas as pl
from jax.experimental.pallas import tpu as pltpu

F32 = jnp.float32
BF16 = jnp.bfloat16

D_MODEL = 1024
HEAD_DIM = 64
NSA_HEADS = 8
NSA_KV = 2
NSA_REP = NSA_HEADS // NSA_KV
RWKV_HEADS = 8
NSA_WIDTH = NSA_HEADS * HEAD_DIM
RWKV_WIDTH = RWKV_HEADS * HEAD_DIM
KV_WIDTH = NSA_KV * HEAD_DIM
ROPE_DIM = HEAD_DIM // 4
ROPE_HALF = ROPE_DIM // 2
ROPE_THETA = 500000.0
CMP_BLOCK = 32
CMP_STRIDE = 16
CMP_HIDDEN = 256
SEL_BLOCK = 64
SEL_TOPK = 8
WINDOW = 512
DECAY_RANK = 64
ICLR_RANK = 64
RWKV_SHIFT_WIDTH = 3 * RWKV_WIDTH + DECAY_RANK + ICLR_RANK
IN_SIZES = (NSA_WIDTH, KV_WIDTH, KV_WIDTH, KV_WIDTH, KV_WIDTH, KV_WIDTH, KV_WIDTH,
            3 * NSA_HEADS, NSA_WIDTH, RWKV_SHIFT_WIDTH, RWKV_WIDTH)
SCALE = HEAD_DIM ** -0.5
RMS_EPS = 1e-6
GN_EPS = 64e-5
NEG_INF = -1e30
FORCE_BONUS = 1e3

LANE = 128
SUBLANE = 8
GATE_ROWS = 16

T_KV = 0
T_RW = T_KV + 4 * KV_WIDTH
T_GR = T_RW + RWKV_SHIFT_WIDTH
T_END = T_GR + RWKV_WIDTH
R_Q = 0
R_VS = R_Q + NSA_WIDTH
R_VW = R_VS + KV_WIDTH
R_GL = R_VW + KV_WIDTH
R_GN = R_GL + NSA_KV * GATE_ROWS
R_END = R_GN + NSA_WIDTH

LOG2E = float(np.log2(np.e))
K_AUG = LANE
V_ROWS = HEAD_DIM + 16
KEY_TILE = 256
RWKV_CHUNK = 64
VMEM_LIMIT = 48 * 1024 * 1024


def _dot(a, b):
    return jnp.dot(a, b, preferred_element_type=F32)


def _dot_nt(a, b):
    return lax.dot_general(a, b, (((1,), (1,)), ((), ())), preferred_element_type=F32)


def _dot_tn(a, b):
    return lax.dot_general(a, b, (((0,), (0,)), ((), ())), preferred_element_type=F32)


def _bmm(a, b):
    return lax.dot_general(a, b, (((2,), (1,)), ((0,), (0,))), preferred_element_type=F32)


def _bmm_nt(a, b):
    return lax.dot_general(a, b, (((2,), (2,)), ((0,), (0,))), preferred_element_type=F32)


def _split3(x):
    hi = x.astype(BF16)
    r1 = x - hi.astype(F32)
    mid = r1.astype(BF16)
    lo = (r1 - mid.astype(F32)).astype(BF16)
    return hi, mid, lo


def _head_sums(x, ones_blk):
    w = ones_blk.shape[0]
    hi = x.astype(BF16)
    lo = (x - hi.astype(F32)).astype(BF16)
    return jnp.concatenate(
        [_dot(hi[:, i:i + w], ones_blk) + _dot(lo[:, i:i + w], ones_blk) for i in range(0, x.shape[1], w)],
        axis=1)


def _x3_dot(w_bf16, x):
    hi, mid, lo = _split3(x)
    return _dot(w_bf16, hi) + _dot(w_bf16, mid) + _dot(w_bf16, lo)


def _sigmoid(x):
    return 1.0 / (1.0 + jnp.exp(-x))


def _interleave(*gens):
    live = list(gens)
    while live:
        for g in list(live):
            try:
                next(g)
            except StopIteration:
                live.remove(g)


def _rope128(t, ra, rm, rp):
    return t * ra + pltpu.roll(t, LANE - ROPE_HALF, 1) * rm + pltpu.roll(t, ROPE_HALF, 1) * rp


def _in_proj_kernel(x_ref, g_ref, wt_ref, wf_ref, ra_ref, rm_ref, rp_ref, cos_ref, sin_ref, oh_ref,
                    kc_ref, vc_ref, ks_ref, kw_ref, rw_ref, gr_ref,
                    qt_ref, qrt_ref, vst_ref, vwt_ref, gate_ref, gn_ref, cmp_scr):
    x = x_ref[...]
    ms = jnp.mean(x * x, axis=-1, keepdims=True)
    y = (x * lax.rsqrt(ms + RMS_EPS) * g_ref[...]).astype(BF16)
    tm = x.shape[0]

    kv = _dot(y, wt_ref[:, T_KV:T_RW])
    for i, ref in enumerate((kc_ref, vc_ref)):
        cmp_scr[i] = kv[:, i * KV_WIDTH:(i + 1) * KV_WIDTH]
        for tau in range(CMP_STRIDE):
            piece = cmp_scr[i, pl.ds(tau, tm // CMP_STRIDE, stride=CMP_STRIDE), :].astype(BF16)
            for g in range(NSA_KV):
                ref[g, :, tau * HEAD_DIM:(tau + 1) * HEAD_DIM] = piece[:, g * HEAD_DIM:(g + 1) * HEAD_DIM]
    ra, rm, rp = ra_ref[...], rm_ref[...], rp_ref[...]
    for i, ref in ((2, ks_ref), (3, kw_ref)):
        t = _rope128(kv[:, i * LANE:(i + 1) * LANE], ra, rm, rp)
        for g in range(NSA_KV):
            tg = t[:, g * HEAD_DIM:(g + 1) * HEAD_DIM].astype(BF16)
            if ref is ks_ref:
                ref[g] = jnp.concatenate([tg, oh_ref[...]], axis=1)
            else:
                ref[g] = tg
    rw_ref[...] = _dot(y, wt_ref[:, T_RW:T_GR])
    gr = _dot(y, wt_ref[:, T_GR:T_END])
    gr_ref[...] = gr * _sigmoid(gr)

    def proj_t(r0, r1):
        return _dot_nt(wf_ref[r0:r1, :], y)

    qt = proj_t(R_Q, R_VS) * (SCALE * LOG2E)
    cos, sin = cos_ref[...], sin_ref[...]
    qt_ref[...] = qt.astype(BF16)
    for h in range(NSA_HEADS):
        r0 = h * HEAD_DIM
        t1 = qt[r0:r0 + ROPE_HALF]
        t2 = qt[r0 + ROPE_HALF:r0 + ROPE_DIM]
        qrt_ref[r0:r0 + ROPE_DIM, :] = jnp.concatenate(
            [t1 * cos - t2 * sin, t2 * cos + t1 * sin], axis=0).astype(BF16)
        qrt_ref[r0 + ROPE_DIM:r0 + HEAD_DIM, :] = qt[r0 + ROPE_DIM:r0 + HEAD_DIM].astype(BF16)
    vt = proj_t(R_VS, R_GL).astype(BF16)
    ones = jnp.ones((V_ROWS - HEAD_DIM, LANE), BF16)
    for j in range(vt.shape[1] // LANE):
        for i, ref in enumerate((vst_ref, vwt_ref)):
            for g in range(NSA_KV):
                r0 = i * KV_WIDTH + g * HEAD_DIM
                ref[j, g * V_ROWS:(g + 1) * V_ROWS, :] = jnp.concatenate(
                    [vt[r0:r0 + HEAD_DIM, j * LANE:(j + 1) * LANE], ones], axis=0)
    gate_ref[...] = _sigmoid(proj_t(R_GL, R_GN))
    gn = proj_t(R_GN, R_END)
    gn_ref[...] = gn * _sigmoid(gn)


def _in_proj(x2, norm_g, w_t, w_f, tabs, *, seq, tm):
    n = x2.shape[0]
    spt = seq // tm
    ra, rm, rp, cos, sin, onehot = tabs
    hm = lambda w: jax.ShapeDtypeStruct((NSA_KV, n, w), BF16)
    hspec = lambda w: pl.BlockSpec((NSA_KV, tm, w), lambda i: (0, i, 0))
    row = lambda w: pl.BlockSpec((tm, w), lambda i: (i, 0))
    col = lambda r: pl.BlockSpec((r, tm), lambda i: (0, i))
    full = lambda a: pl.BlockSpec(a.shape, lambda i: (0,) * a.ndim)
    tab = lambda w: pl.BlockSpec((tm, w), lambda i: (i % spt, 0))
    tabt = pl.BlockSpec((ROPE_HALF, tm), lambda i: (0, i % spt))
    vtile = pl.BlockSpec((tm // LANE, NSA_KV * V_ROWS, LANE), lambda i: (i, 0, 0))
    vsd = jax.ShapeDtypeStruct((n // LANE, NSA_KV * V_ROWS, LANE), BF16)
    cw = CMP_STRIDE * HEAD_DIM
    cspec = pl.BlockSpec((NSA_KV, tm // CMP_STRIDE, cw), lambda i: (0, i, 0))
    csd = jax.ShapeDtypeStruct((NSA_KV, n // CMP_STRIDE, cw), BF16)
    return pl.pallas_call(
        _in_proj_kernel,
        grid=(n // tm,),
        in_specs=[row(D_MODEL), full(norm_g), full(w_t), full(w_f), tab(LANE), tab(LANE), tab(LANE),
                  tabt, tabt, tab(K_AUG - HEAD_DIM)],
        out_specs=[cspec, cspec, hspec(K_AUG), hspec(HEAD_DIM), row(RWKV_SHIFT_WIDTH), row(RWKV_WIDTH),
                   col(NSA_WIDTH), col(NSA_WIDTH), vtile, vtile,
                   col(NSA_KV * GATE_ROWS), col(NSA_WIDTH)],
        out_shape=[csd, csd, hm(K_AUG), hm(HEAD_DIM),
                   jax.ShapeDtypeStruct((n, RWKV_SHIFT_WIDTH), F32),
                   jax.ShapeDtypeStruct((n, RWKV_WIDTH), F32),
                   jax.ShapeDtypeStruct((NSA_WIDTH, n), BF16),
                   jax.ShapeDtypeStruct((NSA_WIDTH, n), BF16),
                   vsd, vsd,
                   jax.ShapeDtypeStruct((NSA_KV * GATE_ROWS, n), F32),
                   jax.ShapeDtypeStruct((NSA_WIDTH, n), F32)],
        scratch_shapes=[pltpu.VMEM((2, tm, KV_WIDTH), F32)],
        compiler_params=pltpu.CompilerParams(dimension_semantics=("arbitrary",),
                                             vmem_limit_bytes=VMEM_LIMIT),
        name="in_proj",
    )(x2, norm_g, w_t, w_f, ra, rm, rp, cos, sin, onehot)


def _compress_kernel(kc_ref, vc_ref, pk_ref, w1k_ref, w2k_ref, pv_ref, w1v_ref, w2vt_ref,
                     ko_ref, vo_ref):
    half = CMP_STRIDE * HEAD_DIM

    def hidden(c_ref, pos_ref, w1_ref):
        c = c_ref[0, 0]
        za = _dot(c, w1_ref[0:half, :])
        zb = _dot(c, w1_ref[half:2 * half, :])
        pos = jnp.broadcast_to(pos_ref[...], (SUBLANE, 2 * half)).astype(BF16)
        pv = _dot(pos, w1_ref[...])[0:1, :]
        hid = za + pltpu.roll(zb, c.shape[0] - 1, 0) + pv
        return (hid * _sigmoid(hid)).astype(BF16)

    ko_ref[0, 0] = _dot(hidden(kc_ref, pk_ref, w1k_ref), w2k_ref[...]).astype(BF16)
    vo_ref[0, 0] = _dot_nt(w2vt_ref[...], hidden(vc_ref, pv_ref, w1v_ref)).astype(BF16)


def _compress(kc_r, vc_r, pk, w1k, w2k, pv, w1v, w2vt):
    g, b, nch, width = kc_r.shape
    blk = pl.BlockSpec((1, 1, nch, width), lambda i, j: (i, j, 0, 0))
    full = lambda a: pl.BlockSpec(a.shape, lambda i, j: (0,) * a.ndim)
    return pl.pallas_call(
        _compress_kernel,
        grid=(g, b),
        in_specs=[blk, blk, full(pk), full(w1k), full(w2k), full(pv), full(w1v), full(w2vt)],
        out_specs=[pl.BlockSpec((1, 1, nch, HEAD_DIM), lambda i, j: (i, j, 0, 0)),
                   pl.BlockSpec((1, 1, HEAD_DIM, nch), lambda i, j: (i, j, 0, 0))],
        out_shape=[jax.ShapeDtypeStruct((g, b, nch, HEAD_DIM), BF16),
                   jax.ShapeDtypeStruct((g, b, HEAD_DIM, nch), BF16)],
        compiler_params=pltpu.CompilerParams(dimension_semantics=("arbitrary", "arbitrary"),
                                             vmem_limit_bytes=VMEM_LIMIT),
        name="compress",
    )(kc_r, vc_r, pk, w1k, w2k, pv, w1v, w2vt)


def _nsa_kernel(qt_ref, qrt_ref, kc_ref, vct_ref, ks_ref, vst_ref, kw_ref, vwt_ref,
                gate_ref, gn_ref, mt_ref, o_ref, *, tq, seq):
    tk = KEY_TILE
    qi = pl.program_id(2)
    q0 = qi * tq
    nl = NSA_REP * tq
    n_sel = seq // SEL_BLOCK
    ncp = kc_ref.shape[2]

    def heads_on_lanes(ref):
        return jnp.concatenate([ref[r * HEAD_DIM:(r + 1) * HEAD_DIM, :] for r in range(NSA_REP)], axis=1)

    def tile4(a):
        return jnp.concatenate([a] * NSA_REP, axis=1)

    qr = heads_on_lanes(qrt_ref)
    k_s = lax.broadcasted_iota(jnp.int32, (tk, tq), 0)
    t_l = q0 + lax.broadcasted_iota(jnp.int32, (tk, tq), 1)

    def update_steps(box, k_ref, vt_ref, q_op, kts, keeps):
        m_i, acc = box[0]

        def scores(i):
            sc = _dot(k_ref[0, pl.ds(pl.multiple_of(kts[i] * tk, tk), tk), :], q_op)
            if keeps[i] is None:
                return sc
            return jnp.concatenate([jnp.where(keeps[i], sc[:, r * tq:(r + 1) * tq], NEG_INF)
                                    for r in range(NSA_REP)], axis=1)

        sc_next = scores(0)
        yield
        for i, kt in enumerate(kts):
            sc = sc_next
            if i + 1 < len(kts):
                sc_next = scores(i + 1)
            m_n = jnp.maximum(m_i, jnp.max(sc, axis=0, keepdims=True))
            pe = jnp.exp2(sc - m_n).astype(BF16)
            yield
            vtb = jnp.concatenate([vt_ref[kt * (tk // LANE) + jj] for jj in range(tk // LANE)], axis=1)
            acc = jnp.exp2(m_i - m_n) * acc + _dot(vtb, pe)
            m_i = m_n
            yield
        box[0] = (m_i, acc)

    def update(carry, k_ref, vt_ref, q_op, kts, keeps):
        box = [carry]
        _interleave(update_steps(box, k_ref, vt_ref, q_op, kts, keeps))
        return box[0]

    def last_tiles(box, k_ref, vt_ref, q_op, low_keep, spare_tile):
        kts, keeps = [], []
        for back, keep_fn in ((0, lambda d: d >= 0), (1, None), (2, low_keep)):
            exists = a >= back
            kt = jnp.where(exists, a - back, 0 if spare_tile is None else spare_tile)
            d = t_l - (kt * tk + k_s)
            kts.append(kt)
            if keep_fn is not None:
                keeps.append(keep_fn(d) & exists)
            else:
                keeps.append(exists if spare_tile is None else None)
        return update_steps(box, k_ref, vt_ref, q_op, kts, keeps)

    sel_out = {}

    def select_steps():
        s = _dot(kc_ref[0, 0], heads_on_lanes(qt_ref))
        t_c = q0 + lax.broadcasted_iota(jnp.int32, (ncp, tq), 1)
        c_c = lax.broadcasted_iota(jnp.int32, (ncp, tq), 0)
        cmask = tile4((c_c * CMP_STRIDE + (CMP_BLOCK - 1)) <= t_c)
        yield
        s = jnp.where(cmask, s, NEG_INF)
        m = jnp.max(s, axis=0, keepdims=True)
        e = jnp.where(cmask, jnp.exp2(s - m), 0.0)
        den = jnp.sum(e, axis=0, keepdims=True)
        p = e * (1.0 / jnp.where(den > 0.0, den, 1.0))
        sel_out["o_cmp"] = _dot(vct_ref[0, 0], p.astype(BF16))
        yield
        psum = p[:, 0:tq]
        for r in range(1, NSA_REP):
            psum = psum + p[:, r * tq:(r + 1) * tq]
        imp = _x3_dot(mt_ref[...], psum)
        j = lax.broadcasted_iota(jnp.int32, (n_sel, tq), 0)
        t = q0 + lax.broadcasted_iota(jnp.int32, (n_sel, tq), 1)
        tb = t // SEL_BLOCK
        forced = (j == 0) | (j == tb) | (j == tb - 1)
        val = jnp.where(j <= tb, imp + jnp.where(forced, FORCE_BONUS, 0.0), -1.0)
        yield
        vals = [val[g0:g0 + SUBLANE] for g0 in range(0, n_sel, SUBLANE)]
        cnts = [jnp.zeros((SUBLANE, tq), F32) for _ in vals]
        srow = lax.broadcasted_iota(jnp.int32, (SUBLANE, tq), 0)
        for i in range(n_sel):
            vi = jnp.broadcast_to(val[i:i + 1, :], (SUBLANE, tq))
            for g, vg in enumerate(vals):
                ge = lambda: jnp.where(vi >= vg, 1.0, 0.0)
                gt = lambda: jnp.where(vi > vg, 1.0, 0.0)
                if g * SUBLANE > i:
                    beat = ge()
                elif (g + 1) * SUBLANE <= i:
                    beat = gt()
                else:
                    beat = jnp.where(srow > i - g * SUBLANE, ge(), gt())
                cnts[g] = cnts[g] + beat
            if i % SUBLANE == SUBLANE - 1:
                yield
        cnt = jnp.concatenate(cnts, axis=0)
        sel_bias = jnp.where(cnt < float(SEL_TOPK), 0.0, NEG_INF).astype(BF16)
        sel_out["qr_sel"] = jnp.concatenate(
            [qr, tile4(sel_bias), jnp.zeros((K_AUG - HEAD_DIM - n_sel, nl), BF16)], axis=0)

    init = (jnp.full((1, nl), NEG_INF, F32), jnp.zeros((V_ROWS, nl), F32))
    a = (q0 + tq - 1) // tk
    win_box = [init]
    _interleave(last_tiles(win_box, kw_ref, vwt_ref, qr, lambda d: d < WINDOW, None), select_steps())
    c_win = win_box[0]
    o_cmp, qr_sel = sel_out["o_cmp"], sel_out["qr_sel"]

    n_old = jnp.maximum(a - WINDOW // tk, 0)
    c_sel = lax.fori_loop(0, n_old % 2, lambda kt, c: update(c, ks_ref, vst_ref, qr_sel, [kt], [None]), init)
    c_sel = lax.fori_loop(0, n_old // 2,
                          lambda i, c: update(c, ks_ref, vst_ref, qr_sel,
                                              [n_old % 2 + 2 * i, n_old % 2 + 2 * i + 1], [None, None]), c_sel)
    assert (seq // tk - 1) * (tk // SEL_BLOCK) >= SEL_TOPK and WINDOW // SEL_BLOCK <= SEL_TOPK
    sel_box = [c_sel]
    _interleave(last_tiles(sel_box, ks_ref, vst_ref, qr_sel, None, seq // tk - 1))
    c_sel = sel_box[0]
    o_sel = c_sel[1][0:HEAD_DIM] * (1.0 / c_sel[1][HEAD_DIM:HEAD_DIM + 1])
    o_win = c_win[1][0:HEAD_DIM] * (1.0 / c_win[1][HEAD_DIM:HEAD_DIM + 1])

    gates = gate_ref[...]
    for r in range(NSA_REP):
        ls = slice(r * tq, (r + 1) * tq)
        o = (gates[3 * r:3 * r + 1, :] * o_cmp[:, ls] + gates[3 * r + 1:3 * r + 2, :] * o_sel[:, ls]
             + gates[3 * r + 2:3 * r + 3, :] * o_win[:, ls])
        rs = slice(r * HEAD_DIM, (r + 1) * HEAD_DIM)
        o_ref[rs, :] = (o * gn_ref[rs, :]).astype(BF16)


def _nsa(qt, qrt, kcmp, vcmpt, ksh, vst, kwh, vwt, gates, gnt, mt, *, batch, seq, tq):
    n = batch * seq
    nq = seq // tq
    ncp = kcmp.shape[2]
    grp = NSA_REP * HEAD_DIM
    qspec = pl.BlockSpec((grp, tq), lambda b, g, i: (g, b * nq + i))
    kspec = lambda w: pl.BlockSpec((1, seq, w), lambda b, g, i: (g, b, 0))
    vspec = pl.BlockSpec((seq // LANE, V_ROWS, LANE), lambda b, g, i: (b, g, 0))
    kern = functools.partial(_nsa_kernel, tq=tq, seq=seq)
    return pl.pallas_call(
        kern,
        grid=(batch, NSA_KV, nq),
        in_specs=[qspec, qspec,
                  pl.BlockSpec((1, 1, ncp, HEAD_DIM), lambda b, g, i: (g, b, 0, 0)),
                  pl.BlockSpec((1, 1, HEAD_DIM, ncp), lambda b, g, i: (g, b, 0, 0)),
                  kspec(K_AUG), vspec, kspec(HEAD_DIM), vspec,
                  pl.BlockSpec((GATE_ROWS, tq), lambda b, g, i: (g, b * nq + i)),
                  qspec,
                  pl.BlockSpec(mt.shape, lambda b, g, i: (0, 0))],
        out_specs=qspec,
        out_shape=jax.ShapeDtypeStruct((NSA_WIDTH, n), BF16),
        compiler_params=pltpu.CompilerParams(
            dimension_semantics=("arbitrary", "arbitrary", "arbitrary"),
            vmem_limit_bytes=VMEM_LIMIT),
        name="nsa",
    )(qt, qrt, kcmp, vcmpt, ksh, vst, kwh, vwt, gates, gnt, mt)


def _rwkv_kernel(p_ref, gr_ref, mu_ref, w0_ref, a0_ref, wab_ref, kk_ref, ka_ref, rk_ref,
                 gw_ref, gb_ref, ones_ref, tril_ref, o_ref, st_ref, carry_ref, *, tt):
    c = RWKV_CHUNK
    hd = HEAD_DIM
    nh = RWKV_HEADS
    nc = tt // c
    step = pl.program_id(1)

    @pl.when(step == 0)
    def _():
        st_ref[...] = jnp.zeros_like(st_ref)
        carry_ref[...] = jnp.zeros_like(carry_ref)

    p = p_ref[...]
    row = lax.broadcasted_iota(jnp.int32, p.shape, 0)
    prev = jnp.where(row == 0, carry_ref[...], pltpu.roll(p, 1, 0))
    carry_ref[...] = p[tt - 1:tt, :]
    ps = p + mu_ref[...] * (prev - p)
    r = ps[:, 0:RWKV_WIDTH]
    k = ps[:, RWKV_WIDTH:2 * RWKV_WIDTH]
    v = ps[:, 2 * RWKV_WIDTH:3 * RWKV_WIDTH]
    lora = ps[:, 3 * RWKV_WIDTH:]
    lane = lax.broadcasted_iota(jnp.int32, lora.shape, 1)
    feat = jnp.where(lane < DECAY_RANK, jnp.tanh(lora), lora).astype(BF16)
    up = _dot(feat, wab_ref[...])
    w = w0_ref[...] + up[:, 0:RWKV_WIDTH]
    lw = _sigmoid(w) * (-float(np.exp(-0.5)))
    a = _sigmoid(a0_ref[...] + up[:, RWKV_WIDTH:])
    ones_bd = ones_ref[...]
    kk = k * kk_ref[...]
    kkn = kk * lax.rsqrt(jnp.maximum(_head_sums(kk * kk, ones_bd), 1e-24))
    k2 = k * (1.0 + (a - 1.0) * ka_ref[...])
    alpha = -kkn
    beta = kkn * a
    bonus = _head_sums(r * k2 * rk_ref[...], ones_bd) * v

    cum = _x3_dot(tril_ref[...], lw)
    cend = jnp.concatenate(
        [jnp.broadcast_to(cum[(ch + 1) * c - 1:(ch + 1) * c, :], (c, RWKV_WIDTH)) for ch in range(nc)], axis=0)
    e_neg = jnp.exp(-cum)
    pc = jnp.exp(cend)
    at = alpha * jnp.exp(cum - lw)
    bt = beta * e_neg
    kt = k2 * e_neg
    rt = r * jnp.exp(cum)
    bh = bt * pc
    kh = kt * pc

    npair = RWKV_WIDTH // LANE

    def pairs(x):
        return jnp.stack([x[ch * c:(ch + 1) * c, j * LANE:(j + 1) * LANE]
                          for ch in range(nc) for j in range(npair)], axis=0)

    at_p, rt_p, bt_p, kt_p, v_p, bh_p, kh_p = (pairs(t) for t in (at, rt, bt, kt, v, bh, kh))
    pc_p = jnp.stack([pc[ch * c:ch * c + 1, j * LANE:(j + 1) * LANE]
                      for ch in range(nc) for j in range(npair)], axis=0)
    lane_c = lax.broadcasted_iota(jnp.int32, (1, c, LANE), 2)
    row_c = lax.broadcasted_iota(jnp.int32, (1, c, LANE), 1)
    even_c = lane_c < hd
    col_c = jnp.where(even_c, lane_c, lane_c - hd)
    low_s = row_c > col_c
    low_i = row_c >= col_c
    lane_2c = lax.broadcasted_iota(jnp.int32, (1, 2 * c, LANE), 2)
    row_2c = lax.broadcasted_iota(jnp.int32, (1, 2 * c, LANE), 1)
    even_2c = lane_2c < hd
    on_bd = (row_2c < hd) == even_2c
    zero_c = jnp.zeros((1, c, LANE), BF16)

    def bd(x):
        xb = x.astype(BF16)
        return jnp.concatenate([jnp.where(even_c, xb, zero_c), jnp.where(even_c, zero_c, xb)], axis=1)

    def abd(x):
        xb = x.astype(BF16)
        return jnp.concatenate([jnp.where(even_c, zero_c, xb), jnp.where(even_c, xb, zero_c)], axis=1)

    la = jnp.concatenate([at_p, rt_p], axis=1).astype(BF16)
    zero_2c = jnp.zeros((1, 2 * c, LANE), BF16)
    r_e = _bmm_nt(jnp.where(even_2c, la, zero_2c), jnp.concatenate([bt_p, kt_p], axis=1).astype(BF16))
    r_o = _bmm_nt(jnp.where(even_2c, zero_2c, la), jnp.concatenate([kt_p, bt_p], axis=1).astype(BF16))
    nab = jnp.where(low_s, jnp.where(even_c, r_e[:, 0:c], r_o[:, 0:c]), 0.0)
    aak_sw = jnp.where(low_s, jnp.where(even_c, r_o[:, 0:c], r_e[:, 0:c]), 0.0).astype(BF16)
    arb = jnp.where(low_i, jnp.where(even_c, r_e[:, c:], r_o[:, c:]), 0.0).astype(BF16)
    ark_sw = jnp.where(low_i, jnp.where(even_c, r_o[:, c:], r_e[:, c:]), 0.0).astype(BF16)
    tinv = jnp.where(row_c == col_c, 1.0, 0.0) + nab
    npow = _bmm(nab.astype(BF16), bd(nab))
    n_dbl = 5
    for it in range(n_dbl):
        nbd = bd(npow)
        if it + 1 < n_dbl:
            res = _bmm(jnp.concatenate([tinv, npow], axis=1).astype(BF16), nbd)
            tinv = tinv + res[:, 0:c]
            npow = res[:, c:]
        else:
            tinv = tinv + _bmm(tinv.astype(BF16), nbd)
    av = _bmm(jnp.concatenate([aak_sw, ark_sw], axis=1), abd(v_p))
    tx = _bmm(tinv.astype(BF16), jnp.concatenate([bd(av[:, 0:c]), bd(at_p)], axis=2))
    u0, ta = tx[:, :, 0:LANE], tx[:, :, LANE:]
    ax = _bmm(arb, jnp.concatenate([bd(ta), bd(u0)], axis=2))
    rq = rt_p + ax[:, :, 0:LANE]
    y0 = ax[:, :, LANE:] + av[:, c:]
    v_b = v_p.astype(BF16)
    w_f = jnp.concatenate([jnp.concatenate([ta, u0], axis=2).astype(BF16),
                           jnp.concatenate([jnp.zeros_like(v_b), v_b], axis=2)], axis=1)
    gh = _bmm(jnp.concatenate([jnp.swapaxes(bh_p, 1, 2), jnp.swapaxes(kh_p, 1, 2)], axis=2).astype(BF16), w_f)
    g_bd = jnp.where(on_bd, gh[:, :, 0:LANE], 0.0) + jnp.where(row_2c == lane_2c, pc_p, 0.0)
    h_bd = jnp.where(on_bd, gh[:, :, LANE:], 0.0)
    lhs = jnp.concatenate([rq, g_bd], axis=1).astype(BF16)

    st = st_ref[...]
    ys = []
    for ch in range(nc):
        sl = slice(ch * npair, (ch + 1) * npair)
        res = _bmm(lhs[sl], st.astype(BF16))
        yc = res[:, 0:c, :] + y0[sl]
        st = res[:, c:, :] + h_bd[sl]
        ys.append(jnp.concatenate([yc[j] for j in range(npair)], axis=1))
    st_ref[...] = st
    y = jnp.concatenate(ys, axis=0) if nc > 1 else ys[0]

    inv_hd = 1.0 / hd
    mean = _head_sums(y, ones_bd) * inv_hd
    ycen = y - mean
    var = _head_sums(ycen * ycen, ones_bd) * inv_hd
    yn = ycen * lax.rsqrt(var + GN_EPS) * gw_ref[...] + gb_ref[...]
    o_ref[...] = ((yn + bonus) * gr_ref[...]).astype(BF16)


def _rwkv(rw, gr, mu, w0, a0, wab, kk, ka, rk, gw, gb, ones_bd, tril, *, batch, seq, tt):
    n = batch * seq
    ns = seq // tt
    row = lambda w: pl.BlockSpec((tt, w), lambda b, i: (b * ns + i, 0))
    full = lambda a: pl.BlockSpec(a.shape, lambda b, i: (0,) * a.ndim)
    kern = functools.partial(_rwkv_kernel, tt=tt)
    consts = (mu, w0, a0, wab, kk, ka, rk, gw, gb, ones_bd, tril)
    return pl.pallas_call(
        kern,
        grid=(batch, ns),
        in_specs=[row(RWKV_SHIFT_WIDTH), row(RWKV_WIDTH)] + [full(a) for a in consts],
        out_specs=row(RWKV_WIDTH),
        out_shape=jax.ShapeDtypeStruct((n, RWKV_WIDTH), BF16),
        scratch_shapes=[pltpu.VMEM((RWKV_WIDTH // LANE, LANE, LANE), F32),
                        pltpu.VMEM((1, RWKV_SHIFT_WIDTH), F32)],
        compiler_params=pltpu.CompilerParams(dimension_semantics=("arbitrary", "arbitrary"),
                                             vmem_limit_bytes=VMEM_LIMIT),
        name="rwkv",
    )(rw, gr, *consts)


def _out_kernel(x_ref, ont_ref, or_ref, wn_ref, wr_ref, g_ref, o_ref):
    h = x_ref[...] + _dot_tn(ont_ref[...], wn_ref[...]) + _dot(or_ref[...], wr_ref[...])
    ms = jnp.mean(h * h, axis=-1, keepdims=True)
    o_ref[...] = h * lax.rsqrt(ms + RMS_EPS) * g_ref[...]


def _out_proj(x2, o_nsa_t, o_rwkv, wn, wr, final_g, *, tm):
    n = x2.shape[0]
    row = lambda w: pl.BlockSpec((tm, w), lambda i: (i, 0))
    full = lambda a: pl.BlockSpec(a.shape, lambda i: (0,) * a.ndim)
    return pl.pallas_call(
        _out_kernel,
        grid=(n // tm,),
        in_specs=[row(D_MODEL), pl.BlockSpec((NSA_WIDTH, tm), lambda i: (0, i)), row(RWKV_WIDTH),
                  full(wn), full(wr), full(final_g)],
        out_specs=row(D_MODEL),
        out_shape=jax.ShapeDtypeStruct((n, D_MODEL), F32),
        compiler_params=pltpu.CompilerParams(dimension_semantics=("arbitrary",),
                                             vmem_limit_bytes=VMEM_LIMIT),
        name="out_proj",
    )(x2, o_nsa_t, o_rwkv, wn, wr, final_g)


def _rope_tables(seq):
    inv = ROPE_THETA ** (-np.arange(ROPE_HALF, dtype=np.float64) / ROPE_HALF)
    ang = np.arange(seq, dtype=np.float64)[:, None] * inv[None, :]
    cos, sin = np.cos(ang), np.sin(ang)
    ra = np.ones((seq, HEAD_DIM)); rm = np.zeros((seq, HEAD_DIM)); rp = np.zeros((seq, HEAD_DIM))
    ra[:, :ROPE_HALF] = cos; ra[:, ROPE_HALF:ROPE_DIM] = cos
    rm[:, :ROPE_HALF] = -sin
    rp[:, ROPE_HALF:ROPE_DIM] = sin
    rep = lambda t: jnp.asarray(np.tile(t, (1, LANE // HEAD_DIM)), F32)
    assert seq // SEL_BLOCK <= K_AUG - HEAD_DIM
    onehot = np.zeros((seq, K_AUG - HEAD_DIM))
    onehot[np.arange(seq), np.arange(seq) // SEL_BLOCK] = 1.0
    return (rep(ra), rep(rm), rep(rp), jnp.asarray(cos.T, F32), jnp.asarray(sin.T, F32),
            jnp.asarray(onehot, BF16))


def _cmp_to_sel_t(n_cmp_pad, n_sel):
    n_cmp = n_cmp_pad - 1
    c0 = np.arange(n_cmp)[:, None] * CMP_STRIDE
    s0 = np.arange(n_sel)[None, :] * SEL_BLOCK
    ov = np.clip(np.minimum(c0 + CMP_BLOCK, s0 + SEL_BLOCK) - np.maximum(c0, s0), 0, None) / CMP_BLOCK
    mt = np.zeros((n_sel, n_cmp_pad))
    mt[:, :n_cmp] = ov.T
    return jnp.asarray(mt, BF16)


def _prep_w_in(w_in):
    idx = np.cumsum(IN_SIZES)[:-1].tolist()
    q, kc, vc, ks, vs, kw, vw, gl, gn, rw, gr = jnp.split(w_in, idx, axis=1)
    w_t = jnp.concatenate([kc, vc, ks, kw, rw, gr], axis=1).astype(BF16)
    per_g = 3 * NSA_REP
    pad = jnp.zeros((D_MODEL, GATE_ROWS - per_g), w_in.dtype)
    gl_p = [t for g in range(NSA_KV) for t in (gl[:, g * per_g:(g + 1) * per_g], pad)]
    w_f = jnp.concatenate([q, vs, vw] + gl_p + [gn], axis=1).T.astype(BF16)
    return w_t, w_f


def _layer(x2, norm_g, w_in, cmp_pos_k, cmp_w1_k, cmp_w2_k, cmp_pos_v, cmp_w1_v, cmp_w2_v,
           shift_mu, decay_w0, decay_up, iclr_a0, iclr_up, k_k, k_a, r_k, gn_w, gn_b, w_out,
           final_g, *, batch, seq):
    tm = 256
    tq = 256
    tt = 256
    assert WINDOW == 2 * KEY_TILE and seq % KEY_TILE == 0 and KEY_TILE % tq == 0
    nch = seq // CMP_STRIDE
    n_sel = seq // SEL_BLOCK
    rowv = lambda t: t.reshape(1, -1).astype(F32)

    w_t, w_f = _prep_w_in(w_in)
    (kch, vch, ksh, kwh, rw, gr, qt, qrt, vst, vwt, gates, gnt) = _in_proj(
        x2, rowv(norm_g), w_t, w_f, _rope_tables(seq), seq=seq, tm=2 * tm)

    chunks = lambda t: t.reshape(NSA_KV, batch, nch, CMP_STRIDE * HEAD_DIM)
    kcmp, vcmpt = _compress(chunks(kch), chunks(vch),
                            rowv(cmp_pos_k), cmp_w1_k.astype(BF16), cmp_w2_k.astype(BF16),
                            rowv(cmp_pos_v), cmp_w1_v.astype(BF16), cmp_w2_v.T.astype(BF16))

    o_nsa_t = _nsa(qt, qrt, kcmp, vcmpt, ksh, vst, kwh, vwt, gates, gnt,
                   _cmp_to_sel_t(nch, n_sel), batch=batch, seq=seq, tq=tq)

    z = jnp.zeros((DECAY_RANK, RWKV_WIDTH), F32)
    wab = jnp.concatenate([jnp.concatenate([decay_up, z], axis=1),
                           jnp.concatenate([z, iclr_up], axis=1)], axis=0).astype(BF16)
    hid = np.arange(2 * LANE) // HEAD_DIM
    ones_bd = jnp.asarray(hid[:, None] == hid[None, :], BF16)
    ti = np.arange(tt)
    tril = jnp.asarray((ti[:, None] >= ti[None, :]) & (ti[:, None] // RWKV_CHUNK == ti[None, :] // RWKV_CHUNK), BF16)
    o_rwkv = _rwkv(rw, gr, rowv(shift_mu), rowv(decay_w0), rowv(iclr_a0), wab, rowv(k_k), rowv(k_a),
                   rowv(r_k), rowv(gn_w), rowv(gn_b), ones_bd, tril, batch=batch, seq=seq, tt=tt)

    w_o = w_out.astype(BF16)
    return _out_proj(x2, o_nsa_t, o_rwkv, w_o[:NSA_WIDTH], w_o[NSA_WIDTH:], rowv(final_g), tm=2 * tm)


def kernel(x, norm_g, w_in, cmp_pos_k, cmp_w1_k, cmp_w2_k, cmp_pos_v, cmp_w1_v, cmp_w2_v, shift_mu, decay_w0, decay_up, iclr_a0, iclr_up, k_k, k_a, r_k, gn_w, gn_b, w_out, final_g):
    batch, seq, d = x.shape
    assert d == D_MODEL and norm_g.shape[0] == 1, "single-layer trunk"
    out = _layer(x.reshape(batch * seq, d), norm_g[0], w_in[0], cmp_pos_k[0], cmp_w1_k[0], cmp_w2_k[0],
                 cmp_pos_v[0], cmp_w1_v[0], cmp_w2_v[0], shift_mu[0], decay_w0[0], decay_up[0],
                 iclr_a0[0], iclr_up[0], k_k[0], k_a[0], r_k[0], gn_w[0], gn_b[0], w_out[0],
                 final_g, batch=batch, seq=seq)
    return out.reshape(batch, seq, d)
```

```python
import functools

import numpy as np
import jax
import jax.numpy as jnp
from jax import lax
from jax.experimental import pallas as pl
from jax.experimental.pallas import tpu as pltpu

F32 = jnp.float32
BF16 = jnp.bfloat16

D_MODEL = 1024
HEAD_DIM = 64
NSA_HEADS = 8
NSA_KV = 2
NSA_REP = NSA_HEADS // NSA_KV
RWKV_HEADS = 8
NSA_WIDTH = NSA_HEADS * HEAD_DIM
RWKV_WIDTH = RWKV_HEADS * HEAD_DIM
KV_WIDTH = NSA_KV * HEAD_DIM
ROPE_DIM = HEAD_DIM // 4
ROPE_HALF = ROPE_DIM // 2
ROPE_THETA = 500000.0
CMP_BLOCK = 32
CMP_STRIDE = 16
CMP_HIDDEN = 256
SEL_BLOCK = 64
SEL_TOPK = 8
WINDOW = 512
DECAY_RANK = 64
ICLR_RANK = 64
RWKV_SHIFT_WIDTH = 3 * RWKV_WIDTH + DECAY_RANK + ICLR_RANK
IN_SIZES = (NSA_WIDTH, KV_WIDTH, KV_WIDTH, KV_WIDTH, KV_WIDTH, KV_WIDTH, KV_WIDTH,
            3 * NSA_HEADS, NSA_WIDTH, RWKV_SHIFT_WIDTH, RWKV_WIDTH)
SCALE = HEAD_DIM ** -0.5
RMS_EPS = 1e-6
GN_EPS = 64e-5
NEG_INF = -1e30
FORCE_BONUS = 1e3

LANE = 128
SUBLANE = 8
GATE_ROWS = 16

T_KV = 0
T_RW = T_KV + 4 * KV_WIDTH
T_GR = T_RW + RWKV_SHIFT_WIDTH
T_END = T_GR + RWKV_WIDTH
R_Q = 0
R_VS = R_Q + NSA_WIDTH
R_VW = R_VS + KV_WIDTH
R_GL = R_VW + KV_WIDTH
R_GN = R_GL + NSA_KV * GATE_ROWS
R_END = R_GN + NSA_WIDTH

LOG2E = float(np.log2(np.e))
K_AUG = LANE
V_ROWS = HEAD_DIM + 16
KEY_TILE = 256
RWKV_CHUNK = 64
VMEM_LIMIT = 48 * 1024 * 1024


def _dot(a, b):
    return jnp.dot(a, b, preferred_element_type=F32)


def _dot_nt(a, b):
    return lax.dot_general(a, b, (((1,), (1,)), ((), ())), preferred_element_type=F32)


def _dot_tn(a, b):
    return lax.dot_general(a, b, (((0,), (0,)), ((), ())), preferred_element_type=F32)


def _bmm(a, b):
    return lax.dot_general(a, b, (((2,), (1,)), ((0,), (0,))), preferred_element_type=F32)


def _bmm_nt(a, b):
    return lax.dot_general(a, b, (((2,), (2,)), ((0,), (0,))), preferred_element_type=F32)


def _split3(x):
    hi = x.astype(BF16)
    r1 = x - hi.astype(F32)
    mid = r1.astype(BF16)
    lo = (r1 - mid.astype(F32)).astype(BF16)
    return hi, mid, lo


def _head_sums(x, ones_blk, terms=2):
    w = ones_blk.shape[0]
    parts = [x.astype(BF16)]
    if terms == 2:
        parts.append((x - parts[0].astype(F32)).astype(BF16))
    cols = []
    for i in range(0, x.shape[1], w):
        acc = _dot(parts[0][:, i:i + w], ones_blk)
        for t in parts[1:]:
            acc = acc + _dot(t[:, i:i + w], ones_blk)
        cols.append(acc)
    return jnp.concatenate(cols, axis=1)


def _x3_dot(w_bf16, x):
    hi, mid, lo = _split3(x)
    return _dot(w_bf16, hi) + _dot(w_bf16, mid) + _dot(w_bf16, lo)


def _sigmoid(x):
    return 0.5 * jnp.tanh(0.5 * x) + 0.5


def _interleave(*gens):
    live = list(gens)
    while live:
        for g in list(live):
            try:
                next(g)
            except StopIteration:
                live.remove(g)


def _rope128(t, ra, rm, rp):
    return t * ra + pltpu.roll(t, LANE - ROPE_HALF, 1) * rm + pltpu.roll(t, ROPE_HALF, 1) * rp


def _in_proj_kernel(x_ref, g_ref, wt_ref, wf_ref, ra_ref, rm_ref, rp_ref, cos_ref, sin_ref, oh_ref,
                    kc_ref, vc_ref, ks_ref, kw_ref, rw_ref, gr_ref,
                    qt_ref, qrt_ref, vst_ref, vwt_ref, gate_ref, gn_ref, cmp_scr):
    x = x_ref[...]
    ms = jnp.mean(x * x, axis=-1, keepdims=True)
    y = (x * lax.rsqrt(ms + RMS_EPS) * g_ref[...]).astype(BF16)
    tm = x.shape[0]

    kv = _dot(y, wt_ref[:, T_KV:T_RW])
    for i, ref in enumerate((kc_ref, vc_ref)):
        cmp_scr[i] = kv[:, i * KV_WIDTH:(i + 1) * KV_WIDTH]
        for tau in range(CMP_STRIDE):
            piece = cmp_scr[i, pl.ds(tau, tm // CMP_STRIDE, stride=CMP_STRIDE), :].astype(BF16)
            for g in range(NSA_KV):
                ref[g, :, tau * HEAD_DIM:(tau + 1) * HEAD_DIM] = piece[:, g * HEAD_DIM:(g + 1) * HEAD_DIM]
    ra, rm, rp = ra_ref[...], rm_ref[...], rp_ref[...]
    for i, ref in ((2, ks_ref), (3, kw_ref)):
        t = _rope128(kv[:, i * LANE:(i + 1) * LANE], ra, rm, rp)
        for g in range(NSA_KV):
            tg = t[:, g * HEAD_DIM:(g + 1) * HEAD_DIM].astype(BF16)
            if ref is ks_ref:
                ref[g] = jnp.concatenate([tg, oh_ref[...]], axis=1)
            else:
                ref[g] = tg
    rw_ref[...] = _dot(y, wt_ref[:, T_RW:T_GR])
    gr = _dot(y, wt_ref[:, T_GR:T_END])
    gr_ref[...] = gr * _sigmoid(gr)

    def proj_t(r0, r1):
        return _dot_nt(wf_ref[r0:r1, :], y)

    qt = proj_t(R_Q, R_VS) * (SCALE * LOG2E)
    cos, sin = cos_ref[...], sin_ref[...]
    qt_ref[...] = qt.astype(BF16)
    for h in range(NSA_HEADS):
        r0 = h * HEAD_DIM
        t1 = qt[r0:r0 + ROPE_HALF]
        t2 = qt[r0 + ROPE_HALF:r0 + ROPE_DIM]
        qrt_ref[r0:r0 + ROPE_DIM, :] = jnp.concatenate(
            [t1 * cos - t2 * sin, t2 * cos + t1 * sin], axis=0).astype(BF16)
        qrt_ref[r0 + ROPE_DIM:r0 + HEAD_DIM, :] = qt[r0 + ROPE_DIM:r0 + HEAD_DIM].astype(BF16)
    vt = proj_t(R_VS, R_GL).astype(BF16)
    ones = jnp.ones((V_ROWS - HEAD_DIM, LANE), BF16)
    for j in range(vt.shape[1] // LANE):
        for i, ref in enumerate((vst_ref, vwt_ref)):
            for g in range(NSA_KV):
                r0 = i * KV_WIDTH + g * HEAD_DIM
                ref[j, g * V_ROWS:(g + 1) * V_ROWS, :] = jnp.concatenate(
                    [vt[r0:r0 + HEAD_DIM, j * LANE:(j + 1) * LANE], ones], axis=0)
    gate_ref[...] = _sigmoid(proj_t(R_GL, R_GN))
    gn = proj_t(R_GN, R_END)
    gn_ref[...] = gn * _sigmoid(gn)


def _in_proj(x2, norm_g, w_t, w_f, tabs, *, seq, tm):
    n = x2.shape[0]
    spt = seq // tm
    ra, rm, rp, cos, sin, onehot = tabs
    hm = lambda w: jax.ShapeDtypeStruct((NSA_KV, n, w), BF16)
    hspec = lambda w: pl.BlockSpec((NSA_KV, tm, w), lambda i: (0, i, 0))
    row = lambda w: pl.BlockSpec((tm, w), lambda i: (i, 0))
    col = lambda r: pl.BlockSpec((r, tm), lambda i: (0, i))
    full = lambda a: pl.BlockSpec(a.shape, lambda i: (0,) * a.ndim)
    tab = lambda w: pl.BlockSpec((tm, w), lambda i: (i % spt, 0))
    tabt = pl.BlockSpec((ROPE_HALF, tm), lambda i: (0, i % spt))
    vtile = pl.BlockSpec((tm // LANE, NSA_KV * V_ROWS, LANE), lambda i: (i, 0, 0))
    vsd = jax.ShapeDtypeStruct((n // LANE, NSA_KV * V_ROWS, LANE), BF16)
    cw = CMP_STRIDE * HEAD_DIM
    cspec = pl.BlockSpec((NSA_KV, tm // CMP_STRIDE, cw), lambda i: (0, i, 0))
    csd = jax.ShapeDtypeStruct((NSA_KV, n // CMP_STRIDE, cw), BF16)
    return pl.pallas_call(
        _in_proj_kernel,
        grid=(n // tm,),
        in_specs=[row(D_MODEL), full(norm_g), full(w_t), full(w_f), tab(LANE), tab(LANE), tab(LANE),
                  tabt, tabt, tab(K_AUG - HEAD_DIM)],
        out_specs=[cspec, cspec, hspec(K_AUG), hspec(HEAD_DIM), row(RWKV_SHIFT_WIDTH), row(RWKV_WIDTH),
                   col(NSA_WIDTH), col(NSA_WIDTH), vtile, vtile,
                   col(NSA_KV * GATE_ROWS), col(NSA_WIDTH)],
        out_shape=[csd, csd, hm(K_AUG), hm(HEAD_DIM),
                   jax.ShapeDtypeStruct((n, RWKV_SHIFT_WIDTH), F32),
                   jax.ShapeDtypeStruct((n, RWKV_WIDTH), F32),
                   jax.ShapeDtypeStruct((NSA_WIDTH, n), BF16),
                   jax.ShapeDtypeStruct((NSA_WIDTH, n), BF16),
                   vsd, vsd,
                   jax.ShapeDtypeStruct((NSA_KV * GATE_ROWS, n), F32),
                   jax.ShapeDtypeStruct((NSA_WIDTH, n), F32)],
        scratch_shapes=[pltpu.VMEM((2, tm, KV_WIDTH), F32)],
        compiler_params=pltpu.CompilerParams(dimension_semantics=("arbitrary",),
                                             vmem_limit_bytes=VMEM_LIMIT),
        name="in_proj",
    )(x2, norm_g, w_t, w_f, ra, rm, rp, cos, sin, onehot)


def _compress_kernel(kc_ref, vc_ref, pk_ref, w1k_ref, w2k_ref, pv_ref, w1v_ref, w2vt_ref,
                     ko_ref, vo_ref):
    half = CMP_STRIDE * HEAD_DIM

    def hidden(c_ref, pos_ref, w1_ref):
        c = c_ref[0, 0]
        za = _dot(c, w1_ref[0:half, :])
        zb = _dot(c, w1_ref[half:2 * half, :])
        pos = jnp.broadcast_to(pos_ref[...], (SUBLANE, 2 * half)).astype(BF16)
        pv = _dot(pos, w1_ref[...])[0:1, :]
        hid = za + pltpu.roll(zb, c.shape[0] - 1, 0) + pv
        return (hid * _sigmoid(hid)).astype(BF16)

    ko_ref[0, 0] = _dot(hidden(kc_ref, pk_ref, w1k_ref), w2k_ref[...]).astype(BF16)
    vo_ref[0, 0] = _dot_nt(w2vt_ref[...], hidden(vc_ref, pv_ref, w1v_ref)).astype(BF16)


def _compress(kc_r, vc_r, pk, w1k, w2k, pv, w1v, w2vt):
    g, b, nch, width = kc_r.shape
    blk = pl.BlockSpec((1, 1, nch, width), lambda i, j: (i, j, 0, 0))
    full = lambda a: pl.BlockSpec(a.shape, lambda i, j: (0,) * a.ndim)
    return pl.pallas_call(
        _compress_kernel,
        grid=(g, b),
        in_specs=[blk, blk, full(pk), full(w1k), full(w2k), full(pv), full(w1v), full(w2vt)],
        out_specs=[pl.BlockSpec((1, 1, nch, HEAD_DIM), lambda i, j: (i, j, 0, 0)),
                   pl.BlockSpec((1, 1, HEAD_DIM, nch), lambda i, j: (i, j, 0, 0))],
        out_shape=[jax.ShapeDtypeStruct((g, b, nch, HEAD_DIM), BF16),
                   jax.ShapeDtypeStruct((g, b, HEAD_DIM, nch), BF16)],
        compiler_params=pltpu.CompilerParams(dimension_semantics=("arbitrary", "arbitrary"),
                                             vmem_limit_bytes=VMEM_LIMIT),
        name="compress",
    )(kc_r, vc_r, pk, w1k, w2k, pv, w1v, w2vt)


def _nsa_kernel(qt_ref, qrt_ref, kc_ref, vct_ref, ks_ref, vst_ref, kw_ref, vwt_ref,
                gate_ref, gn_ref, mt_ref, o_ref, *, tq, seq):
    tk = KEY_TILE
    qi = pl.program_id(2)
    q0 = qi * tq
    nl = NSA_REP * tq
    n_sel = seq // SEL_BLOCK
    ncp = kc_ref.shape[2]

    def heads_on_lanes(ref):
        return jnp.concatenate([ref[r * HEAD_DIM:(r + 1) * HEAD_DIM, :] for r in range(NSA_REP)], axis=1)

    def tile4(a):
        return jnp.concatenate([a] * NSA_REP, axis=1)

    qr = heads_on_lanes(qrt_ref)
    k_s = lax.broadcasted_iota(jnp.int32, (tk, tq), 0)
    t_l = q0 + lax.broadcasted_iota(jnp.int32, (tk, tq), 1)

    def update_steps(box, k_ref, vt_ref, q_op, kts, keeps):
        m_i, acc = box[0]

        def scores(i):
            sc = _dot(k_ref[0, pl.ds(pl.multiple_of(kts[i] * tk, tk), tk), :], q_op)
            if keeps[i] is None:
                return sc
            return jnp.concatenate([jnp.where(keeps[i], sc[:, r * tq:(r + 1) * tq], NEG_INF)
                                    for r in range(NSA_REP)], axis=1)

        sc_next = scores(0)
        yield
        for i, kt in enumerate(kts):
            sc = sc_next
            if i + 1 < len(kts):
                sc_next = scores(i + 1)
            m_n = jnp.maximum(m_i, jnp.max(sc, axis=0, keepdims=True))
            pe = jnp.exp2(sc - m_n).astype(BF16)
            yield
            vtb = jnp.concatenate([vt_ref[kt * (tk // LANE) + jj] for jj in range(tk // LANE)], axis=1)
            acc = jnp.exp2(m_i - m_n) * acc + _dot(vtb, pe)
            m_i = m_n
            yield
        box[0] = (m_i, acc)

    def update(carry, k_ref, vt_ref, q_op, kts, keeps):
        box = [carry]
        _interleave(update_steps(box, k_ref, vt_ref, q_op, kts, keeps))
        return box[0]

    def last_tiles(box, k_ref, vt_ref, q_op, low_keep, spare_tile):
        kts, keeps = [], []
        for back, keep_fn in ((0, lambda d: d >= 0), (1, None), (2, low_keep)):
            exists = a >= back
            kt = jnp.where(exists, a - back, 0 if spare_tile is None else spare_tile)
            d = t_l - (kt * tk + k_s)
            kts.append(kt)
            if keep_fn is not None:
                keeps.append(keep_fn(d) & exists)
            else:
                keeps.append(exists if spare_tile is None else None)
        return update_steps(box, k_ref, vt_ref, q_op, kts, keeps)

    sel_out = {}

    def select_steps():
        s = _dot(kc_ref[0, 0], heads_on_lanes(qt_ref))
        t_c = q0 + lax.broadcasted_iota(jnp.int32, (ncp, tq), 1)
        c_c = lax.broadcasted_iota(jnp.int32, (ncp, tq), 0)
        cmask = tile4((c_c * CMP_STRIDE + (CMP_BLOCK - 1)) <= t_c)
        yield
        s = jnp.where(cmask, s, NEG_INF)
        m = jnp.max(s, axis=0, keepdims=True)
        e = jnp.where(cmask, jnp.exp2(s - m), 0.0)
        den = jnp.sum(e, axis=0, keepdims=True)
        p = e * (1.0 / jnp.where(den > 0.0, den, 1.0))
        sel_out["o_cmp"] = _dot(vct_ref[0, 0], p.astype(BF16))
        yield
        psum = p[:, 0:tq]
        for r in range(1, NSA_REP):
            psum = psum + p[:, r * tq:(r + 1) * tq]
        imp = _x3_dot(mt_ref[...], psum)
        j = lax.broadcasted_iota(jnp.int32, (n_sel, tq), 0)
        t = q0 + lax.broadcasted_iota(jnp.int32, (n_sel, tq), 1)
        tb = t // SEL_BLOCK
        forced = (j == 0) | (j == tb) | (j == tb - 1)
        val = jnp.where(j <= tb, imp + jnp.where(forced, FORCE_BONUS, 0.0), -1.0)
        yield
        vals = [val[g0:g0 + SUBLANE] for g0 in range(0, n_sel, SUBLANE)]
        cnts = [jnp.zeros((SUBLANE, tq), F32) for _ in vals]
        srow = lax.broadcasted_iota(jnp.int32, (SUBLANE, tq), 0)
        for i in range(n_sel):
            vi = jnp.broadcast_to(val[i:i + 1, :], (SUBLANE, tq))
            for g, vg in enumerate(vals):
                ge = lambda: jnp.where(vi >= vg, 1.0, 0.0)
                gt = lambda: jnp.where(vi > vg, 1.0, 0.0)
                if g * SUBLANE > i:
                    beat = ge()
                elif (g + 1) * SUBLANE <= i:
                    beat = gt()
                else:
                    beat = jnp.where(srow > i - g * SUBLANE, ge(), gt())
                cnts[g] = cnts[g] + beat
            if i % SUBLANE == SUBLANE - 1:
                yield
        cnt = jnp.concatenate(cnts, axis=0)
        sel_bias = jnp.where(cnt < float(SEL_TOPK), 0.0, NEG_INF).astype(BF16)
        sel_out["qr_sel"] = jnp.concatenate(
            [qr, tile4(sel_bias), jnp.zeros((K_AUG - HEAD_DIM - n_sel, nl), BF16)], axis=0)

    init = (jnp.full((1, nl), NEG_INF, F32), jnp.zeros((V_ROWS, nl), F32))
    a = (q0 + tq - 1) // tk
    win_box = [init]
    _interleave(last_tiles(win_box, kw_ref, vwt_ref, qr, lambda d: d < WINDOW, None), select_steps())
    c_win = win_box[0]
    o_cmp, qr_sel = sel_out["o_cmp"], sel_out["qr_sel"]

    n_old = jnp.maximum(a - WINDOW // tk, 0)
    c_sel = lax.fori_loop(0, n_old % 2, lambda kt, c: update(c, ks_ref, vst_ref, qr_sel, [kt], [None]), init)
    c_sel = lax.fori_loop(0, n_old // 2,
                          lambda i, c: update(c, ks_ref, vst_ref, qr_sel,
                                              [n_old % 2 + 2 * i, n_old % 2 + 2 * i + 1], [None, None]), c_sel)
    assert (seq // tk - 1) * (tk // SEL_BLOCK) >= SEL_TOPK and WINDOW // SEL_BLOCK <= SEL_TOPK
    sel_box = [c_sel]
    _interleave(last_tiles(sel_box, ks_ref, vst_ref, qr_sel, None, seq // tk - 1))
    c_sel = sel_box[0]
    o_sel = c_sel[1][0:HEAD_DIM] * (1.0 / c_sel[1][HEAD_DIM:HEAD_DIM + 1])
    o_win = c_win[1][0:HEAD_DIM] * (1.0 / c_win[1][HEAD_DIM:HEAD_DIM + 1])

    gates = gate_ref[...]
    for r in range(NSA_REP):
        ls = slice(r * tq, (r + 1) * tq)
        o = (gates[3 * r:3 * r + 1, :] * o_cmp[:, ls] + gates[3 * r + 1:3 * r + 2, :] * o_sel[:, ls]
             + gates[3 * r + 2:3 * r + 3, :] * o_win[:, ls])
        rs = slice(r * HEAD_DIM, (r + 1) * HEAD_DIM)
        o_ref[rs, :] = (o * gn_ref[rs, :]).astype(BF16)


def _nsa(qt, qrt, kcmp, vcmpt, ksh, vst, kwh, vwt, gates, gnt, mt, *, batch, seq, tq):
    n = batch * seq
    nq = seq // tq
    ncp = kcmp.shape[2]
    grp = NSA_REP * HEAD_DIM
    qspec = pl.BlockSpec((grp, tq), lambda b, g, i: (g, b * nq + i))
    kspec = lambda w: pl.BlockSpec((1, seq, w), lambda b, g, i: (g, b, 0))
    vspec = pl.BlockSpec((seq // LANE, V_ROWS, LANE), lambda b, g, i: (b, g, 0))
    kern = functools.partial(_nsa_kernel, tq=tq, seq=seq)
    return pl.pallas_call(
        kern,
        grid=(batch, NSA_KV, nq),
        in_specs=[qspec, qspec,
                  pl.BlockSpec((1, 1, ncp, HEAD_DIM), lambda b, g, i: (g, b, 0, 0)),
                  pl.BlockSpec((1, 1, HEAD_DIM, ncp), lambda b, g, i: (g, b, 0, 0)),
                  kspec(K_AUG), vspec, kspec(HEAD_DIM), vspec,
                  pl.BlockSpec((GATE_ROWS, tq), lambda b, g, i: (g, b * nq + i)),
                  qspec,
                  pl.BlockSpec(mt.shape, lambda b, g, i: (0, 0))],
        out_specs=qspec,
        out_shape=jax.ShapeDtypeStruct((NSA_WIDTH, n), BF16),
        compiler_params=pltpu.CompilerParams(
            dimension_semantics=("arbitrary", "arbitrary", "arbitrary"),
            vmem_limit_bytes=VMEM_LIMIT),
        name="nsa",
    )(qt, qrt, kcmp, vcmpt, ksh, vst, kwh, vwt, gates, gnt, mt)


def _rwkv_kernel(p_ref, gr_ref, mu_ref, w0_ref, a0_ref, wab_ref, kk_ref, ka_ref, rk_ref,
                 gw_ref, gb_ref, ones_ref, tril_ref, o_ref, st_ref, carry_ref, *, tt):
    c = RWKV_CHUNK
    hd = HEAD_DIM
    nh = RWKV_HEADS
    nc = tt // c
    step = pl.program_id(1)

    @pl.when(step == 0)
    def _():
        st_ref[...] = jnp.zeros_like(st_ref)
        carry_ref[...] = jnp.zeros_like(carry_ref)

    p = p_ref[...]
    row = lax.broadcasted_iota(jnp.int32, p.shape, 0)
    prev = jnp.where(row == 0, carry_ref[...], pltpu.roll(p, 1, 0))
    carry_ref[...] = p[tt - 1:tt, :]
    ps = p + mu_ref[...] * (prev - p)
    r = ps[:, 0:RWKV_WIDTH]
    k = ps[:, RWKV_WIDTH:2 * RWKV_WIDTH]
    v = ps[:, 2 * RWKV_WIDTH:3 * RWKV_WIDTH]
    lora = ps[:, 3 * RWKV_WIDTH:]
    lane = lax.broadcasted_iota(jnp.int32, lora.shape, 1)
    feat = jnp.where(lane < DECAY_RANK, jnp.tanh(lora), lora).astype(BF16)
    up = _dot(feat, wab_ref[...])
    w = w0_ref[...] + up[:, 0:RWKV_WIDTH]
    lw = _sigmoid(w) * (-float(np.exp(-0.5)))
    a = _sigmoid(a0_ref[...] + up[:, RWKV_WIDTH:])
    ones_bd = ones_ref[...]
    kk = k * kk_ref[...]
    kkn = kk * lax.rsqrt(jnp.maximum(_head_sums(kk * kk, ones_bd), 1e-24))
    k2 = k * (1.0 + (a - 1.0) * ka_ref[...])
    alpha = -kkn
    beta = kkn * a
    bonus = _head_sums(r * k2 * rk_ref[...], ones_bd, terms=1) * v

    cum = _x3_dot(tril_ref[...], lw)
    cend = jnp.concatenate(
        [jnp.broadcast_to(cum[(ch + 1) * c - 1:(ch + 1) * c, :], (c, RWKV_WIDTH)) for ch in range(nc)], axis=0)
    e_neg = jnp.exp(-cum)
    pc = jnp.exp(cend)
    at = alpha * jnp.exp(cum - lw)
    bt = beta * e_neg
    kt = k2 * e_neg
    rt = r * jnp.exp(cum)
    bh = bt * pc
    kh = kt * pc

    npair = RWKV_WIDTH // LANE

    def pairs(x):
        return jnp.stack([x[ch * c:(ch + 1) * c, j * LANE:(j + 1) * LANE]
                          for ch in range(nc) for j in range(npair)], axis=0)

    at_p, rt_p, bt_p, kt_p, v_p, bh_p, kh_p = (pairs(t) for t in (at, rt, bt, kt, v, bh, kh))
    pc_p = jnp.stack([pc[ch * c:ch * c + 1, j * LANE:(j + 1) * LANE]
                      for ch in range(nc) for j in range(npair)], axis=0)
    lane_c = lax.broadcasted_iota(jnp.int32, (1, c, LANE), 2)
    row_c = lax.broadcasted_iota(jnp.int32, (1, c, LANE), 1)
    even_c = lane_c < hd
    col_c = jnp.where(even_c, lane_c, lane_c - hd)
    low_s = row_c > col_c
    low_i = row_c >= col_c
    lane_2c = lax.broadcasted_iota(jnp.int32, (1, 2 * c, LANE), 2)
    row_2c = lax.broadcasted_iota(jnp.int32, (1, 2 * c, LANE), 1)
    even_2c = lane_2c < hd
    on_bd = (row_2c < hd) == even_2c
    zero_c = jnp.zeros((1, c, LANE), BF16)

    def bd(x):
        xb = x.astype(BF16)
        return jnp.concatenate([jnp.where(even_c, xb, zero_c), jnp.where(even_c, zero_c, xb)], axis=1)

    def abd(x):
        xb = x.astype(BF16)
        return jnp.concatenate([jnp.where(even_c, zero_c, xb), jnp.where(even_c, xb, zero_c)], axis=1)

    la = jnp.concatenate([at_p, rt_p], axis=1).astype(BF16)
    zero_2c = jnp.zeros((1, 2 * c, LANE), BF16)
    r_e = _bmm_nt(jnp.where(even_2c, la, zero_2c), jnp.concatenate([bt_p, kt_p], axis=1).astype(BF16))
    r_o = _bmm_nt(jnp.where(even_2c, zero_2c, la), jnp.concatenate([kt_p, bt_p], axis=1).astype(BF16))
    nab = jnp.where(low_s, jnp.where(even_c, r_e[:, 0:c], r_o[:, 0:c]), 0.0)
    aak_sw = jnp.where(low_s, jnp.where(even_c, r_o[:, 0:c], r_e[:, 0:c]), 0.0).astype(BF16)
    arb = jnp.where(low_i, jnp.where(even_c, r_e[:, c:], r_o[:, c:]), 0.0).astype(BF16)
    ark_sw = jnp.where(low_i, jnp.where(even_c, r_o[:, c:], r_e[:, c:]), 0.0).astype(BF16)
    tinv = jnp.where(row_c == col_c, 1.0, 0.0) + nab
    npow = _bmm(nab.astype(BF16), bd(nab))
    n_dbl = 5
    for it in range(n_dbl):
        nbd = bd(npow)
        if it + 1 < n_dbl:
            res = _bmm(jnp.concatenate([tinv, npow], axis=1).astype(BF16), nbd)
            tinv = tinv + res[:, 0:c]
            npow = res[:, c:]
        else:
            tinv = tinv + _bmm(tinv.astype(BF16), nbd)
    av = _bmm(jnp.concatenate([aak_sw, ark_sw], axis=1), abd(v_p))
    tx = _bmm(tinv.astype(BF16), jnp.concatenate([bd(av[:, 0:c]), bd(at_p)], axis=2))
    u0, ta = tx[:, :, 0:LANE], tx[:, :, LANE:]
    ax = _bmm(arb, jnp.concatenate([bd(ta), bd(u0)], axis=2))
    rq = rt_p + ax[:, :, 0:LANE]
    y0 = ax[:, :, LANE:] + av[:, c:]
    v_b = v_p.astype(BF16)
    w_f = jnp.concatenate([jnp.concatenate([ta, u0], axis=2).astype(BF16),
                           jnp.concatenate([jnp.zeros_like(v_b), v_b], axis=2)], axis=1)
    gh = _bmm(jnp.concatenate([jnp.swapaxes(bh_p, 1, 2), jnp.swapaxes(kh_p, 1, 2)], axis=2).astype(BF16), w_f)
    g_bd = jnp.where(on_bd, gh[:, :, 0:LANE], 0.0) + jnp.where(row_2c == lane_2c, pc_p, 0.0)
    h_bd = jnp.where(on_bd, gh[:, :, LANE:], 0.0)
    lhs = jnp.concatenate([rq, g_bd], axis=1).astype(BF16)

    st = st_ref[...]
    ys = []
    for ch in range(nc):
        sl = slice(ch * npair, (ch + 1) * npair)
        res = _bmm(lhs[sl], st.astype(BF16))
        yc = res[:, 0:c, :] + y0[sl]
        st = res[:, c:, :] + h_bd[sl]
        ys.append(jnp.concatenate([yc[j] for j in range(npair)], axis=1))
    st_ref[...] = st
    y = jnp.concatenate(ys, axis=0) if nc > 1 else ys[0]

    inv_hd = 1.0 / hd
    mean = _head_sums(y, ones_bd, terms=1) * inv_hd
    ycen = y - mean
    var = _head_sums(ycen * ycen, ones_bd, terms=1) * inv_hd
    yn = ycen * lax.rsqrt(var + GN_EPS) * gw_ref[...] + gb_ref[...]
    o_ref[...] = ((yn + bonus) * gr_ref[...]).astype(BF16)


def _rwkv(rw, gr, mu, w0, a0, wab, kk, ka, rk, gw, gb, ones_bd, tril, *, batch, seq, tt):
    n = batch * seq
    ns = seq // tt
    row = lambda w: pl.BlockSpec((tt, w), lambda b, i: (b * ns + i, 0))
    full = lambda a: pl.BlockSpec(a.shape, lambda b, i: (0,) * a.ndim)
    kern = functools.partial(_rwkv_kernel, tt=tt)
    consts = (mu, w0, a0, wab, kk, ka, rk, gw, gb, ones_bd, tril)
    return pl.pallas_call(
        kern,
        grid=(batch, ns),
        in_specs=[row(RWKV_SHIFT_WIDTH), row(RWKV_WIDTH)] + [full(a) for a in consts],
        out_specs=row(RWKV_WIDTH),
        out_shape=jax.ShapeDtypeStruct((n, RWKV_WIDTH), BF16),
        scratch_shapes=[pltpu.VMEM((RWKV_WIDTH // LANE, LANE, LANE), F32),
                        pltpu.VMEM((1, RWKV_SHIFT_WIDTH), F32)],
        compiler_params=pltpu.CompilerParams(dimension_semantics=("arbitrary", "arbitrary"),
                                             vmem_limit_bytes=VMEM_LIMIT),
        name="rwkv",
    )(rw, gr, *consts)


def _out_kernel(x_ref, ont_ref, or_ref, wn_ref, wr_ref, g_ref, o_ref):
    h = x_ref[...] + _dot_tn(ont_ref[...], wn_ref[...]) + _dot(or_ref[...], wr_ref[...])
    ms = jnp.mean(h * h, axis=-1, keepdims=True)
    o_ref[...] = h * lax.rsqrt(ms + RMS_EPS) * g_ref[...]


def _out_proj(x2, o_nsa_t, o_rwkv, wn, wr, final_g, *, tm):
    n = x2.shape[0]
    row = lambda w: pl.BlockSpec((tm, w), lambda i: (i, 0))
    full = lambda a: pl.BlockSpec(a.shape, lambda i: (0,) * a.ndim)
    return pl.pallas_call(
        _out_kernel,
        grid=(n // tm,),
        in_specs=[row(D_MODEL), pl.BlockSpec((NSA_WIDTH, tm), lambda i: (0, i)), row(RWKV_WIDTH),
                  full(wn), full(wr), full(final_g)],
        out_specs=row(D_MODEL),
        out_shape=jax.ShapeDtypeStruct((n, D_MODEL), F32),
        compiler_params=pltpu.CompilerParams(dimension_semantics=("arbitrary",),
                                             vmem_limit_bytes=VMEM_LIMIT),
        name="out_proj",
    )(x2, o_nsa_t, o_rwkv, wn, wr, final_g)


def _rope_tables(seq):
    inv = ROPE_THETA ** (-np.arange(ROPE_HALF, dtype=np.float64) / ROPE_HALF)
    ang = np.arange(seq, dtype=np.float64)[:, None] * inv[None, :]
    cos, sin = np.cos(ang), np.sin(ang)
    ra = np.ones((seq, HEAD_DIM)); rm = np.zeros((seq, HEAD_DIM)); rp = np.zeros((seq, HEAD_DIM))
    ra[:, :ROPE_HALF] = cos; ra[:, ROPE_HALF:ROPE_DIM] = cos
    rm[:, :ROPE_HALF] = -sin
    rp[:, ROPE_HALF:ROPE_DIM] = sin
    rep = lambda t: jnp.asarray(np.tile(t, (1, LANE // HEAD_DIM)), F32)
    assert seq // SEL_BLOCK <= K_AUG - HEAD_DIM
    onehot = np.zeros((seq, K_AUG - HEAD_DIM))
    onehot[np.arange(seq), np.arange(seq) // SEL_BLOCK] = 1.0
    return (rep(ra), rep(rm), rep(rp), jnp.asarray(cos.T, F32), jnp.asarray(sin.T, F32),
            jnp.asarray(onehot, BF16))


def _cmp_to_sel_t(n_cmp_pad, n_sel):
    n_cmp = n_cmp_pad - 1
    c0 = np.arange(n_cmp)[:, None] * CMP_STRIDE
    s0 = np.arange(n_sel)[None, :] * SEL_BLOCK
    ov = np.clip(np.minimum(c0 + CMP_BLOCK, s0 + SEL_BLOCK) - np.maximum(c0, s0), 0, None) / CMP_BLOCK
    mt = np.zeros((n_sel, n_cmp_pad))
    mt[:, :n_cmp] = ov.T
    return jnp.asarray(mt, BF16)


def _prep_w_in(w_in):
    idx = np.cumsum(IN_SIZES)[:-1].tolist()
    q, kc, vc, ks, vs, kw, vw, gl, gn, rw, gr = jnp.split(w_in, idx, axis=1)
    w_t = jnp.concatenate([kc, vc, ks, kw, rw, gr], axis=1).astype(BF16)
    per_g = 3 * NSA_REP
    pad = jnp.zeros((D_MODEL, GATE_ROWS - per_g), w_in.dtype)
    gl_p = [t for g in range(NSA_KV) for t in (gl[:, g * per_g:(g + 1) * per_g], pad)]
    w_f = jnp.concatenate([q, vs, vw] + gl_p + [gn], axis=1).T.astype(BF16)
    return w_t, w_f


def _layer(x2, norm_g, w_in, cmp_pos_k, cmp_w1_k, cmp_w2_k, cmp_pos_v, cmp_w1_v, cmp_w2_v,
           shift_mu, decay_w0, decay_up, iclr_a0, iclr_up, k_k, k_a, r_k, gn_w, gn_b, w_out,
           final_g, *, batch, seq):
    tm = 256
    tq = 256
    tt = 256
    assert WINDOW == 2 * KEY_TILE and seq % KEY_TILE == 0 and KEY_TILE % tq == 0
    nch = seq // CMP_STRIDE
    n_sel = seq // SEL_BLOCK
    rowv = lambda t: t.reshape(1, -1).astype(F32)

    w_t, w_f = _prep_w_in(w_in)
    (kch, vch, ksh, kwh, rw, gr, qt, qrt, vst, vwt, gates, gnt) = _in_proj(
        x2, rowv(norm_g), w_t, w_f, _rope_tables(seq), seq=seq, tm=2 * tm)

    chunks = lambda t: t.reshape(NSA_KV, batch, nch, CMP_STRIDE * HEAD_DIM)
    kcmp, vcmpt = _compress(chunks(kch), chunks(vch),
                            rowv(cmp_pos_k), cmp_w1_k.astype(BF16), cmp_w2_k.astype(BF16),
                            rowv(cmp_pos_v), cmp_w1_v.astype(BF16), cmp_w2_v.T.astype(BF16))

    o_nsa_t = _nsa(qt, qrt, kcmp, vcmpt, ksh, vst, kwh, vwt, gates, gnt,
                   _cmp_to_sel_t(nch, n_sel), batch=batch, seq=seq, tq=tq)

    z = jnp.zeros((DECAY_RANK, RWKV_WIDTH), F32)
    wab = jnp.concatenate([jnp.concatenate([decay_up, z], axis=1),
                           jnp.concatenate([z, iclr_up], axis=1)], axis=0).astype(BF16)
    hid = np.arange(2 * LANE) // HEAD_DIM
    ones_bd = jnp.asarray(hid[:, None] == hid[None, :], BF16)
    ti = np.arange(tt)
    tril = jnp.asarray((ti[:, None] >= ti[None, :]) & (ti[:, None] // RWKV_CHUNK == ti[None, :] // RWKV_CHUNK), BF16)
    o_rwkv = _rwkv(rw, gr, rowv(shift_mu), rowv(decay_w0), rowv(iclr_a0), wab, rowv(k_k), rowv(k_a),
                   rowv(r_k), rowv(gn_w), rowv(gn_b), ones_bd, tril, batch=batch, seq=seq, tt=tt)

    w_o = w_out.astype(BF16)
    return _out_proj(x2, o_nsa_t, o_rwkv, w_o[:NSA_WIDTH], w_o[NSA_WIDTH:], rowv(final_g), tm=4 * tm)


def kernel(x, norm_g, w_in, cmp_pos_k, cmp_w1_k, cmp_w2_k, cmp_pos_v, cmp_w1_v, cmp_w2_v, shift_mu, decay_w0, decay_up, iclr_a0, iclr_up, k_k, k_a, r_k, gn_w, gn_b, w_out, final_g):
    batch, seq, d = x.shape
    assert d == D_MODEL and norm_g.shape[0] == 1, "single-layer trunk"
    out = _layer(x.reshape(batch * seq, d), norm_g[0], w_in[0], cmp_pos_k[0], cmp_w1_k[0], cmp_w2_k[0],
                 cmp_pos_v[0], cmp_w1_v[0], cmp_w2_v[0], shift_mu[0], decay_w0[0], decay_up[0],
                 iclr_a0[0], iclr_up[0], k_k[0], k_a[0], r_k[0], gn_w[0], gn_b[0], w_out[0],
                 final_g, batch=batch, seq=seq)
    return out.reshape(batch, seq, d)
```

```python
import functools

import numpy as np
import jax
import jax.numpy as jnp
from jax import lax
from jax.experimental import pallas as pl
from jax.experimental.pallas import tpu as pltpu

F32 = jnp.float32
BF16 = jnp.bfloat16

D_MODEL = 1024
HEAD_DIM = 64
NSA_HEADS = 8
NSA_KV = 2
NSA_REP = NSA_HEADS // NSA_KV
RWKV_HEADS = 8
NSA_WIDTH = NSA_HEADS * HEAD_DIM
RWKV_WIDTH = RWKV_HEADS * HEAD_DIM
KV_WIDTH = NSA_KV * HEAD_DIM
ROPE_DIM = HEAD_DIM // 4
ROPE_HALF = ROPE_DIM // 2
ROPE_THETA = 500000.0
CMP_BLOCK = 32
CMP_STRIDE = 16
CMP_HIDDEN = 256
SEL_BLOCK = 64
SEL_TOPK = 8
WINDOW = 512
DECAY_RANK = 64
ICLR_RANK = 64
RWKV_SHIFT_WIDTH = 3 * RWKV_WIDTH + DECAY_RANK + ICLR_RANK
IN_SIZES = (NSA_WIDTH, KV_WIDTH, KV_WIDTH, KV_WIDTH, KV_WIDTH, KV_WIDTH, KV_WIDTH,
            3 * NSA_HEADS, NSA_WIDTH, RWKV_SHIFT_WIDTH, RWKV_WIDTH)
SCALE = HEAD_DIM ** -0.5
RMS_EPS = 1e-6
GN_EPS = 64e-5
NEG_INF = -1e30
FORCE_BONUS = 1e3

LANE = 128
SUBLANE = 8
BF16_SUBLANE = 2 * SUBLANE
GATE_ROWS = 2 * SUBLANE

T_KV = 0
T_RW = T_KV + 4 * KV_WIDTH
T_GR = T_RW + RWKV_SHIFT_WIDTH
T_END = T_GR + RWKV_WIDTH
R_Q = 0
R_VS = R_Q + NSA_WIDTH
R_VW = R_VS + KV_WIDTH
R_GL = R_VW + KV_WIDTH
R_GN = R_GL + NSA_KV * GATE_ROWS
R_END = R_GN + NSA_WIDTH

LOG2E = float(np.log2(np.e))
K_AUG = LANE
V_ROWS = HEAD_DIM + BF16_SUBLANE
KEY_TILE = 256
RWKV_CHUNK = 64
VMEM_LIMIT = 48 * 1024 * 1024


def _dot(a, b):
    return jnp.dot(a, b, preferred_element_type=F32)


def _dot_nt(a, b):
    return lax.dot_general(a, b, (((1,), (1,)), ((), ())), preferred_element_type=F32)


def _dot_tn(a, b):
    return lax.dot_general(a, b, (((0,), (0,)), ((), ())), preferred_element_type=F32)


def _bmm(a, b):
    return lax.dot_general(a, b, (((2,), (1,)), ((0,), (0,))), preferred_element_type=F32)


def _bmm_nt(a, b):
    return lax.dot_general(a, b, (((2,), (2,)), ((0,), (0,))), preferred_element_type=F32)


def _split3(x):
    hi = x.astype(BF16)
    r1 = x - hi.astype(F32)
    mid = r1.astype(BF16)
    lo = (r1 - mid.astype(F32)).astype(BF16)
    return hi, mid, lo


def _head_sums(x, ones_blk, terms=2):
    w = ones_blk.shape[0]
    parts = [x.astype(BF16)]
    if terms == 2:
        parts.append((x - parts[0].astype(F32)).astype(BF16))
    cols = []
    for i in range(0, x.shape[1], w):
        acc = _dot(parts[0][:, i:i + w], ones_blk)
        for t in parts[1:]:
            acc = acc + _dot(t[:, i:i + w], ones_blk)
        cols.append(acc)
    return jnp.concatenate(cols, axis=1)


def _x3_dot(w_bf16, x):
    hi, mid, lo = _split3(x)
    return _dot(w_bf16, hi) + _dot(w_bf16, mid) + _dot(w_bf16, lo)


def _sigmoid(x):
    return 0.5 * jnp.tanh(0.5 * x) + 0.5


def _interleave(*gens):
    live = list(gens)
    while live:
        for g in list(live):
            try:
                next(g)
            except StopIteration:
                live.remove(g)


def _rope128(t, ra, rm, rp):
    return t * ra + pltpu.roll(t, LANE - ROPE_HALF, 1) * rm + pltpu.roll(t, ROPE_HALF, 1) * rp


def _in_proj_kernel(x_ref, g_ref, wt_ref, wf_ref, ra_ref, rm_ref, rp_ref, cos_ref, sin_ref, oh_ref,
                    kc_ref, vc_ref, ks_ref, kw_ref, rw_ref, gr_ref,
                    qt_ref, qrt_ref, vst_ref, vwt_ref, gate_ref, gn_ref, cmp_scr):
    x = x_ref[...]
    ms = jnp.mean(x * x, axis=-1, keepdims=True)
    y = (x * lax.rsqrt(ms + RMS_EPS) * g_ref[...]).astype(BF16)
    tm = x.shape[0]

    kv = _dot_nt(y, wt_ref[T_KV:T_RW, :])
    for i, ref in enumerate((kc_ref, vc_ref)):
        cmp_scr[i] = kv[:, i * KV_WIDTH:(i + 1) * KV_WIDTH]
        for tau in range(CMP_STRIDE):
            piece = cmp_scr[i, pl.ds(tau, tm // CMP_STRIDE, stride=CMP_STRIDE), :].astype(BF16)
            for g in range(NSA_KV):
                ref[g, :, tau * HEAD_DIM:(tau + 1) * HEAD_DIM] = piece[:, g * HEAD_DIM:(g + 1) * HEAD_DIM]
    ra, rm, rp = ra_ref[...], rm_ref[...], rp_ref[...]
    for i, ref in ((2, ks_ref), (3, kw_ref)):
        t = _rope128(kv[:, i * LANE:(i + 1) * LANE], ra, rm, rp)
        for g in range(NSA_KV):
            tg = t[:, g * HEAD_DIM:(g + 1) * HEAD_DIM].astype(BF16)
            if ref is ks_ref:
                ref[g] = jnp.concatenate([tg, oh_ref[...]], axis=1)
            else:
                ref[g] = tg
    rw_ref[...] = _dot_nt(y, wt_ref[T_RW:T_GR, :])
    gr = _dot_nt(y, wt_ref[T_GR:T_END, :])
    gr_ref[...] = gr * _sigmoid(gr)

    def proj_t(r0, r1):
        return _dot_nt(wf_ref[r0:r1, :], y)

    qt = proj_t(R_Q, R_VS) * (SCALE * LOG2E)
    cos, sin = cos_ref[...], sin_ref[...]
    qt_ref[...] = qt.astype(BF16)
    for h in range(NSA_HEADS):
        r0 = h * HEAD_DIM
        t1 = qt[r0:r0 + ROPE_HALF]
        t2 = qt[r0 + ROPE_HALF:r0 + ROPE_DIM]
        qrt_ref[r0:r0 + ROPE_DIM, :] = jnp.concatenate(
            [t1 * cos - t2 * sin, t2 * cos + t1 * sin], axis=0).astype(BF16)
        qrt_ref[r0 + ROPE_DIM:r0 + HEAD_DIM, :] = qt[r0 + ROPE_DIM:r0 + HEAD_DIM].astype(BF16)
    vt = proj_t(R_VS, R_GL).astype(BF16)
    ones = jnp.ones((V_ROWS - HEAD_DIM, LANE), BF16)
    for j in range(vt.shape[1] // LANE):
        for i, ref in enumerate((vst_ref, vwt_ref)):
            for g in range(NSA_KV):
                r0 = i * KV_WIDTH + g * HEAD_DIM
                ref[j, g * V_ROWS:(g + 1) * V_ROWS, :] = jnp.concatenate(
                    [vt[r0:r0 + HEAD_DIM, j * LANE:(j + 1) * LANE], ones], axis=0)
    gate_ref[...] = _sigmoid(proj_t(R_GL, R_GN))
    gn = proj_t(R_GN, R_END)
    gn_ref[...] = gn * _sigmoid(gn)


def _in_proj(x2, norm_g, w_t, w_f, tabs, *, seq, tm):
    n = x2.shape[0]
    spt = seq // tm
    ra, rm, rp, cos, sin, onehot = tabs
    hm = lambda w: jax.ShapeDtypeStruct((NSA_KV, n, w), BF16)
    hspec = lambda w: pl.BlockSpec((NSA_KV, tm, w), lambda i: (0, i, 0))
    row = lambda w: pl.BlockSpec((tm, w), lambda i: (i, 0))
    col = lambda r: pl.BlockSpec((r, tm), lambda i: (0, i))
    full = lambda a: pl.BlockSpec(a.shape, lambda i: (0,) * a.ndim)
    tab = lambda w: pl.BlockSpec((tm, w), lambda i: (i % spt, 0))
    tabt = pl.BlockSpec((ROPE_HALF, tm), lambda i: (0, i % spt))
    vtile = pl.BlockSpec((tm // LANE, NSA_KV * V_ROWS, LANE), lambda i: (i, 0, 0))
    vsd = jax.ShapeDtypeStruct((n // LANE, NSA_KV * V_ROWS, LANE), BF16)
    cw = CMP_STRIDE * HEAD_DIM
    cspec = pl.BlockSpec((NSA_KV, tm // CMP_STRIDE, cw), lambda i: (0, i, 0))
    csd = jax.ShapeDtypeStruct((NSA_KV, n // CMP_STRIDE, cw), BF16)
    return pl.pallas_call(
        _in_proj_kernel,
        grid=(n // tm,),
        in_specs=[row(D_MODEL), full(norm_g), full(w_t), full(w_f), tab(LANE), tab(LANE), tab(LANE),
                  tabt, tabt, tab(K_AUG - HEAD_DIM)],
        out_specs=[cspec, cspec, hspec(K_AUG), hspec(HEAD_DIM), row(RWKV_SHIFT_WIDTH), row(RWKV_WIDTH),
                   col(NSA_WIDTH), col(NSA_WIDTH), vtile, vtile,
                   col(NSA_KV * GATE_ROWS), col(NSA_WIDTH)],
        out_shape=[csd, csd, hm(K_AUG), hm(HEAD_DIM),
                   jax.ShapeDtypeStruct((n, RWKV_SHIFT_WIDTH), F32),
                   jax.ShapeDtypeStruct((n, RWKV_WIDTH), F32),
                   jax.ShapeDtypeStruct((NSA_WIDTH, n), BF16),
                   jax.ShapeDtypeStruct((NSA_WIDTH, n), BF16),
                   vsd, vsd,
                   jax.ShapeDtypeStruct((NSA_KV * GATE_ROWS, n), F32),
                   jax.ShapeDtypeStruct((NSA_WIDTH, n), F32)],
        scratch_shapes=[pltpu.VMEM((2, tm, KV_WIDTH), F32)],
        compiler_params=pltpu.CompilerParams(dimension_semantics=("arbitrary",),
                                             vmem_limit_bytes=VMEM_LIMIT),
        name="in_proj",
    )(x2, norm_g, w_t, w_f, ra, rm, rp, cos, sin, onehot)


def _compress_kernel(kc_ref, vc_ref, pk_ref, w1k_ref, w2k_ref, pv_ref, w1v_ref, w2vt_ref,
                     ko_ref, vo_ref):
    half = CMP_STRIDE * HEAD_DIM
    ng, _, nch, _ = kc_ref.shape

    def hidden(c_ref, pos_ref, w1_ref):
        c = c_ref[:, 0].reshape(ng * nch, half)
        pos = jnp.broadcast_to(pos_ref[...], (BF16_SUBLANE, 2 * half)).astype(BF16)
        za = _dot(jnp.concatenate([c, pos[:, 0:half]], axis=0), w1_ref[0:half, :])
        zb = _dot(jnp.concatenate([c, pos[:, half:]], axis=0), w1_ref[half:2 * half, :])
        pv = za[ng * nch:ng * nch + 1] + zb[ng * nch:ng * nch + 1]
        hid = jnp.concatenate(
            [za[g * nch:(g + 1) * nch] + pltpu.roll(zb[g * nch:(g + 1) * nch], nch - 1, 0) for g in range(ng)],
            axis=0) + pv
        return (hid * _sigmoid(hid)).astype(BF16)

    ko = _dot(hidden(kc_ref, pk_ref, w1k_ref), w2k_ref[...]).astype(BF16)
    hv = hidden(vc_ref, pv_ref, w1v_ref)
    for g in range(ng):
        ko_ref[g, 0] = ko[g * nch:(g + 1) * nch]
        vo_ref[g, 0] = _dot_nt(w2vt_ref[...], hv[g * nch:(g + 1) * nch]).astype(BF16)


def _compress(kc_r, vc_r, pk, w1k, w2k, pv, w1v, w2vt):
    g, b, nch, width = kc_r.shape
    blk = pl.BlockSpec((g, 1, nch, width), lambda j: (0, j, 0, 0))
    full = lambda a: pl.BlockSpec(a.shape, lambda j: (0,) * a.ndim)
    return pl.pallas_call(
        _compress_kernel,
        grid=(b,),
        in_specs=[blk, blk, full(pk), full(w1k), full(w2k), full(pv), full(w1v), full(w2vt)],
        out_specs=[pl.BlockSpec((g, 1, nch, HEAD_DIM), lambda j: (0, j, 0, 0)),
                   pl.BlockSpec((g, 1, HEAD_DIM, nch), lambda j: (0, j, 0, 0))],
        out_shape=[jax.ShapeDtypeStruct((g, b, nch, HEAD_DIM), BF16),
                   jax.ShapeDtypeStruct((g, b, HEAD_DIM, nch), BF16)],
        compiler_params=pltpu.CompilerParams(dimension_semantics=("arbitrary",),
                                             vmem_limit_bytes=VMEM_LIMIT),
        name="compress",
    )(kc_r, vc_r, pk, w1k, w2k, pv, w1v, w2vt)


def _nsa_kernel(qt_ref, qrt_ref, kc_ref, vct_ref, ks_ref, vst_ref, kw_ref, vwt_ref,
                gate_ref, gn_ref, mt_ref, o_ref, *, tq, seq):
    tk = KEY_TILE
    n_win = WINDOW // tk
    qi = pl.program_id(2)
    q0 = qi * tq
    nl = NSA_REP * tq
    n_sel = seq // SEL_BLOCK
    ncp = kc_ref.shape[2]

    def heads_on_lanes(ref):
        return jnp.concatenate([ref[r * HEAD_DIM:(r + 1) * HEAD_DIM, :] for r in range(NSA_REP)], axis=1)

    def tile4(a):
        return jnp.concatenate([a] * NSA_REP, axis=1)

    qr = heads_on_lanes(qrt_ref)
    k_s = lax.broadcasted_iota(jnp.int32, (tk, tq), 0)
    t_l = q0 + lax.broadcasted_iota(jnp.int32, (tk, tq), 1)

    def update_steps(box, k_ref, vt_ref, q_op, kts, keeps):
        m_i, acc = box[0]

        def scores(i):
            sc = _dot(k_ref[0, pl.ds(pl.multiple_of(kts[i] * tk, tk), tk), :], q_op)
            if keeps[i] is None:
                return sc
            return jnp.concatenate([jnp.where(keeps[i], sc[:, r * tq:(r + 1) * tq], NEG_INF)
                                    for r in range(NSA_REP)], axis=1)

        sc_next = scores(0)
        yield
        for i, kt in enumerate(kts):
            sc = sc_next
            if i + 1 < len(kts):
                sc_next = scores(i + 1)
            m_n = jnp.maximum(m_i, jnp.max(sc, axis=0, keepdims=True))
            pe = jnp.exp2(sc - m_n).astype(BF16)
            yield
            vtb = jnp.concatenate([vt_ref[kt * (tk // LANE) + jj] for jj in range(tk // LANE)], axis=1)
            acc = jnp.exp2(m_i - m_n) * acc + _dot(vtb, pe)
            m_i = m_n
            yield
        box[0] = (m_i, acc)

    def update(carry, k_ref, vt_ref, q_op, kts, keeps):
        box = [carry]
        _interleave(update_steps(box, k_ref, vt_ref, q_op, kts, keeps))
        return box[0]

    def last_tiles(box, k_ref, vt_ref, q_op, low_keep, spare_tile):
        kts, keeps = [], []
        tiles = [(0, lambda d: d >= 0)] + [(i, None) for i in range(1, n_win)] + [(n_win, low_keep)]
        for back, keep_fn in tiles:
            exists = a >= back
            kt = jnp.where(exists, a - back, 0 if spare_tile is None else spare_tile)
            d = t_l - (kt * tk + k_s)
            kts.append(kt)
            if keep_fn is not None:
                keeps.append(keep_fn(d) & exists)
            else:
                keeps.append(exists if spare_tile is None else None)
        return update_steps(box, k_ref, vt_ref, q_op, kts, keeps)

    sel_out = {}

    def select_steps():
        s = _dot(kc_ref[0, 0], heads_on_lanes(qt_ref))
        t_c = q0 + lax.broadcasted_iota(jnp.int32, (ncp, tq), 1)
        c_c = lax.broadcasted_iota(jnp.int32, (ncp, tq), 0)
        cmask = tile4((c_c * CMP_STRIDE + (CMP_BLOCK - 1)) <= t_c)
        yield
        s = jnp.where(cmask, s, NEG_INF)
        m = jnp.max(s, axis=0, keepdims=True)
        e = jnp.where(cmask, jnp.exp2(s - m), 0.0)
        den = jnp.sum(e, axis=0, keepdims=True)
        p = e * (1.0 / jnp.where(den > 0.0, den, 1.0))
        sel_out["o_cmp"] = _dot(vct_ref[0, 0], p.astype(BF16))
        yield
        psum = p[:, 0:tq]
        for r in range(1, NSA_REP):
            psum = psum + p[:, r * tq:(r + 1) * tq]
        imp = _x3_dot(mt_ref[...], psum)
        j = lax.broadcasted_iota(jnp.int32, (n_sel, tq), 0)
        t = q0 + lax.broadcasted_iota(jnp.int32, (n_sel, tq), 1)
        tb = t // SEL_BLOCK
        forced = (j == 0) | (j == tb) | (j == tb - 1)
        val = jnp.where(j <= tb, imp + jnp.where(forced, FORCE_BONUS, 0.0), -1.0)
        yield
        vals = [val[g0:g0 + SUBLANE] for g0 in range(0, n_sel, SUBLANE)]
        cnts = [jnp.zeros((SUBLANE, tq), F32) for _ in vals]
        srow = lax.broadcasted_iota(jnp.int32, (SUBLANE, tq), 0)
        for i in range(n_sel):
            vi = jnp.broadcast_to(val[i:i + 1, :], (SUBLANE, tq))
            for g, vg in enumerate(vals):
                ge = lambda: jnp.where(vi >= vg, 1.0, 0.0)
                gt = lambda: jnp.where(vi > vg, 1.0, 0.0)
                if g * SUBLANE > i:
                    beat = ge()
                elif (g + 1) * SUBLANE <= i:
                    beat = gt()
                else:
                    beat = jnp.where(srow > i - g * SUBLANE, ge(), gt())
                cnts[g] = cnts[g] + beat
            if i % SUBLANE == SUBLANE - 1:
                yield
        cnt = jnp.concatenate(cnts, axis=0)
        sel_bias = jnp.where(cnt < float(SEL_TOPK), 0.0, NEG_INF).astype(BF16)
        sel_out["qr_sel"] = jnp.concatenate(
            [qr, tile4(sel_bias), jnp.zeros((K_AUG - HEAD_DIM - n_sel, nl), BF16)], axis=0)

    init = (jnp.full((1, nl), NEG_INF, F32), jnp.zeros((V_ROWS, nl), F32))
    a = (q0 + tq - 1) // tk
    win_box = [init]
    _interleave(last_tiles(win_box, kw_ref, vwt_ref, qr, lambda d: d < WINDOW, None), select_steps())
    c_win = win_box[0]
    o_cmp, qr_sel = sel_out["o_cmp"], sel_out["qr_sel"]

    n_old = jnp.maximum(a - n_win, 0)
    c_sel = lax.fori_loop(0, n_old % 2, lambda kt, c: update(c, ks_ref, vst_ref, qr_sel, [kt], [None]), init)
    c_sel = lax.fori_loop(0, n_old // 2,
                          lambda i, c: update(c, ks_ref, vst_ref, qr_sel,
                                              [n_old % 2 + 2 * i, n_old % 2 + 2 * i + 1], [None, None]), c_sel)
    assert (seq // tk - 1) * (tk // SEL_BLOCK) >= SEL_TOPK and WINDOW // SEL_BLOCK <= SEL_TOPK
    sel_box = [c_sel]
    _interleave(last_tiles(sel_box, ks_ref, vst_ref, qr_sel, None, seq // tk - 1))
    c_sel = sel_box[0]
    o_sel = c_sel[1][0:HEAD_DIM] * (1.0 / c_sel[1][HEAD_DIM:HEAD_DIM + 1])
    o_win = c_win[1][0:HEAD_DIM] * (1.0 / c_win[1][HEAD_DIM:HEAD_DIM + 1])

    gates = gate_ref[...]
    for r in range(NSA_REP):
        ls = slice(r * tq, (r + 1) * tq)
        o = (gates[3 * r:3 * r + 1, :] * o_cmp[:, ls] + gates[3 * r + 1:3 * r + 2, :] * o_sel[:, ls]
             + gates[3 * r + 2:3 * r + 3, :] * o_win[:, ls])
        rs = slice(r * HEAD_DIM, (r + 1) * HEAD_DIM)
        o_ref[rs, :] = (o * gn_ref[rs, :]).astype(BF16)


def _nsa(qt, qrt, kcmp, vcmpt, ksh, vst, kwh, vwt, gates, gnt, mt, *, batch, seq, tq):
    n = batch * seq
    nq = seq // tq
    ncp = kcmp.shape[2]
    grp = NSA_REP * HEAD_DIM
    qspec = pl.BlockSpec((grp, tq), lambda b, g, i: (g, b * nq + i))
    kspec = lambda w: pl.BlockSpec((1, seq, w), lambda b, g, i: (g, b, 0))
    vspec = pl.BlockSpec((seq // LANE, V_ROWS, LANE), lambda b, g, i: (b, g, 0))
    kern = functools.partial(_nsa_kernel, tq=tq, seq=seq)
    return pl.pallas_call(
        kern,
        grid=(batch, NSA_KV, nq),
        in_specs=[qspec, qspec,
                  pl.BlockSpec((1, 1, ncp, HEAD_DIM), lambda b, g, i: (g, b, 0, 0)),
                  pl.BlockSpec((1, 1, HEAD_DIM, ncp), lambda b, g, i: (g, b, 0, 0)),
                  kspec(K_AUG), vspec, kspec(HEAD_DIM), vspec,
                  pl.BlockSpec((GATE_ROWS, tq), lambda b, g, i: (g, b * nq + i)),
                  qspec,
                  pl.BlockSpec(mt.shape, lambda b, g, i: (0, 0))],
        out_specs=qspec,
        out_shape=jax.ShapeDtypeStruct((NSA_WIDTH, n), BF16),
        compiler_params=pltpu.CompilerParams(
            dimension_semantics=("arbitrary", "arbitrary", "arbitrary"),
            vmem_limit_bytes=VMEM_LIMIT),
        name="nsa",
    )(qt, qrt, kcmp, vcmpt, ksh, vst, kwh, vwt, gates, gnt, mt)


def _rwkv_kernel(p_ref, gr_ref, mu_ref, w0_ref, a0_ref, wab_ref, kk_ref, ka_ref, rk_ref,
                 gw_ref, gb_ref, ones_ref, tril_ref, o_ref, st_ref, carry_ref, *, tt):
    c = RWKV_CHUNK
    hd = HEAD_DIM
    nc = tt // c
    step = pl.program_id(1)

    @pl.when(step == 0)
    def _():
        st_ref[...] = jnp.zeros_like(st_ref)
        carry_ref[...] = jnp.zeros_like(carry_ref)

    p = p_ref[...]
    row = lax.broadcasted_iota(jnp.int32, p.shape, 0)
    prev = jnp.where(row == 0, carry_ref[...], pltpu.roll(p, 1, 0))
    carry_ref[...] = p[tt - 1:tt, :]
    ps = p + mu_ref[...] * (prev - p)
    r = ps[:, 0:RWKV_WIDTH]
    k = ps[:, RWKV_WIDTH:2 * RWKV_WIDTH]
    v = ps[:, 2 * RWKV_WIDTH:3 * RWKV_WIDTH]
    lora = ps[:, 3 * RWKV_WIDTH:]
    lane = lax.broadcasted_iota(jnp.int32, lora.shape, 1)
    feat = jnp.where(lane < DECAY_RANK, jnp.tanh(lora), lora).astype(BF16)
    up = _dot(feat, wab_ref[...])
    w = w0_ref[...] + up[:, 0:RWKV_WIDTH]
    lw = _sigmoid(w) * (-float(np.exp(-0.5)))
    a = _sigmoid(a0_ref[...] + up[:, RWKV_WIDTH:])
    ones_bd = ones_ref[...]
    kk = k * kk_ref[...]
    kkn = kk * lax.rsqrt(jnp.maximum(_head_sums(kk * kk, ones_bd), 1e-24))
    k2 = k * (1.0 + (a - 1.0) * ka_ref[...])
    alpha = -kkn
    beta = kkn * a
    bonus = _head_sums(r * k2 * rk_ref[...], ones_bd, terms=1) * v

    cum = _x3_dot(tril_ref[...], lw)
    cend = jnp.concatenate(
        [jnp.broadcast_to(cum[(ch + 1) * c - 1:(ch + 1) * c, :], (c, RWKV_WIDTH)) for ch in range(nc)], axis=0)
    e_neg = jnp.exp(-cum)
    pc = jnp.exp(cend)
    at = alpha * jnp.exp(cum - lw)
    bt = beta * e_neg
    kt = k2 * e_neg
    rt = r * jnp.exp(cum)
    bh = bt * pc
    kh = kt * pc

    npair = RWKV_WIDTH // LANE

    def pairs(x):
        return jnp.stack([x[ch * c:(ch + 1) * c, j * LANE:(j + 1) * LANE]
                          for ch in range(nc) for j in range(npair)], axis=0)

    at_p, rt_p, bt_p, kt_p, v_p, bh_p, kh_p = (pairs(t) for t in (at, rt, bt, kt, v, bh, kh))
    pc_p = jnp.stack([pc[ch * c:ch * c + 1, j * LANE:(j + 1) * LANE]
                      for ch in range(nc) for j in range(npair)], axis=0)
    lane_c = lax.broadcasted_iota(jnp.int32, (1, c, LANE), 2)
    row_c = lax.broadcasted_iota(jnp.int32, (1, c, LANE), 1)
    even_c = lane_c < hd
    col_c = jnp.where(even_c, lane_c, lane_c - hd)
    low_s = row_c > col_c
    low_i = row_c >= col_c
    lane_2c = lax.broadcasted_iota(jnp.int32, (1, 2 * c, LANE), 2)
    row_2c = lax.broadcasted_iota(jnp.int32, (1, 2 * c, LANE), 1)
    even_2c = lane_2c < hd
    on_bd = (row_2c < hd) == even_2c
    zero_c = jnp.zeros((1, c, LANE), BF16)

    def bd(x):
        xb = x.astype(BF16)
        return jnp.concatenate([jnp.where(even_c, xb, zero_c), jnp.where(even_c, zero_c, xb)], axis=1)

    def abd(x):
        xb = x.astype(BF16)
        return jnp.concatenate([jnp.where(even_c, zero_c, xb), jnp.where(even_c, xb, zero_c)], axis=1)

    la = jnp.concatenate([at_p, rt_p], axis=1).astype(BF16)
    zero_2c = jnp.zeros((1, 2 * c, LANE), BF16)
    r_e = _bmm_nt(jnp.where(even_2c, la, zero_2c), jnp.concatenate([bt_p, kt_p], axis=1).astype(BF16))
    r_o = _bmm_nt(jnp.where(even_2c, zero_2c, la), jnp.concatenate([kt_p, bt_p], axis=1).astype(BF16))
    nab = jnp.where(low_s, jnp.where(even_c, r_e[:, 0:c], r_o[:, 0:c]), 0.0)
    aak_sw = jnp.where(low_s, jnp.where(even_c, r_o[:, 0:c], r_e[:, 0:c]), 0.0).astype(BF16)
    arb = jnp.where(low_i, jnp.where(even_c, r_e[:, c:], r_o[:, c:]), 0.0).astype(BF16)
    ark_sw = jnp.where(low_i, jnp.where(even_c, r_o[:, c:], r_e[:, c:]), 0.0).astype(BF16)
    tinv = jnp.where(row_c == col_c, 1.0, 0.0) + nab
    npow = _bmm(nab.astype(BF16), bd(nab))
    n_dbl = 5
    for it in range(n_dbl):
        nbd = bd(npow)
        if it + 1 < n_dbl:
            res = _bmm(jnp.concatenate([tinv, npow], axis=1).astype(BF16), nbd)
            tinv = tinv + res[:, 0:c]
            npow = res[:, c:]
        else:
            tinv = tinv + _bmm(tinv.astype(BF16), nbd)
    av = _bmm(jnp.concatenate([aak_sw, ark_sw], axis=1), abd(v_p))
    tx = _bmm(tinv.astype(BF16), jnp.concatenate([bd(av[:, 0:c]), bd(at_p)], axis=2))
    u0, ta = tx[:, :, 0:LANE], tx[:, :, LANE:]
    ax = _bmm(arb, jnp.concatenate([bd(ta), bd(u0)], axis=2))
    rq = rt_p + ax[:, :, 0:LANE]
    y0 = ax[:, :, LANE:] + av[:, c:]
    v_b = v_p.astype(BF16)
    w_f = jnp.concatenate([jnp.concatenate([ta, u0], axis=2).astype(BF16),
                           jnp.concatenate([jnp.zeros_like(v_b), v_b], axis=2)], axis=1)
    gh = _bmm(jnp.concatenate([jnp.swapaxes(bh_p, 1, 2), jnp.swapaxes(kh_p, 1, 2)], axis=2).astype(BF16), w_f)
    g_bd = jnp.where(on_bd, gh[:, :, 0:LANE], 0.0) + jnp.where(row_2c == lane_2c, pc_p, 0.0)
    h_bd = jnp.where(on_bd, gh[:, :, LANE:], 0.0)
    lhs = jnp.concatenate([rq, g_bd], axis=1).astype(BF16)

    st = st_ref[...]
    ys = []
    for ch in range(nc):
        sl = slice(ch * npair, (ch + 1) * npair)
        res = _bmm(lhs[sl], st.astype(BF16))
        yc = res[:, 0:c, :] + y0[sl]
        st = res[:, c:, :] + h_bd[sl]
        ys.append(jnp.concatenate([yc[j] for j in range(npair)], axis=1))
    st_ref[...] = st
    y = jnp.concatenate(ys, axis=0) if nc > 1 else ys[0]

    inv_hd = 1.0 / hd
    mean = _head_sums(y, ones_bd, terms=1) * inv_hd
    ycen = y - mean
    var = _head_sums(ycen * ycen, ones_bd, terms=1) * inv_hd
    yn = ycen * lax.rsqrt(var + GN_EPS) * gw_ref[...] + gb_ref[...]
    o_ref[...] = ((yn + bonus) * gr_ref[...]).astype(BF16)


def _rwkv(rw, gr, mu, w0, a0, wab, kk, ka, rk, gw, gb, ones_bd, tril, *, batch, seq, tt):
    n = batch * seq
    ns = seq // tt
    row = lambda w: pl.BlockSpec((tt, w), lambda b, i: (b * ns + i, 0))
    full = lambda a: pl.BlockSpec(a.shape, lambda b, i: (0,) * a.ndim)
    kern = functools.partial(_rwkv_kernel, tt=tt)
    consts = (mu, w0, a0, wab, kk, ka, rk, gw, gb, ones_bd, tril)
    return pl.pallas_call(
        kern,
        grid=(batch, ns),
        in_specs=[row(RWKV_SHIFT_WIDTH), row(RWKV_WIDTH)] + [full(a) for a in consts],
        out_specs=row(RWKV_WIDTH),
        out_shape=jax.ShapeDtypeStruct((n, RWKV_WIDTH), BF16),
        scratch_shapes=[pltpu.VMEM((RWKV_WIDTH // LANE, LANE, LANE), F32),
                        pltpu.VMEM((1, RWKV_SHIFT_WIDTH), F32)],
        compiler_params=pltpu.CompilerParams(dimension_semantics=("arbitrary", "arbitrary"),
                                             vmem_limit_bytes=VMEM_LIMIT),
        name="rwkv",
    )(rw, gr, *consts)


def _out_kernel(x_ref, ont_ref, or_ref, wn_ref, wr_ref, g_ref, o_ref):
    h = x_ref[...] + _dot_tn(ont_ref[...], wn_ref[...]) + _dot(or_ref[...], wr_ref[...])
    ms = jnp.mean(h * h, axis=-1, keepdims=True)
    o_ref[...] = h * lax.rsqrt(ms + RMS_EPS) * g_ref[...]


def _out_proj(x2, o_nsa_t, o_rwkv, wn, wr, final_g, *, tm):
    n = x2.shape[0]
    row = lambda w: pl.BlockSpec((tm, w), lambda i: (i, 0))
    full = lambda a: pl.BlockSpec(a.shape, lambda i: (0,) * a.ndim)
    return pl.pallas_call(
        _out_kernel,
        grid=(n // tm,),
        in_specs=[row(D_MODEL), pl.BlockSpec((NSA_WIDTH, tm), lambda i: (0, i)), row(RWKV_WIDTH),
                  full(wn), full(wr), full(final_g)],
        out_specs=row(D_MODEL),
        out_shape=jax.ShapeDtypeStruct((n, D_MODEL), F32),
        compiler_params=pltpu.CompilerParams(dimension_semantics=("arbitrary",),
                                             vmem_limit_bytes=VMEM_LIMIT),
        name="out_proj",
    )(x2, o_nsa_t, o_rwkv, wn, wr, final_g)


def _rope_tables(seq):
    inv = ROPE_THETA ** (-np.arange(ROPE_HALF, dtype=np.float64) / ROPE_HALF)
    ang = np.arange(seq, dtype=np.float64)[:, None] * inv[None, :]
    cos, sin = np.cos(ang), np.sin(ang)
    ra = np.ones((seq, HEAD_DIM)); rm = np.zeros((seq, HEAD_DIM)); rp = np.zeros((seq, HEAD_DIM))
    ra[:, :ROPE_HALF] = cos; ra[:, ROPE_HALF:ROPE_DIM] = cos
    rm[:, :ROPE_HALF] = -sin
    rp[:, ROPE_HALF:ROPE_DIM] = sin
    rep = lambda t: jnp.asarray(np.tile(t, (1, LANE // HEAD_DIM)), F32)
    assert seq // SEL_BLOCK <= K_AUG - HEAD_DIM
    onehot = np.zeros((seq, K_AUG - HEAD_DIM))
    onehot[np.arange(seq), np.arange(seq) // SEL_BLOCK] = 1.0
    return (rep(ra), rep(rm), rep(rp), jnp.asarray(cos.T, F32), jnp.asarray(sin.T, F32),
            jnp.asarray(onehot, BF16))


def _cmp_to_sel_t(n_cmp_pad, n_sel):
    n_cmp = n_cmp_pad - 1
    c0 = np.arange(n_cmp)[:, None] * CMP_STRIDE
    s0 = np.arange(n_sel)[None, :] * SEL_BLOCK
    ov = np.clip(np.minimum(c0 + CMP_BLOCK, s0 + SEL_BLOCK) - np.maximum(c0, s0), 0, None) / CMP_BLOCK
    mt = np.zeros((n_sel, n_cmp_pad))
    mt[:, :n_cmp] = ov.T
    return jnp.asarray(mt, BF16)


def _prep_w_in(w_in):
    idx = np.cumsum(IN_SIZES)[:-1].tolist()
    q, kc, vc, ks, vs, kw, vw, gl, gn, rw, gr = jnp.split(w_in, idx, axis=1)
    w_t = jnp.concatenate([kc, vc, ks, kw, rw, gr], axis=1).T.astype(BF16)
    per_g = 3 * NSA_REP
    pad = jnp.zeros((D_MODEL, GATE_ROWS - per_g), w_in.dtype)
    gl_p = [t for g in range(NSA_KV) for t in (gl[:, g * per_g:(g + 1) * per_g], pad)]
    w_f = jnp.concatenate([q, vs, vw] + gl_p + [gn], axis=1).T.astype(BF16)
    return w_t, w_f


def _layer(x2, norm_g, w_in, cmp_pos_k, cmp_w1_k, cmp_w2_k, cmp_pos_v, cmp_w1_v, cmp_w2_v,
           shift_mu, decay_w0, decay_up, iclr_a0, iclr_up, k_k, k_a, r_k, gn_w, gn_b, w_out,
           final_g, *, batch, seq):
    tm = 256
    tq = 256
    tt = 256
    assert WINDOW % KEY_TILE == 0 and seq % KEY_TILE == 0 and KEY_TILE % tq == 0
    nch = seq // CMP_STRIDE
    n_sel = seq // SEL_BLOCK
    rowv = lambda t: t.reshape(1, -1).astype(F32)

    w_t, w_f = _prep_w_in(w_in)
    (kch, vch, ksh, kwh, rw, gr, qt, qrt, vst, vwt, gates, gnt) = _in_proj(
        x2, rowv(norm_g), w_t, w_f, _rope_tables(seq), seq=seq, tm=2 * tm)

    chunks = lambda t: t.reshape(NSA_KV, batch, nch, CMP_STRIDE * HEAD_DIM)
    kcmp, vcmpt = _compress(chunks(kch), chunks(vch),
                            rowv(cmp_pos_k), cmp_w1_k.astype(BF16), cmp_w2_k.astype(BF16),
                            rowv(cmp_pos_v), cmp_w1_v.astype(BF16), cmp_w2_v.T.astype(BF16))

    o_nsa_t = _nsa(qt, qrt, kcmp, vcmpt, ksh, vst, kwh, vwt, gates, gnt,
                   _cmp_to_sel_t(nch, n_sel), batch=batch, seq=seq, tq=tq)

    z = jnp.zeros((DECAY_RANK, RWKV_WIDTH), F32)
    wab = jnp.concatenate([jnp.concatenate([decay_up, z], axis=1),
                           jnp.concatenate([z, iclr_up], axis=1)], axis=0).astype(BF16)
    hid = np.arange(2 * LANE) // HEAD_DIM
    ones_bd = jnp.asarray(hid[:, None] == hid[None, :], BF16)
    ti = np.arange(tt)
    tril = jnp.asarray((ti[:, None] >= ti[None, :]) & (ti[:, None] // RWKV_CHUNK == ti[None, :] // RWKV_CHUNK), BF16)
    o_rwkv = _rwkv(rw, gr, rowv(shift_mu), rowv(decay_w0), rowv(iclr_a0), wab, rowv(k_k), rowv(k_a),
                   rowv(r_k), rowv(gn_w), rowv(gn_b), ones_bd, tril, batch=batch, seq=seq, tt=tt)

    w_o = w_out.astype(BF16)
    return _out_proj(x2, o_nsa_t, o_rwkv, w_o[:NSA_WIDTH], w_o[NSA_WIDTH:], rowv(final_g), tm=4 * tm)


def kernel(x, norm_g, w_in, cmp_pos_k, cmp_w1_k, cmp_w2_k, cmp_pos_v, cmp_w1_v, cmp_w2_v, shift_mu, decay_w0, decay_up, iclr_a0, iclr_up, k_k, k_a, r_k, gn_w, gn_b, w_out, final_g):
    batch, seq, d = x.shape
    assert d == D_MODEL and norm_g.shape[0] == 1, "single-layer trunk"
    out = _layer(x.reshape(batch * seq, d), norm_g[0], w_in[0], cmp_pos_k[0], cmp_w1_k[0], cmp_w2_k[0],
                 cmp_pos_v[0], cmp_w1_v[0], cmp_w2_v[0], shift_mu[0], decay_w0[0], decay_up[0],
                 iclr_a0[0], iclr_up[0], k_k[0], k_a[0], r_k[0], gn_w[0], gn_b[0], w_out[0],
                 final_g, batch=batch, seq=seq)
    return out.reshape(batch, seq, d)
```

```python
import functools

import numpy as np
import jax
import jax.numpy as jnp
from jax import lax
from jax.experimental import pallas as pl
from jax.experimental.pallas import tpu as pltpu

F32 = jnp.float32
BF16 = jnp.bfloat16

D_MODEL = 1024
HEAD_DIM = 64
NSA_HEADS = 8
NSA_KV = 2
NSA_REP = NSA_HEADS // NSA_KV
RWKV_HEADS = 8
NSA_WIDTH = NSA_HEADS * HEAD_DIM
RWKV_WIDTH = RWKV_HEADS * HEAD_DIM
KV_WIDTH = NSA_KV * HEAD_DIM
ROPE_DIM = HEAD_DIM // 4
ROPE_HALF = ROPE_DIM // 2
ROPE_THETA = 500000.0
CMP_BLOCK = 32
CMP_STRIDE = 16
CMP_HIDDEN = 256
SEL_BLOCK = 64
SEL_TOPK = 8
WINDOW = 512
DECAY_RANK = 64
ICLR_RANK = 64
RWKV_SHIFT_WIDTH = 3 * RWKV_WIDTH + DECAY_RANK + ICLR_RANK
IN_SIZES = (NSA_WIDTH, KV_WIDTH, KV_WIDTH, KV_WIDTH, KV_WIDTH, KV_WIDTH, KV_WIDTH,
            3 * NSA_HEADS, NSA_WIDTH, RWKV_SHIFT_WIDTH, RWKV_WIDTH)
SCALE = HEAD_DIM ** -0.5
RMS_EPS = 1e-6
GN_EPS = 64e-5
NEG_INF = -1e30
FORCE_BONUS = 1e3

LANE = 128
SUBLANE = 8
BF16_SUBLANE = 2 * SUBLANE
GATE_ROWS = 2 * BF16_SUBLANE

T_KV = 0
T_RW = T_KV + 4 * KV_WIDTH
T_GR = T_RW + RWKV_SHIFT_WIDTH
T_END = T_GR + RWKV_WIDTH
R_Q = 0
R_VS = R_Q + NSA_WIDTH
R_VW = R_VS + KV_WIDTH
R_GL = R_VW + KV_WIDTH
R_GN = R_GL + GATE_ROWS
R_END = R_GN + NSA_WIDTH

LOG2E = float(np.log2(np.e))
K_AUG = LANE
V_ROWS = HEAD_DIM + BF16_SUBLANE
KEY_TILE = 256
RWKV_CHUNK = 64
VMEM_LIMIT = 48 * 1024 * 1024


def _dot(a, b):
    return jnp.dot(a, b, preferred_element_type=F32)


def _dot_nt(a, b):
    return lax.dot_general(a, b, (((1,), (1,)), ((), ())), preferred_element_type=F32)


def _dot_tn(a, b):
    return lax.dot_general(a, b, (((0,), (0,)), ((), ())), preferred_element_type=F32)


def _bmm(a, b):
    return lax.dot_general(a, b, (((2,), (1,)), ((0,), (0,))), preferred_element_type=F32)


def _bmm_nt(a, b):
    return lax.dot_general(a, b, (((2,), (2,)), ((0,), (0,))), preferred_element_type=F32)


def _split3(x):
    hi = x.astype(BF16)
    r1 = x - hi.astype(F32)
    mid = r1.astype(BF16)
    lo = (r1 - mid.astype(F32)).astype(BF16)
    return hi, mid, lo


def _head_sums(x, ones_blk, terms=2):
    w = ones_blk.shape[0]
    parts = [x.astype(BF16)]
    if terms == 2:
        parts.append((x - parts[0].astype(F32)).astype(BF16))
    cols = []
    for i in range(0, x.shape[1], w):
        acc = _dot(parts[0][:, i:i + w], ones_blk)
        for t in parts[1:]:
            acc = acc + _dot(t[:, i:i + w], ones_blk)
        cols.append(acc)
    return jnp.concatenate(cols, axis=1)


def _x3_dot(w_bf16, x):
    hi, mid, lo = _split3(x)
    return _dot(w_bf16, hi) + _dot(w_bf16, mid) + _dot(w_bf16, lo)


def _sigmoid(x):
    return 0.5 * jnp.tanh(0.5 * x) + 0.5


def _interleave(*gens):
    live = list(gens)
    while live:
        for g in list(live):
            try:
                next(g)
            except StopIteration:
                live.remove(g)


def _rope128(t, ra, rm, rp):
    return t * ra + pltpu.roll(t, LANE - ROPE_HALF, 1) * rm + pltpu.roll(t, ROPE_HALF, 1) * rp


def _in_proj_kernel(x_ref, g_ref, wt_ref, wf_ref, ra_ref, rm_ref, rp_ref, cos_ref, sin_ref, oh_ref,
                    kc_ref, vc_ref, ks_ref, kw_ref, rw_ref, gr_ref,
                    qt_ref, qrt_ref, vst_ref, vwt_ref, gate_ref, gn_ref, cmp_scr):
    x = x_ref[...]
    ms = jnp.mean(x * x, axis=-1, keepdims=True)
    y = (x * lax.rsqrt(ms + RMS_EPS) * g_ref[...]).astype(BF16)
    tm = x.shape[0]

    kv = _dot_nt(y, wt_ref[T_KV:T_RW, :])
    for i, ref in enumerate((kc_ref, vc_ref)):
        cmp_scr[i] = kv[:, i * KV_WIDTH:(i + 1) * KV_WIDTH]
        for tau in range(CMP_STRIDE):
            piece = cmp_scr[i, pl.ds(tau, tm // CMP_STRIDE, stride=CMP_STRIDE), :].astype(BF16)
            for g in range(NSA_KV):
                ref[g, :, tau * HEAD_DIM:(tau + 1) * HEAD_DIM] = piece[:, g * HEAD_DIM:(g + 1) * HEAD_DIM]
    ra, rm, rp = ra_ref[...], rm_ref[...], rp_ref[...]
    for i, ref in ((2, ks_ref), (3, kw_ref)):
        t = _rope128(kv[:, i * LANE:(i + 1) * LANE], ra, rm, rp)
        for g in range(NSA_KV):
            tg = t[:, g * HEAD_DIM:(g + 1) * HEAD_DIM].astype(BF16)
            if ref is ks_ref:
                ref[g] = jnp.concatenate([tg, oh_ref[...]], axis=1)
            else:
                ref[g] = tg
    rw_ref[...] = _dot_nt(y, wt_ref[T_RW:T_GR, :])
    gr = _dot_nt(y, wt_ref[T_GR:T_END, :])
    gr_ref[...] = gr * _sigmoid(gr)

    def proj_t(r0, r1):
        return _dot_nt(wf_ref[r0:r1, :], y)

    qt = proj_t(R_Q, R_VS) * (SCALE * LOG2E)
    cos, sin = cos_ref[...], sin_ref[...]
    qt_ref[...] = qt.astype(BF16)
    for h in range(NSA_HEADS):
        r0 = h * HEAD_DIM
        t1 = qt[r0:r0 + ROPE_HALF]
        t2 = qt[r0 + ROPE_HALF:r0 + ROPE_DIM]
        qrt_ref[r0:r0 + ROPE_DIM, :] = jnp.concatenate(
            [t1 * cos - t2 * sin, t2 * cos + t1 * sin], axis=0).astype(BF16)
        qrt_ref[r0 + ROPE_DIM:r0 + HEAD_DIM, :] = qt[r0 + ROPE_DIM:r0 + HEAD_DIM].astype(BF16)
    vt = proj_t(R_VS, R_GL).astype(BF16)
    ones = jnp.ones((V_ROWS - HEAD_DIM, LANE), BF16)
    for j in range(vt.shape[1] // LANE):
        for i, ref in enumerate((vst_ref, vwt_ref)):
            for g in range(NSA_KV):
                r0 = i * KV_WIDTH + g * HEAD_DIM
                ref[j, g * V_ROWS:(g + 1) * V_ROWS, :] = jnp.concatenate(
                    [vt[r0:r0 + HEAD_DIM, j * LANE:(j + 1) * LANE], ones], axis=0)
    gate_ref[...] = _sigmoid(proj_t(R_GL, R_GN))
    gn = proj_t(R_GN, R_END)
    gn_ref[...] = gn * _sigmoid(gn)


def _in_proj(x2, norm_g, w_t, w_f, tabs, *, seq, tm):
    n = x2.shape[0]
    spt = seq // tm
    ra, rm, rp, cos, sin, onehot = tabs
    hm = lambda w: jax.ShapeDtypeStruct((NSA_KV, n, w), BF16)
    hspec = lambda w: pl.BlockSpec((NSA_KV, tm, w), lambda i: (0, i, 0))
    row = lambda w: pl.BlockSpec((tm, w), lambda i: (i, 0))
    col = lambda r: pl.BlockSpec((r, tm), lambda i: (0, i))
    full = lambda a: pl.BlockSpec(a.shape, lambda i: (0,) * a.ndim)
    tab = lambda w: pl.BlockSpec((tm, w), lambda i: (i % spt, 0))
    tabt = pl.BlockSpec((ROPE_HALF, tm), lambda i: (0, i % spt))
    vtile = pl.BlockSpec((tm // LANE, NSA_KV * V_ROWS, LANE), lambda i: (i, 0, 0))
    vsd = jax.ShapeDtypeStruct((n // LANE, NSA_KV * V_ROWS, LANE), BF16)
    cw = CMP_STRIDE * HEAD_DIM
    cspec = pl.BlockSpec((NSA_KV, tm // CMP_STRIDE, cw), lambda i: (0, i, 0))
    csd = jax.ShapeDtypeStruct((NSA_KV, n // CMP_STRIDE, cw), BF16)
    return pl.pallas_call(
        _in_proj_kernel,
        grid=(n // tm,),
        in_specs=[row(D_MODEL), full(norm_g), full(w_t), full(w_f), tab(LANE), tab(LANE), tab(LANE),
                  tabt, tabt, tab(K_AUG - HEAD_DIM)],
        out_specs=[cspec, cspec, hspec(K_AUG), hspec(HEAD_DIM), row(RWKV_SHIFT_WIDTH), row(RWKV_WIDTH),
                   col(NSA_WIDTH), col(NSA_WIDTH), vtile, vtile,
                   col(GATE_ROWS), col(NSA_WIDTH)],
        out_shape=[csd, csd, hm(K_AUG), hm(HEAD_DIM),
                   jax.ShapeDtypeStruct((n, RWKV_SHIFT_WIDTH), F32),
                   jax.ShapeDtypeStruct((n, RWKV_WIDTH), F32),
                   jax.ShapeDtypeStruct((NSA_WIDTH, n), BF16),
                   jax.ShapeDtypeStruct((NSA_WIDTH, n), BF16),
                   vsd, vsd,
                   jax.ShapeDtypeStruct((GATE_ROWS, n), F32),
                   jax.ShapeDtypeStruct((NSA_WIDTH, n), F32)],
        scratch_shapes=[pltpu.VMEM((2, tm, KV_WIDTH), F32)],
        compiler_params=pltpu.CompilerParams(dimension_semantics=("arbitrary",),
                                             vmem_limit_bytes=VMEM_LIMIT),
        name="in_proj",
    )(x2, norm_g, w_t, w_f, ra, rm, rp, cos, sin, onehot)


def _compress_kernel(kc_ref, vc_ref, pk_ref, w1k_ref, w2k_ref, pv_ref, w1v_ref, w2vt_ref,
                     ko_ref, vo_ref):
    half = CMP_STRIDE * HEAD_DIM
    ng, _, nch, _ = kc_ref.shape

    def hidden(c_ref, pos_ref, w1_ref):
        c = c_ref[:, 0].reshape(ng * nch, half)
        pos = jnp.broadcast_to(pos_ref[...], (BF16_SUBLANE, 2 * half)).astype(BF16)
        za = _dot(jnp.concatenate([c, pos[:, 0:half]], axis=0), w1_ref[0:half, :])
        zb = _dot(jnp.concatenate([c, pos[:, half:]], axis=0), w1_ref[half:2 * half, :])
        pv = za[ng * nch:ng * nch + 1] + zb[ng * nch:ng * nch + 1]
        hid = jnp.concatenate(
            [za[g * nch:(g + 1) * nch] + pltpu.roll(zb[g * nch:(g + 1) * nch], nch - 1, 0) for g in range(ng)],
            axis=0) + pv
        return (hid * _sigmoid(hid)).astype(BF16)

    ko = _dot(hidden(kc_ref, pk_ref, w1k_ref), w2k_ref[...]).astype(BF16)
    hv = hidden(vc_ref, pv_ref, w1v_ref)
    for g in range(ng):
        ko_ref[g, 0] = ko[g * nch:(g + 1) * nch]
        vo_ref[g, 0] = _dot_nt(w2vt_ref[...], hv[g * nch:(g + 1) * nch]).astype(BF16)


def _compress(kc_r, vc_r, pk, w1k, w2k, pv, w1v, w2vt):
    g, b, nch, width = kc_r.shape
    blk = pl.BlockSpec((g, 1, nch, width), lambda j: (0, j, 0, 0))
    full = lambda a: pl.BlockSpec(a.shape, lambda j: (0,) * a.ndim)
    return pl.pallas_call(
        _compress_kernel,
        grid=(b,),
        in_specs=[blk, blk, full(pk), full(w1k), full(w2k), full(pv), full(w1v), full(w2vt)],
        out_specs=[pl.BlockSpec((g, 1, nch, HEAD_DIM), lambda j: (0, j, 0, 0)),
                   pl.BlockSpec((g, 1, HEAD_DIM, nch), lambda j: (0, j, 0, 0))],
        out_shape=[jax.ShapeDtypeStruct((g, b, nch, HEAD_DIM), BF16),
                   jax.ShapeDtypeStruct((g, b, HEAD_DIM, nch), BF16)],
        compiler_params=pltpu.CompilerParams(dimension_semantics=("arbitrary",),
                                             vmem_limit_bytes=VMEM_LIMIT),
        name="compress",
    )(kc_r, vc_r, pk, w1k, w2k, pv, w1v, w2vt)


def _nsa_kernel(qt_ref, qrt_ref, kc_ref, vct_ref, ks_ref, vst_ref, kw_ref, vwt_ref,
                gate_ref, gn_ref, mt_ref, o_ref, *, tq, seq):
    tk = KEY_TILE
    n_win = WINDOW // tk
    qi = pl.program_id(2)
    q0 = qi * tq
    nl = NSA_REP * tq
    n_sel = seq // SEL_BLOCK
    ncp = kc_ref.shape[2]

    def heads_on_lanes(ref):
        return jnp.concatenate([ref[r * HEAD_DIM:(r + 1) * HEAD_DIM, :] for r in range(NSA_REP)], axis=1)

    def tile4(a):
        return jnp.concatenate([a] * NSA_REP, axis=1)

    qr = heads_on_lanes(qrt_ref)
    k_s = lax.broadcasted_iota(jnp.int32, (tk, tq), 0)
    t_l = q0 + lax.broadcasted_iota(jnp.int32, (tk, tq), 1)

    def update_steps(box, k_ref, vt_ref, q_op, kts, keeps):
        m_i, acc = box[0]

        def scores(i):
            sc = _dot(k_ref[0, pl.ds(pl.multiple_of(kts[i] * tk, tk), tk), :], q_op)
            if keeps[i] is None:
                return sc
            return jnp.concatenate([jnp.where(keeps[i], sc[:, r * tq:(r + 1) * tq], NEG_INF)
                                    for r in range(NSA_REP)], axis=1)

        sc_next = scores(0)
        yield
        for i, kt in enumerate(kts):
            sc = sc_next
            if i + 1 < len(kts):
                sc_next = scores(i + 1)
            m_n = jnp.maximum(m_i, jnp.max(sc, axis=0, keepdims=True))
            pe = jnp.exp2(sc - m_n).astype(BF16)
            yield
            vtb = jnp.concatenate([vt_ref[kt * (tk // LANE) + jj] for jj in range(tk // LANE)], axis=1)
            acc = jnp.exp2(m_i - m_n) * acc + _dot(vtb, pe)
            m_i = m_n
            yield
        box[0] = (m_i, acc)

    def update(carry, k_ref, vt_ref, q_op, kts, keeps):
        box = [carry]
        _interleave(update_steps(box, k_ref, vt_ref, q_op, kts, keeps))
        return box[0]

    def last_tiles(box, k_ref, vt_ref, q_op, low_keep, spare_tile):
        kts, keeps = [], []
        tiles = [(0, lambda d: d >= 0)] + [(i, None) for i in range(1, n_win)] + [(n_win, low_keep)]
        for back, keep_fn in tiles:
            exists = a >= back
            kt = jnp.where(exists, a - back, 0 if spare_tile is None else spare_tile)
            d = t_l - (kt * tk + k_s)
            kts.append(kt)
            if keep_fn is not None:
                keeps.append(keep_fn(d) & exists)
            else:
                keeps.append(exists if spare_tile is None else None)
        return update_steps(box, k_ref, vt_ref, q_op, kts, keeps)

    sel_out = {}

    def select_steps():
        s = _dot(kc_ref[0, 0], heads_on_lanes(qt_ref))
        t_c = q0 + lax.broadcasted_iota(jnp.int32, (ncp, tq), 1)
        c_c = lax.broadcasted_iota(jnp.int32, (ncp, tq), 0)
        cmask = tile4((c_c * CMP_STRIDE + (CMP_BLOCK - 1)) <= t_c)
        yield
        s = jnp.where(cmask, s, NEG_INF)
        m = jnp.max(s, axis=0, keepdims=True)
        e = jnp.where(cmask, jnp.exp2(s - m), 0.0)
        den = jnp.sum(e, axis=0, keepdims=True)
        p = e * (1.0 / jnp.where(den > 0.0, den, 1.0))
        sel_out["o_cmp"] = _dot(vct_ref[0, 0], p.astype(BF16))
        yield
        psum = p[:, 0:tq]
        for r in range(1, NSA_REP):
            psum = psum + p[:, r * tq:(r + 1) * tq]
        imp = _x3_dot(mt_ref[...], psum)
        j = lax.broadcasted_iota(jnp.int32, (n_sel, tq), 0)
        t = q0 + lax.broadcasted_iota(jnp.int32, (n_sel, tq), 1)
        tb = t // SEL_BLOCK
        forced = (j == 0) | (j == tb) | (j == tb - 1)
        val = jnp.where(j <= tb, imp + jnp.where(forced, FORCE_BONUS, 0.0), -1.0)
        yield
        vals = [val[g0:g0 + SUBLANE] for g0 in range(0, n_sel, SUBLANE)]
        cnts = [jnp.zeros((SUBLANE, tq), F32) for _ in vals]
        srow = lax.broadcasted_iota(jnp.int32, (SUBLANE, tq), 0)
        for i in range(n_sel):
            vi = jnp.broadcast_to(val[i:i + 1, :], (SUBLANE, tq))
            for g, vg in enumerate(vals):
                ge = lambda: jnp.where(vi >= vg, 1.0, 0.0)
                gt = lambda: jnp.where(vi > vg, 1.0, 0.0)
                if g * SUBLANE > i:
                    beat = ge()
                elif (g + 1) * SUBLANE <= i:
                    beat = gt()
                else:
                    beat = jnp.where(srow > i - g * SUBLANE, ge(), gt())
                cnts[g] = cnts[g] + beat
            if i % SUBLANE == SUBLANE - 1:
                yield
        cnt = jnp.concatenate(cnts, axis=0)
        sel_bias = jnp.where(cnt < float(SEL_TOPK), 0.0, NEG_INF).astype(BF16)
        sel_out["qr_sel"] = jnp.concatenate(
            [qr, tile4(sel_bias), jnp.zeros((K_AUG - HEAD_DIM - n_sel, nl), BF16)], axis=0)

    init = (jnp.full((1, nl), NEG_INF, F32), jnp.zeros((V_ROWS, nl), F32))
    a = (q0 + tq - 1) // tk
    win_box = [init]
    _interleave(last_tiles(win_box, kw_ref, vwt_ref, qr, lambda d: d < WINDOW, None), select_steps())
    c_win = win_box[0]
    o_cmp, qr_sel = sel_out["o_cmp"], sel_out["qr_sel"]

    n_old = jnp.maximum(a - n_win, 0)
    c_sel = lax.fori_loop(0, n_old % 2, lambda kt, c: update(c, ks_ref, vst_ref, qr_sel, [kt], [None]), init)
    c_sel = lax.fori_loop(0, n_old // 2,
                          lambda i, c: update(c, ks_ref, vst_ref, qr_sel,
                                              [n_old % 2 + 2 * i, n_old % 2 + 2 * i + 1], [None, None]), c_sel)
    assert (seq // tk - 1) * (tk // SEL_BLOCK) >= SEL_TOPK and WINDOW // SEL_BLOCK <= SEL_TOPK
    sel_box = [c_sel]
    _interleave(last_tiles(sel_box, ks_ref, vst_ref, qr_sel, None, seq // tk - 1))
    c_sel = sel_box[0]
    o_sel = c_sel[1][0:HEAD_DIM] * (1.0 / c_sel[1][HEAD_DIM:HEAD_DIM + 1])
    o_win = c_win[1][0:HEAD_DIM] * (1.0 / c_win[1][HEAD_DIM:HEAD_DIM + 1])

    g0 = pl.program_id(1) * (3 * NSA_REP)
    for r in range(NSA_REP):
        ls = slice(r * tq, (r + 1) * tq)
        gate = lambda c: gate_ref[pl.ds(g0 + 3 * r + c, 1), :]
        o = gate(0) * o_cmp[:, ls] + gate(1) * o_sel[:, ls] + gate(2) * o_win[:, ls]
        rs = slice(r * HEAD_DIM, (r + 1) * HEAD_DIM)
        o_ref[rs, :] = (o * gn_ref[rs, :]).astype(BF16)


def _nsa(qt, qrt, kcmp, vcmpt, ksh, vst, kwh, vwt, gates, gnt, mt, *, batch, seq, tq):
    n = batch * seq
    nq = seq // tq
    ncp = kcmp.shape[2]
    grp = NSA_REP * HEAD_DIM
    qspec = pl.BlockSpec((grp, tq), lambda b, g, i: (g, b * nq + i))
    kspec = lambda w: pl.BlockSpec((1, seq, w), lambda b, g, i: (g, b, 0))
    vspec = pl.BlockSpec((seq // LANE, V_ROWS, LANE), lambda b, g, i: (b, g, 0))
    kern = functools.partial(_nsa_kernel, tq=tq, seq=seq)
    return pl.pallas_call(
        kern,
        grid=(batch, NSA_KV, nq),
        in_specs=[qspec, qspec,
                  pl.BlockSpec((1, 1, ncp, HEAD_DIM), lambda b, g, i: (g, b, 0, 0)),
                  pl.BlockSpec((1, 1, HEAD_DIM, ncp), lambda b, g, i: (g, b, 0, 0)),
                  kspec(K_AUG), vspec, kspec(HEAD_DIM), vspec,
                  pl.BlockSpec((GATE_ROWS, tq), lambda b, g, i: (0, b * nq + i)),
                  qspec,
                  pl.BlockSpec(mt.shape, lambda b, g, i: (0, 0))],
        out_specs=qspec,
        out_shape=jax.ShapeDtypeStruct((NSA_WIDTH, n), BF16),
        compiler_params=pltpu.CompilerParams(
            dimension_semantics=("arbitrary", "arbitrary", "arbitrary"),
            vmem_limit_bytes=VMEM_LIMIT),
        name="nsa",
    )(qt, qrt, kcmp, vcmpt, ksh, vst, kwh, vwt, gates, gnt, mt)


def _rwkv_kernel(p_ref, gr_ref, mu_ref, w0_ref, a0_ref, wab_ref, kk_ref, ka_ref, rk_ref,
                 gw_ref, gb_ref, ones_ref, tril_ref, o_ref, st_ref, carry_ref, *, tt):
    c = RWKV_CHUNK
    hd = HEAD_DIM
    nc = tt // c
    step = pl.program_id(1)

    @pl.when(step == 0)
    def _():
        st_ref[...] = jnp.zeros_like(st_ref)
        carry_ref[...] = jnp.zeros_like(carry_ref)

    p = p_ref[...]
    row = lax.broadcasted_iota(jnp.int32, p.shape, 0)
    prev = jnp.where(row == 0, carry_ref[...], pltpu.roll(p, 1, 0))
    carry_ref[...] = p[tt - 1:tt, :]
    ps = p + mu_ref[...] * (prev - p)
    r = ps[:, 0:RWKV_WIDTH]
    k = ps[:, RWKV_WIDTH:2 * RWKV_WIDTH]
    v = ps[:, 2 * RWKV_WIDTH:3 * RWKV_WIDTH]
    lora = ps[:, 3 * RWKV_WIDTH:]
    lane = lax.broadcasted_iota(jnp.int32, lora.shape, 1)
    feat = jnp.where(lane < DECAY_RANK, jnp.tanh(lora), lora).astype(BF16)
    up = _dot(feat, wab_ref[...])
    w = w0_ref[...] + up[:, 0:RWKV_WIDTH]
    lw = _sigmoid(w) * (-float(np.exp(-0.5)))
    a = _sigmoid(a0_ref[...] + up[:, RWKV_WIDTH:])
    ones_bd = ones_ref[...]
    kk = k * kk_ref[...]
    kkn = kk * lax.rsqrt(jnp.maximum(_head_sums(kk * kk, ones_bd), 1e-24))
    k2 = k * (1.0 + (a - 1.0) * ka_ref[...])
    alpha = -kkn
    beta = kkn * a
    bonus = _head_sums(r * k2 * rk_ref[...], ones_bd, terms=1) * v

    cum = _x3_dot(tril_ref[...], lw)
    cend = jnp.concatenate(
        [jnp.broadcast_to(cum[(ch + 1) * c - 1:(ch + 1) * c, :], (c, RWKV_WIDTH)) for ch in range(nc)], axis=0)
    e_neg = jnp.exp(-cum)
    pc = jnp.exp(cend)
    at = alpha * jnp.exp(cum - lw)
    bt = beta * e_neg
    kt = k2 * e_neg
    rt = r * jnp.exp(cum)
    bh = bt * pc
    kh = kt * pc

    npair = RWKV_WIDTH // LANE

    def pairs(x):
        return jnp.stack([x[ch * c:(ch + 1) * c, j * LANE:(j + 1) * LANE]
                          for ch in range(nc) for j in range(npair)], axis=0)

    at_p, rt_p, bt_p, kt_p, v_p, bh_p, kh_p = (pairs(t) for t in (at, rt, bt, kt, v, bh, kh))
    pc_p = jnp.stack([pc[ch * c:ch * c + 1, j * LANE:(j + 1) * LANE]
                      for ch in range(nc) for j in range(npair)], axis=0)
    lane_c = lax.broadcasted_iota(jnp.int32, (1, c, LANE), 2)
    row_c = lax.broadcasted_iota(jnp.int32, (1, c, LANE), 1)
    even_c = lane_c < hd
    col_c = jnp.where(even_c, lane_c, lane_c - hd)
    low_s = row_c > col_c
    low_i = row_c >= col_c
    lane_2c = lax.broadcasted_iota(jnp.int32, (1, 2 * c, LANE), 2)
    row_2c = lax.broadcasted_iota(jnp.int32, (1, 2 * c, LANE), 1)
    even_2c = lane_2c < hd
    on_bd = (row_2c < hd) == even_2c
    zero_c = jnp.zeros((1, c, LANE), BF16)

    def bd(x):
        xb = x.astype(BF16)
        return jnp.concatenate([jnp.where(even_c, xb, zero_c), jnp.where(even_c, zero_c, xb)], axis=1)

    def abd(x):
        xb = x.astype(BF16)
        return jnp.concatenate([jnp.where(even_c, zero_c, xb), jnp.where(even_c, xb, zero_c)], axis=1)

    la = jnp.concatenate([at_p, rt_p], axis=1).astype(BF16)
    zero_2c = jnp.zeros((1, 2 * c, LANE), BF16)
    r_e = _bmm_nt(jnp.where(even_2c, la, zero_2c), jnp.concatenate([bt_p, kt_p], axis=1).astype(BF16))
    r_o = _bmm_nt(jnp.where(even_2c, zero_2c, la), jnp.concatenate([kt_p, bt_p], axis=1).astype(BF16))
    nab = jnp.where(low_s, jnp.where(even_c, r_e[:, 0:c], r_o[:, 0:c]), 0.0)
    aak_sw = jnp.where(low_s, jnp.where(even_c, r_o[:, 0:c], r_e[:, 0:c]), 0.0).astype(BF16)
    arb = jnp.where(low_i, jnp.where(even_c, r_e[:, c:], r_o[:, c:]), 0.0).astype(BF16)
    ark_sw = jnp.where(low_i, jnp.where(even_c, r_o[:, c:], r_e[:, c:]), 0.0).astype(BF16)
    tinv = jnp.where(row_c == col_c, 1.0, 0.0) + nab
    npow = _bmm(nab.astype(BF16), bd(nab))
    n_dbl = 5
    for it in range(n_dbl):
        nbd = bd(npow)
        if it + 1 < n_dbl:
            res = _bmm(jnp.concatenate([tinv, npow], axis=1).astype(BF16), nbd)
            tinv = tinv + res[:, 0:c]
            npow = res[:, c:]
        else:
            tinv = tinv + _bmm(tinv.astype(BF16), nbd)
    av = _bmm(jnp.concatenate([aak_sw, ark_sw], axis=1), abd(v_p))
    tx = _bmm(tinv.astype(BF16), jnp.concatenate([bd(av[:, 0:c]), bd(at_p)], axis=2))
    u0, ta = tx[:, :, 0:LANE], tx[:, :, LANE:]
    ax = _bmm(arb, jnp.concatenate([bd(ta), bd(u0)], axis=2))
    rq = rt_p + ax[:, :, 0:LANE]
    y0 = ax[:, :, LANE:] + av[:, c:]
    v_b = v_p.astype(BF16)
    w_f = jnp.concatenate([jnp.concatenate([ta, u0], axis=2).astype(BF16),
                           jnp.concatenate([jnp.zeros_like(v_b), v_b], axis=2)], axis=1)
    gh = _bmm(jnp.concatenate([jnp.swapaxes(bh_p, 1, 2), jnp.swapaxes(kh_p, 1, 2)], axis=2).astype(BF16), w_f)
    g_bd = jnp.where(on_bd, gh[:, :, 0:LANE], 0.0) + jnp.where(row_2c == lane_2c, pc_p, 0.0)
    h_bd = jnp.where(on_bd, gh[:, :, LANE:], 0.0)
    lhs = jnp.concatenate([rq, g_bd], axis=1).astype(BF16)

    st = st_ref[...]
    ys = []
    for ch in range(nc):
        sl = slice(ch * npair, (ch + 1) * npair)
        res = _bmm(lhs[sl], st.astype(BF16))
        yc = res[:, 0:c, :] + y0[sl]
        st = res[:, c:, :] + h_bd[sl]
        ys.append(jnp.concatenate([yc[j] for j in range(npair)], axis=1))
    st_ref[...] = st
    y = jnp.concatenate(ys, axis=0) if nc > 1 else ys[0]

    inv_hd = 1.0 / hd
    mean = _head_sums(y, ones_bd, terms=1) * inv_hd
    ycen = y - mean
    var = _head_sums(ycen * ycen, ones_bd, terms=1) * inv_hd
    yn = ycen * lax.rsqrt(var + GN_EPS) * gw_ref[...] + gb_ref[...]
    o_ref[...] = ((yn + bonus) * gr_ref[...]).astype(BF16)


def _rwkv(rw, gr, mu, w0, a0, wab, kk, ka, rk, gw, gb, ones_bd, tril, *, batch, seq, tt):
    n = batch * seq
    ns = seq // tt
    row = lambda w: pl.BlockSpec((tt, w), lambda b, i: (b * ns + i, 0))
    full = lambda a: pl.BlockSpec(a.shape, lambda b, i: (0,) * a.ndim)
    kern = functools.partial(_rwkv_kernel, tt=tt)
    consts = (mu, w0, a0, wab, kk, ka, rk, gw, gb, ones_bd, tril)
    return pl.pallas_call(
        kern,
        grid=(batch, ns),
        in_specs=[row(RWKV_SHIFT_WIDTH), row(RWKV_WIDTH)] + [full(a) for a in consts],
        out_specs=row(RWKV_WIDTH),
        out_shape=jax.ShapeDtypeStruct((n, RWKV_WIDTH), BF16),
        scratch_shapes=[pltpu.VMEM((RWKV_WIDTH // LANE, LANE, LANE), F32),
                        pltpu.VMEM((1, RWKV_SHIFT_WIDTH), F32)],
        compiler_params=pltpu.CompilerParams(dimension_semantics=("arbitrary", "arbitrary"),
                                             vmem_limit_bytes=VMEM_LIMIT),
        name="rwkv",
    )(rw, gr, *consts)


def _out_kernel(x_ref, ont_ref, or_ref, wn_ref, wr_ref, g_ref, o_ref):
    h = x_ref[...] + _dot_tn(ont_ref[...], wn_ref[...]) + _dot(or_ref[...], wr_ref[...])
    ms = jnp.mean(h * h, axis=-1, keepdims=True)
    o_ref[...] = h * lax.rsqrt(ms + RMS_EPS) * g_ref[...]


def _out_proj(x2, o_nsa_t, o_rwkv, wn, wr, final_g, *, tm):
    n = x2.shape[0]
    row = lambda w: pl.BlockSpec((tm, w), lambda i: (i, 0))
    full = lambda a: pl.BlockSpec(a.shape, lambda i: (0,) * a.ndim)
    return pl.pallas_call(
        _out_kernel,
        grid=(n // tm,),
        in_specs=[row(D_MODEL), pl.BlockSpec((NSA_WIDTH, tm), lambda i: (0, i)), row(RWKV_WIDTH),
                  full(wn), full(wr), full(final_g)],
        out_specs=row(D_MODEL),
        out_shape=jax.ShapeDtypeStruct((n, D_MODEL), F32),
        compiler_params=pltpu.CompilerParams(dimension_semantics=("arbitrary",),
                                             vmem_limit_bytes=VMEM_LIMIT),
        name="out_proj",
    )(x2, o_nsa_t, o_rwkv, wn, wr, final_g)


def _rope_tables(seq):
    inv = ROPE_THETA ** (-np.arange(ROPE_HALF, dtype=np.float64) / ROPE_HALF)
    ang = np.arange(seq, dtype=np.float64)[:, None] * inv[None, :]
    cos, sin = np.cos(ang), np.sin(ang)
    ra = np.ones((seq, HEAD_DIM)); rm = np.zeros((seq, HEAD_DIM)); rp = np.zeros((seq, HEAD_DIM))
    ra[:, :ROPE_HALF] = cos; ra[:, ROPE_HALF:ROPE_DIM] = cos
    rm[:, :ROPE_HALF] = -sin
    rp[:, ROPE_HALF:ROPE_DIM] = sin
    rep = lambda t: jnp.asarray(np.tile(t, (1, LANE // HEAD_DIM)), F32)
    assert seq // SEL_BLOCK <= K_AUG - HEAD_DIM
    onehot = np.zeros((seq, K_AUG - HEAD_DIM))
    onehot[np.arange(seq), np.arange(seq) // SEL_BLOCK] = 1.0
    return (rep(ra), rep(rm), rep(rp), jnp.asarray(cos.T, F32), jnp.asarray(sin.T, F32),
            jnp.asarray(onehot, BF16))


def _cmp_to_sel_t(n_cmp_pad, n_sel):
    n_cmp = n_cmp_pad - 1
    c0 = np.arange(n_cmp)[:, None] * CMP_STRIDE
    s0 = np.arange(n_sel)[None, :] * SEL_BLOCK
    ov = np.clip(np.minimum(c0 + CMP_BLOCK, s0 + SEL_BLOCK) - np.maximum(c0, s0), 0, None) / CMP_BLOCK
    mt = np.zeros((n_sel, n_cmp_pad))
    mt[:, :n_cmp] = ov.T
    return jnp.asarray(mt, BF16)


def _prep_w_in(w_in):
    idx = np.cumsum(IN_SIZES)[:-1].tolist()
    q, kc, vc, ks, vs, kw, vw, gl, gn, rw, gr = jnp.split(w_in, idx, axis=1)
    w_t = jnp.concatenate([kc, vc, ks, kw, rw, gr], axis=1).T.astype(BF16)
    pad = jnp.zeros((D_MODEL, GATE_ROWS - gl.shape[1]), w_in.dtype)
    w_f = jnp.concatenate([q, vs, vw, gl, pad, gn], axis=1).T.astype(BF16)
    return w_t, w_f


def _layer(x2, norm_g, w_in, cmp_pos_k, cmp_w1_k, cmp_w2_k, cmp_pos_v, cmp_w1_v, cmp_w2_v,
           shift_mu, decay_w0, decay_up, iclr_a0, iclr_up, k_k, k_a, r_k, gn_w, gn_b, w_out,
           final_g, *, batch, seq):
    tm = 256
    tq = 256
    tt = 256
    assert WINDOW % KEY_TILE == 0 and seq % KEY_TILE == 0 and KEY_TILE % tq == 0
    nch = seq // CMP_STRIDE
    n_sel = seq // SEL_BLOCK
    rowv = lambda t: t.reshape(1, -1).astype(F32)

    w_t, w_f = _prep_w_in(w_in)
    (kch, vch, ksh, kwh, rw, gr, qt, qrt, vst, vwt, gates, gnt) = _in_proj(
        x2, rowv(norm_g), w_t, w_f, _rope_tables(seq), seq=seq, tm=2 * tm)

    chunks = lambda t: t.reshape(NSA_KV, batch, nch, CMP_STRIDE * HEAD_DIM)
    kcmp, vcmpt = _compress(chunks(kch), chunks(vch),
                            rowv(cmp_pos_k), cmp_w1_k.astype(BF16), cmp_w2_k.astype(BF16),
                            rowv(cmp_pos_v), cmp_w1_v.astype(BF16), cmp_w2_v.T.astype(BF16))

    o_nsa_t = _nsa(qt, qrt, kcmp, vcmpt, ksh, vst, kwh, vwt, gates, gnt,
                   _cmp_to_sel_t(nch, n_sel), batch=batch, seq=seq, tq=tq)

    z = jnp.zeros((DECAY_RANK, RWKV_WIDTH), F32)
    wab = jnp.concatenate([jnp.concatenate([decay_up, z], axis=1),
                           jnp.concatenate([z, iclr_up], axis=1)], axis=0).astype(BF16)
    hid = np.arange(2 * LANE) // HEAD_DIM
    ones_bd = jnp.asarray(hid[:, None] == hid[None, :], BF16)
    ti = np.arange(tt)
    tril = jnp.asarray((ti[:, None] >= ti[None, :]) & (ti[:, None] // RWKV_CHUNK == ti[None, :] // RWKV_CHUNK), BF16)
    o_rwkv = _rwkv(rw, gr, rowv(shift_mu), rowv(decay_w0), rowv(iclr_a0), wab, rowv(k_k), rowv(k_a),
                   rowv(r_k), rowv(gn_w), rowv(gn_b), ones_bd, tril, batch=batch, seq=seq, tt=tt)

    w_o = w_out.astype(BF16)
    return _out_proj(x2, o_nsa_t, o_rwkv, w_o[:NSA_WIDTH], w_o[NSA_WIDTH:], rowv(final_g), tm=4 * tm)


def kernel(x, norm_g, w_in, cmp_pos_k, cmp_w1_k, cmp_w2_k, cmp_pos_v, cmp_w1_v, cmp_w2_v, shift_mu, decay_w0, decay_up, iclr_a0, iclr_up, k_k, k_a, r_k, gn_w, gn_b, w_out, final_g):
    batch, seq, d = x.shape
    assert d == D_MODEL and norm_g.shape[0] == 1, "single-layer trunk"
    out = _layer(x.reshape(batch * seq, d), norm_g[0], w_in[0], cmp_pos_k[0], cmp_w1_k[0], cmp_w2_k[0],
                 cmp_pos_v[0], cmp_w1_v[0], cmp_w2_v[0], shift_mu[0], decay_w0[0], decay_up[0],
                 iclr_a0[0], iclr_up[0], k_k[0], k_a[0], r_k[0], gn_w[0], gn_b[0], w_out[0],
                 final_g, batch=batch, seq=seq)
    return out.reshape(batch, seq, d)
```

```python
import functools

import numpy as np
import jax
import jax.numpy as jnp
from jax import lax
from jax.experimental import pallas as pl
from jax.experimental.pallas import tpu as pltpu

F32 = jnp.float32
BF16 = jnp.bfloat16

D_MODEL = 1024
HEAD_DIM = 64
NSA_HEADS = 8
NSA_KV = 2
NSA_REP = NSA_HEADS // NSA_KV
RWKV_HEADS = 8
NSA_WIDTH = NSA_HEADS * HEAD_DIM
RWKV_WIDTH = RWKV_HEADS * HEAD_DIM
KV_WIDTH = NSA_KV * HEAD_DIM
ROPE_DIM = HEAD_DIM // 4
ROPE_HALF = ROPE_DIM // 2
ROPE_THETA = 500000.0
CMP_BLOCK = 32
CMP_STRIDE = 16
CMP_HIDDEN = 256
SEL_BLOCK = 64
SEL_TOPK = 8
WINDOW = 512
DECAY_RANK = 64
ICLR_RANK = 64
RWKV_SHIFT_WIDTH = 3 * RWKV_WIDTH + DECAY_RANK + ICLR_RANK
IN_SIZES = (NSA_WIDTH, KV_WIDTH, KV_WIDTH, KV_WIDTH, KV_WIDTH, KV_WIDTH, KV_WIDTH,
            3 * NSA_HEADS, NSA_WIDTH, RWKV_SHIFT_WIDTH, RWKV_WIDTH)
SCALE = HEAD_DIM ** -0.5
RMS_EPS = 1e-6
GN_EPS = 64e-5
NEG_INF = -1e30
FORCE_BONUS = 1e3

LANE = 128
SUBLANE = 8
BF16_SUBLANE = 2 * SUBLANE
GATE_ROWS = 2 * BF16_SUBLANE

T_KV = 0
T_RW = T_KV + 4 * KV_WIDTH
T_GR = T_RW + RWKV_SHIFT_WIDTH
T_END = T_GR + RWKV_WIDTH
R_Q = 0
R_VS = R_Q + NSA_WIDTH
R_VW = R_VS + KV_WIDTH
R_GL = R_VW + KV_WIDTH
R_GN = R_GL + GATE_ROWS
R_END = R_GN + NSA_WIDTH

LOG2E = float(np.log2(np.e))
K_AUG = LANE
V_ROWS = HEAD_DIM + BF16_SUBLANE
KEY_TILE = 256
RWKV_CHUNK = 64
VMEM_LIMIT = 48 * 1024 * 1024


def _dot(a, b):
    return jnp.dot(a, b, preferred_element_type=F32)


def _dot_nt(a, b):
    return lax.dot_general(a, b, (((1,), (1,)), ((), ())), preferred_element_type=F32)


def _dot_tn(a, b):
    return lax.dot_general(a, b, (((0,), (0,)), ((), ())), preferred_element_type=F32)


def _bmm(a, b):
    return lax.dot_general(a, b, (((2,), (1,)), ((0,), (0,))), preferred_element_type=F32)


def _bmm_nt(a, b):
    return lax.dot_general(a, b, (((2,), (2,)), ((0,), (0,))), preferred_element_type=F32)


def _split3(x):
    hi = x.astype(BF16)
    r1 = x - hi.astype(F32)
    mid = r1.astype(BF16)
    lo = (r1 - mid.astype(F32)).astype(BF16)
    return hi, mid, lo


def _head_sums(x, ones_blk, terms=2):
    w = ones_blk.shape[0]
    parts = [x.astype(BF16)]
    if terms == 2:
        parts.append((x - parts[0].astype(F32)).astype(BF16))
    cols = []
    for i in range(0, x.shape[1], w):
        acc = _dot(parts[0][:, i:i + w], ones_blk)
        for t in parts[1:]:
            acc = acc + _dot(t[:, i:i + w], ones_blk)
        cols.append(acc)
    return jnp.concatenate(cols, axis=1)


def _x3_dot(w_bf16, x):
    hi, mid, lo = _split3(x)
    return _dot(w_bf16, hi) + _dot(w_bf16, mid) + _dot(w_bf16, lo)


def _sigmoid(x):
    return 0.5 * jnp.tanh(0.5 * x) + 0.5


def _interleave(*gens):
    live = list(gens)
    while live:
        for g in list(live):
            try:
                next(g)
            except StopIteration:
                live.remove(g)


def _rope128(t, ra, rm, rp):
    return t * ra + pltpu.roll(t, LANE - ROPE_HALF, 1) * rm + pltpu.roll(t, ROPE_HALF, 1) * rp


def _in_proj_kernel(x_ref, g_ref, wt_ref, wf_ref, ra_ref, rm_ref, rp_ref, cos_ref, sin_ref, oh_ref,
                    kc_ref, vc_ref, ks_ref, kw_ref, rw_ref, gr_ref,
                    qt_ref, qrt_ref, vst_ref, vwt_ref, gate_ref, gn_ref, cmp_scr):
    x = x_ref[...]
    ms = jnp.mean(x * x, axis=-1, keepdims=True)
    y = (x * lax.rsqrt(ms + RMS_EPS) * g_ref[...]).astype(BF16)
    tm = x.shape[0]

    kv = _dot_nt(y, wt_ref[T_KV:T_RW, :])
    for i, ref in enumerate((kc_ref, vc_ref)):
        cmp_scr[i] = kv[:, i * KV_WIDTH:(i + 1) * KV_WIDTH]
        for tau in range(CMP_STRIDE):
            piece = cmp_scr[i, pl.ds(tau, tm // CMP_STRIDE, stride=CMP_STRIDE), :].astype(BF16)
            for g in range(NSA_KV):
                ref[g, :, tau * HEAD_DIM:(tau + 1) * HEAD_DIM] = piece[:, g * HEAD_DIM:(g + 1) * HEAD_DIM]
    ra, rm, rp = ra_ref[...], rm_ref[...], rp_ref[...]
    for i, ref in ((2, ks_ref), (3, kw_ref)):
        t = _rope128(kv[:, i * LANE:(i + 1) * LANE], ra, rm, rp)
        for g in range(NSA_KV):
            tg = t[:, g * HEAD_DIM:(g + 1) * HEAD_DIM].astype(BF16)
            if ref is ks_ref:
                ref[g] = jnp.concatenate([tg, oh_ref[...]], axis=1)
            else:
                ref[g] = tg
    rw_ref[...] = _dot_nt(y, wt_ref[T_RW:T_GR, :])
    gr = _dot_nt(y, wt_ref[T_GR:T_END, :])
    gr_ref[...] = gr * _sigmoid(gr)

    def proj_t(r0, r1):
        return _dot_nt(wf_ref[r0:r1, :], y)

    qt = proj_t(R_Q, R_VS) * (SCALE * LOG2E)
    cos, sin = cos_ref[...], sin_ref[...]
    qt_ref[...] = qt.astype(BF16)
    for h in range(NSA_HEADS):
        r0 = h * HEAD_DIM
        t1 = qt[r0:r0 + ROPE_HALF]
        t2 = qt[r0 + ROPE_HALF:r0 + ROPE_DIM]
        qrt_ref[r0:r0 + ROPE_DIM, :] = jnp.concatenate(
            [t1 * cos - t2 * sin, t2 * cos + t1 * sin], axis=0).astype(BF16)
        qrt_ref[r0 + ROPE_DIM:r0 + HEAD_DIM, :] = qt[r0 + ROPE_DIM:r0 + HEAD_DIM].astype(BF16)
    vt = proj_t(R_VS, R_GL).astype(BF16)
    ones = jnp.ones((V_ROWS - HEAD_DIM, LANE), BF16)
    for j in range(vt.shape[1] // LANE):
        for i, ref in enumerate((vst_ref, vwt_ref)):
            for g in range(NSA_KV):
                r0 = i * KV_WIDTH + g * HEAD_DIM
                ref[j, g * V_ROWS:(g + 1) * V_ROWS, :] = jnp.concatenate(
                    [vt[r0:r0 + HEAD_DIM, j * LANE:(j + 1) * LANE], ones], axis=0)
    gate_ref[...] = _sigmoid(proj_t(R_GL, R_GN))
    gn = proj_t(R_GN, R_END)
    gn_ref[...] = gn * _sigmoid(gn)


def _in_proj(x2, norm_g, w_t, w_f, tabs, *, seq, tm):
    n = x2.shape[0]
    spt = seq // tm
    ra, rm, rp, cos, sin, onehot = tabs
    hm = lambda w: jax.ShapeDtypeStruct((NSA_KV, n, w), BF16)
    hspec = lambda w: pl.BlockSpec((NSA_KV, tm, w), lambda i: (0, i, 0))
    row = lambda w: pl.BlockSpec((tm, w), lambda i: (i, 0))
    col = lambda r: pl.BlockSpec((r, tm), lambda i: (0, i))
    full = lambda a: pl.BlockSpec(a.shape, lambda i: (0,) * a.ndim)
    tab = lambda w: pl.BlockSpec((tm, w), lambda i: (i % spt, 0))
    tabt = pl.BlockSpec((ROPE_HALF, tm), lambda i: (0, i % spt))
    vtile = pl.BlockSpec((tm // LANE, NSA_KV * V_ROWS, LANE), lambda i: (i, 0, 0))
    vsd = jax.ShapeDtypeStruct((n // LANE, NSA_KV * V_ROWS, LANE), BF16)
    cw = CMP_STRIDE * HEAD_DIM
    cspec = pl.BlockSpec((NSA_KV, tm // CMP_STRIDE, cw), lambda i: (0, i, 0))
    csd = jax.ShapeDtypeStruct((NSA_KV, n // CMP_STRIDE, cw), BF16)
    return pl.pallas_call(
        _in_proj_kernel,
        grid=(n // tm,),
        in_specs=[row(D_MODEL), full(norm_g), full(w_t), full(w_f), tab(LANE), tab(LANE), tab(LANE),
                  tabt, tabt, tab(K_AUG - HEAD_DIM)],
        out_specs=[cspec, cspec, hspec(K_AUG), hspec(HEAD_DIM), row(RWKV_SHIFT_WIDTH), row(RWKV_WIDTH),
                   col(NSA_WIDTH), col(NSA_WIDTH), vtile, vtile,
                   col(GATE_ROWS), col(NSA_WIDTH)],
        out_shape=[csd, csd, hm(K_AUG), hm(HEAD_DIM),
                   jax.ShapeDtypeStruct((n, RWKV_SHIFT_WIDTH), F32),
                   jax.ShapeDtypeStruct((n, RWKV_WIDTH), F32),
                   jax.ShapeDtypeStruct((NSA_WIDTH, n), BF16),
                   jax.ShapeDtypeStruct((NSA_WIDTH, n), BF16),
                   vsd, vsd,
                   jax.ShapeDtypeStruct((GATE_ROWS, n), F32),
                   jax.ShapeDtypeStruct((NSA_WIDTH, n), F32)],
        scratch_shapes=[pltpu.VMEM((2, tm, KV_WIDTH), F32)],
        compiler_params=pltpu.CompilerParams(dimension_semantics=("arbitrary",),
                                             vmem_limit_bytes=VMEM_LIMIT),
        name="in_proj",
    )(x2, norm_g, w_t, w_f, ra, rm, rp, cos, sin, onehot)


def _compress_kernel(kc_ref, vc_ref, pk_ref, w1k_ref, w2k_ref, pv_ref, w1v_ref, w2vt_ref,
                     ko_ref, vo_ref):
    half = CMP_STRIDE * HEAD_DIM
    ng, _, nch, _ = kc_ref.shape

    def hidden(c_ref, pos_ref, w1_ref):
        c = c_ref[:, 0].reshape(ng * nch, half)
        pos = jnp.broadcast_to(pos_ref[...], (BF16_SUBLANE, 2 * half)).astype(BF16)
        za = _dot(jnp.concatenate([c, pos[:, 0:half]], axis=0), w1_ref[0:half, :])
        zb = _dot(jnp.concatenate([c, pos[:, half:]], axis=0), w1_ref[half:2 * half, :])
        pv = za[ng * nch:ng * nch + 1] + zb[ng * nch:ng * nch + 1]
        hid = jnp.concatenate(
            [za[g * nch:(g + 1) * nch] + pltpu.roll(zb[g * nch:(g + 1) * nch], nch - 1, 0) for g in range(ng)],
            axis=0) + pv
        return (hid * _sigmoid(hid)).astype(BF16)

    ko = _dot(hidden(kc_ref, pk_ref, w1k_ref), w2k_ref[...]).astype(BF16)
    hv = hidden(vc_ref, pv_ref, w1v_ref)
    for g in range(ng):
        ko_ref[g, 0] = ko[g * nch:(g + 1) * nch]
        vo_ref[g, 0] = _dot_nt(w2vt_ref[...], hv[g * nch:(g + 1) * nch]).astype(BF16)


def _compress(kc_r, vc_r, pk, w1k, w2k, pv, w1v, w2vt):
    g, b, nch, width = kc_r.shape
    blk = pl.BlockSpec((g, 1, nch, width), lambda j: (0, j, 0, 0))
    full = lambda a: pl.BlockSpec(a.shape, lambda j: (0,) * a.ndim)
    return pl.pallas_call(
        _compress_kernel,
        grid=(b,),
        in_specs=[blk, blk, full(pk), full(w1k), full(w2k), full(pv), full(w1v), full(w2vt)],
        out_specs=[pl.BlockSpec((g, 1, nch, HEAD_DIM), lambda j: (0, j, 0, 0)),
                   pl.BlockSpec((g, 1, HEAD_DIM, nch), lambda j: (0, j, 0, 0))],
        out_shape=[jax.ShapeDtypeStruct((g, b, nch, HEAD_DIM), BF16),
                   jax.ShapeDtypeStruct((g, b, HEAD_DIM, nch), BF16)],
        compiler_params=pltpu.CompilerParams(dimension_semantics=("arbitrary",),
                                             vmem_limit_bytes=VMEM_LIMIT),
        name="compress",
    )(kc_r, vc_r, pk, w1k, w2k, pv, w1v, w2vt)


def _nsa_kernel(qt_ref, qrt_ref, kc_ref, vct_ref, ks_ref, vst_ref, kw_ref, vwt_ref,
                gate_ref, gn_ref, mt_ref, o_ref, *, tq, seq):
    tk = KEY_TILE
    n_win = WINDOW // tk
    qi = pl.program_id(2)
    q0 = qi * tq
    nl = NSA_REP * tq
    n_sel = seq // SEL_BLOCK
    ncp = kc_ref.shape[2]

    def heads_on_lanes(ref):
        return jnp.concatenate([ref[r * HEAD_DIM:(r + 1) * HEAD_DIM, :] for r in range(NSA_REP)], axis=1)

    def tile4(a):
        return jnp.concatenate([a] * NSA_REP, axis=1)

    qr = heads_on_lanes(qrt_ref)
    k_s = lax.broadcasted_iota(jnp.int32, (tk, tq), 0)
    t_l = q0 + lax.broadcasted_iota(jnp.int32, (tk, tq), 1)

    def update_steps(box, k_ref, vt_ref, q_op, kts, keeps):
        m_i, acc = box[0]

        def scores(i):
            sc = _dot(k_ref[0, pl.ds(pl.multiple_of(kts[i] * tk, tk), tk), :], q_op)
            if keeps[i] is None:
                return sc
            return jnp.concatenate([jnp.where(keeps[i], sc[:, r * tq:(r + 1) * tq], NEG_INF)
                                    for r in range(NSA_REP)], axis=1)

        sc_next = scores(0)
        yield
        for i, kt in enumerate(kts):
            sc = sc_next
            if i + 1 < len(kts):
                sc_next = scores(i + 1)
            m_n = jnp.maximum(m_i, jnp.max(sc, axis=0, keepdims=True))
            pe = jnp.exp2(sc - m_n).astype(BF16)
            yield
            vtb = jnp.concatenate([vt_ref[kt * (tk // LANE) + jj] for jj in range(tk // LANE)], axis=1)
            acc = jnp.exp2(m_i - m_n) * acc + _dot(vtb, pe)
            m_i = m_n
            yield
        box[0] = (m_i, acc)

    def update(carry, k_ref, vt_ref, q_op, kts, keeps):
        box = [carry]
        _interleave(update_steps(box, k_ref, vt_ref, q_op, kts, keeps))
        return box[0]

    def last_tiles(box, k_ref, vt_ref, q_op, low_keep, spare_tile):
        kts, keeps = [], []
        tiles = [(0, lambda d: d >= 0)] + [(i, None) for i in range(1, n_win)] + [(n_win, low_keep)]
        for back, keep_fn in tiles:
            exists = a >= back
            kt = jnp.where(exists, a - back, 0 if spare_tile is None else spare_tile)
            d = t_l - (kt * tk + k_s)
            kts.append(kt)
            if keep_fn is not None:
                keeps.append(keep_fn(d) & exists)
            else:
                keeps.append(exists if spare_tile is None else None)
        return update_steps(box, k_ref, vt_ref, q_op, kts, keeps)

    sel_out = {}

    def select_steps():
        s = _dot(kc_ref[0, 0], heads_on_lanes(qt_ref))
        t_c = q0 + lax.broadcasted_iota(jnp.int32, (ncp, tq), 1)
        c_c = lax.broadcasted_iota(jnp.int32, (ncp, tq), 0)
        cmask = tile4((c_c * CMP_STRIDE + (CMP_BLOCK - 1)) <= t_c)
        yield
        s = jnp.where(cmask, s, NEG_INF)
        m = jnp.max(s, axis=0, keepdims=True)
        e = jnp.where(cmask, jnp.exp2(s - m), 0.0)
        den = jnp.sum(e, axis=0, keepdims=True)
        p = e * (1.0 / jnp.where(den > 0.0, den, 1.0))
        sel_out["o_cmp"] = _dot(vct_ref[0, 0], p.astype(BF16))
        yield
        psum = p[:, 0:tq]
        for r in range(1, NSA_REP):
            psum = psum + p[:, r * tq:(r + 1) * tq]
        imp = _x3_dot(mt_ref[...], psum)
        j = lax.broadcasted_iota(jnp.int32, (n_sel, tq), 0)
        t = q0 + lax.broadcasted_iota(jnp.int32, (n_sel, tq), 1)
        tb = t // SEL_BLOCK
        forced = (j == 0) | (j == tb) | (j == tb - 1)
        val = jnp.where(j <= tb, imp + jnp.where(forced, FORCE_BONUS, 0.0), -1.0)
        yield
        vals = [val[g0:g0 + SUBLANE] for g0 in range(0, n_sel, SUBLANE)]
        cnts = [jnp.zeros((SUBLANE, tq), F32) for _ in vals]
        srow = lax.broadcasted_iota(jnp.int32, (SUBLANE, tq), 0)
        for i in range(n_sel):
            vi = jnp.broadcast_to(val[i:i + 1, :], (SUBLANE, tq))
            for g, vg in enumerate(vals):
                ge = lambda: jnp.where(vi >= vg, 1.0, 0.0)
                gt = lambda: jnp.where(vi > vg, 1.0, 0.0)
                if g * SUBLANE > i:
                    beat = ge()
                elif (g + 1) * SUBLANE <= i:
                    beat = gt()
                else:
                    beat = jnp.where(srow > i - g * SUBLANE, ge(), gt())
                cnts[g] = cnts[g] + beat
            if i % SUBLANE == SUBLANE - 1:
                yield
        cnt = jnp.concatenate(cnts, axis=0)
        sel_bias = jnp.where(cnt < float(SEL_TOPK), 0.0, NEG_INF).astype(BF16)
        sel_out["qr_sel"] = jnp.concatenate(
            [qr, tile4(sel_bias), jnp.zeros((K_AUG - HEAD_DIM - n_sel, nl), BF16)], axis=0)

    init = (jnp.full((1, nl), NEG_INF, F32), jnp.zeros((V_ROWS, nl), F32))
    a = (q0 + tq - 1) // tk
    win_box = [init]
    _interleave(last_tiles(win_box, kw_ref, vwt_ref, qr, lambda d: d < WINDOW, None), select_steps())
    c_win = win_box[0]
    o_cmp, qr_sel = sel_out["o_cmp"], sel_out["qr_sel"]

    assert (seq // tk - 1) * (tk // SEL_BLOCK) >= SEL_TOPK and WINDOW // SEL_BLOCK <= SEL_TOPK
    sel_box = [init]
    _interleave(last_tiles(sel_box, ks_ref, vst_ref, qr_sel, None, seq // tk - 1))
    c_sel = sel_box[0]
    n_old = jnp.maximum(a - n_win, 0)
    c_sel = lax.fori_loop(0, n_old % 2, lambda kt, c: update(c, ks_ref, vst_ref, qr_sel, [kt], [None]), c_sel)
    c_sel = lax.fori_loop(0, n_old // 2,
                          lambda i, c: update(c, ks_ref, vst_ref, qr_sel,
                                              [n_old % 2 + 2 * i, n_old % 2 + 2 * i + 1], [None, None]), c_sel)
    o_sel = c_sel[1][0:HEAD_DIM] * (1.0 / c_sel[1][HEAD_DIM:HEAD_DIM + 1])
    o_win = c_win[1][0:HEAD_DIM] * (1.0 / c_win[1][HEAD_DIM:HEAD_DIM + 1])

    g0 = pl.program_id(1) * (3 * NSA_REP)
    for r in range(NSA_REP):
        ls = slice(r * tq, (r + 1) * tq)
        gate = lambda c: gate_ref[pl.ds(g0 + 3 * r + c, 1), :]
        o = gate(0) * o_cmp[:, ls] + gate(1) * o_sel[:, ls] + gate(2) * o_win[:, ls]
        rs = slice(r * HEAD_DIM, (r + 1) * HEAD_DIM)
        o_ref[rs, :] = (o * gn_ref[rs, :]).astype(BF16)


def _nsa(qt, qrt, kcmp, vcmpt, ksh, vst, kwh, vwt, gates, gnt, mt, *, batch, seq, tq):
    n = batch * seq
    nq = seq // tq
    ncp = kcmp.shape[2]
    grp = NSA_REP * HEAD_DIM
    qspec = pl.BlockSpec((grp, tq), lambda b, g, i: (g, b * nq + i))
    kspec = lambda w: pl.BlockSpec((1, seq, w), lambda b, g, i: (g, b, 0))
    vspec = pl.BlockSpec((seq // LANE, V_ROWS, LANE), lambda b, g, i: (b, g, 0))
    kern = functools.partial(_nsa_kernel, tq=tq, seq=seq)
    return pl.pallas_call(
        kern,
        grid=(batch, NSA_KV, nq),
        in_specs=[qspec, qspec,
                  pl.BlockSpec((1, 1, ncp, HEAD_DIM), lambda b, g, i: (g, b, 0, 0)),
                  pl.BlockSpec((1, 1, HEAD_DIM, ncp), lambda b, g, i: (g, b, 0, 0)),
                  kspec(K_AUG), vspec, kspec(HEAD_DIM), vspec,
                  pl.BlockSpec((GATE_ROWS, tq), lambda b, g, i: (0, b * nq + i)),
                  qspec,
                  pl.BlockSpec(mt.shape, lambda b, g, i: (0, 0))],
        out_specs=qspec,
        out_shape=jax.ShapeDtypeStruct((NSA_WIDTH, n), BF16),
        compiler_params=pltpu.CompilerParams(
            dimension_semantics=("arbitrary", "arbitrary", "arbitrary"),
            vmem_limit_bytes=VMEM_LIMIT),
        name="nsa",
    )(qt, qrt, kcmp, vcmpt, ksh, vst, kwh, vwt, gates, gnt, mt)


def _rwkv_kernel(p_ref, gr_ref, mu_ref, w0_ref, a0_ref, wab_ref, kk_ref, ka_ref, rk_ref,
                 gw_ref, gb_ref, ones_ref, tril_ref, o_ref, st_ref, carry_ref, *, tt):
    c = RWKV_CHUNK
    hd = HEAD_DIM
    nc = tt // c
    step = pl.program_id(1)

    @pl.when(step == 0)
    def _():
        st_ref[...] = jnp.zeros_like(st_ref)
        carry_ref[...] = jnp.zeros_like(carry_ref)

    p = p_ref[...]
    row = lax.broadcasted_iota(jnp.int32, p.shape, 0)
    prev = jnp.where(row == 0, carry_ref[...], pltpu.roll(p, 1, 0))
    carry_ref[...] = p[tt - 1:tt, :]
    ps = p + mu_ref[...] * (prev - p)
    r = ps[:, 0:RWKV_WIDTH]
    k = ps[:, RWKV_WIDTH:2 * RWKV_WIDTH]
    v = ps[:, 2 * RWKV_WIDTH:3 * RWKV_WIDTH]
    lora = ps[:, 3 * RWKV_WIDTH:]
    lane = lax.broadcasted_iota(jnp.int32, lora.shape, 1)
    feat = jnp.where(lane < DECAY_RANK, jnp.tanh(lora), lora).astype(BF16)
    up = _dot(feat, wab_ref[...])
    w = w0_ref[...] + up[:, 0:RWKV_WIDTH]
    lw = _sigmoid(w) * (-float(np.exp(-0.5)))
    a = _sigmoid(a0_ref[...] + up[:, RWKV_WIDTH:])
    ones_bd = ones_ref[...]
    kk = k * kk_ref[...]
    kkn = kk * lax.rsqrt(jnp.maximum(_head_sums(kk * kk, ones_bd), 1e-24))
    k2 = k * (1.0 + (a - 1.0) * ka_ref[...])
    alpha = -kkn
    beta = kkn * a
    bonus = _head_sums(r * k2 * rk_ref[...], ones_bd, terms=1) * v

    cum = _x3_dot(tril_ref[...], lw)
    cend = jnp.concatenate(
        [jnp.broadcast_to(cum[(ch + 1) * c - 1:(ch + 1) * c, :], (c, RWKV_WIDTH)) for ch in range(nc)], axis=0)
    e_neg = jnp.exp(-cum)
    pc = jnp.exp(cend)
    at = alpha * jnp.exp(cum - lw)
    bt = beta * e_neg
    kt = k2 * e_neg
    rt = r * jnp.exp(cum)
    bh = bt * pc
    kh = kt * pc

    npair = RWKV_WIDTH // LANE

    def pairs(x):
        return jnp.stack([x[ch * c:(ch + 1) * c, j * LANE:(j + 1) * LANE]
                          for ch in range(nc) for j in range(npair)], axis=0)

    at_p, rt_p, bt_p, kt_p, v_p, bh_p, kh_p = (pairs(t) for t in (at, rt, bt, kt, v, bh, kh))
    pc_p = jnp.stack([pc[ch * c:ch * c + 1, j * LANE:(j + 1) * LANE]
                      for ch in range(nc) for j in range(npair)], axis=0)
    lane_c = lax.broadcasted_iota(jnp.int32, (1, c, LANE), 2)
    row_c = lax.broadcasted_iota(jnp.int32, (1, c, LANE), 1)
    even_c = lane_c < hd
    col_c = jnp.where(even_c, lane_c, lane_c - hd)
    low_s = row_c > col_c
    low_i = row_c >= col_c
    lane_2c = lax.broadcasted_iota(jnp.int32, (1, 2 * c, LANE), 2)
    row_2c = lax.broadcasted_iota(jnp.int32, (1, 2 * c, LANE), 1)
    even_2c = lane_2c < hd
    on_bd = (row_2c < hd) == even_2c
    zero_c = jnp.zeros((1, c, LANE), BF16)

    def bd(x):
        xb = x.astype(BF16)
        return jnp.concatenate([jnp.where(even_c, xb, zero_c), jnp.where(even_c, zero_c, xb)], axis=1)

    def abd(x):
        xb = x.astype(BF16)
        return jnp.concatenate([jnp.where(even_c, zero_c, xb), jnp.where(even_c, xb, zero_c)], axis=1)

    la = jnp.concatenate([at_p, rt_p], axis=1).astype(BF16)
    zero_2c = jnp.zeros((1, 2 * c, LANE), BF16)
    r_e = _bmm_nt(jnp.where(even_2c, la, zero_2c), jnp.concatenate([bt_p, kt_p], axis=1).astype(BF16))
    r_o = _bmm_nt(jnp.where(even_2c, zero_2c, la), jnp.concatenate([kt_p, bt_p], axis=1).astype(BF16))
    nab = jnp.where(low_s, jnp.where(even_c, r_e[:, 0:c], r_o[:, 0:c]), 0.0)
    aak_sw = jnp.where(low_s, jnp.where(even_c, r_o[:, 0:c], r_e[:, 0:c]), 0.0).astype(BF16)
    arb = jnp.where(low_i, jnp.where(even_c, r_e[:, c:], r_o[:, c:]), 0.0).astype(BF16)
    ark_sw = jnp.where(low_i, jnp.where(even_c, r_o[:, c:], r_e[:, c:]), 0.0).astype(BF16)
    tinv = jnp.where(row_c == col_c, 1.0, 0.0) + nab
    npow = _bmm(nab.astype(BF16), bd(nab))
    n_dbl = 5
    for it in range(n_dbl):
        nbd = bd(npow)
        if it + 1 < n_dbl:
            res = _bmm(jnp.concatenate([tinv, npow], axis=1).astype(BF16), nbd)
            tinv = tinv + res[:, 0:c]
            npow = res[:, c:]
        else:
            tinv = tinv + _bmm(tinv.astype(BF16), nbd)
    av = _bmm(jnp.concatenate([aak_sw, ark_sw], axis=1), abd(v_p))
    tx = _bmm(tinv.astype(BF16), jnp.concatenate([bd(av[:, 0:c]), bd(at_p)], axis=2))
    u0, ta = tx[:, :, 0:LANE], tx[:, :, LANE:]
    ax = _bmm(arb, jnp.concatenate([bd(ta), bd(u0)], axis=2))
    rq = rt_p + ax[:, :, 0:LANE]
    y0 = ax[:, :, LANE:] + av[:, c:]
    v_b = v_p.astype(BF16)
    w_f = jnp.concatenate([jnp.concatenate([ta, u0], axis=2).astype(BF16),
                           jnp.concatenate([jnp.zeros_like(v_b), v_b], axis=2)], axis=1)
    gh = _bmm(jnp.concatenate([jnp.swapaxes(bh_p, 1, 2), jnp.swapaxes(kh_p, 1, 2)], axis=2).astype(BF16), w_f)
    g_bd = jnp.where(on_bd, gh[:, :, 0:LANE], 0.0) + jnp.where(row_2c == lane_2c, pc_p, 0.0)
    h_bd = jnp.where(on_bd, gh[:, :, LANE:], 0.0)
    lhs = jnp.concatenate([rq, g_bd], axis=1).astype(BF16)

    st = st_ref[...]
    ys = []
    for ch in range(nc):
        sl = slice(ch * npair, (ch + 1) * npair)
        res = _bmm(lhs[sl], st.astype(BF16))
        yc = res[:, 0:c, :] + y0[sl]
        st = res[:, c:, :] + h_bd[sl]
        ys.append(jnp.concatenate([yc[j] for j in range(npair)], axis=1))
    st_ref[...] = st
    y = jnp.concatenate(ys, axis=0) if nc > 1 else ys[0]

    inv_hd = 1.0 / hd
    mean = _head_sums(y, ones_bd, terms=1) * inv_hd
    ycen = y - mean
    var = _head_sums(ycen * ycen, ones_bd, terms=1) * inv_hd
    yn = ycen * lax.rsqrt(var + GN_EPS) * gw_ref[...] + gb_ref[...]
    o_ref[...] = ((yn + bonus) * gr_ref[...]).astype(BF16)


def _rwkv(rw, gr, mu, w0, a0, wab, kk, ka, rk, gw, gb, ones_bd, tril, *, batch, seq, tt):
    n = batch * seq
    ns = seq // tt
    row = lambda w: pl.BlockSpec((tt, w), lambda b, i: (b * ns + i, 0))
    full = lambda a: pl.BlockSpec(a.shape, lambda b, i: (0,) * a.ndim)
    kern = functools.partial(_rwkv_kernel, tt=tt)
    consts = (mu, w0, a0, wab, kk, ka, rk, gw, gb, ones_bd, tril)
    return pl.pallas_call(
        kern,
        grid=(batch, ns),
        in_specs=[row(RWKV_SHIFT_WIDTH), row(RWKV_WIDTH)] + [full(a) for a in consts],
        out_specs=row(RWKV_WIDTH),
        out_shape=jax.ShapeDtypeStruct((n, RWKV_WIDTH), BF16),
        scratch_shapes=[pltpu.VMEM((RWKV_WIDTH // LANE, LANE, LANE), F32),
                        pltpu.VMEM((1, RWKV_SHIFT_WIDTH), F32)],
        compiler_params=pltpu.CompilerParams(dimension_semantics=("arbitrary", "arbitrary"),
                                             vmem_limit_bytes=VMEM_LIMIT),
        name="rwkv",
    )(rw, gr, *consts)


def _out_kernel(x_ref, ont_ref, or_ref, wn_ref, wr_ref, g_ref, o_ref):
    h = x_ref[...] + _dot_tn(ont_ref[...], wn_ref[...]) + _dot(or_ref[...], wr_ref[...])
    ms = jnp.mean(h * h, axis=-1, keepdims=True)
    o_ref[...] = h * lax.rsqrt(ms + RMS_EPS) * g_ref[...]


def _out_proj(x2, o_nsa_t, o_rwkv, wn, wr, final_g, *, tm):
    n = x2.shape[0]
    row = lambda w: pl.BlockSpec((tm, w), lambda i: (i, 0))
    full = lambda a: pl.BlockSpec(a.shape, lambda i: (0,) * a.ndim)
    return pl.pallas_call(
        _out_kernel,
        grid=(n // tm,),
        in_specs=[row(D_MODEL), pl.BlockSpec((NSA_WIDTH, tm), lambda i: (0, i)), row(RWKV_WIDTH),
                  full(wn), full(wr), full(final_g)],
        out_specs=row(D_MODEL),
        out_shape=jax.ShapeDtypeStruct((n, D_MODEL), F32),
        compiler_params=pltpu.CompilerParams(dimension_semantics=("arbitrary",),
                                             vmem_limit_bytes=VMEM_LIMIT),
        name="out_proj",
    )(x2, o_nsa_t, o_rwkv, wn, wr, final_g)


def _rope_tables(seq):
    inv = ROPE_THETA ** (-np.arange(ROPE_HALF, dtype=np.float64) / ROPE_HALF)
    ang = np.arange(seq, dtype=np.float64)[:, None] * inv[None, :]
    cos, sin = np.cos(ang), np.sin(ang)
    ra = np.ones((seq, HEAD_DIM)); rm = np.zeros((seq, HEAD_DIM)); rp = np.zeros((seq, HEAD_DIM))
    ra[:, :ROPE_HALF] = cos; ra[:, ROPE_HALF:ROPE_DIM] = cos
    rm[:, :ROPE_HALF] = -sin
    rp[:, ROPE_HALF:ROPE_DIM] = sin
    rep = lambda t: jnp.asarray(np.tile(t, (1, LANE // HEAD_DIM)), F32)
    assert seq // SEL_BLOCK <= K_AUG - HEAD_DIM
    onehot = np.zeros((seq, K_AUG - HEAD_DIM))
    onehot[np.arange(seq), np.arange(seq) // SEL_BLOCK] = 1.0
    return (rep(ra), rep(rm), rep(rp), jnp.asarray(cos.T, F32), jnp.asarray(sin.T, F32),
            jnp.asarray(onehot, BF16))


def _cmp_to_sel_t(n_cmp_pad, n_sel):
    n_cmp = n_cmp_pad - 1
    c0 = np.arange(n_cmp)[:, None] * CMP_STRIDE
    s0 = np.arange(n_sel)[None, :] * SEL_BLOCK
    ov = np.clip(np.minimum(c0 + CMP_BLOCK, s0 + SEL_BLOCK) - np.maximum(c0, s0), 0, None) / CMP_BLOCK
    mt = np.zeros((n_sel, n_cmp_pad))
    mt[:, :n_cmp] = ov.T
    return jnp.asarray(mt, BF16)


def _prep_w_in(w_in):
    idx = np.cumsum(IN_SIZES)[:-1].tolist()
    q, kc, vc, ks, vs, kw, vw, gl, gn, rw, gr = jnp.split(w_in, idx, axis=1)
    w_t = jnp.concatenate([kc, vc, ks, kw, rw, gr], axis=1).T.astype(BF16)
    pad = jnp.zeros((D_MODEL, GATE_ROWS - gl.shape[1]), w_in.dtype)
    w_f = jnp.concatenate([q, vs, vw, gl, pad, gn], axis=1).T.astype(BF16)
    return w_t, w_f


def _layer(x2, norm_g, w_in, cmp_pos_k, cmp_w1_k, cmp_w2_k, cmp_pos_v, cmp_w1_v, cmp_w2_v,
           shift_mu, decay_w0, decay_up, iclr_a0, iclr_up, k_k, k_a, r_k, gn_w, gn_b, w_out,
           final_g, *, batch, seq):
    tm = 256
    tq = 256
    tt = 256
    assert WINDOW % KEY_TILE == 0 and seq % KEY_TILE == 0 and KEY_TILE % tq == 0
    nch = seq // CMP_STRIDE
    n_sel = seq // SEL_BLOCK
    rowv = lambda t: t.reshape(1, -1).astype(F32)

    w_t, w_f = _prep_w_in(w_in)
    (kch, vch, ksh, kwh, rw, gr, qt, qrt, vst, vwt, gates, gnt) = _in_proj(
        x2, rowv(norm_g), w_t, w_f, _rope_tables(seq), seq=seq, tm=2 * tm)

    chunks = lambda t: t.reshape(NSA_KV, batch, nch, CMP_STRIDE * HEAD_DIM)
    kcmp, vcmpt = _compress(chunks(kch), chunks(vch),
                            rowv(cmp_pos_k), cmp_w1_k.astype(BF16), cmp_w2_k.astype(BF16),
                            rowv(cmp_pos_v), cmp_w1_v.astype(BF16), cmp_w2_v.T.astype(BF16))

    o_nsa_t = _nsa(qt, qrt, kcmp, vcmpt, ksh, vst, kwh, vwt, gates, gnt,
                   _cmp_to_sel_t(nch, n_sel), batch=batch, seq=seq, tq=tq)

    z = jnp.zeros((DECAY_RANK, RWKV_WIDTH), F32)
    wab = jnp.concatenate([jnp.concatenate([decay_up, z], axis=1),
                           jnp.concatenate([z, iclr_up], axis=1)], axis=0).astype(BF16)
    hid = np.arange(2 * LANE) // HEAD_DIM
    ones_bd = jnp.asarray(hid[:, None] == hid[None, :], BF16)
    ti = np.arange(tt)
    tril = jnp.asarray((ti[:, None] >= ti[None, :]) & (ti[:, None] // RWKV_CHUNK == ti[None, :] // RWKV_CHUNK), BF16)
    o_rwkv = _rwkv(rw, gr, rowv(shift_mu), rowv(decay_w0), rowv(iclr_a0), wab, rowv(k_k), rowv(k_a),
                   rowv(r_k), rowv(gn_w), rowv(gn_b), ones_bd, tril, batch=batch, seq=seq, tt=tt)

    w_o = w_out.astype(BF16)
    return _out_proj(x2, o_nsa_t, o_rwkv, w_o[:NSA_WIDTH], w_o[NSA_WIDTH:], rowv(final_g), tm=4 * tm)


def kernel(x, norm_g, w_in, cmp_pos_k, cmp_w1_k, cmp_w2_k, cmp_pos_v, cmp_w1_v, cmp_w2_v, shift_mu, decay_w0, decay_up, iclr_a0, iclr_up, k_k, k_a, r_k, gn_w, gn_b, w_out, final_g):
    batch, seq, d = x.shape
    assert d == D_MODEL and norm_g.shape[0] == 1, "single-layer trunk"
    out = _layer(x.reshape(batch * seq, d), norm_g[0], w_in[0], cmp_pos_k[0], cmp_w1_k[0], cmp_w2_k[0],
                 cmp_pos_v[0], cmp_w1_v[0], cmp_w2_v[0], shift_mu[0], decay_w0[0], decay_up[0],
                 iclr_a0[0], iclr_up[0], k_k[0], k_a[0], r_k[0], gn_w[0], gn_b[0], w_out[0],
                 final_g, batch=batch, seq=seq)
    return out.reshape(batch, seq, d)
```

```python
import functools

import numpy as np
import jax
import jax.numpy as jnp
from jax import lax
from jax.experimental import pallas as pl
from jax.experimental.pallas import tpu as pltpu

F32 = jnp.float32
BF16 = jnp.bfloat16

D_MODEL = 1024
HEAD_DIM = 64
NSA_HEADS = 8
NSA_KV = 2
NSA_REP = NSA_HEADS // NSA_KV
RWKV_HEADS = 8
NSA_WIDTH = NSA_HEADS * HEAD_DIM
RWKV_WIDTH = RWKV_HEADS * HEAD_DIM
KV_WIDTH = NSA_KV * HEAD_DIM
ROPE_DIM = HEAD_DIM // 4
ROPE_HALF = ROPE_DIM // 2
ROPE_THETA = 500000.0
CMP_BLOCK = 32
CMP_STRIDE = 16
CMP_HIDDEN = 256
SEL_BLOCK = 64
SEL_TOPK = 8
WINDOW = 512
DECAY_RANK = 64
ICLR_RANK = 64
RWKV_SHIFT_WIDTH = 3 * RWKV_WIDTH + DECAY_RANK + ICLR_RANK
IN_SIZES = (NSA_WIDTH, KV_WIDTH, KV_WIDTH, KV_WIDTH, KV_WIDTH, KV_WIDTH, KV_WIDTH,
            3 * NSA_HEADS, NSA_WIDTH, RWKV_SHIFT_WIDTH, RWKV_WIDTH)
SCALE = HEAD_DIM ** -0.5
RMS_EPS = 1e-6
GN_EPS = 64e-5
NEG_INF = -1e30
FORCE_BONUS = 1e3

LANE = 128
SUBLANE = 8
BF16_SUBLANE = 2 * SUBLANE
GATE_ROWS = 2 * BF16_SUBLANE

T_KV = 0
T_RW = T_KV + 4 * KV_WIDTH
T_GR = T_RW + RWKV_SHIFT_WIDTH
T_END = T_GR + RWKV_WIDTH
R_Q = 0
R_VS = R_Q + NSA_WIDTH
R_VW = R_VS + KV_WIDTH
R_GL = R_VW + KV_WIDTH
R_GN = R_GL + GATE_ROWS
R_END = R_GN + NSA_WIDTH

LOG2E = float(np.log2(np.e))
K_AUG = LANE
V_ROWS = HEAD_DIM + BF16_SUBLANE
KEY_TILE = 256
RWKV_CHUNK = 64
VMEM_LIMIT = 48 * 1024 * 1024


def _dot(a, b):
    return jnp.dot(a, b, preferred_element_type=F32)


def _dot_nt(a, b):
    return lax.dot_general(a, b, (((1,), (1,)), ((), ())), preferred_element_type=F32)


def _dot_tn(a, b):
    return lax.dot_general(a, b, (((0,), (0,)), ((), ())), preferred_element_type=F32)


def _bmm(a, b):
    return lax.dot_general(a, b, (((2,), (1,)), ((0,), (0,))), preferred_element_type=F32)


def _bmm_nt(a, b):
    return lax.dot_general(a, b, (((2,), (2,)), ((0,), (0,))), preferred_element_type=F32)


def _split3(x):
    hi = x.astype(BF16)
    r1 = x - hi.astype(F32)
    mid = r1.astype(BF16)
    lo = (r1 - mid.astype(F32)).astype(BF16)
    return hi, mid, lo


def _head_sums(x, ones_blk, terms=2):
    w = ones_blk.shape[0]
    parts = [x.astype(BF16)]
    if terms == 2:
        parts.append((x - parts[0].astype(F32)).astype(BF16))
    cols = []
    for i in range(0, x.shape[1], w):
        acc = _dot(parts[0][:, i:i + w], ones_blk)
        for t in parts[1:]:
            acc = acc + _dot(t[:, i:i + w], ones_blk)
        cols.append(acc)
    return jnp.concatenate(cols, axis=1)


def _x3_dot(w_bf16, x):
    hi, mid, lo = _split3(x)
    return _dot(w_bf16, hi) + _dot(w_bf16, mid) + _dot(w_bf16, lo)


def _sigmoid(x):
    return 0.5 * jnp.tanh(0.5 * x) + 0.5


def _interleave(*gens):
    live = list(gens)
    while live:
        for g in list(live):
            try:
                next(g)
            except StopIteration:
                live.remove(g)


def _rope128(t, ra, rm, rp):
    return t * ra + pltpu.roll(t, LANE - ROPE_HALF, 1) * rm + pltpu.roll(t, ROPE_HALF, 1) * rp


def _in_proj_kernel(x_ref, g_ref, wt_ref, wf_ref, ra_ref, rm_ref, rp_ref, cos_ref, sin_ref, oh_ref,
                    kc_ref, vc_ref, ks_ref, kw_ref, rw_ref, gr_ref,
                    qt_ref, qrt_ref, vst_ref, vwt_ref, gate_ref, gn_ref, cmp_scr):
    x = x_ref[...]
    ms = jnp.mean(x * x, axis=-1, keepdims=True)
    y = (x * lax.rsqrt(ms + RMS_EPS) * g_ref[...]).astype(BF16)
    tm = x.shape[0]

    kv = _dot_nt(y, wt_ref[T_KV:T_RW, :])
    for i, ref in enumerate((kc_ref, vc_ref)):
        cmp_scr[i] = kv[:, i * KV_WIDTH:(i + 1) * KV_WIDTH]
        for tau in range(CMP_STRIDE):
            piece = cmp_scr[i, pl.ds(tau, tm // CMP_STRIDE, stride=CMP_STRIDE), :].astype(BF16)
            for g in range(NSA_KV):
                ref[g, :, tau * HEAD_DIM:(tau + 1) * HEAD_DIM] = piece[:, g * HEAD_DIM:(g + 1) * HEAD_DIM]
    ra, rm, rp = ra_ref[...], rm_ref[...], rp_ref[...]
    for i, ref in ((2, ks_ref), (3, kw_ref)):
        t = _rope128(kv[:, i * LANE:(i + 1) * LANE], ra, rm, rp)
        for g in range(NSA_KV):
            tg = t[:, g * HEAD_DIM:(g + 1) * HEAD_DIM].astype(BF16)
            if ref is ks_ref:
                ref[g] = jnp.concatenate([tg, oh_ref[...]], axis=1)
            else:
                ref[g] = tg
    rw_ref[...] = _dot_nt(y, wt_ref[T_RW:T_GR, :])
    gr = _dot_nt(y, wt_ref[T_GR:T_END, :])
    gr_ref[...] = gr * _sigmoid(gr)

    def proj_t(r0, r1):
        return _dot_nt(wf_ref[r0:r1, :], y)

    qt = proj_t(R_Q, R_VS) * (SCALE * LOG2E)
    cos, sin = cos_ref[...], sin_ref[...]
    qt_ref[...] = qt.astype(BF16)
    for h in range(NSA_HEADS):
        r0 = h * HEAD_DIM
        t1 = qt[r0:r0 + ROPE_HALF]
        t2 = qt[r0 + ROPE_HALF:r0 + ROPE_DIM]
        qrt_ref[r0:r0 + ROPE_DIM, :] = jnp.concatenate(
            [t1 * cos - t2 * sin, t2 * cos + t1 * sin], axis=0).astype(BF16)
        qrt_ref[r0 + ROPE_DIM:r0 + HEAD_DIM, :] = qt[r0 + ROPE_DIM:r0 + HEAD_DIM].astype(BF16)
    vt = proj_t(R_VS, R_GL).astype(BF16)
    ones = jnp.ones((V_ROWS - HEAD_DIM, LANE), BF16)
    for j in range(vt.shape[1] // LANE):
        for i, ref in enumerate((vst_ref, vwt_ref)):
            for g in range(NSA_KV):
                r0 = i * KV_WIDTH + g * HEAD_DIM
                ref[j, g * V_ROWS:(g + 1) * V_ROWS, :] = jnp.concatenate(
                    [vt[r0:r0 + HEAD_DIM, j * LANE:(j + 1) * LANE], ones], axis=0)
    gate_ref[...] = _sigmoid(proj_t(R_GL, R_GN))
    gn = proj_t(R_GN, R_END)
    gn_ref[...] = gn * _sigmoid(gn)


def _in_proj(x2, norm_g, w_t, w_f, tabs, *, seq, tm):
    n = x2.shape[0]
    spt = seq // tm
    ra, rm, rp, cos, sin, onehot = tabs
    hm = lambda w: jax.ShapeDtypeStruct((NSA_KV, n, w), BF16)
    hspec = lambda w: pl.BlockSpec((NSA_KV, tm, w), lambda i: (0, i, 0))
    row = lambda w: pl.BlockSpec((tm, w), lambda i: (i, 0))
    col = lambda r: pl.BlockSpec((r, tm), lambda i: (0, i))
    full = lambda a: pl.BlockSpec(a.shape, lambda i: (0,) * a.ndim)
    tab = lambda w: pl.BlockSpec((tm, w), lambda i: (i % spt, 0))
    tabt = pl.BlockSpec((ROPE_HALF, tm), lambda i: (0, i % spt))
    vtile = pl.BlockSpec((tm // LANE, NSA_KV * V_ROWS, LANE), lambda i: (i, 0, 0))
    vsd = jax.ShapeDtypeStruct((n // LANE, NSA_KV * V_ROWS, LANE), BF16)
    cw = CMP_STRIDE * HEAD_DIM
    cspec = pl.BlockSpec((NSA_KV, tm // CMP_STRIDE, cw), lambda i: (0, i, 0))
    csd = jax.ShapeDtypeStruct((NSA_KV, n // CMP_STRIDE, cw), BF16)
    return pl.pallas_call(
        _in_proj_kernel,
        grid=(n // tm,),
        in_specs=[row(D_MODEL), full(norm_g), full(w_t), full(w_f), tab(LANE), tab(LANE), tab(LANE),
                  tabt, tabt, tab(K_AUG - HEAD_DIM)],
        out_specs=[cspec, cspec, hspec(K_AUG), hspec(HEAD_DIM), row(RWKV_SHIFT_WIDTH), row(RWKV_WIDTH),
                   col(NSA_WIDTH), col(NSA_WIDTH), vtile, vtile,
                   col(GATE_ROWS), col(NSA_WIDTH)],
        out_shape=[csd, csd, hm(K_AUG), hm(HEAD_DIM),
                   jax.ShapeDtypeStruct((n, RWKV_SHIFT_WIDTH), F32),
                   jax.ShapeDtypeStruct((n, RWKV_WIDTH), F32),
                   jax.ShapeDtypeStruct((NSA_WIDTH, n), BF16),
                   jax.ShapeDtypeStruct((NSA_WIDTH, n), BF16),
                   vsd, vsd,
                   jax.ShapeDtypeStruct((GATE_ROWS, n), F32),
                   jax.ShapeDtypeStruct((NSA_WIDTH, n), F32)],
        scratch_shapes=[pltpu.VMEM((2, tm, KV_WIDTH), F32)],
        compiler_params=pltpu.CompilerParams(dimension_semantics=("arbitrary",),
                                             vmem_limit_bytes=VMEM_LIMIT),
        name="in_proj",
    )(x2, norm_g, w_t, w_f, ra, rm, rp, cos, sin, onehot)


def _compress_kernel(kc_ref, vc_ref, pk_ref, w1k_ref, w2k_ref, pv_ref, w1v_ref, w2vt_ref,
                     ko_ref, vo_ref):
    half = CMP_STRIDE * HEAD_DIM
    ng, _, nch, _ = kc_ref.shape

    def hidden(c_ref, pos_ref, w1_ref):
        c = c_ref[:, 0].reshape(ng * nch, half)
        pos = jnp.broadcast_to(pos_ref[...], (BF16_SUBLANE, 2 * half)).astype(BF16)
        za = _dot(jnp.concatenate([c, pos[:, 0:half]], axis=0), w1_ref[0:half, :])
        zb = _dot(jnp.concatenate([c, pos[:, half:]], axis=0), w1_ref[half:2 * half, :])
        pv = za[ng * nch:ng * nch + 1] + zb[ng * nch:ng * nch + 1]
        hid = jnp.concatenate(
            [za[g * nch:(g + 1) * nch] + pltpu.roll(zb[g * nch:(g + 1) * nch], nch - 1, 0) for g in range(ng)],
            axis=0) + pv
        return (hid * _sigmoid(hid)).astype(BF16)

    ko = _dot(hidden(kc_ref, pk_ref, w1k_ref), w2k_ref[...]).astype(BF16)
    hv = hidden(vc_ref, pv_ref, w1v_ref)
    for g in range(ng):
        ko_ref[g, 0] = ko[g * nch:(g + 1) * nch]
        vo_ref[g, 0] = _dot_nt(w2vt_ref[...], hv[g * nch:(g + 1) * nch]).astype(BF16)


def _compress(kc_r, vc_r, pk, w1k, w2k, pv, w1v, w2vt):
    g, b, nch, width = kc_r.shape
    blk = pl.BlockSpec((g, 1, nch, width), lambda j: (0, j, 0, 0))
    full = lambda a: pl.BlockSpec(a.shape, lambda j: (0,) * a.ndim)
    return pl.pallas_call(
        _compress_kernel,
        grid=(b,),
        in_specs=[blk, blk, full(pk), full(w1k), full(w2k), full(pv), full(w1v), full(w2vt)],
        out_specs=[pl.BlockSpec((g, 1, nch, HEAD_DIM), lambda j: (0, j, 0, 0)),
                   pl.BlockSpec((g, 1, HEAD_DIM, nch), lambda j: (0, j, 0, 0))],
        out_shape=[jax.ShapeDtypeStruct((g, b, nch, HEAD_DIM), BF16),
                   jax.ShapeDtypeStruct((g, b, HEAD_DIM, nch), BF16)],
        compiler_params=pltpu.CompilerParams(dimension_semantics=("arbitrary",),
                                             vmem_limit_bytes=VMEM_LIMIT),
        name="compress",
    )(kc_r, vc_r, pk, w1k, w2k, pv, w1v, w2vt)


def _nsa_kernel(qt_ref, qrt_ref, kc_ref, vct_ref, ks_ref, vst_ref, kw_ref, vwt_ref,
                gate_ref, gn_ref, mt_ref, o_ref, *, tq, seq):
    tk = KEY_TILE
    n_win = WINDOW // tk
    qi = pl.program_id(1)
    q0 = qi * tq
    nl = NSA_REP * tq
    n_sel = seq // SEL_BLOCK
    ncp = kc_ref.shape[2]
    groups = range(NSA_KV)

    def heads_on_lanes(ref, g):
        rows = [(g * NSA_REP + r) * HEAD_DIM for r in range(NSA_REP)]
        return jnp.concatenate([ref[r0:r0 + HEAD_DIM, :] for r0 in rows], axis=1)

    def tile4(a):
        return jnp.concatenate([a] * NSA_REP, axis=1)

    k_s = lax.broadcasted_iota(jnp.int32, (tk, tq), 0)
    t_l = q0 + lax.broadcasted_iota(jnp.int32, (tk, tq), 1)

    def update_steps(box, k_ref, vt_ref, g, q_op, kts, keeps):
        m_i, acc = box[0]

        def scores(i):
            sc = _dot(k_ref[g, pl.ds(pl.multiple_of(kts[i] * tk, tk), tk), :], q_op)
            if keeps[i] is None:
                return sc
            return jnp.concatenate([jnp.where(keeps[i], sc[:, r * tq:(r + 1) * tq], NEG_INF)
                                    for r in range(NSA_REP)], axis=1)

        sc_next = scores(0)
        yield
        for i, kt in enumerate(kts):
            sc = sc_next
            if i + 1 < len(kts):
                sc_next = scores(i + 1)
            m_n = jnp.maximum(m_i, jnp.max(sc, axis=0, keepdims=True))
            pe = jnp.exp2(sc - m_n).astype(BF16)
            yield
            vtb = jnp.concatenate([vt_ref[kt * (tk // LANE) + jj, g * V_ROWS:(g + 1) * V_ROWS, :]
                                   for jj in range(tk // LANE)], axis=1)
            acc = jnp.exp2(m_i - m_n) * acc + _dot(vtb, pe)
            m_i = m_n
            yield
        box[0] = (m_i, acc)

    def last_tiles(box, k_ref, vt_ref, g, q_op, low_keep, spare_tile):
        kts, keeps = [], []
        tiles = [(0, lambda d: d >= 0)] + [(i, None) for i in range(1, n_win)] + [(n_win, low_keep)]
        for back, keep_fn in tiles:
            exists = a >= back
            kt = jnp.where(exists, a - back, 0 if spare_tile is None else spare_tile)
            d = t_l - (kt * tk + k_s)
            kts.append(kt)
            if keep_fn is not None:
                keeps.append(keep_fn(d) & exists)
            else:
                keeps.append(exists if spare_tile is None else None)
        return update_steps(box, k_ref, vt_ref, g, q_op, kts, keeps)

    def select_steps(g, qr, sel_out):
        s = _dot(kc_ref[g, 0], heads_on_lanes(qt_ref, g))
        t_c = q0 + lax.broadcasted_iota(jnp.int32, (ncp, tq), 1)
        c_c = lax.broadcasted_iota(jnp.int32, (ncp, tq), 0)
        cmask = tile4((c_c * CMP_STRIDE + (CMP_BLOCK - 1)) <= t_c)
        yield
        s = jnp.where(cmask, s, NEG_INF)
        m = jnp.max(s, axis=0, keepdims=True)
        e = jnp.where(cmask, jnp.exp2(s - m), 0.0)
        den = jnp.sum(e, axis=0, keepdims=True)
        p = e * (1.0 / jnp.where(den > 0.0, den, 1.0))
        sel_out["o_cmp"] = _dot(vct_ref[g, 0], p.astype(BF16))
        yield
        psum = p[:, 0:tq]
        for r in range(1, NSA_REP):
            psum = psum + p[:, r * tq:(r + 1) * tq]
        imp = _x3_dot(mt_ref[...], psum)
        j = lax.broadcasted_iota(jnp.int32, (n_sel, tq), 0)
        t = q0 + lax.broadcasted_iota(jnp.int32, (n_sel, tq), 1)
        tb = t // SEL_BLOCK
        forced = (j == 0) | (j == tb) | (j == tb - 1)
        val = jnp.where(j <= tb, imp + jnp.where(forced, FORCE_BONUS, 0.0), -1.0)
        yield
        vals = [val[g0:g0 + SUBLANE] for g0 in range(0, n_sel, SUBLANE)]
        cnts = [jnp.zeros((SUBLANE, tq), F32) for _ in vals]
        srow = lax.broadcasted_iota(jnp.int32, (SUBLANE, tq), 0)
        for i in range(n_sel):
            vi = jnp.broadcast_to(val[i:i + 1, :], (SUBLANE, tq))
            for g, vg in enumerate(vals):
                ge = lambda: jnp.where(vi >= vg, 1.0, 0.0)
                gt = lambda: jnp.where(vi > vg, 1.0, 0.0)
                if g * SUBLANE > i:
                    beat = ge()
                elif (g + 1) * SUBLANE <= i:
                    beat = gt()
                else:
                    beat = jnp.where(srow > i - g * SUBLANE, ge(), gt())
                cnts[g] = cnts[g] + beat
            if i % SUBLANE == SUBLANE - 1:
                yield
        cnt = jnp.concatenate(cnts, axis=0)
        sel_bias = jnp.where(cnt < float(SEL_TOPK), 0.0, NEG_INF).astype(BF16)
        sel_out["qr_sel"] = jnp.concatenate(
            [qr, tile4(sel_bias), jnp.zeros((K_AUG - HEAD_DIM - n_sel, nl), BF16)], axis=0)

    init = (jnp.full((1, nl), NEG_INF, F32), jnp.zeros((V_ROWS, nl), F32))
    a = (q0 + tq - 1) // tk
    assert (seq // tk - 1) * (tk // SEL_BLOCK) >= SEL_TOPK and WINDOW // SEL_BLOCK <= SEL_TOPK
    c_wins, c_sels, o_cmps, qr_sels = [], [], [], []
    for g in groups:
        qr = heads_on_lanes(qrt_ref, g)
        win_box, sel_out = [init], {}
        _interleave(last_tiles(win_box, kw_ref, vwt_ref, g, qr, lambda d: d < WINDOW, None),
                    select_steps(g, qr, sel_out))
        sel_box = [init]
        _interleave(last_tiles(sel_box, ks_ref, vst_ref, g, sel_out["qr_sel"], None, seq // tk - 1))
        c_wins.append(win_box[0])
        c_sels.append(sel_box[0])
        o_cmps.append(sel_out["o_cmp"])
        qr_sels.append(sel_out["qr_sel"])

    def old_tiles(kts):
        def body(carry):
            boxes = [[c] for c in carry]
            _interleave(*[update_steps(boxes[g], ks_ref, vst_ref, g, qr_sels[g], kts, [None] * len(kts))
                          for g in groups])
            return tuple(b[0] for b in boxes)
        return body

    n_old = jnp.maximum(a - n_win, 0)
    c_sels = lax.fori_loop(0, n_old % 2, lambda kt, c: old_tiles([kt])(c), tuple(c_sels))
    c_sels = lax.fori_loop(0, n_old // 2,
                           lambda i, c: old_tiles([n_old % 2 + 2 * i, n_old % 2 + 2 * i + 1])(c), c_sels)

    for g in groups:
        o_sel = c_sels[g][1][0:HEAD_DIM] * (1.0 / c_sels[g][1][HEAD_DIM:HEAD_DIM + 1])
        o_win = c_wins[g][1][0:HEAD_DIM] * (1.0 / c_wins[g][1][HEAD_DIM:HEAD_DIM + 1])
        for r in range(NSA_REP):
            ls = slice(r * tq, (r + 1) * tq)
            h = g * NSA_REP + r
            gate = lambda c: gate_ref[3 * h + c:3 * h + c + 1, :]
            o = gate(0) * o_cmps[g][:, ls] + gate(1) * o_sel[:, ls] + gate(2) * o_win[:, ls]
            rs = slice(h * HEAD_DIM, (h + 1) * HEAD_DIM)
            o_ref[rs, :] = (o * gn_ref[rs, :]).astype(BF16)


def _nsa(qt, qrt, kcmp, vcmpt, ksh, vst, kwh, vwt, gates, gnt, mt, *, batch, seq, tq):
    n = batch * seq
    nq = seq // tq
    ncp = kcmp.shape[2]
    qspec = pl.BlockSpec((NSA_WIDTH, tq), lambda b, i: (0, b * nq + i))
    kspec = lambda w: pl.BlockSpec((NSA_KV, seq, w), lambda b, i: (0, b, 0))
    vspec = pl.BlockSpec((seq // LANE, NSA_KV * V_ROWS, LANE), lambda b, i: (b, 0, 0))
    kern = functools.partial(_nsa_kernel, tq=tq, seq=seq)
    return pl.pallas_call(
        kern,
        grid=(batch, nq),
        in_specs=[qspec, qspec,
                  pl.BlockSpec((NSA_KV, 1, ncp, HEAD_DIM), lambda b, i: (0, b, 0, 0)),
                  pl.BlockSpec((NSA_KV, 1, HEAD_DIM, ncp), lambda b, i: (0, b, 0, 0)),
                  kspec(K_AUG), vspec, kspec(HEAD_DIM), vspec,
                  pl.BlockSpec((GATE_ROWS, tq), lambda b, i: (0, b * nq + i)),
                  qspec,
                  pl.BlockSpec(mt.shape, lambda b, i: (0, 0))],
        out_specs=qspec,
        out_shape=jax.ShapeDtypeStruct((NSA_WIDTH, n), BF16),
        compiler_params=pltpu.CompilerParams(
            dimension_semantics=("arbitrary", "arbitrary"),
            vmem_limit_bytes=VMEM_LIMIT),
        name="nsa",
    )(qt, qrt, kcmp, vcmpt, ksh, vst, kwh, vwt, gates, gnt, mt)


def _rwkv_kernel(p_ref, gr_ref, mu_ref, w0_ref, a0_ref, wab_ref, kk_ref, ka_ref, rk_ref,
                 gw_ref, gb_ref, ones_ref, tril_ref, o_ref, st_ref, carry_ref, *, tt):
    c = RWKV_CHUNK
    hd = HEAD_DIM
    nc = tt // c
    step = pl.program_id(1)

    @pl.when(step == 0)
    def _():
        st_ref[...] = jnp.zeros_like(st_ref)
        carry_ref[...] = jnp.zeros_like(carry_ref)

    p = p_ref[...]
    row = lax.broadcasted_iota(jnp.int32, p.shape, 0)
    prev = jnp.where(row == 0, carry_ref[...], pltpu.roll(p, 1, 0))
    carry_ref[...] = p[tt - 1:tt, :]
    ps = p + mu_ref[...] * (prev - p)
    r = ps[:, 0:RWKV_WIDTH]
    k = ps[:, RWKV_WIDTH:2 * RWKV_WIDTH]
    v = ps[:, 2 * RWKV_WIDTH:3 * RWKV_WIDTH]
    lora = ps[:, 3 * RWKV_WIDTH:]
    lane = lax.broadcasted_iota(jnp.int32, lora.shape, 1)
    feat = jnp.where(lane < DECAY_RANK, jnp.tanh(lora), lora).astype(BF16)
    up = _dot(feat, wab_ref[...])
    w = w0_ref[...] + up[:, 0:RWKV_WIDTH]
    lw = _sigmoid(w) * (-float(np.exp(-0.5)))
    a = _sigmoid(a0_ref[...] + up[:, RWKV_WIDTH:])
    ones_bd = ones_ref[...]
    kk = k * kk_ref[...]
    kkn = kk * lax.rsqrt(jnp.maximum(_head_sums(kk * kk, ones_bd), 1e-24))
    k2 = k * (1.0 + (a - 1.0) * ka_ref[...])
    alpha = -kkn
    beta = kkn * a
    bonus = _head_sums(r * k2 * rk_ref[...], ones_bd, terms=1) * v

    cum = _x3_dot(tril_ref[...], lw)
    cend = jnp.concatenate(
        [jnp.broadcast_to(cum[(ch + 1) * c - 1:(ch + 1) * c, :], (c, RWKV_WIDTH)) for ch in range(nc)], axis=0)
    e_neg = jnp.exp(-cum)
    pc = jnp.exp(cend)
    at = alpha * jnp.exp(cum - lw)
    bt = beta * e_neg
    kt = k2 * e_neg
    rt = r * jnp.exp(cum)
    bh = bt * pc
    kh = kt * pc

    npair = RWKV_WIDTH // LANE

    def pairs(x):
        return jnp.stack([x[ch * c:(ch + 1) * c, j * LANE:(j + 1) * LANE]
                          for ch in range(nc) for j in range(npair)], axis=0)

    at_p, rt_p, bt_p, kt_p, v_p, bh_p, kh_p = (pairs(t) for t in (at, rt, bt, kt, v, bh, kh))
    pc_p = jnp.stack([pc[ch * c:ch * c + 1, j * LANE:(j + 1) * LANE]
                      for ch in range(nc) for j in range(npair)], axis=0)
    lane_c = lax.broadcasted_iota(jnp.int32, (1, c, LANE), 2)
    row_c = lax.broadcasted_iota(jnp.int32, (1, c, LANE), 1)
    even_c = lane_c < hd
    col_c = jnp.where(even_c, lane_c, lane_c - hd)
    low_s = row_c > col_c
    low_i = row_c >= col_c
    lane_2c = lax.broadcasted_iota(jnp.int32, (1, 2 * c, LANE), 2)
    row_2c = lax.broadcasted_iota(jnp.int32, (1, 2 * c, LANE), 1)
    even_2c = lane_2c < hd
    on_bd = (row_2c < hd) == even_2c
    zero_c = jnp.zeros((1, c, LANE), BF16)

    def bd(x):
        xb = x.astype(BF16)
        return jnp.concatenate([jnp.where(even_c, xb, zero_c), jnp.where(even_c, zero_c, xb)], axis=1)

    def abd(x):
        xb = x.astype(BF16)
        return jnp.concatenate([jnp.where(even_c, zero_c, xb), jnp.where(even_c, xb, zero_c)], axis=1)

    la = jnp.concatenate([at_p, rt_p], axis=1).astype(BF16)
    zero_2c = jnp.zeros((1, 2 * c, LANE), BF16)
    r_e = _bmm_nt(jnp.where(even_2c, la, zero_2c), jnp.concatenate([bt_p, kt_p], axis=1).astype(BF16))
    r_o = _bmm_nt(jnp.where(even_2c, zero_2c, la), jnp.concatenate([kt_p, bt_p], axis=1).astype(BF16))
    nab = jnp.where(low_s, jnp.where(even_c, r_e[:, 0:c], r_o[:, 0:c]), 0.0)
    aak_sw = jnp.where(low_s, jnp.where(even_c, r_o[:, 0:c], r_e[:, 0:c]), 0.0).astype(BF16)
    arb = jnp.where(low_i, jnp.where(even_c, r_e[:, c:], r_o[:, c:]), 0.0).astype(BF16)
    ark_sw = jnp.where(low_i, jnp.where(even_c, r_o[:, c:], r_e[:, c:]), 0.0).astype(BF16)
    tinv = jnp.where(row_c == col_c, 1.0, 0.0) + nab
    npow = _bmm(nab.astype(BF16), bd(nab))
    n_dbl = 5
    for it in range(n_dbl):
        nbd = bd(npow)
        if it + 1 < n_dbl:
            res = _bmm(jnp.concatenate([tinv, npow], axis=1).astype(BF16), nbd)
            tinv = tinv + res[:, 0:c]
            npow = res[:, c:]
        else:
            tinv = tinv + _bmm(tinv.astype(BF16), nbd)
    av = _bmm(jnp.concatenate([aak_sw, ark_sw], axis=1), abd(v_p))
    tx = _bmm(tinv.astype(BF16), jnp.concatenate([bd(av[:, 0:c]), bd(at_p)], axis=2))
    u0, ta = tx[:, :, 0:LANE], tx[:, :, LANE:]
    ax = _bmm(arb, jnp.concatenate([bd(ta), bd(u0)], axis=2))
    rq = rt_p + ax[:, :, 0:LANE]
    y0 = ax[:, :, LANE:] + av[:, c:]
    v_b = v_p.astype(BF16)
    w_f = jnp.concatenate([jnp.concatenate([ta, u0], axis=2).astype(BF16),
                           jnp.concatenate([jnp.zeros_like(v_b), v_b], axis=2)], axis=1)
    gh = _bmm(jnp.concatenate([jnp.swapaxes(bh_p, 1, 2), jnp.swapaxes(kh_p, 1, 2)], axis=2).astype(BF16), w_f)
    g_bd = jnp.where(on_bd, gh[:, :, 0:LANE], 0.0) + jnp.where(row_2c == lane_2c, pc_p, 0.0)
    h_bd = jnp.where(on_bd, gh[:, :, LANE:], 0.0)
    lhs = jnp.concatenate([rq, g_bd], axis=1).astype(BF16)

    st = st_ref[...]
    ys = []
    for ch in range(nc):
        sl = slice(ch * npair, (ch + 1) * npair)
        res = _bmm(lhs[sl], st.astype(BF16))
        yc = res[:, 0:c, :] + y0[sl]
        st = res[:, c:, :] + h_bd[sl]
        ys.append(jnp.concatenate([yc[j] for j in range(npair)], axis=1))
    st_ref[...] = st
    y = jnp.concatenate(ys, axis=0) if nc > 1 else ys[0]

    inv_hd = 1.0 / hd
    mean = _head_sums(y, ones_bd, terms=1) * inv_hd
    ycen = y - mean
    var = _head_sums(ycen * ycen, ones_bd, terms=1) * inv_hd
    yn = ycen * lax.rsqrt(var + GN_EPS) * gw_ref[...] + gb_ref[...]
    o_ref[...] = ((yn + bonus) * gr_ref[...]).astype(BF16)


def _rwkv(rw, gr, mu, w0, a0, wab, kk, ka, rk, gw, gb, ones_bd, tril, *, batch, seq, tt):
    n = batch * seq
    ns = seq // tt
    row = lambda w: pl.BlockSpec((tt, w), lambda b, i: (b * ns + i, 0))
    full = lambda a: pl.BlockSpec(a.shape, lambda b, i: (0,) * a.ndim)
    kern = functools.partial(_rwkv_kernel, tt=tt)
    consts = (mu, w0, a0, wab, kk, ka, rk, gw, gb, ones_bd, tril)
    return pl.pallas_call(
        kern,
        grid=(batch, ns),
        in_specs=[row(RWKV_SHIFT_WIDTH), row(RWKV_WIDTH)] + [full(a) for a in consts],
        out_specs=row(RWKV_WIDTH),
        out_shape=jax.ShapeDtypeStruct((n, RWKV_WIDTH), BF16),
        scratch_shapes=[pltpu.VMEM((RWKV_WIDTH // LANE, LANE, LANE), F32),
                        pltpu.VMEM((1, RWKV_SHIFT_WIDTH), F32)],
        compiler_params=pltpu.CompilerParams(dimension_semantics=("arbitrary", "arbitrary"),
                                             vmem_limit_bytes=VMEM_LIMIT),
        name="rwkv",
    )(rw, gr, *consts)


def _out_kernel(x_ref, ont_ref, or_ref, wn_ref, wr_ref, g_ref, o_ref):
    h = x_ref[...] + _dot_tn(ont_ref[...], wn_ref[...]) + _dot(or_ref[...], wr_ref[...])
    ms = jnp.mean(h * h, axis=-1, keepdims=True)
    o_ref[...] = h * lax.rsqrt(ms + RMS_EPS) * g_ref[...]


def _out_proj(x2, o_nsa_t, o_rwkv, wn, wr, final_g, *, tm):
    n = x2.shape[0]
    row = lambda w: pl.BlockSpec((tm, w), lambda i: (i, 0))
    full = lambda a: pl.BlockSpec(a.shape, lambda i: (0,) * a.ndim)
    return pl.pallas_call(
        _out_kernel,
        grid=(n // tm,),
        in_specs=[row(D_MODEL), pl.BlockSpec((NSA_WIDTH, tm), lambda i: (0, i)), row(RWKV_WIDTH),
                  full(wn), full(wr), full(final_g)],
        out_specs=row(D_MODEL),
        out_shape=jax.ShapeDtypeStruct((n, D_MODEL), F32),
        compiler_params=pltpu.CompilerParams(dimension_semantics=("arbitrary",),
                                             vmem_limit_bytes=VMEM_LIMIT),
        name="out_proj",
    )(x2, o_nsa_t, o_rwkv, wn, wr, final_g)


def _rope_tables(seq):
    inv = ROPE_THETA ** (-np.arange(ROPE_HALF, dtype=np.float64) / ROPE_HALF)
    ang = np.arange(seq, dtype=np.float64)[:, None] * inv[None, :]
    cos, sin = np.cos(ang), np.sin(ang)
    ra = np.ones((seq, HEAD_DIM)); rm = np.zeros((seq, HEAD_DIM)); rp = np.zeros((seq, HEAD_DIM))
    ra[:, :ROPE_HALF] = cos; ra[:, ROPE_HALF:ROPE_DIM] = cos
    rm[:, :ROPE_HALF] = -sin
    rp[:, ROPE_HALF:ROPE_DIM] = sin
    rep = lambda t: jnp.asarray(np.tile(t, (1, LANE // HEAD_DIM)), F32)
    assert seq // SEL_BLOCK <= K_AUG - HEAD_DIM
    onehot = np.zeros((seq, K_AUG - HEAD_DIM))
    onehot[np.arange(seq), np.arange(seq) // SEL_BLOCK] = 1.0
    return (rep(ra), rep(rm), rep(rp), jnp.asarray(cos.T, F32), jnp.asarray(sin.T, F32),
            jnp.asarray(onehot, BF16))


def _cmp_to_sel_t(n_cmp_pad, n_sel):
    n_cmp = n_cmp_pad - 1
    c0 = np.arange(n_cmp)[:, None] * CMP_STRIDE
    s0 = np.arange(n_sel)[None, :] * SEL_BLOCK
    ov = np.clip(np.minimum(c0 + CMP_BLOCK, s0 + SEL_BLOCK) - np.maximum(c0, s0), 0, None) / CMP_BLOCK
    mt = np.zeros((n_sel, n_cmp_pad))
    mt[:, :n_cmp] = ov.T
    return jnp.asarray(mt, BF16)


def _prep_w_in(w_in):
    idx = np.cumsum(IN_SIZES)[:-1].tolist()
    q, kc, vc, ks, vs, kw, vw, gl, gn, rw, gr = jnp.split(w_in, idx, axis=1)
    w_t = jnp.concatenate([kc, vc, ks, kw, rw, gr], axis=1).T.astype(BF16)
    pad = jnp.zeros((D_MODEL, GATE_ROWS - gl.shape[1]), w_in.dtype)
    w_f = jnp.concatenate([q, vs, vw, gl, pad, gn], axis=1).T.astype(BF16)
    return w_t, w_f


def _layer(x2, norm_g, w_in, cmp_pos_k, cmp_w1_k, cmp_w2_k, cmp_pos_v, cmp_w1_v, cmp_w2_v,
           shift_mu, decay_w0, decay_up, iclr_a0, iclr_up, k_k, k_a, r_k, gn_w, gn_b, w_out,
           final_g, *, batch, seq):
    tm = 256
    tq = 256
    tt = 256
    assert WINDOW % KEY_TILE == 0 and seq % KEY_TILE == 0 and KEY_TILE % tq == 0
    nch = seq // CMP_STRIDE
    n_sel = seq // SEL_BLOCK
    rowv = lambda t: t.reshape(1, -1).astype(F32)

    w_t, w_f = _prep_w_in(w_in)
    (kch, vch, ksh, kwh, rw, gr, qt, qrt, vst, vwt, gates, gnt) = _in_proj(
        x2, rowv(norm_g), w_t, w_f, _rope_tables(seq), seq=seq, tm=2 * tm)

    chunks = lambda t: t.reshape(NSA_KV, batch, nch, CMP_STRIDE * HEAD_DIM)
    kcmp, vcmpt = _compress(chunks(kch), chunks(vch),
                            rowv(cmp_pos_k), cmp_w1_k.astype(BF16), cmp_w2_k.astype(BF16),
                            rowv(cmp_pos_v), cmp_w1_v.astype(BF16), cmp_w2_v.T.astype(BF16))

    o_nsa_t = _nsa(qt, qrt, kcmp, vcmpt, ksh, vst, kwh, vwt, gates, gnt,
                   _cmp_to_sel_t(nch, n_sel), batch=batch, seq=seq, tq=tq)

    z = jnp.zeros((DECAY_RANK, RWKV_WIDTH), F32)
    wab = jnp.concatenate([jnp.concatenate([decay_up, z], axis=1),
                           jnp.concatenate([z, iclr_up], axis=1)], axis=0).astype(BF16)
    hid = np.arange(2 * LANE) // HEAD_DIM
    ones_bd = jnp.asarray(hid[:, None] == hid[None, :], BF16)
    ti = np.arange(tt)
    tril = jnp.asarray((ti[:, None] >= ti[None, :]) & (ti[:, None] // RWKV_CHUNK == ti[None, :] // RWKV_CHUNK), BF16)
    o_rwkv = _rwkv(rw, gr, rowv(shift_mu), rowv(decay_w0), rowv(iclr_a0), wab, rowv(k_k), rowv(k_a),
                   rowv(r_k), rowv(gn_w), rowv(gn_b), ones_bd, tril, batch=batch, seq=seq, tt=tt)

    w_o = w_out.astype(BF16)
    return _out_proj(x2, o_nsa_t, o_rwkv, w_o[:NSA_WIDTH], w_o[NSA_WIDTH:], rowv(final_g), tm=4 * tm)


def kernel(x, norm_g, w_in, cmp_pos_k, cmp_w1_k, cmp_w2_k, cmp_pos_v, cmp_w1_v, cmp_w2_v, shift_mu, decay_w0, decay_up, iclr_a0, iclr_up, k_k, k_a, r_k, gn_w, gn_b, w_out, final_g):
    batch, seq, d = x.shape
    assert d == D_MODEL and norm_g.shape[0] == 1, "single-layer trunk"
    out = _layer(x.reshape(batch * seq, d), norm_g[0], w_in[0], cmp_pos_k[0], cmp_w1_k[0], cmp_w2_k[0],
                 cmp_pos_v[0], cmp_w1_v[0], cmp_w2_v[0], shift_mu[0], decay_w0[0], decay_up[0],
                 iclr_a0[0], iclr_up[0], k_k[0], k_a[0], r_k[0], gn_w[0], gn_b[0], w_out[0],
                 final_g, batch=batch, seq=seq)
    return out.reshape(batch, seq, d)
```

```python
import functools

import numpy as np
import jax
import jax.numpy as jnp
from jax import lax
from jax.experimental import pallas as pl
from jax.experimental.pallas import tpu as pltpu

F32 = jnp.float32
BF16 = jnp.bfloat16

D_MODEL = 1024
HEAD_DIM = 64
NSA_HEADS = 8
NSA_KV = 2
NSA_REP = NSA_HEADS // NSA_KV
RWKV_HEADS = 8
NSA_WIDTH = NSA_HEADS * HEAD_DIM
RWKV_WIDTH = RWKV_HEADS * HEAD_DIM
KV_WIDTH = NSA_KV * HEAD_DIM
ROPE_DIM = HEAD_DIM // 4
ROPE_HALF = ROPE_DIM // 2
ROPE_THETA = 500000.0
CMP_BLOCK = 32
CMP_STRIDE = 16
CMP_HIDDEN = 256
SEL_BLOCK = 64
SEL_TOPK = 8
WINDOW = 512
DECAY_RANK = 64
ICLR_RANK = 64
RWKV_SHIFT_WIDTH = 3 * RWKV_WIDTH + DECAY_RANK + ICLR_RANK
IN_SIZES = (NSA_WIDTH, KV_WIDTH, KV_WIDTH, KV_WIDTH, KV_WIDTH, KV_WIDTH, KV_WIDTH,
            3 * NSA_HEADS, NSA_WIDTH, RWKV_SHIFT_WIDTH, RWKV_WIDTH)
SCALE = HEAD_DIM ** -0.5
RMS_EPS = 1e-6
GN_EPS = 64e-5
NEG_INF = -1e30
FORCE_BONUS = 1e3

LANE = 128
SUBLANE = 8
BF16_SUBLANE = 2 * SUBLANE
GATE_ROWS = 2 * BF16_SUBLANE

T_KV = 0
T_RW = T_KV + 4 * KV_WIDTH
T_GR = T_RW + RWKV_SHIFT_WIDTH
T_END = T_GR + RWKV_WIDTH
R_Q = 0
R_VS = R_Q + NSA_WIDTH
R_VW = R_VS + KV_WIDTH
R_GL = R_VW + KV_WIDTH
R_GN = R_GL + GATE_ROWS
R_END = R_GN + NSA_WIDTH

LOG2E = float(np.log2(np.e))
K_AUG = LANE
V_ROWS = HEAD_DIM + BF16_SUBLANE
KEY_TILE = 256
RWKV_CHUNK = 64
VMEM_LIMIT = 48 * 1024 * 1024


def _dot(a, b):
    return jnp.dot(a, b, preferred_element_type=F32)


def _dot_nt(a, b):
    return lax.dot_general(a, b, (((1,), (1,)), ((), ())), preferred_element_type=F32)


def _dot_tn(a, b):
    return lax.dot_general(a, b, (((0,), (0,)), ((), ())), preferred_element_type=F32)


def _bmm(a, b):
    return lax.dot_general(a, b, (((2,), (1,)), ((0,), (0,))), preferred_element_type=F32)


def _bmm_nt(a, b):
    return lax.dot_general(a, b, (((2,), (2,)), ((0,), (0,))), preferred_element_type=F32)


def _split3(x):
    hi = x.astype(BF16)
    r1 = x - hi.astype(F32)
    mid = r1.astype(BF16)
    lo = (r1 - mid.astype(F32)).astype(BF16)
    return hi, mid, lo


def _head_sums(x, ones_blk, terms=2):
    w = ones_blk.shape[0]
    parts = [x.astype(BF16)]
    if terms == 2:
        parts.append((x - parts[0].astype(F32)).astype(BF16))
    cols = []
    for i in range(0, x.shape[1], w):
        acc = _dot(parts[0][:, i:i + w], ones_blk)
        for t in parts[1:]:
            acc = acc + _dot(t[:, i:i + w], ones_blk)
        cols.append(acc)
    return jnp.concatenate(cols, axis=1)


def _x3_dot(w_bf16, x):
    hi, mid, lo = _split3(x)
    return _dot(w_bf16, hi) + _dot(w_bf16, mid) + _dot(w_bf16, lo)


def _sigmoid(x):
    return 0.5 * jnp.tanh(0.5 * x) + 0.5


def _interleave(*gens):
    live = list(gens)
    while live:
        for g in list(live):
            try:
                next(g)
            except StopIteration:
                live.remove(g)


def _rope128(t, ra, rm, rp):
    return t * ra + pltpu.roll(t, LANE - ROPE_HALF, 1) * rm + pltpu.roll(t, ROPE_HALF, 1) * rp


def _in_proj_kernel(x_ref, g_ref, wt_ref, wf_ref, ra_ref, rm_ref, rp_ref, cos_ref, sin_ref, oh_ref,
                    kc_ref, vc_ref, ks_ref, kw_ref, rw_ref, gr_ref,
                    qt_ref, qrt_ref, vst_ref, vwt_ref, gate_ref, gn_ref, cmp_scr):
    x = x_ref[...]
    ms = jnp.mean(x * x, axis=-1, keepdims=True)
    y = (x * lax.rsqrt(ms + RMS_EPS) * g_ref[...]).astype(BF16)
    tm = x.shape[0]

    kv = _dot_nt(y, wt_ref[T_KV:T_RW, :])
    for i, ref in enumerate((kc_ref, vc_ref)):
        cmp_scr[i] = kv[:, i * KV_WIDTH:(i + 1) * KV_WIDTH]
        for tau in range(CMP_STRIDE):
            piece = cmp_scr[i, pl.ds(tau, tm // CMP_STRIDE, stride=CMP_STRIDE), :].astype(BF16)
            for g in range(NSA_KV):
                ref[g, :, tau * HEAD_DIM:(tau + 1) * HEAD_DIM] = piece[:, g * HEAD_DIM:(g + 1) * HEAD_DIM]
    ra, rm, rp = ra_ref[...], rm_ref[...], rp_ref[...]
    for i, ref in ((2, ks_ref), (3, kw_ref)):
        t = _rope128(kv[:, i * LANE:(i + 1) * LANE], ra, rm, rp)
        for g in range(NSA_KV):
            tg = t[:, g * HEAD_DIM:(g + 1) * HEAD_DIM].astype(BF16)
            if ref is ks_ref:
                ref[g] = jnp.concatenate([tg, oh_ref[...]], axis=1)
            else:
                ref[g] = tg
    rw_ref[...] = _dot_nt(y, wt_ref[T_RW:T_GR, :])
    gr = _dot_nt(y, wt_ref[T_GR:T_END, :])
    gr_ref[...] = gr * _sigmoid(gr)

    def proj_t(r0, r1):
        return _dot_nt(wf_ref[r0:r1, :], y)

    qt = proj_t(R_Q, R_VS) * (SCALE * LOG2E)
    cos, sin = cos_ref[...], sin_ref[...]
    qt_ref[...] = qt.astype(BF16)
    for h in range(NSA_HEADS):
        r0 = h * HEAD_DIM
        t1 = qt[r0:r0 + ROPE_HALF]
        t2 = qt[r0 + ROPE_HALF:r0 + ROPE_DIM]
        qrt_ref[r0:r0 + ROPE_DIM, :] = jnp.concatenate(
            [t1 * cos - t2 * sin, t2 * cos + t1 * sin], axis=0).astype(BF16)
        qrt_ref[r0 + ROPE_DIM:r0 + HEAD_DIM, :] = qt[r0 + ROPE_DIM:r0 + HEAD_DIM].astype(BF16)
    vt = proj_t(R_VS, R_GL).astype(BF16)
    ones = jnp.ones((V_ROWS - HEAD_DIM, LANE), BF16)
    for j in range(vt.shape[1] // LANE):
        for i, ref in enumerate((vst_ref, vwt_ref)):
            for g in range(NSA_KV):
                r0 = i * KV_WIDTH + g * HEAD_DIM
                ref[j, g * V_ROWS:(g + 1) * V_ROWS, :] = jnp.concatenate(
                    [vt[r0:r0 + HEAD_DIM, j * LANE:(j + 1) * LANE], ones], axis=0)
    gate_ref[...] = _sigmoid(proj_t(R_GL, R_GN))
    gn = proj_t(R_GN, R_END)
    gn_ref[...] = gn * _sigmoid(gn)


def _in_proj(x2, norm_g, w_t, w_f, tabs, *, seq, tm):
    n = x2.shape[0]
    spt = seq // tm
    ra, rm, rp, cos, sin, onehot = tabs
    hm = lambda w: jax.ShapeDtypeStruct((NSA_KV, n, w), BF16)
    hspec = lambda w: pl.BlockSpec((NSA_KV, tm, w), lambda i: (0, i, 0))
    row = lambda w: pl.BlockSpec((tm, w), lambda i: (i, 0))
    col = lambda r: pl.BlockSpec((r, tm), lambda i: (0, i))
    full = lambda a: pl.BlockSpec(a.shape, lambda i: (0,) * a.ndim)
    tab = lambda w: pl.BlockSpec((tm, w), lambda i: (i % spt, 0))
    tabt = pl.BlockSpec((ROPE_HALF, tm), lambda i: (0, i % spt))
    vtile = pl.BlockSpec((tm // LANE, NSA_KV * V_ROWS, LANE), lambda i: (i, 0, 0))
    vsd = jax.ShapeDtypeStruct((n // LANE, NSA_KV * V_ROWS, LANE), BF16)
    cw = CMP_STRIDE * HEAD_DIM
    cspec = pl.BlockSpec((NSA_KV, tm // CMP_STRIDE, cw), lambda i: (0, i, 0))
    csd = jax.ShapeDtypeStruct((NSA_KV, n // CMP_STRIDE, cw), BF16)
    return pl.pallas_call(
        _in_proj_kernel,
        grid=(n // tm,),
        in_specs=[row(D_MODEL), full(norm_g), full(w_t), full(w_f), tab(LANE), tab(LANE), tab(LANE),
                  tabt, tabt, tab(K_AUG - HEAD_DIM)],
        out_specs=[cspec, cspec, hspec(K_AUG), hspec(HEAD_DIM), row(RWKV_SHIFT_WIDTH), row(RWKV_WIDTH),
                   col(NSA_WIDTH), col(NSA_WIDTH), vtile, vtile,
                   col(GATE_ROWS), col(NSA_WIDTH)],
        out_shape=[csd, csd, hm(K_AUG), hm(HEAD_DIM),
                   jax.ShapeDtypeStruct((n, RWKV_SHIFT_WIDTH), F32),
                   jax.ShapeDtypeStruct((n, RWKV_WIDTH), F32),
                   jax.ShapeDtypeStruct((NSA_WIDTH, n), BF16),
                   jax.ShapeDtypeStruct((NSA_WIDTH, n), BF16),
                   vsd, vsd,
                   jax.ShapeDtypeStruct((GATE_ROWS, n), F32),
                   jax.ShapeDtypeStruct((NSA_WIDTH, n), F32)],
        scratch_shapes=[pltpu.VMEM((2, tm, KV_WIDTH), F32)],
        compiler_params=pltpu.CompilerParams(dimension_semantics=("arbitrary",),
                                             vmem_limit_bytes=VMEM_LIMIT),
        name="in_proj",
    )(x2, norm_g, w_t, w_f, ra, rm, rp, cos, sin, onehot)


def _compress_kernel(kc_ref, vc_ref, pk_ref, w1k_ref, w2k_ref, pv_ref, w1v_ref, w2vt_ref,
                     ko_ref, vo_ref):
    half = CMP_STRIDE * HEAD_DIM
    ng, _, nch, _ = kc_ref.shape

    def hidden(c_ref, pos_ref, w1_ref):
        c = c_ref[:, 0].reshape(ng * nch, half)
        pos = jnp.broadcast_to(pos_ref[...], (BF16_SUBLANE, 2 * half)).astype(BF16)
        za = _dot(jnp.concatenate([c, pos[:, 0:half]], axis=0), w1_ref[0:half, :])
        zb = _dot(jnp.concatenate([c, pos[:, half:]], axis=0), w1_ref[half:2 * half, :])
        pv = za[ng * nch:ng * nch + 1] + zb[ng * nch:ng * nch + 1]
        hid = jnp.concatenate(
            [za[g * nch:(g + 1) * nch] + pltpu.roll(zb[g * nch:(g + 1) * nch], nch - 1, 0) for g in range(ng)],
            axis=0) + pv
        return (hid * _sigmoid(hid)).astype(BF16)

    ko = _dot(hidden(kc_ref, pk_ref, w1k_ref), w2k_ref[...]).astype(BF16)
    hv = hidden(vc_ref, pv_ref, w1v_ref)
    for g in range(ng):
        ko_ref[g, 0] = ko[g * nch:(g + 1) * nch]
        vo_ref[g, 0] = _dot_nt(w2vt_ref[...], hv[g * nch:(g + 1) * nch]).astype(BF16)


def _compress(kc_r, vc_r, pk, w1k, w2k, pv, w1v, w2vt):
    g, b, nch, width = kc_r.shape
    blk = pl.BlockSpec((g, 1, nch, width), lambda j: (0, j, 0, 0))
    full = lambda a: pl.BlockSpec(a.shape, lambda j: (0,) * a.ndim)
    return pl.pallas_call(
        _compress_kernel,
        grid=(b,),
        in_specs=[blk, blk, full(pk), full(w1k), full(w2k), full(pv), full(w1v), full(w2vt)],
        out_specs=[pl.BlockSpec((g, 1, nch, HEAD_DIM), lambda j: (0, j, 0, 0)),
                   pl.BlockSpec((g, 1, HEAD_DIM, nch), lambda j: (0, j, 0, 0))],
        out_shape=[jax.ShapeDtypeStruct((g, b, nch, HEAD_DIM), BF16),
                   jax.ShapeDtypeStruct((g, b, HEAD_DIM, nch), BF16)],
        compiler_params=pltpu.CompilerParams(dimension_semantics=("arbitrary",),
                                             vmem_limit_bytes=VMEM_LIMIT),
        name="compress",
    )(kc_r, vc_r, pk, w1k, w2k, pv, w1v, w2vt)


def _nsa_kernel(qt_ref, qrt_ref, kc_ref, vct_ref, ks_ref, vst_ref, kw_ref, vwt_ref,
                gate_ref, gn_ref, mt_ref, o_ref, *, tq, seq):
    tk = KEY_TILE
    n_win = WINDOW // tk
    qi = pl.program_id(1)
    q0 = qi * tq
    nl = NSA_REP * tq
    n_sel = seq // SEL_BLOCK
    ncp = kc_ref.shape[2]
    groups = range(NSA_KV)

    def heads_on_lanes(ref, g):
        rows = [(g * NSA_REP + r) * HEAD_DIM for r in range(NSA_REP)]
        return jnp.concatenate([ref[r0:r0 + HEAD_DIM, :] for r0 in rows], axis=1)

    def tile4(a):
        return jnp.concatenate([a] * NSA_REP, axis=1)

    k_s = lax.broadcasted_iota(jnp.int32, (tk, tq), 0)
    t_l = q0 + lax.broadcasted_iota(jnp.int32, (tk, tq), 1)

    def update_steps(box, k_ref, vt_ref, g, q_op, kts, keeps):
        m_i, acc = box[0]

        def scores(i):
            sc = _dot(k_ref[g, pl.ds(pl.multiple_of(kts[i] * tk, tk), tk), :], q_op)
            if keeps[i] is None:
                return sc
            return jnp.concatenate([jnp.where(keeps[i], sc[:, r * tq:(r + 1) * tq], NEG_INF)
                                    for r in range(NSA_REP)], axis=1)

        sc_next = scores(0)
        yield
        for i, kt in enumerate(kts):
            sc = sc_next
            if i + 1 < len(kts):
                sc_next = scores(i + 1)
            m_n = jnp.maximum(m_i, jnp.max(sc, axis=0, keepdims=True))
            pe = jnp.exp2(sc - m_n).astype(BF16)
            yield
            vtb = jnp.concatenate([vt_ref[kt * (tk // LANE) + jj, g * V_ROWS:(g + 1) * V_ROWS, :]
                                   for jj in range(tk // LANE)], axis=1)
            acc = jnp.exp2(m_i - m_n) * acc + _dot(vtb, pe)
            m_i = m_n
            yield
        box[0] = (m_i, acc)

    def last_tiles(box, k_ref, vt_ref, g, q_op, n_back, low_keep):
        kts, keeps = [], []
        for back in range(n_back + 1):
            d = t_l - ((a - back) * tk + k_s)
            kts.append(a - back)
            if back == 0:
                keeps.append(d >= 0)
            elif back == n_win and low_keep is not None:
                keeps.append(low_keep(d))
            else:
                keeps.append(None)
        return update_steps(box, k_ref, vt_ref, g, q_op, kts, keeps)

    def select_steps(g, qr, sel_out):
        s = _dot(kc_ref[g, 0], heads_on_lanes(qt_ref, g))
        t_c = q0 + lax.broadcasted_iota(jnp.int32, (ncp, tq), 1)
        c_c = lax.broadcasted_iota(jnp.int32, (ncp, tq), 0)
        cmask = tile4((c_c * CMP_STRIDE + (CMP_BLOCK - 1)) <= t_c)
        yield
        s = jnp.where(cmask, s, NEG_INF)
        m = jnp.max(s, axis=0, keepdims=True)
        e = jnp.where(cmask, jnp.exp2(s - m), 0.0)
        den = jnp.sum(e, axis=0, keepdims=True)
        p = e * (1.0 / jnp.where(den > 0.0, den, 1.0))
        sel_out["o_cmp"] = _dot(vct_ref[g, 0], p.astype(BF16))
        yield
        psum = p[:, 0:tq]
        for r in range(1, NSA_REP):
            psum = psum + p[:, r * tq:(r + 1) * tq]
        imp = _x3_dot(mt_ref[...], psum)
        j = lax.broadcasted_iota(jnp.int32, (n_sel, tq), 0)
        t = q0 + lax.broadcasted_iota(jnp.int32, (n_sel, tq), 1)
        tb = t // SEL_BLOCK
        forced = (j == 0) | (j == tb) | (j == tb - 1)
        val = jnp.where(j <= tb, imp + jnp.where(forced, FORCE_BONUS, 0.0), -1.0)
        yield
        vals = [val[g0:g0 + SUBLANE] for g0 in range(0, n_sel, SUBLANE)]
        cnts = [jnp.zeros((SUBLANE, tq), F32) for _ in vals]
        srow = lax.broadcasted_iota(jnp.int32, (SUBLANE, tq), 0)
        for i in range(n_sel):
            vi = jnp.broadcast_to(val[i:i + 1, :], (SUBLANE, tq))
            for g, vg in enumerate(vals):
                ge = lambda: jnp.where(vi >= vg, 1.0, 0.0)
                gt = lambda: jnp.where(vi > vg, 1.0, 0.0)
                if g * SUBLANE > i:
                    beat = ge()
                elif (g + 1) * SUBLANE <= i:
                    beat = gt()
                else:
                    beat = jnp.where(srow > i - g * SUBLANE, ge(), gt())
                cnts[g] = cnts[g] + beat
            if i % SUBLANE == SUBLANE - 1:
                yield
        cnt = jnp.concatenate(cnts, axis=0)
        sel_bias = jnp.where(cnt < float(SEL_TOPK), 0.0, NEG_INF).astype(BF16)
        sel_out["qr_sel"] = jnp.concatenate(
            [qr, tile4(sel_bias), jnp.zeros((K_AUG - HEAD_DIM - n_sel, nl), BF16)], axis=0)

    init = (jnp.full((1, nl), NEG_INF, F32), jnp.zeros((V_ROWS, nl), F32))
    a = (q0 + tq - 1) // tk

    def attend(n_back):
        c_wins, c_sels, o_cmps, qr_sels = [], [], [], []
        for g in groups:
            qr = heads_on_lanes(qrt_ref, g)
            win_box, sel_out = [init], {}
            _interleave(last_tiles(win_box, kw_ref, vwt_ref, g, qr, n_back, lambda d: d < WINDOW),
                        select_steps(g, qr, sel_out))
            sel_box = [init]
            _interleave(last_tiles(sel_box, ks_ref, vst_ref, g, sel_out["qr_sel"], n_back, None))
            c_wins.append(win_box[0])
            c_sels.append(sel_box[0])
            o_cmps.append(sel_out["o_cmp"])
            qr_sels.append(sel_out["qr_sel"])

        def old_tiles(kts):
            def body(carry):
                boxes = [[c] for c in carry]
                _interleave(*[update_steps(boxes[g], ks_ref, vst_ref, g, qr_sels[g], kts, [None] * len(kts))
                              for g in groups])
                return tuple(b[0] for b in boxes)
            return body

        if n_back == n_win:
            n_old = a - n_win
            c_sels = lax.fori_loop(0, n_old % 2, lambda kt, c: old_tiles([kt])(c), tuple(c_sels))
            c_sels = lax.fori_loop(0, n_old // 2,
                                   lambda i, c: old_tiles([n_old % 2 + 2 * i, n_old % 2 + 2 * i + 1])(c), c_sels)

        for g in groups:
            o_sel = c_sels[g][1][0:HEAD_DIM] * (1.0 / c_sels[g][1][HEAD_DIM:HEAD_DIM + 1])
            o_win = c_wins[g][1][0:HEAD_DIM] * (1.0 / c_wins[g][1][HEAD_DIM:HEAD_DIM + 1])
            for r in range(NSA_REP):
                ls = slice(r * tq, (r + 1) * tq)
                h = g * NSA_REP + r
                gate = lambda c: gate_ref[3 * h + c:3 * h + c + 1, :]
                o = gate(0) * o_cmps[g][:, ls] + gate(1) * o_sel[:, ls] + gate(2) * o_win[:, ls]
                rs = slice(h * HEAD_DIM, (h + 1) * HEAD_DIM)
                o_ref[rs, :] = (o * gn_ref[rs, :]).astype(BF16)

    def dispatch(n_back):
        if n_back == 0:
            attend(0)
        else:
            lax.cond(a >= n_back, lambda: attend(n_back), lambda: dispatch(n_back - 1))

    dispatch(n_win)


def _nsa(qt, qrt, kcmp, vcmpt, ksh, vst, kwh, vwt, gates, gnt, mt, *, batch, seq, tq):
    n = batch * seq
    nq = seq // tq
    ncp = kcmp.shape[2]
    qspec = pl.BlockSpec((NSA_WIDTH, tq), lambda b, i: (0, b * nq + i))
    kspec = lambda w: pl.BlockSpec((NSA_KV, seq, w), lambda b, i: (0, b, 0))
    vspec = pl.BlockSpec((seq // LANE, NSA_KV * V_ROWS, LANE), lambda b, i: (b, 0, 0))
    kern = functools.partial(_nsa_kernel, tq=tq, seq=seq)
    return pl.pallas_call(
        kern,
        grid=(batch, nq),
        in_specs=[qspec, qspec,
                  pl.BlockSpec((NSA_KV, 1, ncp, HEAD_DIM), lambda b, i: (0, b, 0, 0)),
                  pl.BlockSpec((NSA_KV, 1, HEAD_DIM, ncp), lambda b, i: (0, b, 0, 0)),
                  kspec(K_AUG), vspec, kspec(HEAD_DIM), vspec,
                  pl.BlockSpec((GATE_ROWS, tq), lambda b, i: (0, b * nq + i)),
                  qspec,
                  pl.BlockSpec(mt.shape, lambda b, i: (0, 0))],
        out_specs=qspec,
        out_shape=jax.ShapeDtypeStruct((NSA_WIDTH, n), BF16),
        compiler_params=pltpu.CompilerParams(
            dimension_semantics=("arbitrary", "arbitrary"),
            vmem_limit_bytes=VMEM_LIMIT),
        name="nsa",
    )(qt, qrt, kcmp, vcmpt, ksh, vst, kwh, vwt, gates, gnt, mt)


def _rwkv_kernel(p_ref, gr_ref, mu_ref, w0_ref, a0_ref, wab_ref, kk_ref, ka_ref, rk_ref,
                 gw_ref, gb_ref, ones_ref, tril_ref, o_ref, st_ref, carry_ref, *, tt):
    c = RWKV_CHUNK
    hd = HEAD_DIM
    nc = tt // c
    step = pl.program_id(1)

    @pl.when(step == 0)
    def _():
        st_ref[...] = jnp.zeros_like(st_ref)
        carry_ref[...] = jnp.zeros_like(carry_ref)

    p = p_ref[...]
    row = lax.broadcasted_iota(jnp.int32, p.shape, 0)
    prev = jnp.where(row == 0, carry_ref[...], pltpu.roll(p, 1, 0))
    carry_ref[...] = p[tt - 1:tt, :]
    ps = p + mu_ref[...] * (prev - p)
    r = ps[:, 0:RWKV_WIDTH]
    k = ps[:, RWKV_WIDTH:2 * RWKV_WIDTH]
    v = ps[:, 2 * RWKV_WIDTH:3 * RWKV_WIDTH]
    lora = ps[:, 3 * RWKV_WIDTH:]
    lane = lax.broadcasted_iota(jnp.int32, lora.shape, 1)
    feat = jnp.where(lane < DECAY_RANK, jnp.tanh(lora), lora).astype(BF16)
    up = _dot(feat, wab_ref[...])
    w = w0_ref[...] + up[:, 0:RWKV_WIDTH]
    lw = _sigmoid(w) * (-float(np.exp(-0.5)))
    a = _sigmoid(a0_ref[...] + up[:, RWKV_WIDTH:])
    ones_bd = ones_ref[...]
    kk = k * kk_ref[...]
    kkn = kk * lax.rsqrt(jnp.maximum(_head_sums(kk * kk, ones_bd), 1e-24))
    k2 = k * (1.0 + (a - 1.0) * ka_ref[...])
    alpha = -kkn
    beta = kkn * a
    bonus = _head_sums(r * k2 * rk_ref[...], ones_bd, terms=1) * v

    cum = _x3_dot(tril_ref[...], lw)
    cend = jnp.concatenate(
        [jnp.broadcast_to(cum[(ch + 1) * c - 1:(ch + 1) * c, :], (c, RWKV_WIDTH)) for ch in range(nc)], axis=0)
    e_neg = jnp.exp(-cum)
    pc = jnp.exp(cend)
    at = alpha * jnp.exp(cum - lw)
    bt = beta * e_neg
    kt = k2 * e_neg
    rt = r * jnp.exp(cum)
    bh = bt * pc
    kh = kt * pc

    npair = RWKV_WIDTH // LANE

    def pairs(x):
        return jnp.stack([x[ch * c:(ch + 1) * c, j * LANE:(j + 1) * LANE]
                          for ch in range(nc) for j in range(npair)], axis=0)

    at_p, rt_p, bt_p, kt_p, v_p, bh_p, kh_p = (pairs(t) for t in (at, rt, bt, kt, v, bh, kh))
    pc_p = jnp.stack([pc[ch * c:ch * c + 1, j * LANE:(j + 1) * LANE]
                      for ch in range(nc) for j in range(npair)], axis=0)
    lane_c = lax.broadcasted_iota(jnp.int32, (1, c, LANE), 2)
    row_c = lax.broadcasted_iota(jnp.int32, (1, c, LANE), 1)
    even_c = lane_c < hd
    col_c = jnp.where(even_c, lane_c, lane_c - hd)
    low_s = row_c > col_c
    low_i = row_c >= col_c
    lane_2c = lax.broadcasted_iota(jnp.int32, (1, 2 * c, LANE), 2)
    row_2c = lax.broadcasted_iota(jnp.int32, (1, 2 * c, LANE), 1)
    even_2c = lane_2c < hd
    on_bd = (row_2c < hd) == even_2c
    zero_c = jnp.zeros((1, c, LANE), BF16)

    def bd(x):
        xb = x.astype(BF16)
        return jnp.concatenate([jnp.where(even_c, xb, zero_c), jnp.where(even_c, zero_c, xb)], axis=1)

    def abd(x):
        xb = x.astype(BF16)
        return jnp.concatenate([jnp.where(even_c, zero_c, xb), jnp.where(even_c, xb, zero_c)], axis=1)

    la = jnp.concatenate([at_p, rt_p], axis=1).astype(BF16)
    zero_2c = jnp.zeros((1, 2 * c, LANE), BF16)
    r_e = _bmm_nt(jnp.where(even_2c, la, zero_2c), jnp.concatenate([bt_p, kt_p], axis=1).astype(BF16))
    r_o = _bmm_nt(jnp.where(even_2c, zero_2c, la), jnp.concatenate([kt_p, bt_p], axis=1).astype(BF16))
    nab = jnp.where(low_s, jnp.where(even_c, r_e[:, 0:c], r_o[:, 0:c]), 0.0)
    aak_sw = jnp.where(low_s, jnp.where(even_c, r_o[:, 0:c], r_e[:, 0:c]), 0.0).astype(BF16)
    arb = jnp.where(low_i, jnp.where(even_c, r_e[:, c:], r_o[:, c:]), 0.0).astype(BF16)
    ark_sw = jnp.where(low_i, jnp.where(even_c, r_o[:, c:], r_e[:, c:]), 0.0).astype(BF16)
    tinv = jnp.where(row_c == col_c, 1.0, 0.0) + nab
    npow = _bmm(nab.astype(BF16), bd(nab))
    n_dbl = 5
    for it in range(n_dbl):
        nbd = bd(npow)
        if it + 1 < n_dbl:
            res = _bmm(jnp.concatenate([tinv, npow], axis=1).astype(BF16), nbd)
            tinv = tinv + res[:, 0:c]
            npow = res[:, c:]
        else:
            tinv = tinv + _bmm(tinv.astype(BF16), nbd)
    av = _bmm(jnp.concatenate([aak_sw, ark_sw], axis=1), abd(v_p))
    tx = _bmm(tinv.astype(BF16), jnp.concatenate([bd(av[:, 0:c]), bd(at_p)], axis=2))
    u0, ta = tx[:, :, 0:LANE], tx[:, :, LANE:]
    ax = _bmm(arb, jnp.concatenate([bd(ta), bd(u0)], axis=2))
    rq = rt_p + ax[:, :, 0:LANE]
    y0 = ax[:, :, LANE:] + av[:, c:]
    v_b = v_p.astype(BF16)
    w_f = jnp.concatenate([jnp.concatenate([ta, u0], axis=2).astype(BF16),
                           jnp.concatenate([jnp.zeros_like(v_b), v_b], axis=2)], axis=1)
    gh = _bmm(jnp.concatenate([jnp.swapaxes(bh_p, 1, 2), jnp.swapaxes(kh_p, 1, 2)], axis=2).astype(BF16), w_f)
    g_bd = jnp.where(on_bd, gh[:, :, 0:LANE], 0.0) + jnp.where(row_2c == lane_2c, pc_p, 0.0)
    h_bd = jnp.where(on_bd, gh[:, :, LANE:], 0.0)
    lhs = jnp.concatenate([rq, g_bd], axis=1).astype(BF16)

    st = st_ref[...]
    ys = []
    for ch in range(nc):
        sl = slice(ch * npair, (ch + 1) * npair)
        res = _bmm(lhs[sl], st.astype(BF16))
        yc = res[:, 0:c, :] + y0[sl]
        st = res[:, c:, :] + h_bd[sl]
        ys.append(jnp.concatenate([yc[j] for j in range(npair)], axis=1))
    st_ref[...] = st
    y = jnp.concatenate(ys, axis=0) if nc > 1 else ys[0]

    inv_hd = 1.0 / hd
    mean = _head_sums(y, ones_bd, terms=1) * inv_hd
    ycen = y - mean
    var = _head_sums(ycen * ycen, ones_bd, terms=1) * inv_hd
    yn = ycen * lax.rsqrt(var + GN_EPS) * gw_ref[...] + gb_ref[...]
    o_ref[...] = ((yn + bonus) * gr_ref[...]).astype(BF16)


def _rwkv(rw, gr, mu, w0, a0, wab, kk, ka, rk, gw, gb, ones_bd, tril, *, batch, seq, tt):
    n = batch * seq
    ns = seq // tt
    row = lambda w: pl.BlockSpec((tt, w), lambda b, i: (b * ns + i, 0))
    full = lambda a: pl.BlockSpec(a.shape, lambda b, i: (0,) * a.ndim)
    kern = functools.partial(_rwkv_kernel, tt=tt)
    consts = (mu, w0, a0, wab, kk, ka, rk, gw, gb, ones_bd, tril)
    return pl.pallas_call(
        kern,
        grid=(batch, ns),
        in_specs=[row(RWKV_SHIFT_WIDTH), row(RWKV_WIDTH)] + [full(a) for a in consts],
        out_specs=row(RWKV_WIDTH),
        out_shape=jax.ShapeDtypeStruct((n, RWKV_WIDTH), BF16),
        scratch_shapes=[pltpu.VMEM((RWKV_WIDTH // LANE, LANE, LANE), F32),
                        pltpu.VMEM((1, RWKV_SHIFT_WIDTH), F32)],
        compiler_params=pltpu.CompilerParams(dimension_semantics=("arbitrary", "arbitrary"),
                                             vmem_limit_bytes=VMEM_LIMIT),
        name="rwkv",
    )(rw, gr, *consts)


def _out_kernel(x_ref, ont_ref, or_ref, wn_ref, wr_ref, g_ref, o_ref):
    h = x_ref[...] + _dot_tn(ont_ref[...], wn_ref[...]) + _dot(or_ref[...], wr_ref[...])
    ms = jnp.mean(h * h, axis=-1, keepdims=True)
    o_ref[...] = h * lax.rsqrt(ms + RMS_EPS) * g_ref[...]


def _out_proj(x2, o_nsa_t, o_rwkv, wn, wr, final_g, *, tm):
    n = x2.shape[0]
    row = lambda w: pl.BlockSpec((tm, w), lambda i: (i, 0))
    full = lambda a: pl.BlockSpec(a.shape, lambda i: (0,) * a.ndim)
    return pl.pallas_call(
        _out_kernel,
        grid=(n // tm,),
        in_specs=[row(D_MODEL), pl.BlockSpec((NSA_WIDTH, tm), lambda i: (0, i)), row(RWKV_WIDTH),
                  full(wn), full(wr), full(final_g)],
        out_specs=row(D_MODEL),
        out_shape=jax.ShapeDtypeStruct((n, D_MODEL), F32),
        compiler_params=pltpu.CompilerParams(dimension_semantics=("arbitrary",),
                                             vmem_limit_bytes=VMEM_LIMIT),
        name="out_proj",
    )(x2, o_nsa_t, o_rwkv, wn, wr, final_g)


def _rope_tables(seq):
    inv = ROPE_THETA ** (-np.arange(ROPE_HALF, dtype=np.float64) / ROPE_HALF)
    ang = np.arange(seq, dtype=np.float64)[:, None] * inv[None, :]
    cos, sin = np.cos(ang), np.sin(ang)
    ra = np.ones((seq, HEAD_DIM)); rm = np.zeros((seq, HEAD_DIM)); rp = np.zeros((seq, HEAD_DIM))
    ra[:, :ROPE_HALF] = cos; ra[:, ROPE_HALF:ROPE_DIM] = cos
    rm[:, :ROPE_HALF] = -sin
    rp[:, ROPE_HALF:ROPE_DIM] = sin
    rep = lambda t: jnp.asarray(np.tile(t, (1, LANE // HEAD_DIM)), F32)
    assert seq // SEL_BLOCK <= K_AUG - HEAD_DIM
    onehot = np.zeros((seq, K_AUG - HEAD_DIM))
    onehot[np.arange(seq), np.arange(seq) // SEL_BLOCK] = 1.0
    return (rep(ra), rep(rm), rep(rp), jnp.asarray(cos.T, F32), jnp.asarray(sin.T, F32),
            jnp.asarray(onehot, BF16))


def _cmp_to_sel_t(n_cmp_pad, n_sel):
    n_cmp = n_cmp_pad - 1
    c0 = np.arange(n_cmp)[:, None] * CMP_STRIDE
    s0 = np.arange(n_sel)[None, :] * SEL_BLOCK
    ov = np.clip(np.minimum(c0 + CMP_BLOCK, s0 + SEL_BLOCK) - np.maximum(c0, s0), 0, None) / CMP_BLOCK
    mt = np.zeros((n_sel, n_cmp_pad))
    mt[:, :n_cmp] = ov.T
    return jnp.asarray(mt, BF16)


def _prep_w_in(w_in):
    idx = np.cumsum(IN_SIZES)[:-1].tolist()
    q, kc, vc, ks, vs, kw, vw, gl, gn, rw, gr = jnp.split(w_in, idx, axis=1)
    w_t = jnp.concatenate([kc, vc, ks, kw, rw, gr], axis=1).T.astype(BF16)
    pad = jnp.zeros((D_MODEL, GATE_ROWS - gl.shape[1]), w_in.dtype)
    w_f = jnp.concatenate([q, vs, vw, gl, pad, gn], axis=1).T.astype(BF16)
    return w_t, w_f


def _layer(x2, norm_g, w_in, cmp_pos_k, cmp_w1_k, cmp_w2_k, cmp_pos_v, cmp_w1_v, cmp_w2_v,
           shift_mu, decay_w0, decay_up, iclr_a0, iclr_up, k_k, k_a, r_k, gn_w, gn_b, w_out,
           final_g, *, batch, seq):
    tm = 256
    tq = 256
    tt = 256
    assert WINDOW % KEY_TILE == 0 and seq % KEY_TILE == 0 and KEY_TILE % tq == 0
    nch = seq // CMP_STRIDE
    n_sel = seq // SEL_BLOCK
    rowv = lambda t: t.reshape(1, -1).astype(F32)

    w_t, w_f = _prep_w_in(w_in)
    (kch, vch, ksh, kwh, rw, gr, qt, qrt, vst, vwt, gates, gnt) = _in_proj(
        x2, rowv(norm_g), w_t, w_f, _rope_tables(seq), seq=seq, tm=2 * tm)

    chunks = lambda t: t.reshape(NSA_KV, batch, nch, CMP_STRIDE * HEAD_DIM)
    kcmp, vcmpt = _compress(chunks(kch), chunks(vch),
                            rowv(cmp_pos_k), cmp_w1_k.astype(BF16), cmp_w2_k.astype(BF16),
                            rowv(cmp_pos_v), cmp_w1_v.astype(BF16), cmp_w2_v.T.astype(BF16))

    o_nsa_t = _nsa(qt, qrt, kcmp, vcmpt, ksh, vst, kwh, vwt, gates, gnt,
                   _cmp_to_sel_t(nch, n_sel), batch=batch, seq=seq, tq=tq)

    z = jnp.zeros((DECAY_RANK, RWKV_WIDTH), F32)
    wab = jnp.concatenate([jnp.concatenate([decay_up, z], axis=1),
                           jnp.concatenate([z, iclr_up], axis=1)], axis=0).astype(BF16)
    hid = np.arange(2 * LANE) // HEAD_DIM
    ones_bd = jnp.asarray(hid[:, None] == hid[None, :], BF16)
    ti = np.arange(tt)
    tril = jnp.asarray((ti[:, None] >= ti[None, :]) & (ti[:, None] // RWKV_CHUNK == ti[None, :] // RWKV_CHUNK), BF16)
    o_rwkv = _rwkv(rw, gr, rowv(shift_mu), rowv(decay_w0), rowv(iclr_a0), wab, rowv(k_k), rowv(k_a),
                   rowv(r_k), rowv(gn_w), rowv(gn_b), ones_bd, tril, batch=batch, seq=seq, tt=tt)

    w_o = w_out.astype(BF16)
    return _out_proj(x2, o_nsa_t, o_rwkv, w_o[:NSA_WIDTH], w_o[NSA_WIDTH:], rowv(final_g), tm=4 * tm)


def kernel(x, norm_g, w_in, cmp_pos_k, cmp_w1_k, cmp_w2_k, cmp_pos_v, cmp_w1_v, cmp_w2_v, shift_mu, decay_w0, decay_up, iclr_a0, iclr_up, k_k, k_a, r_k, gn_w, gn_b, w_out, final_g):
    batch, seq, d = x.shape
    assert d == D_MODEL and norm_g.shape[0] == 1, "single-layer trunk"
    out = _layer(x.reshape(batch * seq, d), norm_g[0], w_in[0], cmp_pos_k[0], cmp_w1_k[0], cmp_w2_k[0],
                 cmp_pos_v[0], cmp_w1_v[0], cmp_w2_v[0], shift_mu[0], decay_w0[0], decay_up[0],
                 iclr_a0[0], iclr_up[0], k_k[0], k_a[0], r_k[0], gn_w[0], gn_b[0], w_out[0],
                 final_g, batch=batch, seq=seq)
    return out.reshape(batch, seq, d)
```

```python
import functools

import numpy as np
import jax
import jax.numpy as jnp
from jax import lax
from jax.experimental import pallas as pl
from jax.experimental.pallas import tpu as pltpu

F32 = jnp.float32
BF16 = jnp.bfloat16

D_MODEL = 1024
HEAD_DIM = 64
NSA_HEADS = 8
NSA_KV = 2
NSA_REP = NSA_HEADS // NSA_KV
RWKV_HEADS = 8
NSA_WIDTH = NSA_HEADS * HEAD_DIM
RWKV_WIDTH = RWKV_HEADS * HEAD_DIM
KV_WIDTH = NSA_KV * HEAD_DIM
ROPE_DIM = HEAD_DIM // 4
ROPE_HALF = ROPE_DIM // 2
ROPE_THETA = 500000.0
CMP_BLOCK = 32
CMP_STRIDE = 16
CMP_HIDDEN = 256
SEL_BLOCK = 64
SEL_TOPK = 8
WINDOW = 512
DECAY_RANK = 64
ICLR_RANK = 64
RWKV_SHIFT_WIDTH = 3 * RWKV_WIDTH + DECAY_RANK + ICLR_RANK
IN_SIZES = (NSA_WIDTH, KV_WIDTH, KV_WIDTH, KV_WIDTH, KV_WIDTH, KV_WIDTH, KV_WIDTH,
            3 * NSA_HEADS, NSA_WIDTH, RWKV_SHIFT_WIDTH, RWKV_WIDTH)
SCALE = HEAD_DIM ** -0.5
RMS_EPS = 1e-6
GN_EPS = 64e-5
NEG_INF = -1e30
FORCE_BONUS = 1e3

LANE = 128
SUBLANE = 8
BF16_SUBLANE = 2 * SUBLANE
GATE_ROWS = 2 * BF16_SUBLANE

T_KV = 0
T_RW = T_KV + 4 * KV_WIDTH
T_GR = T_RW + RWKV_SHIFT_WIDTH
T_END = T_GR + RWKV_WIDTH
R_Q = 0
R_VS = R_Q + NSA_WIDTH
R_VW = R_VS + KV_WIDTH
R_GL = R_VW + KV_WIDTH
R_GN = R_GL + GATE_ROWS
R_END = R_GN + NSA_WIDTH

LOG2E = float(np.log2(np.e))
K_AUG = LANE
V_ROWS = HEAD_DIM + BF16_SUBLANE
KEY_TILE = 256
RWKV_CHUNK = 64
VMEM_LIMIT = 48 * 1024 * 1024


def _dot(a, b):
    return jnp.dot(a, b, preferred_element_type=F32)


def _dot_nt(a, b):
    return lax.dot_general(a, b, (((1,), (1,)), ((), ())), preferred_element_type=F32)


def _dot_tn(a, b):
    return lax.dot_general(a, b, (((0,), (0,)), ((), ())), preferred_element_type=F32)


def _bmm(a, b):
    return lax.dot_general(a, b, (((2,), (1,)), ((0,), (0,))), preferred_element_type=F32)


def _bmm_nt(a, b):
    return lax.dot_general(a, b, (((2,), (2,)), ((0,), (0,))), preferred_element_type=F32)


def _split3(x):
    hi = x.astype(BF16)
    r1 = x - hi.astype(F32)
    mid = r1.astype(BF16)
    lo = (r1 - mid.astype(F32)).astype(BF16)
    return hi, mid, lo


def _head_sums(x, ones_blk, terms=2):
    w = ones_blk.shape[0]
    parts = [x.astype(BF16)]
    if terms == 2:
        parts.append((x - parts[0].astype(F32)).astype(BF16))
    cols = []
    for i in range(0, x.shape[1], w):
        acc = _dot(parts[0][:, i:i + w], ones_blk)
        for t in parts[1:]:
            acc = acc + _dot(t[:, i:i + w], ones_blk)
        cols.append(acc)
    return jnp.concatenate(cols, axis=1)


def _x3_dot(w_bf16, x):
    hi, mid, lo = _split3(x)
    return _dot(w_bf16, hi) + _dot(w_bf16, mid) + _dot(w_bf16, lo)


def _sigmoid(x):
    return 0.5 * jnp.tanh(0.5 * x) + 0.5


def _interleave(*gens):
    live = list(gens)
    while live:
        for g in list(live):
            try:
                next(g)
            except StopIteration:
                live.remove(g)


def _rope128(t, ra, rm, rp):
    return t * ra + pltpu.roll(t, LANE - ROPE_HALF, 1) * rm + pltpu.roll(t, ROPE_HALF, 1) * rp


def _in_proj_kernel(x_ref, g_ref, wt_ref, wf_ref, ra_ref, rm_ref, rp_ref, cos_ref, sin_ref, oh_ref,
                    kc_ref, vc_ref, ks_ref, kw_ref, rw_ref, gr_ref,
                    qt_ref, qrt_ref, vst_ref, vwt_ref, gate_ref, gn_ref, cmp_scr):
    x = x_ref[...]
    ms = jnp.mean(x * x, axis=-1, keepdims=True)
    y = (x * lax.rsqrt(ms + RMS_EPS) * g_ref[...]).astype(BF16)
    tm = x.shape[0]

    kv = _dot_nt(y, wt_ref[T_KV:T_RW, :])
    for i, ref in enumerate((kc_ref, vc_ref)):
        cmp_scr[i] = kv[:, i * KV_WIDTH:(i + 1) * KV_WIDTH]
        for tau in range(CMP_STRIDE):
            piece = cmp_scr[i, pl.ds(tau, tm // CMP_STRIDE, stride=CMP_STRIDE), :].astype(BF16)
            for g in range(NSA_KV):
                ref[g, :, tau * HEAD_DIM:(tau + 1) * HEAD_DIM] = piece[:, g * HEAD_DIM:(g + 1) * HEAD_DIM]
    ra, rm, rp = ra_ref[...], rm_ref[...], rp_ref[...]
    for i, ref in ((2, ks_ref), (3, kw_ref)):
        t = _rope128(kv[:, i * LANE:(i + 1) * LANE], ra, rm, rp)
        for g in range(NSA_KV):
            tg = t[:, g * HEAD_DIM:(g + 1) * HEAD_DIM].astype(BF16)
            if ref is ks_ref:
                ref[g] = jnp.concatenate([tg, oh_ref[...]], axis=1)
            else:
                ref[g] = tg
    rw_ref[...] = _dot_nt(y, wt_ref[T_RW:T_GR, :])
    gr = _dot_nt(y, wt_ref[T_GR:T_END, :])
    gr_ref[...] = gr * _sigmoid(gr)

    def proj_t(r0, r1):
        return _dot_nt(wf_ref[r0:r1, :], y)

    qt = proj_t(R_Q, R_VS) * (SCALE * LOG2E)
    cos, sin = cos_ref[...], sin_ref[...]
    qt_ref[...] = qt.astype(BF16)
    for h in range(NSA_HEADS):
        r0 = h * HEAD_DIM
        t1 = qt[r0:r0 + ROPE_HALF]
        t2 = qt[r0 + ROPE_HALF:r0 + ROPE_DIM]
        qrt_ref[r0:r0 + ROPE_DIM, :] = jnp.concatenate(
            [t1 * cos - t2 * sin, t2 * cos + t1 * sin], axis=0).astype(BF16)
        qrt_ref[r0 + ROPE_DIM:r0 + HEAD_DIM, :] = qt[r0 + ROPE_DIM:r0 + HEAD_DIM].astype(BF16)
    vt = proj_t(R_VS, R_GL).astype(BF16)
    ones = jnp.ones((V_ROWS - HEAD_DIM, LANE), BF16)
    for j in range(vt.shape[1] // LANE):
        for i, ref in enumerate((vst_ref, vwt_ref)):
            for g in range(NSA_KV):
                r0 = i * KV_WIDTH + g * HEAD_DIM
                ref[j, g * V_ROWS:(g + 1) * V_ROWS, :] = jnp.concatenate(
                    [vt[r0:r0 + HEAD_DIM, j * LANE:(j + 1) * LANE], ones], axis=0)
    gate_ref[...] = _sigmoid(proj_t(R_GL, R_GN))
    gn = proj_t(R_GN, R_END)
    gn_ref[...] = gn * _sigmoid(gn)


def _in_proj(x2, norm_g, w_t, w_f, tabs, *, seq, tm):
    n = x2.shape[0]
    spt = seq // tm
    ra, rm, rp, cos, sin, onehot = tabs
    hm = lambda w: jax.ShapeDtypeStruct((NSA_KV, n, w), BF16)
    hspec = lambda w: pl.BlockSpec((NSA_KV, tm, w), lambda i: (0, i, 0))
    row = lambda w: pl.BlockSpec((tm, w), lambda i: (i, 0))
    col = lambda r: pl.BlockSpec((r, tm), lambda i: (0, i))
    full = lambda a: pl.BlockSpec(a.shape, lambda i: (0,) * a.ndim)
    tab = lambda w: pl.BlockSpec((tm, w), lambda i: (i % spt, 0))
    tabt = pl.BlockSpec((ROPE_HALF, tm), lambda i: (0, i % spt))
    vtile = pl.BlockSpec((tm // LANE, NSA_KV * V_ROWS, LANE), lambda i: (i, 0, 0))
    vsd = jax.ShapeDtypeStruct((n // LANE, NSA_KV * V_ROWS, LANE), BF16)
    cw = CMP_STRIDE * HEAD_DIM
    cspec = pl.BlockSpec((NSA_KV, tm // CMP_STRIDE, cw), lambda i: (0, i, 0))
    csd = jax.ShapeDtypeStruct((NSA_KV, n // CMP_STRIDE, cw), BF16)
    return pl.pallas_call(
        _in_proj_kernel,
        grid=(n // tm,),
        in_specs=[row(D_MODEL), full(norm_g), full(w_t), full(w_f), tab(LANE), tab(LANE), tab(LANE),
                  tabt, tabt, tab(K_AUG - HEAD_DIM)],
        out_specs=[cspec, cspec, hspec(K_AUG), hspec(HEAD_DIM), row(RWKV_SHIFT_WIDTH), row(RWKV_WIDTH),
                   col(NSA_WIDTH), col(NSA_WIDTH), vtile, vtile,
                   col(GATE_ROWS), col(NSA_WIDTH)],
        out_shape=[csd, csd, hm(K_AUG), hm(HEAD_DIM),
                   jax.ShapeDtypeStruct((n, RWKV_SHIFT_WIDTH), F32),
                   jax.ShapeDtypeStruct((n, RWKV_WIDTH), F32),
                   jax.ShapeDtypeStruct((NSA_WIDTH, n), BF16),
                   jax.ShapeDtypeStruct((NSA_WIDTH, n), BF16),
                   vsd, vsd,
                   jax.ShapeDtypeStruct((GATE_ROWS, n), F32),
                   jax.ShapeDtypeStruct((NSA_WIDTH, n), F32)],
        scratch_shapes=[pltpu.VMEM((2, tm, KV_WIDTH), F32)],
        compiler_params=pltpu.CompilerParams(dimension_semantics=("arbitrary",),
                                             vmem_limit_bytes=VMEM_LIMIT),
        name="in_proj",
    )(x2, norm_g, w_t, w_f, ra, rm, rp, cos, sin, onehot)


def _compress_kernel(kc_ref, vc_ref, pk_ref, w1k_ref, w2k_ref, pv_ref, w1v_ref, w2vt_ref,
                     ko_ref, vo_ref):
    half = CMP_STRIDE * HEAD_DIM
    ng, _, nch, _ = kc_ref.shape

    def hidden(c_ref, pos_ref, w1_ref):
        c = c_ref[:, 0].reshape(ng * nch, half)
        pos = jnp.broadcast_to(pos_ref[...], (BF16_SUBLANE, 2 * half)).astype(BF16)
        za = _dot(jnp.concatenate([c, pos[:, 0:half]], axis=0), w1_ref[0:half, :])
        zb = _dot(jnp.concatenate([c, pos[:, half:]], axis=0), w1_ref[half:2 * half, :])
        pv = za[ng * nch:ng * nch + 1] + zb[ng * nch:ng * nch + 1]
        hid = jnp.concatenate(
            [za[g * nch:(g + 1) * nch] + pltpu.roll(zb[g * nch:(g + 1) * nch], nch - 1, 0) for g in range(ng)],
            axis=0) + pv
        return (hid * _sigmoid(hid)).astype(BF16)

    ko = _dot(hidden(kc_ref, pk_ref, w1k_ref), w2k_ref[...]).astype(BF16)
    hv = hidden(vc_ref, pv_ref, w1v_ref)
    for g in range(ng):
        ko_ref[g, 0] = ko[g * nch:(g + 1) * nch]
        vo_ref[g, 0] = _dot_nt(w2vt_ref[...], hv[g * nch:(g + 1) * nch]).astype(BF16)


def _compress(kc_r, vc_r, pk, w1k, w2k, pv, w1v, w2vt):
    g, b, nch, width = kc_r.shape
    blk = pl.BlockSpec((g, 1, nch, width), lambda j: (0, j, 0, 0))
    full = lambda a: pl.BlockSpec(a.shape, lambda j: (0,) * a.ndim)
    return pl.pallas_call(
        _compress_kernel,
        grid=(b,),
        in_specs=[blk, blk, full(pk), full(w1k), full(w2k), full(pv), full(w1v), full(w2vt)],
        out_specs=[pl.BlockSpec((g, 1, nch, HEAD_DIM), lambda j: (0, j, 0, 0)),
                   pl.BlockSpec((g, 1, HEAD_DIM, nch), lambda j: (0, j, 0, 0))],
        out_shape=[jax.ShapeDtypeStruct((g, b, nch, HEAD_DIM), BF16),
                   jax.ShapeDtypeStruct((g, b, HEAD_DIM, nch), BF16)],
        compiler_params=pltpu.CompilerParams(dimension_semantics=("arbitrary",),
                                             vmem_limit_bytes=VMEM_LIMIT),
        name="compress",
    )(kc_r, vc_r, pk, w1k, w2k, pv, w1v, w2vt)


def _nsa_kernel(qt_ref, qrt_ref, kc_ref, vct_ref, ks_ref, vst_ref, kw_ref, vwt_ref,
                gate_ref, gn_ref, mt_ref, o_ref, *, tq, seq):
    tk = KEY_TILE
    n_win = WINDOW // tk
    qi = pl.program_id(1)
    q0 = qi * tq
    nl = NSA_REP * tq
    n_sel = seq // SEL_BLOCK
    ncp = kc_ref.shape[2]
    groups = range(NSA_KV)

    def heads_on_lanes(ref, g):
        rows = [(g * NSA_REP + r) * HEAD_DIM for r in range(NSA_REP)]
        return jnp.concatenate([ref[r0:r0 + HEAD_DIM, :] for r0 in rows], axis=1)

    def tile4(a):
        return jnp.concatenate([a] * NSA_REP, axis=1)

    k_s = lax.broadcasted_iota(jnp.int32, (tk, tq), 0)
    t_l = q0 + lax.broadcasted_iota(jnp.int32, (tk, tq), 1)

    def update_steps(box, k_ref, vt_ref, g, q_op, kts, keeps):
        m_i, acc = box[0]

        def scores(i):
            sc = _dot(k_ref[g, pl.ds(pl.multiple_of(kts[i] * tk, tk), tk), :], q_op)
            if keeps[i] is None:
                return sc
            return jnp.concatenate([jnp.where(keeps[i], sc[:, r * tq:(r + 1) * tq], NEG_INF)
                                    for r in range(NSA_REP)], axis=1)

        sc_next = scores(0)
        yield
        for i, kt in enumerate(kts):
            sc = sc_next
            if i + 1 < len(kts):
                sc_next = scores(i + 1)
            m_n = jnp.maximum(m_i, jnp.max(sc, axis=0, keepdims=True))
            pe = jnp.exp2(sc - m_n).astype(BF16)
            yield
            vtb = jnp.concatenate([vt_ref[kt * (tk // LANE) + jj, g * V_ROWS:(g + 1) * V_ROWS, :]
                                   for jj in range(tk // LANE)], axis=1)
            acc = jnp.exp2(m_i - m_n) * acc + _dot(vtb, pe)
            m_i = m_n
            yield
        box[0] = (m_i, acc)

    def last_tiles(box, k_ref, vt_ref, g, q_op, n_back, low_keep):
        kts, keeps = [], []
        for back in range(n_back + 1):
            d = t_l - ((a - back) * tk + k_s)
            kts.append(a - back)
            if back == 0:
                keeps.append(d >= 0)
            elif back == n_win and low_keep is not None:
                keeps.append(low_keep(d))
            else:
                keeps.append(None)
        return update_steps(box, k_ref, vt_ref, g, q_op, kts, keeps)

    def select_steps(g, qr, sel_out, all_selected):
        s = _dot(kc_ref[g, 0], heads_on_lanes(qt_ref, g))
        t_c = q0 + lax.broadcasted_iota(jnp.int32, (ncp, tq), 1)
        c_c = lax.broadcasted_iota(jnp.int32, (ncp, tq), 0)
        cmask = tile4((c_c * CMP_STRIDE + (CMP_BLOCK - 1)) <= t_c)
        yield
        s = jnp.where(cmask, s, NEG_INF)
        m = jnp.maximum(jnp.max(s, axis=0, keepdims=True), 0.5 * NEG_INF)
        e = jnp.exp2(s - m)
        den = jnp.sum(e, axis=0, keepdims=True)
        p = e * (1.0 / jnp.where(den > 0.0, den, 1.0))
        sel_out["o_cmp"] = _dot(vct_ref[g, 0], p.astype(BF16))
        yield
        if all_selected:
            sel_out["qr_sel"] = jnp.concatenate([qr, jnp.zeros((K_AUG - HEAD_DIM, nl), BF16)], axis=0)
            return
        psum = p[:, 0:tq]
        for r in range(1, NSA_REP):
            psum = psum + p[:, r * tq:(r + 1) * tq]
        imp = _x3_dot(mt_ref[...], psum)
        j = lax.broadcasted_iota(jnp.int32, (n_sel, tq), 0)
        t = q0 + lax.broadcasted_iota(jnp.int32, (n_sel, tq), 1)
        tb = t // SEL_BLOCK
        forced = (j == 0) | (j == tb) | (j == tb - 1)
        val = jnp.where(j <= tb, imp + jnp.where(forced, FORCE_BONUS, 0.0), -1.0)
        yield
        vals = [val[g0:g0 + SUBLANE] for g0 in range(0, n_sel, SUBLANE)]
        cnts = [jnp.zeros((SUBLANE, tq), F32) for _ in vals]
        srow = lax.broadcasted_iota(jnp.int32, (SUBLANE, tq), 0)
        for i in range(n_sel):
            vi = jnp.broadcast_to(val[i:i + 1, :], (SUBLANE, tq))
            for g, vg in enumerate(vals):
                ge = lambda: jnp.where(vi >= vg, 1.0, 0.0)
                gt = lambda: jnp.where(vi > vg, 1.0, 0.0)
                if g * SUBLANE > i:
                    beat = ge()
                elif (g + 1) * SUBLANE <= i:
                    beat = gt()
                else:
                    beat = jnp.where(srow > i - g * SUBLANE, ge(), gt())
                cnts[g] = cnts[g] + beat
            if i % SUBLANE == SUBLANE - 1:
                yield
        cnt = jnp.concatenate(cnts, axis=0)
        sel_bias = jnp.where(cnt < float(SEL_TOPK), 0.0, NEG_INF).astype(BF16)
        sel_out["qr_sel"] = jnp.concatenate(
            [qr, tile4(sel_bias), jnp.zeros((K_AUG - HEAD_DIM - n_sel, nl), BF16)], axis=0)

    init = (jnp.full((1, nl), NEG_INF, F32), jnp.zeros((V_ROWS, nl), F32))
    a = (q0 + tq - 1) // tk

    def attend(n_back):
        win_boxes, sel_boxes, sel_outs = [[init] for _ in groups], [[init] for _ in groups], [{} for _ in groups]
        all_selected = n_back < n_win and (n_back + 1) * tk <= SEL_TOPK * SEL_BLOCK
        for g in groups:
            qr = heads_on_lanes(qrt_ref, g)
            _interleave(last_tiles(win_boxes[g], kw_ref, vwt_ref, g, qr, n_back, lambda d: d < WINDOW),
                        select_steps(g, qr, sel_outs[g], all_selected))
        _interleave(*[last_tiles(sel_boxes[g], ks_ref, vst_ref, g, sel_outs[g]["qr_sel"], n_back, None)
                      for g in groups])
        c_wins = [b[0] for b in win_boxes]
        c_sels = [b[0] for b in sel_boxes]
        o_cmps = [o["o_cmp"] for o in sel_outs]
        qr_sels = [o["qr_sel"] for o in sel_outs]

        def old_tiles(kts):
            def body(carry):
                boxes = [[c] for c in carry]
                _interleave(*[update_steps(boxes[g], ks_ref, vst_ref, g, qr_sels[g], kts, [None] * len(kts))
                              for g in groups])
                return tuple(b[0] for b in boxes)
            return body

        if n_back == n_win:
            n_old = a - n_win
            c_sels = lax.fori_loop(0, n_old % 2, lambda kt, c: old_tiles([kt])(c), tuple(c_sels))
            c_sels = lax.fori_loop(0, n_old // 2,
                                   lambda i, c: old_tiles([n_old % 2 + 2 * i, n_old % 2 + 2 * i + 1])(c), c_sels)

        for g in groups:
            o_sel = c_sels[g][1][0:HEAD_DIM] * (1.0 / c_sels[g][1][HEAD_DIM:HEAD_DIM + 1])
            o_win = c_wins[g][1][0:HEAD_DIM] * (1.0 / c_wins[g][1][HEAD_DIM:HEAD_DIM + 1])
            for r in range(NSA_REP):
                ls = slice(r * tq, (r + 1) * tq)
                h = g * NSA_REP + r
                gate = lambda c: gate_ref[3 * h + c:3 * h + c + 1, :]
                o = gate(0) * o_cmps[g][:, ls] + gate(1) * o_sel[:, ls] + gate(2) * o_win[:, ls]
                rs = slice(h * HEAD_DIM, (h + 1) * HEAD_DIM)
                o_ref[rs, :] = (o * gn_ref[rs, :]).astype(BF16)

    def dispatch(n_back):
        if n_back == 0:
            attend(0)
        else:
            lax.cond(a >= n_back, lambda: attend(n_back), lambda: dispatch(n_back - 1))

    dispatch(n_win)


def _nsa(qt, qrt, kcmp, vcmpt, ksh, vst, kwh, vwt, gates, gnt, mt, *, batch, seq, tq):
    n = batch * seq
    nq = seq // tq
    ncp = kcmp.shape[2]
    qspec = pl.BlockSpec((NSA_WIDTH, tq), lambda b, i: (0, b * nq + i))
    kspec = lambda w: pl.BlockSpec((NSA_KV, seq, w), lambda b, i: (0, b, 0))
    vspec = pl.BlockSpec((seq // LANE, NSA_KV * V_ROWS, LANE), lambda b, i: (b, 0, 0))
    kern = functools.partial(_nsa_kernel, tq=tq, seq=seq)
    return pl.pallas_call(
        kern,
        grid=(batch, nq),
        in_specs=[qspec, qspec,
                  pl.BlockSpec((NSA_KV, 1, ncp, HEAD_DIM), lambda b, i: (0, b, 0, 0)),
                  pl.BlockSpec((NSA_KV, 1, HEAD_DIM, ncp), lambda b, i: (0, b, 0, 0)),
                  kspec(K_AUG), vspec, kspec(HEAD_DIM), vspec,
                  pl.BlockSpec((GATE_ROWS, tq), lambda b, i: (0, b * nq + i)),
                  qspec,
                  pl.BlockSpec(mt.shape, lambda b, i: (0, 0))],
        out_specs=qspec,
        out_shape=jax.ShapeDtypeStruct((NSA_WIDTH, n), BF16),
        compiler_params=pltpu.CompilerParams(
            dimension_semantics=("arbitrary", "arbitrary"),
            vmem_limit_bytes=VMEM_LIMIT),
        name="nsa",
    )(qt, qrt, kcmp, vcmpt, ksh, vst, kwh, vwt, gates, gnt, mt)


def _rwkv_kernel(p_ref, gr_ref, mu_ref, w0_ref, a0_ref, wab_ref, kk_ref, ka_ref, rk_ref,
                 gw_ref, gb_ref, ones_ref, tril_ref, o_ref, st_ref, carry_ref, *, tt):
    c = RWKV_CHUNK
    hd = HEAD_DIM
    nc = tt // c
    step = pl.program_id(1)

    @pl.when(step == 0)
    def _():
        st_ref[...] = jnp.zeros_like(st_ref)
        carry_ref[...] = jnp.zeros_like(carry_ref)

    p = p_ref[...]
    row = lax.broadcasted_iota(jnp.int32, p.shape, 0)
    prev = jnp.where(row == 0, carry_ref[...], pltpu.roll(p, 1, 0))
    carry_ref[...] = p[tt - 1:tt, :]
    ps = p + mu_ref[...] * (prev - p)
    r = ps[:, 0:RWKV_WIDTH]
    k = ps[:, RWKV_WIDTH:2 * RWKV_WIDTH]
    v = ps[:, 2 * RWKV_WIDTH:3 * RWKV_WIDTH]
    lora = ps[:, 3 * RWKV_WIDTH:]
    lane = lax.broadcasted_iota(jnp.int32, lora.shape, 1)
    feat = jnp.where(lane < DECAY_RANK, jnp.tanh(lora), lora).astype(BF16)
    up = _dot(feat, wab_ref[...])
    w = w0_ref[...] + up[:, 0:RWKV_WIDTH]
    lw = _sigmoid(w) * (-float(np.exp(-0.5)))
    a = _sigmoid(a0_ref[...] + up[:, RWKV_WIDTH:])
    ones_bd = ones_ref[...]
    kk = k * kk_ref[...]
    kkn = kk * lax.rsqrt(jnp.maximum(_head_sums(kk * kk, ones_bd), 1e-24))
    k2 = k * (1.0 + (a - 1.0) * ka_ref[...])
    alpha = -kkn
    beta = kkn * a
    bonus = _head_sums(r * k2 * rk_ref[...], ones_bd, terms=1) * v

    cum = _x3_dot(tril_ref[...], lw)
    cend = jnp.concatenate(
        [jnp.broadcast_to(cum[(ch + 1) * c - 1:(ch + 1) * c, :], (c, RWKV_WIDTH)) for ch in range(nc)], axis=0)
    e_neg = jnp.exp(-cum)
    pc = jnp.exp(cend)
    at = alpha * jnp.exp(cum - lw)
    bt = beta * e_neg
    kt = k2 * e_neg
    rt = r * jnp.exp(cum)
    bh = bt * pc
    kh = kt * pc

    npair = RWKV_WIDTH // LANE

    def pairs(x):
        return jnp.stack([x[ch * c:(ch + 1) * c, j * LANE:(j + 1) * LANE]
                          for ch in range(nc) for j in range(npair)], axis=0)

    at_p, rt_p, bt_p, kt_p, v_p, bh_p, kh_p = (pairs(t) for t in (at, rt, bt, kt, v, bh, kh))
    pc_p = jnp.stack([pc[ch * c:ch * c + 1, j * LANE:(j + 1) * LANE]
                      for ch in range(nc) for j in range(npair)], axis=0)
    lane_c = lax.broadcasted_iota(jnp.int32, (1, c, LANE), 2)
    row_c = lax.broadcasted_iota(jnp.int32, (1, c, LANE), 1)
    even_c = lane_c < hd
    col_c = jnp.where(even_c, lane_c, lane_c - hd)
    low_s = row_c > col_c
    low_i = row_c >= col_c
    lane_2c = lax.broadcasted_iota(jnp.int32, (1, 2 * c, LANE), 2)
    row_2c = lax.broadcasted_iota(jnp.int32, (1, 2 * c, LANE), 1)
    even_2c = lane_2c < hd
    on_bd = (row_2c < hd) == even_2c
    zero_c = jnp.zeros((1, c, LANE), BF16)

    def bd(x):
        xb = x.astype(BF16)
        return jnp.concatenate([jnp.where(even_c, xb, zero_c), jnp.where(even_c, zero_c, xb)], axis=1)

    def abd(x):
        xb = x.astype(BF16)
        return jnp.concatenate([jnp.where(even_c, zero_c, xb), jnp.where(even_c, xb, zero_c)], axis=1)

    la = jnp.concatenate([at_p, rt_p], axis=1).astype(BF16)
    zero_2c = jnp.zeros((1, 2 * c, LANE), BF16)
    r_e = _bmm_nt(jnp.where(even_2c, la, zero_2c), jnp.concatenate([bt_p, kt_p], axis=1).astype(BF16))
    r_o = _bmm_nt(jnp.where(even_2c, zero_2c, la), jnp.concatenate([kt_p, bt_p], axis=1).astype(BF16))
    nab = jnp.where(low_s, jnp.where(even_c, r_e[:, 0:c], r_o[:, 0:c]), 0.0)
    aak_sw = jnp.where(low_s, jnp.where(even_c, r_o[:, 0:c], r_e[:, 0:c]), 0.0).astype(BF16)
    arb = jnp.where(low_i, jnp.where(even_c, r_e[:, c:], r_o[:, c:]), 0.0).astype(BF16)
    ark_sw = jnp.where(low_i, jnp.where(even_c, r_o[:, c:], r_e[:, c:]), 0.0).astype(BF16)
    tinv = jnp.where(row_c == col_c, 1.0, 0.0) + nab
    npow = _bmm(nab.astype(BF16), bd(nab))
    n_dbl = 5
    for it in range(n_dbl):
        nbd = bd(npow)
        if it + 1 < n_dbl:
            res = _bmm(jnp.concatenate([tinv, npow], axis=1).astype(BF16), nbd)
            tinv = tinv + res[:, 0:c]
            npow = res[:, c:]
        else:
            tinv = tinv + _bmm(tinv.astype(BF16), nbd)
    av = _bmm(jnp.concatenate([aak_sw, ark_sw], axis=1), abd(v_p))
    tx = _bmm(tinv.astype(BF16), jnp.concatenate([bd(av[:, 0:c]), bd(at_p)], axis=2))
    u0, ta = tx[:, :, 0:LANE], tx[:, :, LANE:]
    ax = _bmm(arb, jnp.concatenate([bd(ta), bd(u0)], axis=2))
    rq = rt_p + ax[:, :, 0:LANE]
    y0 = ax[:, :, LANE:] + av[:, c:]
    v_b = v_p.astype(BF16)
    w_f = jnp.concatenate([jnp.concatenate([ta, u0], axis=2).astype(BF16),
                           jnp.concatenate([jnp.zeros_like(v_b), v_b], axis=2)], axis=1)
    gh = _bmm(jnp.concatenate([jnp.swapaxes(bh_p, 1, 2), jnp.swapaxes(kh_p, 1, 2)], axis=2).astype(BF16), w_f)
    g_bd = jnp.where(on_bd, gh[:, :, 0:LANE], 0.0) + jnp.where(row_2c == lane_2c, pc_p, 0.0)
    h_bd = jnp.where(on_bd, gh[:, :, LANE:], 0.0)
    lhs = jnp.concatenate([rq, g_bd], axis=1).astype(BF16)

    st = st_ref[...]
    ys = []
    for ch in range(nc):
        sl = slice(ch * npair, (ch + 1) * npair)
        res = _bmm(lhs[sl], st.astype(BF16))
        yc = res[:, 0:c, :] + y0[sl]
        st = res[:, c:, :] + h_bd[sl]
        ys.append(jnp.concatenate([yc[j] for j in range(npair)], axis=1))
    st_ref[...] = st
    y = jnp.concatenate(ys, axis=0) if nc > 1 else ys[0]

    inv_hd = 1.0 / hd
    mean = _head_sums(y, ones_bd, terms=1) * inv_hd
    ycen = y - mean
    var = _head_sums(ycen * ycen, ones_bd, terms=1) * inv_hd
    yn = ycen * lax.rsqrt(var + GN_EPS) * gw_ref[...] + gb_ref[...]
    o_ref[...] = ((yn + bonus) * gr_ref[...]).astype(BF16)


def _rwkv(rw, gr, mu, w0, a0, wab, kk, ka, rk, gw, gb, ones_bd, tril, *, batch, seq, tt):
    n = batch * seq
    ns = seq // tt
    row = lambda w: pl.BlockSpec((tt, w), lambda b, i: (b * ns + i, 0))
    full = lambda a: pl.BlockSpec(a.shape, lambda b, i: (0,) * a.ndim)
    kern = functools.partial(_rwkv_kernel, tt=tt)
    consts = (mu, w0, a0, wab, kk, ka, rk, gw, gb, ones_bd, tril)
    return pl.pallas_call(
        kern,
        grid=(batch, ns),
        in_specs=[row(RWKV_SHIFT_WIDTH), row(RWKV_WIDTH)] + [full(a) for a in consts],
        out_specs=row(RWKV_WIDTH),
        out_shape=jax.ShapeDtypeStruct((n, RWKV_WIDTH), BF16),
        scratch_shapes=[pltpu.VMEM((RWKV_WIDTH // LANE, LANE, LANE), F32),
                        pltpu.VMEM((1, RWKV_SHIFT_WIDTH), F32)],
        compiler_params=pltpu.CompilerParams(dimension_semantics=("arbitrary", "arbitrary"),
                                             vmem_limit_bytes=VMEM_LIMIT),
        name="rwkv",
    )(rw, gr, *consts)


def _out_kernel(x_ref, ont_ref, or_ref, wn_ref, wr_ref, g_ref, o_ref):
    h = x_ref[...] + _dot_tn(ont_ref[...], wn_ref[...]) + _dot(or_ref[...], wr_ref[...])
    ms = jnp.mean(h * h, axis=-1, keepdims=True)
    o_ref[...] = h * lax.rsqrt(ms + RMS_EPS) * g_ref[...]


def _out_proj(x2, o_nsa_t, o_rwkv, wn, wr, final_g, *, tm):
    n = x2.shape[0]
    row = lambda w: pl.BlockSpec((tm, w), lambda i: (i, 0))
    full = lambda a: pl.BlockSpec(a.shape, lambda i: (0,) * a.ndim)
    return pl.pallas_call(
        _out_kernel,
        grid=(n // tm,),
        in_specs=[row(D_MODEL), pl.BlockSpec((NSA_WIDTH, tm), lambda i: (0, i)), row(RWKV_WIDTH),
                  full(wn), full(wr), full(final_g)],
        out_specs=row(D_MODEL),
        out_shape=jax.ShapeDtypeStruct((n, D_MODEL), F32),
        compiler_params=pltpu.CompilerParams(dimension_semantics=("arbitrary",),
                                             vmem_limit_bytes=VMEM_LIMIT),
        name="out_proj",
    )(x2, o_nsa_t, o_rwkv, wn, wr, final_g)


def _rope_tables(seq):
    inv = ROPE_THETA ** (-np.arange(ROPE_HALF, dtype=np.float64) / ROPE_HALF)
    ang = np.arange(seq, dtype=np.float64)[:, None] * inv[None, :]
    cos, sin = np.cos(ang), np.sin(ang)
    ra = np.ones((seq, HEAD_DIM)); rm = np.zeros((seq, HEAD_DIM)); rp = np.zeros((seq, HEAD_DIM))
    ra[:, :ROPE_HALF] = cos; ra[:, ROPE_HALF:ROPE_DIM] = cos
    rm[:, :ROPE_HALF] = -sin
    rp[:, ROPE_HALF:ROPE_DIM] = sin
    rep = lambda t: jnp.asarray(np.tile(t, (1, LANE // HEAD_DIM)), F32)
    assert seq // SEL_BLOCK <= K_AUG - HEAD_DIM
    onehot = np.zeros((seq, K_AUG - HEAD_DIM))
    onehot[np.arange(seq), np.arange(seq) // SEL_BLOCK] = 1.0
    return (rep(ra), rep(rm), rep(rp), jnp.asarray(cos.T, F32), jnp.asarray(sin.T, F32),
            jnp.asarray(onehot, BF16))


def _cmp_to_sel_t(n_cmp_pad, n_sel):
    n_cmp = n_cmp_pad - 1
    c0 = np.arange(n_cmp)[:, None] * CMP_STRIDE
    s0 = np.arange(n_sel)[None, :] * SEL_BLOCK
    ov = np.clip(np.minimum(c0 + CMP_BLOCK, s0 + SEL_BLOCK) - np.maximum(c0, s0), 0, None) / CMP_BLOCK
    mt = np.zeros((n_sel, n_cmp_pad))
    mt[:, :n_cmp] = ov.T
    return jnp.asarray(mt, BF16)


def _prep_w_in(w_in):
    idx = np.cumsum(IN_SIZES)[:-1].tolist()
    q, kc, vc, ks, vs, kw, vw, gl, gn, rw, gr = jnp.split(w_in, idx, axis=1)
    w_t = jnp.concatenate([kc, vc, ks, kw, rw, gr], axis=1).T.astype(BF16)
    pad = jnp.zeros((D_MODEL, GATE_ROWS - gl.shape[1]), w_in.dtype)
    w_f = jnp.concatenate([q, vs, vw, gl, pad, gn], axis=1).T.astype(BF16)
    return w_t, w_f


def _layer(x2, norm_g, w_in, cmp_pos_k, cmp_w1_k, cmp_w2_k, cmp_pos_v, cmp_w1_v, cmp_w2_v,
           shift_mu, decay_w0, decay_up, iclr_a0, iclr_up, k_k, k_a, r_k, gn_w, gn_b, w_out,
           final_g, *, batch, seq):
    tm = 256
    tq = 256
    tt = 256
    assert WINDOW % KEY_TILE == 0 and seq % KEY_TILE == 0 and KEY_TILE % tq == 0
    nch = seq // CMP_STRIDE
    n_sel = seq // SEL_BLOCK
    rowv = lambda t: t.reshape(1, -1).astype(F32)

    w_t, w_f = _prep_w_in(w_in)
    (kch, vch, ksh, kwh, rw, gr, qt, qrt, vst, vwt, gates, gnt) = _in_proj(
        x2, rowv(norm_g), w_t, w_f, _rope_tables(seq), seq=seq, tm=2 * tm)

    chunks = lambda t: t.reshape(NSA_KV, batch, nch, CMP_STRIDE * HEAD_DIM)
    kcmp, vcmpt = _compress(chunks(kch), chunks(vch),
                            rowv(cmp_pos_k), cmp_w1_k.astype(BF16), cmp_w2_k.astype(BF16),
                            rowv(cmp_pos_v), cmp_w1_v.astype(BF16), cmp_w2_v.T.astype(BF16))

    o_nsa_t = _nsa(qt, qrt, kcmp, vcmpt, ksh, vst, kwh, vwt, gates, gnt,
                   _cmp_to_sel_t(nch, n_sel), batch=batch, seq=seq, tq=tq)

    z = jnp.zeros((DECAY_RANK, RWKV_WIDTH), F32)
    wab = jnp.concatenate([jnp.concatenate([decay_up, z], axis=1),
                           jnp.concatenate([z, iclr_up], axis=1)], axis=0).astype(BF16)
    hid = np.arange(2 * LANE) // HEAD_DIM
    ones_bd = jnp.asarray(hid[:, None] == hid[None, :], BF16)
    ti = np.arange(tt)
    tril = jnp.asarray((ti[:, None] >= ti[None, :]) & (ti[:, None] // RWKV_CHUNK == ti[None, :] // RWKV_CHUNK), BF16)
    o_rwkv = _rwkv(rw, gr, rowv(shift_mu), rowv(decay_w0), rowv(iclr_a0), wab, rowv(k_k), rowv(k_a),
                   rowv(r_k), rowv(gn_w), rowv(gn_b), ones_bd, tril, batch=batch, seq=seq, tt=tt)

    w_o = w_out.astype(BF16)
    return _out_proj(x2, o_nsa_t, o_rwkv, w_o[:NSA_WIDTH], w_o[NSA_WIDTH:], rowv(final_g), tm=4 * tm)


def kernel(x, norm_g, w_in, cmp_pos_k, cmp_w1_k, cmp_w2_k, cmp_pos_v, cmp_w1_v, cmp_w2_v, shift_mu, decay_w0, decay_up, iclr_a0, iclr_up, k_k, k_a, r_k, gn_w, gn_b, w_out, final_g):
    batch, seq, d = x.shape
    assert d == D_MODEL and norm_g.shape[0] == 1, "single-layer trunk"
    out = _layer(x.reshape(batch * seq, d), norm_g[0], w_in[0], cmp_pos_k[0], cmp_w1_k[0], cmp_w2_k[0],
                 cmp_pos_v[0], cmp_w1_v[0], cmp_w2_v[0], shift_mu[0], decay_w0[0], decay_up[0],
                 iclr_a0[0], iclr_up[0], k_k[0], k_a[0], r_k[0], gn_w[0], gn_b[0], w_out[0],
                 final_g, batch=batch, seq=seq)
    return out.reshape(batch, seq, d)
```

```python
import functools

import numpy as np
import jax
import jax.numpy as jnp
from jax import lax
from jax.experimental import pallas as pl
from jax.experimental.pallas import tpu as pltpu

F32 = jnp.float32
BF16 = jnp.bfloat16

D_MODEL = 1024
HEAD_DIM = 64
NSA_HEADS = 8
NSA_KV = 2
NSA_REP = NSA_HEADS // NSA_KV
RWKV_HEADS = 8
NSA_WIDTH = NSA_HEADS * HEAD_DIM
RWKV_WIDTH = RWKV_HEADS * HEAD_DIM
KV_WIDTH = NSA_KV * HEAD_DIM
ROPE_DIM = HEAD_DIM // 4
ROPE_HALF = ROPE_DIM // 2
ROPE_THETA = 500000.0
CMP_BLOCK = 32
CMP_STRIDE = 16
CMP_HIDDEN = 256
SEL_BLOCK = 64
SEL_TOPK = 8
WINDOW = 512
DECAY_RANK = 64
ICLR_RANK = 64
RWKV_SHIFT_WIDTH = 3 * RWKV_WIDTH + DECAY_RANK + ICLR_RANK
IN_SIZES = (NSA_WIDTH, KV_WIDTH, KV_WIDTH, KV_WIDTH, KV_WIDTH, KV_WIDTH, KV_WIDTH,
            3 * NSA_HEADS, NSA_WIDTH, RWKV_SHIFT_WIDTH, RWKV_WIDTH)
SCALE = HEAD_DIM ** -0.5
RMS_EPS = 1e-6
GN_EPS = 64e-5
NEG_INF = -1e30
FORCE_BONUS = 1e3

LANE = 128
SUBLANE = 8
BF16_SUBLANE = 2 * SUBLANE
GATE_ROWS = 2 * BF16_SUBLANE

T_KV = 0
T_RW = T_KV + 4 * KV_WIDTH
T_GR = T_RW + RWKV_SHIFT_WIDTH
T_END = T_GR + RWKV_WIDTH
R_Q = 0
R_VS = R_Q + NSA_WIDTH
R_VW = R_VS + KV_WIDTH
R_GL = R_VW + KV_WIDTH
R_GN = R_GL + GATE_ROWS
R_END = R_GN + NSA_WIDTH

LOG2E = float(np.log2(np.e))
K_AUG = LANE
V_ROWS = HEAD_DIM + BF16_SUBLANE
KEY_TILE = 256
RWKV_CHUNK = 64
VMEM_LIMIT = 48 * 1024 * 1024


def _dot(a, b):
    return jnp.dot(a, b, preferred_element_type=F32)


def _dot_nt(a, b):
    return lax.dot_general(a, b, (((1,), (1,)), ((), ())), preferred_element_type=F32)


def _dot_tn(a, b):
    return lax.dot_general(a, b, (((0,), (0,)), ((), ())), preferred_element_type=F32)


def _bmm(a, b):
    return lax.dot_general(a, b, (((2,), (1,)), ((0,), (0,))), preferred_element_type=F32)


def _bmm_nt(a, b):
    return lax.dot_general(a, b, (((2,), (2,)), ((0,), (0,))), preferred_element_type=F32)


def _split3(x):
    hi = x.astype(BF16)
    r1 = x - hi.astype(F32)
    mid = r1.astype(BF16)
    lo = (r1 - mid.astype(F32)).astype(BF16)
    return hi, mid, lo


def _head_sums(x, ones_blk, terms=2):
    w = ones_blk.shape[0]
    parts = [x.astype(BF16)]
    if terms == 2:
        parts.append((x - parts[0].astype(F32)).astype(BF16))
    cols = []
    for i in range(0, x.shape[1], w):
        acc = _dot(parts[0][:, i:i + w], ones_blk)
        for t in parts[1:]:
            acc = acc + _dot(t[:, i:i + w], ones_blk)
        cols.append(acc)
    return jnp.concatenate(cols, axis=1)


def _x3_dot(w_bf16, x):
    hi, mid, lo = _split3(x)
    return _dot(w_bf16, hi) + _dot(w_bf16, mid) + _dot(w_bf16, lo)


def _sigmoid(x):
    return 0.5 * jnp.tanh(0.5 * x) + 0.5


def _interleave(*gens):
    live = list(gens)
    while live:
        for g in list(live):
            try:
                next(g)
            except StopIteration:
                live.remove(g)


def _rope128(t, ra, rm, rp):
    return t * ra + pltpu.roll(t, LANE - ROPE_HALF, 1) * rm + pltpu.roll(t, ROPE_HALF, 1) * rp


def _in_proj_kernel(x_ref, g_ref, wt_ref, wf_ref, ra_ref, rm_ref, rp_ref, cos_ref, sin_ref, oh_ref,
                    kc_ref, vc_ref, ks_ref, kw_ref, rw_ref, gr_ref,
                    qt_ref, qrt_ref, vst_ref, vwt_ref, gate_ref, gn_ref, cmp_scr):
    x = x_ref[...]
    ms = jnp.mean(x * x, axis=-1, keepdims=True)
    y = (x * lax.rsqrt(ms + RMS_EPS) * g_ref[...]).astype(BF16)
    tm = x.shape[0]

    kv = _dot_nt(y, wt_ref[T_KV:T_RW, :])
    for i, ref in enumerate((kc_ref, vc_ref)):
        cmp_scr[i] = kv[:, i * KV_WIDTH:(i + 1) * KV_WIDTH]
        for tau in range(CMP_STRIDE):
            piece = cmp_scr[i, pl.ds(tau, tm // CMP_STRIDE, stride=CMP_STRIDE), :].astype(BF16)
            for g in range(NSA_KV):
                ref[g, :, tau * HEAD_DIM:(tau + 1) * HEAD_DIM] = piece[:, g * HEAD_DIM:(g + 1) * HEAD_DIM]
    ra, rm, rp = ra_ref[...], rm_ref[...], rp_ref[...]
    for i, ref in ((2, ks_ref), (3, kw_ref)):
        t = _rope128(kv[:, i * LANE:(i + 1) * LANE], ra, rm, rp)
        for g in range(NSA_KV):
            tg = t[:, g * HEAD_DIM:(g + 1) * HEAD_DIM].astype(BF16)
            if ref is ks_ref:
                ref[g] = jnp.concatenate([tg, oh_ref[...]], axis=1)
            else:
                ref[g] = tg
    rw_ref[...] = _dot_nt(y, wt_ref[T_RW:T_GR, :])
    gr = _dot_nt(y, wt_ref[T_GR:T_END, :])
    gr_ref[...] = gr * _sigmoid(gr)

    def proj_t(r0, r1):
        return _dot_nt(wf_ref[r0:r1, :], y)

    qt = proj_t(R_Q, R_VS) * (SCALE * LOG2E)
    cos, sin = cos_ref[...], sin_ref[...]
    qt_ref[...] = qt.astype(BF16)
    for h in range(NSA_HEADS):
        r0 = h * HEAD_DIM
        t1 = qt[r0:r0 + ROPE_HALF]
        t2 = qt[r0 + ROPE_HALF:r0 + ROPE_DIM]
        qrt_ref[r0:r0 + ROPE_DIM, :] = jnp.concatenate(
            [t1 * cos - t2 * sin, t2 * cos + t1 * sin], axis=0).astype(BF16)
        qrt_ref[r0 + ROPE_DIM:r0 + HEAD_DIM, :] = qt[r0 + ROPE_DIM:r0 + HEAD_DIM].astype(BF16)
    vt = proj_t(R_VS, R_GL).astype(BF16)
    ones = jnp.ones((V_ROWS - HEAD_DIM, LANE), BF16)
    for j in range(vt.shape[1] // LANE):
        for i, ref in enumerate((vst_ref, vwt_ref)):
            for g in range(NSA_KV):
                r0 = i * KV_WIDTH + g * HEAD_DIM
                ref[j, g * V_ROWS:(g + 1) * V_ROWS, :] = jnp.concatenate(
                    [vt[r0:r0 + HEAD_DIM, j * LANE:(j + 1) * LANE], ones], axis=0)
    gate_ref[...] = _sigmoid(proj_t(R_GL, R_GN))
    gn = proj_t(R_GN, R_END)
    gn_ref[...] = gn * _sigmoid(gn)


def _in_proj(x2, norm_g, w_t, w_f, tabs, *, seq, tm):
    n = x2.shape[0]
    spt = seq // tm
    ra, rm, rp, cos, sin, onehot = tabs
    hm = lambda w: jax.ShapeDtypeStruct((NSA_KV, n, w), BF16)
    hspec = lambda w: pl.BlockSpec((NSA_KV, tm, w), lambda i: (0, i, 0))
    row = lambda w: pl.BlockSpec((tm, w), lambda i: (i, 0))
    col = lambda r: pl.BlockSpec((r, tm), lambda i: (0, i))
    full = lambda a: pl.BlockSpec(a.shape, lambda i: (0,) * a.ndim)
    tab = lambda w: pl.BlockSpec((tm, w), lambda i: (i % spt, 0))
    tabt = pl.BlockSpec((ROPE_HALF, tm), lambda i: (0, i % spt))
    vtile = pl.BlockSpec((tm // LANE, NSA_KV * V_ROWS, LANE), lambda i: (i, 0, 0))
    vsd = jax.ShapeDtypeStruct((n // LANE, NSA_KV * V_ROWS, LANE), BF16)
    cw = CMP_STRIDE * HEAD_DIM
    cspec = pl.BlockSpec((NSA_KV, tm // CMP_STRIDE, cw), lambda i: (0, i, 0))
    csd = jax.ShapeDtypeStruct((NSA_KV, n // CMP_STRIDE, cw), BF16)
    return pl.pallas_call(
        _in_proj_kernel,
        grid=(n // tm,),
        in_specs=[row(D_MODEL), full(norm_g), full(w_t), full(w_f), tab(LANE), tab(LANE), tab(LANE),
                  tabt, tabt, tab(K_AUG - HEAD_DIM)],
        out_specs=[cspec, cspec, hspec(K_AUG), hspec(HEAD_DIM), row(RWKV_SHIFT_WIDTH), row(RWKV_WIDTH),
                   col(NSA_WIDTH), col(NSA_WIDTH), vtile, vtile,
                   col(GATE_ROWS), col(NSA_WIDTH)],
        out_shape=[csd, csd, hm(K_AUG), hm(HEAD_DIM),
                   jax.ShapeDtypeStruct((n, RWKV_SHIFT_WIDTH), F32),
                   jax.ShapeDtypeStruct((n, RWKV_WIDTH), F32),
                   jax.ShapeDtypeStruct((NSA_WIDTH, n), BF16),
                   jax.ShapeDtypeStruct((NSA_WIDTH, n), BF16),
                   vsd, vsd,
                   jax.ShapeDtypeStruct((GATE_ROWS, n), F32),
                   jax.ShapeDtypeStruct((NSA_WIDTH, n), F32)],
        scratch_shapes=[pltpu.VMEM((2, tm, KV_WIDTH), F32)],
        compiler_params=pltpu.CompilerParams(dimension_semantics=("arbitrary",),
                                             vmem_limit_bytes=VMEM_LIMIT),
        name="in_proj",
    )(x2, norm_g, w_t, w_f, ra, rm, rp, cos, sin, onehot)


def _compress_kernel(kc_ref, vc_ref, pk_ref, w1k_ref, w2k_ref, pv_ref, w1v_ref, w2vt_ref,
                     ko_ref, vo_ref):
    half = CMP_STRIDE * HEAD_DIM
    ng, _, nch, _ = kc_ref.shape

    def hidden(c_ref, pos_ref, w1_ref):
        c = c_ref[:, 0].reshape(ng * nch, half)
        pos = jnp.broadcast_to(pos_ref[...], (BF16_SUBLANE, 2 * half)).astype(BF16)
        za = _dot(jnp.concatenate([c, pos[:, 0:half]], axis=0), w1_ref[0:half, :])
        zb = _dot(jnp.concatenate([c, pos[:, half:]], axis=0), w1_ref[half:2 * half, :])
        pv = za[ng * nch:ng * nch + 1] + zb[ng * nch:ng * nch + 1]
        hid = jnp.concatenate(
            [za[g * nch:(g + 1) * nch] + pltpu.roll(zb[g * nch:(g + 1) * nch], nch - 1, 0) for g in range(ng)],
            axis=0) + pv
        return (hid * _sigmoid(hid)).astype(BF16)

    ko = _dot(hidden(kc_ref, pk_ref, w1k_ref), w2k_ref[...]).astype(BF16)
    hv = hidden(vc_ref, pv_ref, w1v_ref)
    for g in range(ng):
        ko_ref[g, 0] = ko[g * nch:(g + 1) * nch]
        vo_ref[g, 0] = _dot_nt(w2vt_ref[...], hv[g * nch:(g + 1) * nch]).astype(BF16)


def _compress(kc_r, vc_r, pk, w1k, w2k, pv, w1v, w2vt):
    g, b, nch, width = kc_r.shape
    blk = pl.BlockSpec((g, 1, nch, width), lambda j: (0, j, 0, 0))
    full = lambda a: pl.BlockSpec(a.shape, lambda j: (0,) * a.ndim)
    return pl.pallas_call(
        _compress_kernel,
        grid=(b,),
        in_specs=[blk, blk, full(pk), full(w1k), full(w2k), full(pv), full(w1v), full(w2vt)],
        out_specs=[pl.BlockSpec((g, 1, nch, HEAD_DIM), lambda j: (0, j, 0, 0)),
                   pl.BlockSpec((g, 1, HEAD_DIM, nch), lambda j: (0, j, 0, 0))],
        out_shape=[jax.ShapeDtypeStruct((g, b, nch, HEAD_DIM), BF16),
                   jax.ShapeDtypeStruct((g, b, HEAD_DIM, nch), BF16)],
        compiler_params=pltpu.CompilerParams(dimension_semantics=("arbitrary",),
                                             vmem_limit_bytes=VMEM_LIMIT),
        name="compress",
    )(kc_r, vc_r, pk, w1k, w2k, pv, w1v, w2vt)


def _nsa_kernel(qt_ref, qrt_ref, kc_ref, vct_ref, ks_ref, vst_ref, kw_ref, vwt_ref,
                gate_ref, gn_ref, mt_ref, o_ref, *, tq, seq):
    tk = KEY_TILE
    n_win = WINDOW // tk
    qi = pl.program_id(1)
    q0 = qi * tq
    nl = NSA_REP * tq
    n_sel = seq // SEL_BLOCK
    ncp = kc_ref.shape[2]
    groups = range(NSA_KV)

    def heads_on_lanes(ref, g):
        rows = [(g * NSA_REP + r) * HEAD_DIM for r in range(NSA_REP)]
        return jnp.concatenate([ref[r0:r0 + HEAD_DIM, :] for r0 in rows], axis=1)

    def tile4(a):
        return jnp.concatenate([a] * NSA_REP, axis=1)

    k_s = lax.broadcasted_iota(jnp.int32, (tk, tq), 0)
    t_l = q0 + lax.broadcasted_iota(jnp.int32, (tk, tq), 1)

    def update_steps(box, k_ref, vt_ref, g, q_op, kts, keeps):
        m_i, acc = box[0]

        def scores(i):
            sc = _dot(k_ref[g, pl.ds(pl.multiple_of(kts[i] * tk, tk), tk), :], q_op)
            if keeps[i] is None:
                return sc
            return jnp.concatenate([jnp.where(keeps[i], sc[:, r * tq:(r + 1) * tq], NEG_INF)
                                    for r in range(NSA_REP)], axis=1)

        sc_next = scores(0)
        yield
        for i, kt in enumerate(kts):
            sc = sc_next
            if i + 1 < len(kts):
                sc_next = scores(i + 1)
            m_n = jnp.maximum(m_i, jnp.max(sc, axis=0, keepdims=True))
            pe = jnp.exp2(sc - m_n).astype(BF16)
            yield
            vtb = jnp.concatenate([vt_ref[kt * (tk // LANE) + jj, g * V_ROWS:(g + 1) * V_ROWS, :]
                                   for jj in range(tk // LANE)], axis=1)
            acc = jnp.exp2(m_i - m_n) * acc + _dot(vtb, pe)
            m_i = m_n
            yield
        box[0] = (m_i, acc)

    def last_tiles(box, k_ref, vt_ref, g, q_op, n_back, low_keep):
        kts, keeps = [], []
        for back in range(n_back + 1):
            d = t_l - ((a - back) * tk + k_s)
            kts.append(a - back)
            if back == 0:
                keeps.append(d >= 0)
            elif back == n_win and low_keep is not None:
                keeps.append(low_keep(d))
            else:
                keeps.append(None)
        return update_steps(box, k_ref, vt_ref, g, q_op, kts, keeps)

    def select_steps(g, qr, sel_out, all_selected):
        s = _dot(kc_ref[g, 0], heads_on_lanes(qt_ref, g))
        t_c = q0 + lax.broadcasted_iota(jnp.int32, (ncp, tq), 1)
        c_c = lax.broadcasted_iota(jnp.int32, (ncp, tq), 0)
        cmask = tile4((c_c * CMP_STRIDE + (CMP_BLOCK - 1)) <= t_c)
        yield
        s = jnp.where(cmask, s, NEG_INF)
        m = jnp.maximum(jnp.max(s, axis=0, keepdims=True), 0.5 * NEG_INF)
        e = jnp.exp2(s - m)
        den = jnp.sum(e, axis=0, keepdims=True)
        p = e * (1.0 / jnp.where(den > 0.0, den, 1.0))
        sel_out["o_cmp"] = _dot(vct_ref[g, 0], p.astype(BF16))
        yield
        if all_selected:
            sel_out["qr_sel"] = jnp.concatenate([qr, jnp.zeros((K_AUG - HEAD_DIM, nl), BF16)], axis=0)
            return
        psum = p[:, 0:tq]
        for r in range(1, NSA_REP):
            psum = psum + p[:, r * tq:(r + 1) * tq]
        imp = _x3_dot(mt_ref[...], psum)
        j = lax.broadcasted_iota(jnp.int32, (n_sel, tq), 0)
        t = q0 + lax.broadcasted_iota(jnp.int32, (n_sel, tq), 1)
        tb = t // SEL_BLOCK
        forced = (j == 0) | (j == tb) | (j == tb - 1)
        val = jnp.where(j <= tb, imp + jnp.where(forced, FORCE_BONUS, 0.0), -1.0)
        yield
        vals = [val[g0:g0 + SUBLANE] for g0 in range(0, n_sel, SUBLANE)]
        cnts = [jnp.zeros((SUBLANE, tq), F32) for _ in vals]
        srow = lax.broadcasted_iota(jnp.int32, (SUBLANE, tq), 0)
        for i in range(n_sel):
            vi = jnp.broadcast_to(val[i:i + 1, :], (SUBLANE, tq))
            for g, vg in enumerate(vals):
                ge = lambda: jnp.where(vi >= vg, 1.0, 0.0)
                gt = lambda: jnp.where(vi > vg, 1.0, 0.0)
                if g * SUBLANE > i:
                    beat = ge()
                elif (g + 1) * SUBLANE <= i:
                    beat = gt()
                else:
                    beat = jnp.where(srow > i - g * SUBLANE, ge(), gt())
                cnts[g] = cnts[g] + beat
            if i % SUBLANE == SUBLANE - 1:
                yield
        cnt = jnp.concatenate(cnts, axis=0)
        sel_bias = jnp.where(cnt < float(SEL_TOPK), 0.0, NEG_INF).astype(BF16)
        sel_out["qr_sel"] = jnp.concatenate(
            [qr, tile4(sel_bias), jnp.zeros((K_AUG - HEAD_DIM - n_sel, nl), BF16)], axis=0)

    init = (jnp.full((1, nl), NEG_INF, F32), jnp.zeros((V_ROWS, nl), F32))
    a = (q0 + tq - 1) // tk

    def attend(n_back):
        win_boxes, sel_boxes, sel_outs = [[init] for _ in groups], [[init] for _ in groups], [{} for _ in groups]
        all_selected = n_back < n_win and (n_back + 1) * tk <= SEL_TOPK * SEL_BLOCK
        for g in groups:
            qr = heads_on_lanes(qrt_ref, g)
            _interleave(last_tiles(win_boxes[g], kw_ref, vwt_ref, g, qr, n_back, lambda d: d < WINDOW),
                        select_steps(g, qr, sel_outs[g], all_selected))
        _interleave(*[last_tiles(sel_boxes[g], ks_ref, vst_ref, g, sel_outs[g]["qr_sel"], n_back, None)
                      for g in groups])
        c_wins = [b[0] for b in win_boxes]
        c_sels = [b[0] for b in sel_boxes]
        o_cmps = [o["o_cmp"] for o in sel_outs]
        qr_sels = [o["qr_sel"] for o in sel_outs]

        def old_tiles(kts):
            def body(carry):
                boxes = [[c] for c in carry]
                _interleave(*[update_steps(boxes[g], ks_ref, vst_ref, g, qr_sels[g], kts, [None] * len(kts))
                              for g in groups])
                return tuple(b[0] for b in boxes)
            return body

        if n_back == n_win:
            n_old = a - n_win
            c_sels = lax.fori_loop(0, n_old % 2, lambda kt, c: old_tiles([kt])(c), tuple(c_sels))
            c_sels = lax.fori_loop(0, n_old // 2,
                                   lambda i, c: old_tiles([n_old % 2 + 2 * i, n_old % 2 + 2 * i + 1])(c), c_sels)

        for g in groups:
            o_sel = c_sels[g][1][0:HEAD_DIM] * (1.0 / c_sels[g][1][HEAD_DIM:HEAD_DIM + 1])
            o_win = c_wins[g][1][0:HEAD_DIM] * (1.0 / c_wins[g][1][HEAD_DIM:HEAD_DIM + 1])
            for r in range(NSA_REP):
                ls = slice(r * tq, (r + 1) * tq)
                h = g * NSA_REP + r
                gate = lambda c: gate_ref[3 * h + c:3 * h + c + 1, :]
                o = gate(0) * o_cmps[g][:, ls] + gate(1) * o_sel[:, ls] + gate(2) * o_win[:, ls]
                rs = slice(h * HEAD_DIM, (h + 1) * HEAD_DIM)
                o_ref[rs, :] = (o * gn_ref[rs, :]).astype(BF16)

    def dispatch(n_back):
        if n_back == 0:
            attend(0)
        else:
            lax.cond(a >= n_back, lambda: attend(n_back), lambda: dispatch(n_back - 1))

    dispatch(n_win)


def _nsa(qt, qrt, kcmp, vcmpt, ksh, vst, kwh, vwt, gates, gnt, mt, *, batch, seq, tq):
    n = batch * seq
    nq = seq // tq
    ncp = kcmp.shape[2]
    qspec = pl.BlockSpec((NSA_WIDTH, tq), lambda b, i: (0, b * nq + i))
    kspec = lambda w: pl.BlockSpec((NSA_KV, seq, w), lambda b, i: (0, b, 0))
    vspec = pl.BlockSpec((seq // LANE, NSA_KV * V_ROWS, LANE), lambda b, i: (b, 0, 0))
    kern = functools.partial(_nsa_kernel, tq=tq, seq=seq)
    return pl.pallas_call(
        kern,
        grid=(batch, nq),
        in_specs=[qspec, qspec,
                  pl.BlockSpec((NSA_KV, 1, ncp, HEAD_DIM), lambda b, i: (0, b, 0, 0)),
                  pl.BlockSpec((NSA_KV, 1, HEAD_DIM, ncp), lambda b, i: (0, b, 0, 0)),
                  kspec(K_AUG), vspec, kspec(HEAD_DIM), vspec,
                  pl.BlockSpec((GATE_ROWS, tq), lambda b, i: (0, b * nq + i)),
                  qspec,
                  pl.BlockSpec(mt.shape, lambda b, i: (0, 0))],
        out_specs=qspec,
        out_shape=jax.ShapeDtypeStruct((NSA_WIDTH, n), BF16),
        compiler_params=pltpu.CompilerParams(
            dimension_semantics=("arbitrary", "arbitrary"),
            vmem_limit_bytes=VMEM_LIMIT),
        name="nsa",
    )(qt, qrt, kcmp, vcmpt, ksh, vst, kwh, vwt, gates, gnt, mt)


def _rwkv_kernel(p_ref, gr_ref, mu_ref, w0_ref, a0_ref, wab_ref, kk_ref, ka_ref, rk_ref,
                 gw_ref, gb_ref, ones_ref, tril_ref, o_ref, st_ref, carry_ref, *, tt):
    c = RWKV_CHUNK
    hd = HEAD_DIM
    nc = tt // c
    step = pl.program_id(1)

    @pl.when(step == 0)
    def _():
        st_ref[...] = jnp.zeros_like(st_ref)
        carry_ref[...] = jnp.zeros_like(carry_ref)

    p = p_ref[...]
    row = lax.broadcasted_iota(jnp.int32, p.shape, 0)
    prev = jnp.where(row == 0, carry_ref[...], pltpu.roll(p, 1, 0))
    carry_ref[...] = p[tt - 1:tt, :]
    ps = p + mu_ref[...] * (prev - p)
    r = ps[:, 0:RWKV_WIDTH]
    k = ps[:, RWKV_WIDTH:2 * RWKV_WIDTH]
    v = ps[:, 2 * RWKV_WIDTH:3 * RWKV_WIDTH]
    lora = ps[:, 3 * RWKV_WIDTH:]
    lane = lax.broadcasted_iota(jnp.int32, lora.shape, 1)
    feat = jnp.where(lane < DECAY_RANK, jnp.tanh(lora), lora).astype(BF16)
    up = _dot(feat, wab_ref[...])
    w = w0_ref[...] + up[:, 0:RWKV_WIDTH]
    lw = _sigmoid(w) * (-float(np.exp(-0.5)))
    a = _sigmoid(a0_ref[...] + up[:, RWKV_WIDTH:])
    ones_bd = ones_ref[...]
    kk = k * kk_ref[...]
    kkn = kk * lax.rsqrt(jnp.maximum(_head_sums(kk * kk, ones_bd), 1e-24))
    k2 = k * (1.0 + (a - 1.0) * ka_ref[...])
    alpha = -kkn
    beta = kkn * a
    bonus = _head_sums(r * k2 * rk_ref[...], ones_bd, terms=1) * v

    cum = _x3_dot(tril_ref[...], lw)
    cend = jnp.concatenate(
        [jnp.broadcast_to(cum[(ch + 1) * c - 1:(ch + 1) * c, :], (c, RWKV_WIDTH)) for ch in range(nc)], axis=0)
    e_neg = jnp.exp(-cum)
    pc = jnp.exp(cend)
    at = alpha * jnp.exp(cum - lw)
    bt = beta * e_neg
    kt = k2 * e_neg
    rt = r * jnp.exp(cum)
    bh = bt * pc
    kh = kt * pc

    npair = RWKV_WIDTH // LANE

    def pairs(x):
        return jnp.stack([x[ch * c:(ch + 1) * c, j * LANE:(j + 1) * LANE]
                          for ch in range(nc) for j in range(npair)], axis=0)

    at_p, rt_p, bt_p, kt_p, v_p, bh_p, kh_p = (pairs(t) for t in (at, rt, bt, kt, v, bh, kh))
    pc_p = jnp.stack([pc[ch * c:ch * c + 1, j * LANE:(j + 1) * LANE]
                      for ch in range(nc) for j in range(npair)], axis=0)
    lane_c = lax.broadcasted_iota(jnp.int32, (1, c, LANE), 2)
    row_c = lax.broadcasted_iota(jnp.int32, (1, c, LANE), 1)
    even_c = lane_c < hd
    col_c = jnp.where(even_c, lane_c, lane_c - hd)
    low_s = row_c > col_c
    low_i = row_c >= col_c
    lane_2c = lax.broadcasted_iota(jnp.int32, (1, 2 * c, LANE), 2)
    row_2c = lax.broadcasted_iota(jnp.int32, (1, 2 * c, LANE), 1)
    even_2c = lane_2c < hd
    on_bd = (row_2c < hd) == even_2c
    zero_c = jnp.zeros((1, c, LANE), BF16)

    def bd(x):
        xb = x.astype(BF16)
        return jnp.concatenate([jnp.where(even_c, xb, zero_c), jnp.where(even_c, zero_c, xb)], axis=1)

    def abd(x):
        xb = x.astype(BF16)
        return jnp.concatenate([jnp.where(even_c, zero_c, xb), jnp.where(even_c, xb, zero_c)], axis=1)

    la = jnp.concatenate([at_p, rt_p], axis=1).astype(BF16)
    zero_2c = jnp.zeros((1, 2 * c, LANE), BF16)
    r_e = _bmm_nt(jnp.where(even_2c, la, zero_2c), jnp.concatenate([bt_p, kt_p], axis=1).astype(BF16))
    r_o = _bmm_nt(jnp.where(even_2c, zero_2c, la), jnp.concatenate([kt_p, bt_p], axis=1).astype(BF16))
    nab = jnp.where(low_s, jnp.where(even_c, r_e[:, 0:c], r_o[:, 0:c]), 0.0)
    aak_sw = jnp.where(low_s, jnp.where(even_c, r_o[:, 0:c], r_e[:, 0:c]), 0.0).astype(BF16)
    arb = jnp.where(low_i, jnp.where(even_c, r_e[:, c:], r_o[:, c:]), 0.0).astype(BF16)
    ark_sw = jnp.where(low_i, jnp.where(even_c, r_o[:, c:], r_e[:, c:]), 0.0).astype(BF16)
    tinv = jnp.where(row_c == col_c, 1.0, 0.0) + nab
    npow = _bmm(nab.astype(BF16), bd(nab))
    n_dbl = 5
    for it in range(n_dbl):
        nbd = bd(npow)
        if it + 1 < n_dbl:
            res = _bmm(jnp.concatenate([tinv, npow], axis=1).astype(BF16), nbd)
            tinv = tinv + res[:, 0:c]
            npow = res[:, c:]
        else:
            tinv = tinv + _bmm(tinv.astype(BF16), nbd)
    av = _bmm(jnp.concatenate([aak_sw, ark_sw], axis=1), abd(v_p))
    tx = _bmm(tinv.astype(BF16), jnp.concatenate([bd(av[:, 0:c]), bd(at_p)], axis=2))
    u0, ta = tx[:, :, 0:LANE], tx[:, :, LANE:]
    ax = _bmm(arb, jnp.concatenate([bd(ta), bd(u0)], axis=2))
    rq = rt_p + ax[:, :, 0:LANE]
    y0 = ax[:, :, LANE:] + av[:, c:]
    v_b = v_p.astype(BF16)
    w_f = jnp.concatenate([jnp.concatenate([ta, u0], axis=2).astype(BF16),
                           jnp.concatenate([jnp.zeros_like(v_b), v_b], axis=2)], axis=1)
    gh = _bmm(jnp.concatenate([jnp.swapaxes(bh_p, 1, 2), jnp.swapaxes(kh_p, 1, 2)], axis=2).astype(BF16), w_f)
    g_bd = jnp.where(on_bd, gh[:, :, 0:LANE], 0.0) + jnp.where(row_2c == lane_2c, pc_p, 0.0)
    h_bd = jnp.where(on_bd, gh[:, :, LANE:], 0.0)
    lhs = jnp.concatenate([rq, g_bd], axis=1).astype(BF16)

    st = st_ref[...]
    ys = []
    for ch in range(nc):
        sl = slice(ch * npair, (ch + 1) * npair)
        res = _bmm(lhs[sl], st.astype(BF16))
        yc = res[:, 0:c, :] + y0[sl]
        st = res[:, c:, :] + h_bd[sl]
        ys.append(jnp.concatenate([yc[j] for j in range(npair)], axis=1))
    st_ref[...] = st
    y = jnp.concatenate(ys, axis=0) if nc > 1 else ys[0]

    inv_hd = 1.0 / hd
    mean = _head_sums(y, ones_bd, terms=1) * inv_hd
    ycen = y - mean
    var = _head_sums(ycen * ycen, ones_bd, terms=1) * inv_hd
    yn = ycen * lax.rsqrt(var + GN_EPS) * gw_ref[...] + gb_ref[...]
    o_ref[...] = ((yn + bonus) * gr_ref[...]).astype(BF16)


def _rwkv(rw, gr, mu, w0, a0, wab, kk, ka, rk, gw, gb, ones_bd, tril, *, batch, seq, tt):
    n = batch * seq
    ns = seq // tt
    row = lambda w: pl.BlockSpec((tt, w), lambda b, i: (b * ns + i, 0))
    full = lambda a: pl.BlockSpec(a.shape, lambda b, i: (0,) * a.ndim)
    kern = functools.partial(_rwkv_kernel, tt=tt)
    consts = (mu, w0, a0, wab, kk, ka, rk, gw, gb, ones_bd, tril)
    return pl.pallas_call(
        kern,
        grid=(batch, ns),
        in_specs=[row(RWKV_SHIFT_WIDTH), row(RWKV_WIDTH)] + [full(a) for a in consts],
        out_specs=row(RWKV_WIDTH),
        out_shape=jax.ShapeDtypeStruct((n, RWKV_WIDTH), BF16),
        scratch_shapes=[pltpu.VMEM((RWKV_WIDTH // LANE, LANE, LANE), F32),
                        pltpu.VMEM((1, RWKV_SHIFT_WIDTH), F32)],
        compiler_params=pltpu.CompilerParams(dimension_semantics=("arbitrary", "arbitrary"),
                                             vmem_limit_bytes=VMEM_LIMIT),
        name="rwkv",
    )(rw, gr, *consts)


def _out_kernel(x_ref, ont_ref, or_ref, wn_ref, wr_ref, g_ref, o_ref, *, n_sub):
    ts = x_ref.shape[0] // n_sub

    def project(j):
        rows = slice(j * ts, (j + 1) * ts)
        return (x_ref[rows, :] + _dot_tn(ont_ref[:, rows], wn_ref[...])
                + _dot(or_ref[rows, :], wr_ref[...]))

    def finish(j, h):
        ms = jnp.mean(h * h, axis=-1, keepdims=True)
        o_ref[j * ts:(j + 1) * ts, :] = h * lax.rsqrt(ms + RMS_EPS) * g_ref[...]

    h = project(0)
    for j in range(1, n_sub):
        h_next = project(j)
        finish(j - 1, h)
        h = h_next
    finish(n_sub - 1, h)


def _out_proj(x2, o_nsa_t, o_rwkv, wn, wr, final_g, *, tm, n_sub):
    n = x2.shape[0]
    row = lambda w: pl.BlockSpec((tm, w), lambda i: (i, 0))
    full = lambda a: pl.BlockSpec(a.shape, lambda i: (0,) * a.ndim)
    return pl.pallas_call(
        functools.partial(_out_kernel, n_sub=n_sub),
        grid=(n // tm,),
        in_specs=[row(D_MODEL), pl.BlockSpec((NSA_WIDTH, tm), lambda i: (0, i)), row(RWKV_WIDTH),
                  full(wn), full(wr), full(final_g)],
        out_specs=row(D_MODEL),
        out_shape=jax.ShapeDtypeStruct((n, D_MODEL), F32),
        compiler_params=pltpu.CompilerParams(dimension_semantics=("arbitrary",),
                                             vmem_limit_bytes=VMEM_LIMIT),
        name="out_proj",
    )(x2, o_nsa_t, o_rwkv, wn, wr, final_g)


def _rope_tables(seq):
    inv = ROPE_THETA ** (-np.arange(ROPE_HALF, dtype=np.float64) / ROPE_HALF)
    ang = np.arange(seq, dtype=np.float64)[:, None] * inv[None, :]
    cos, sin = np.cos(ang), np.sin(ang)
    ra = np.ones((seq, HEAD_DIM)); rm = np.zeros((seq, HEAD_DIM)); rp = np.zeros((seq, HEAD_DIM))
    ra[:, :ROPE_HALF] = cos; ra[:, ROPE_HALF:ROPE_DIM] = cos
    rm[:, :ROPE_HALF] = -sin
    rp[:, ROPE_HALF:ROPE_DIM] = sin
    rep = lambda t: jnp.asarray(np.tile(t, (1, LANE // HEAD_DIM)), F32)
    assert seq // SEL_BLOCK <= K_AUG - HEAD_DIM
    onehot = np.zeros((seq, K_AUG - HEAD_DIM))
    onehot[np.arange(seq), np.arange(seq) // SEL_BLOCK] = 1.0
    return (rep(ra), rep(rm), rep(rp), jnp.asarray(cos.T, F32), jnp.asarray(sin.T, F32),
            jnp.asarray(onehot, BF16))


def _cmp_to_sel_t(n_cmp_pad, n_sel):
    n_cmp = n_cmp_pad - 1
    c0 = np.arange(n_cmp)[:, None] * CMP_STRIDE
    s0 = np.arange(n_sel)[None, :] * SEL_BLOCK
    ov = np.clip(np.minimum(c0 + CMP_BLOCK, s0 + SEL_BLOCK) - np.maximum(c0, s0), 0, None) / CMP_BLOCK
    mt = np.zeros((n_sel, n_cmp_pad))
    mt[:, :n_cmp] = ov.T
    return jnp.asarray(mt, BF16)


def _prep_w_in(w_in):
    idx = np.cumsum(IN_SIZES)[:-1].tolist()
    q, kc, vc, ks, vs, kw, vw, gl, gn, rw, gr = jnp.split(w_in, idx, axis=1)
    w_t = jnp.concatenate([kc, vc, ks, kw, rw, gr], axis=1).T.astype(BF16)
    pad = jnp.zeros((D_MODEL, GATE_ROWS - gl.shape[1]), w_in.dtype)
    w_f = jnp.concatenate([q, vs, vw, gl, pad, gn], axis=1).T.astype(BF16)
    return w_t, w_f


def _layer(x2, norm_g, w_in, cmp_pos_k, cmp_w1_k, cmp_w2_k, cmp_pos_v, cmp_w1_v, cmp_w2_v,
           shift_mu, decay_w0, decay_up, iclr_a0, iclr_up, k_k, k_a, r_k, gn_w, gn_b, w_out,
           final_g, *, batch, seq):
    tm = 256
    tq = 256
    tt = 256
    assert WINDOW % KEY_TILE == 0 and seq % KEY_TILE == 0 and KEY_TILE % tq == 0
    nch = seq // CMP_STRIDE
    n_sel = seq // SEL_BLOCK
    rowv = lambda t: t.reshape(1, -1).astype(F32)

    w_t, w_f = _prep_w_in(w_in)
    (kch, vch, ksh, kwh, rw, gr, qt, qrt, vst, vwt, gates, gnt) = _in_proj(
        x2, rowv(norm_g), w_t, w_f, _rope_tables(seq), seq=seq, tm=2 * tm)

    chunks = lambda t: t.reshape(NSA_KV, batch, nch, CMP_STRIDE * HEAD_DIM)
    kcmp, vcmpt = _compress(chunks(kch), chunks(vch),
                            rowv(cmp_pos_k), cmp_w1_k.astype(BF16), cmp_w2_k.astype(BF16),
                            rowv(cmp_pos_v), cmp_w1_v.astype(BF16), cmp_w2_v.T.astype(BF16))

    o_nsa_t = _nsa(qt, qrt, kcmp, vcmpt, ksh, vst, kwh, vwt, gates, gnt,
                   _cmp_to_sel_t(nch, n_sel), batch=batch, seq=seq, tq=tq)

    z = jnp.zeros((DECAY_RANK, RWKV_WIDTH), F32)
    wab = jnp.concatenate([jnp.concatenate([decay_up, z], axis=1),
                           jnp.concatenate([z, iclr_up], axis=1)], axis=0).astype(BF16)
    hid = np.arange(2 * LANE) // HEAD_DIM
    ones_bd = jnp.asarray(hid[:, None] == hid[None, :], BF16)
    ti = np.arange(tt)
    tril = jnp.asarray((ti[:, None] >= ti[None, :]) & (ti[:, None] // RWKV_CHUNK == ti[None, :] // RWKV_CHUNK), BF16)
    o_rwkv = _rwkv(rw, gr, rowv(shift_mu), rowv(decay_w0), rowv(iclr_a0), wab, rowv(k_k), rowv(k_a),
                   rowv(r_k), rowv(gn_w), rowv(gn_b), ones_bd, tril, batch=batch, seq=seq, tt=tt)

    w_o = w_out.astype(BF16)
    return _out_proj(x2, o_nsa_t, o_rwkv, w_o[:NSA_WIDTH], w_o[NSA_WIDTH:], rowv(final_g), tm=4 * tm, n_sub=4)


def kernel(x, norm_g, w_in, cmp_pos_k, cmp_w1_k, cmp_w2_k, cmp_pos_v, cmp_w1_v, cmp_w2_v, shift_mu, decay_w0, decay_up, iclr_a0, iclr_up, k_k, k_a, r_k, gn_w, gn_b, w_out, final_g):
    batch, seq, d = x.shape
    assert d == D_MODEL and norm_g.shape[0] == 1, "single-layer trunk"
    out = _layer(x.reshape(batch * seq, d), norm_g[0], w_in[0], cmp_pos_k[0], cmp_w1_k[0], cmp_w2_k[0],
                 cmp_pos_v[0], cmp_w1_v[0], cmp_w2_v[0], shift_mu[0], decay_w0[0], decay_up[0],
                 iclr_a0[0], iclr_up[0], k_k[0], k_a[0], r_k[0], gn_w[0], gn_b[0], w_out[0],
                 final_g, batch=batch, seq=seq)
    return out.reshape(batch, seq, d)
```

```python
import functools

import numpy as np
import jax
import jax.numpy as jnp
from jax import lax
from jax.experimental import pallas as pl
from jax.experimental.pallas import tpu as pltpu

F32 = jnp.float32
BF16 = jnp.bfloat16

D_MODEL = 1024
HEAD_DIM = 64
NSA_HEADS = 8
NSA_KV = 2
NSA_REP = NSA_HEADS // NSA_KV
RWKV_HEADS = 8
NSA_WIDTH = NSA_HEADS * HEAD_DIM
RWKV_WIDTH = RWKV_HEADS * HEAD_DIM
KV_WIDTH = NSA_KV * HEAD_DIM
ROPE_DIM = HEAD_DIM // 4
ROPE_HALF = ROPE_DIM // 2
ROPE_THETA = 500000.0
CMP_BLOCK = 32
CMP_STRIDE = 16
CMP_HIDDEN = 256
SEL_BLOCK = 64
SEL_TOPK = 8
WINDOW = 512
DECAY_RANK = 64
ICLR_RANK = 64
RWKV_SHIFT_WIDTH = 3 * RWKV_WIDTH + DECAY_RANK + ICLR_RANK
IN_SIZES = (NSA_WIDTH, KV_WIDTH, KV_WIDTH, KV_WIDTH, KV_WIDTH, KV_WIDTH, KV_WIDTH,
            3 * NSA_HEADS, NSA_WIDTH, RWKV_SHIFT_WIDTH, RWKV_WIDTH)
SCALE = HEAD_DIM ** -0.5
RMS_EPS = 1e-6
GN_EPS = 64e-5
NEG_INF = -1e30
FORCE_BONUS = 1e3

LANE = 128
SUBLANE = 8
BF16_SUBLANE = 2 * SUBLANE
GATE_ROWS = 2 * BF16_SUBLANE

T_KV = 0
T_RW = T_KV + 4 * KV_WIDTH
T_GR = T_RW + RWKV_SHIFT_WIDTH
T_END = T_GR + RWKV_WIDTH
R_Q = 0
R_VS = R_Q + NSA_WIDTH
R_VW = R_VS + KV_WIDTH
R_GL = R_VW + KV_WIDTH
R_GN = R_GL + GATE_ROWS
R_END = R_GN + NSA_WIDTH

LOG2E = float(np.log2(np.e))
K_AUG = LANE
V_ROWS = HEAD_DIM + BF16_SUBLANE
KEY_TILE = 256
RWKV_CHUNK = 64
VMEM_LIMIT = 48 * 1024 * 1024


def _dot(a, b):
    return jnp.dot(a, b, preferred_element_type=F32)


def _dot_nt(a, b):
    return lax.dot_general(a, b, (((1,), (1,)), ((), ())), preferred_element_type=F32)


def _dot_tn(a, b):
    return lax.dot_general(a, b, (((0,), (0,)), ((), ())), preferred_element_type=F32)


def _bmm(a, b):
    return lax.dot_general(a, b, (((2,), (1,)), ((0,), (0,))), preferred_element_type=F32)


def _bmm_nt(a, b):
    return lax.dot_general(a, b, (((2,), (2,)), ((0,), (0,))), preferred_element_type=F32)


def _split3(x):
    hi = x.astype(BF16)
    r1 = x - hi.astype(F32)
    mid = r1.astype(BF16)
    lo = (r1 - mid.astype(F32)).astype(BF16)
    return hi, mid, lo


def _head_sums(x, ones_blk):
    w = ones_blk.shape[0]
    xb = x.astype(BF16)
    return jnp.concatenate([_dot(xb[:, i:i + w], ones_blk) for i in range(0, x.shape[1], w)], axis=1)


def _x2_dot(ww_bf16, x):
    hi = x.astype(BF16)
    lo = (x - hi.astype(F32)).astype(BF16)
    return _dot(ww_bf16, jnp.concatenate([hi, lo], axis=0))


def _x3_dot(w_bf16, x):
    hi, mid, lo = _split3(x)
    return _dot(w_bf16, hi) + _dot(w_bf16, mid) + _dot(w_bf16, lo)


def _sigmoid(x):
    return 0.5 * jnp.tanh(0.5 * x) + 0.5


def _interleave(*gens):
    live = list(gens)
    while live:
        for g in list(live):
            try:
                next(g)
            except StopIteration:
                live.remove(g)


def _rope128(t, ra, rm, rp):
    return t * ra + pltpu.roll(t, LANE - ROPE_HALF, 1) * rm + pltpu.roll(t, ROPE_HALF, 1) * rp


def _in_proj_kernel(x_ref, g_ref, wt_ref, wf_ref, ra_ref, rm_ref, rp_ref, cos_ref, sin_ref, oh_ref,
                    kc_ref, vc_ref, ks_ref, kw_ref, rw_ref, gr_ref,
                    qt_ref, qrt_ref, vst_ref, vwt_ref, gate_ref, gn_ref, cmp_scr):
    x = x_ref[...]
    ms = jnp.mean(x * x, axis=-1, keepdims=True)
    y = (x * lax.rsqrt(ms + RMS_EPS) * g_ref[...]).astype(BF16)
    tm = x.shape[0]

    kv = _dot_nt(y, wt_ref[T_KV:T_RW, :])
    for i, ref in enumerate((kc_ref, vc_ref)):
        cmp_scr[i] = kv[:, i * KV_WIDTH:(i + 1) * KV_WIDTH]
        for tau in range(CMP_STRIDE):
            piece = cmp_scr[i, pl.ds(tau, tm // CMP_STRIDE, stride=CMP_STRIDE), :].astype(BF16)
            for g in range(NSA_KV):
                ref[g, :, tau * HEAD_DIM:(tau + 1) * HEAD_DIM] = piece[:, g * HEAD_DIM:(g + 1) * HEAD_DIM]
    ra, rm, rp = ra_ref[...], rm_ref[...], rp_ref[...]
    for i, ref in ((2, ks_ref), (3, kw_ref)):
        t = _rope128(kv[:, i * LANE:(i + 1) * LANE], ra, rm, rp)
        for g in range(NSA_KV):
            tg = t[:, g * HEAD_DIM:(g + 1) * HEAD_DIM].astype(BF16)
            if ref is ks_ref:
                ref[g] = jnp.concatenate([tg, oh_ref[...]], axis=1)
            else:
                ref[g] = tg
    rw_ref[...] = _dot_nt(y, wt_ref[T_RW:T_GR, :])
    gr = _dot_nt(y, wt_ref[T_GR:T_END, :])
    gr_ref[...] = gr * _sigmoid(gr)

    def proj_t(r0, r1):
        return _dot_nt(wf_ref[r0:r1, :], y)

    qt = proj_t(R_Q, R_VS) * (SCALE * LOG2E)
    cos, sin = cos_ref[...], sin_ref[...]
    qt_ref[...] = qt.astype(BF16)
    for h in range(NSA_HEADS):
        r0 = h * HEAD_DIM
        t1 = qt[r0:r0 + ROPE_HALF]
        t2 = qt[r0 + ROPE_HALF:r0 + ROPE_DIM]
        qrt_ref[r0:r0 + ROPE_DIM, :] = jnp.concatenate(
            [t1 * cos - t2 * sin, t2 * cos + t1 * sin], axis=0).astype(BF16)
        qrt_ref[r0 + ROPE_DIM:r0 + HEAD_DIM, :] = qt[r0 + ROPE_DIM:r0 + HEAD_DIM].astype(BF16)
    vt = proj_t(R_VS, R_GL).astype(BF16)
    ones = jnp.ones((V_ROWS - HEAD_DIM, LANE), BF16)
    for j in range(vt.shape[1] // LANE):
        for i, ref in enumerate((vst_ref, vwt_ref)):
            for g in range(NSA_KV):
                r0 = i * KV_WIDTH + g * HEAD_DIM
                ref[j, g * V_ROWS:(g + 1) * V_ROWS, :] = jnp.concatenate(
                    [vt[r0:r0 + HEAD_DIM, j * LANE:(j + 1) * LANE], ones], axis=0)
    gate_ref[...] = _sigmoid(proj_t(R_GL, R_GN))
    gn = proj_t(R_GN, R_END)
    gn_ref[...] = gn * _sigmoid(gn)


def _in_proj(x2, norm_g, w_t, w_f, tabs, *, seq, tm):
    n = x2.shape[0]
    spt = seq // tm
    ra, rm, rp, cos, sin, onehot = tabs
    hm = lambda w: jax.ShapeDtypeStruct((NSA_KV, n, w), BF16)
    hspec = lambda w: pl.BlockSpec((NSA_KV, tm, w), lambda i: (0, i, 0))
    row = lambda w: pl.BlockSpec((tm, w), lambda i: (i, 0))
    col = lambda r: pl.BlockSpec((r, tm), lambda i: (0, i))
    full = lambda a: pl.BlockSpec(a.shape, lambda i: (0,) * a.ndim)
    tab = lambda w: pl.BlockSpec((tm, w), lambda i: (i % spt, 0))
    tabt = pl.BlockSpec((ROPE_HALF, tm), lambda i: (0, i % spt))
    vtile = pl.BlockSpec((tm // LANE, NSA_KV * V_ROWS, LANE), lambda i: (i, 0, 0))
    vsd = jax.ShapeDtypeStruct((n // LANE, NSA_KV * V_ROWS, LANE), BF16)
    cw = CMP_STRIDE * HEAD_DIM
    cspec = pl.BlockSpec((NSA_KV, tm // CMP_STRIDE, cw), lambda i: (0, i, 0))
    csd = jax.ShapeDtypeStruct((NSA_KV, n // CMP_STRIDE, cw), BF16)
    return pl.pallas_call(
        _in_proj_kernel,
        grid=(n // tm,),
        in_specs=[row(D_MODEL), full(norm_g), full(w_t), full(w_f), tab(LANE), tab(LANE), tab(LANE),
                  tabt, tabt, tab(K_AUG - HEAD_DIM)],
        out_specs=[cspec, cspec, hspec(K_AUG), hspec(HEAD_DIM), row(RWKV_SHIFT_WIDTH), row(RWKV_WIDTH),
                   col(NSA_WIDTH), col(NSA_WIDTH), vtile, vtile,
                   col(GATE_ROWS), col(NSA_WIDTH)],
        out_shape=[csd, csd, hm(K_AUG), hm(HEAD_DIM),
                   jax.ShapeDtypeStruct((n, RWKV_SHIFT_WIDTH), F32),
                   jax.ShapeDtypeStruct((n, RWKV_WIDTH), F32),
                   jax.ShapeDtypeStruct((NSA_WIDTH, n), BF16),
                   jax.ShapeDtypeStruct((NSA_WIDTH, n), BF16),
                   vsd, vsd,
                   jax.ShapeDtypeStruct((GATE_ROWS, n), F32),
                   jax.ShapeDtypeStruct((NSA_WIDTH, n), F32)],
        scratch_shapes=[pltpu.VMEM((2, tm, KV_WIDTH), F32)],
        compiler_params=pltpu.CompilerParams(dimension_semantics=("arbitrary",),
                                             vmem_limit_bytes=VMEM_LIMIT),
        name="in_proj",
    )(x2, norm_g, w_t, w_f, ra, rm, rp, cos, sin, onehot)


def _compress_kernel(kc_ref, vc_ref, pk_ref, w1k_ref, w2k_ref, pv_ref, w1v_ref, w2vt_ref,
                     ko_ref, vo_ref):
    half = CMP_STRIDE * HEAD_DIM
    ng, _, nch, _ = kc_ref.shape

    def hidden(c_ref, pos_ref, w1_ref):
        c = c_ref[:, 0].reshape(ng * nch, half)
        pos = jnp.broadcast_to(pos_ref[...], (BF16_SUBLANE, 2 * half)).astype(BF16)
        za = _dot(jnp.concatenate([c, pos[:, 0:half]], axis=0), w1_ref[0:half, :])
        zb = _dot(jnp.concatenate([c, pos[:, half:]], axis=0), w1_ref[half:2 * half, :])
        pv = za[ng * nch:ng * nch + 1] + zb[ng * nch:ng * nch + 1]
        hid = jnp.concatenate(
            [za[g * nch:(g + 1) * nch] + pltpu.roll(zb[g * nch:(g + 1) * nch], nch - 1, 0) for g in range(ng)],
            axis=0) + pv
        return (hid * _sigmoid(hid)).astype(BF16)

    ko = _dot(hidden(kc_ref, pk_ref, w1k_ref), w2k_ref[...]).astype(BF16)
    hv = hidden(vc_ref, pv_ref, w1v_ref)
    for g in range(ng):
        ko_ref[g, 0] = ko[g * nch:(g + 1) * nch]
        vo_ref[g, 0] = _dot_nt(w2vt_ref[...], hv[g * nch:(g + 1) * nch]).astype(BF16)


def _compress(kc_r, vc_r, pk, w1k, w2k, pv, w1v, w2vt):
    g, b, nch, width = kc_r.shape
    blk = pl.BlockSpec((g, 1, nch, width), lambda j: (0, j, 0, 0))
    full = lambda a: pl.BlockSpec(a.shape, lambda j: (0,) * a.ndim)
    return pl.pallas_call(
        _compress_kernel,
        grid=(b,),
        in_specs=[blk, blk, full(pk), full(w1k), full(w2k), full(pv), full(w1v), full(w2vt)],
        out_specs=[pl.BlockSpec((g, 1, nch, HEAD_DIM), lambda j: (0, j, 0, 0)),
                   pl.BlockSpec((g, 1, HEAD_DIM, nch), lambda j: (0, j, 0, 0))],
        out_shape=[jax.ShapeDtypeStruct((g, b, nch, HEAD_DIM), BF16),
                   jax.ShapeDtypeStruct((g, b, HEAD_DIM, nch), BF16)],
        compiler_params=pltpu.CompilerParams(dimension_semantics=("arbitrary",),
                                             vmem_limit_bytes=VMEM_LIMIT),
        name="compress",
    )(kc_r, vc_r, pk, w1k, w2k, pv, w1v, w2vt)


def _nsa_kernel(qt_ref, qrt_ref, kc_ref, vct_ref, ks_ref, vst_ref, kw_ref, vwt_ref,
                gate_ref, gn_ref, mt_ref, o_ref, *, tq, seq):
    tk = KEY_TILE
    n_win = WINDOW // tk
    qi = pl.program_id(1)
    q0 = qi * tq
    nl = NSA_REP * tq
    n_sel = seq // SEL_BLOCK
    ncp = kc_ref.shape[2]
    groups = range(NSA_KV)

    def heads_on_lanes(ref, g):
        rows = [(g * NSA_REP + r) * HEAD_DIM for r in range(NSA_REP)]
        return jnp.concatenate([ref[r0:r0 + HEAD_DIM, :] for r0 in rows], axis=1)

    def tile4(a):
        return jnp.concatenate([a] * NSA_REP, axis=1)

    k_s = lax.broadcasted_iota(jnp.int32, (tk, tq), 0)
    t_l = q0 + lax.broadcasted_iota(jnp.int32, (tk, tq), 1)

    def update_steps(box, k_ref, vt_ref, g, q_op, kts, keeps):
        m_i, acc = box[0]

        def scores(i):
            sc = _dot(k_ref[g, pl.ds(pl.multiple_of(kts[i] * tk, tk), tk), :], q_op)
            if keeps[i] is None:
                return sc
            return jnp.concatenate([jnp.where(keeps[i], sc[:, r * tq:(r + 1) * tq], NEG_INF)
                                    for r in range(NSA_REP)], axis=1)

        sc_next = scores(0)
        yield
        for i, kt in enumerate(kts):
            sc = sc_next
            if i + 1 < len(kts):
                sc_next = scores(i + 1)
            m_n = jnp.maximum(m_i, jnp.max(sc, axis=0, keepdims=True))
            pe = jnp.exp2(sc - m_n).astype(BF16)
            yield
            vtb = jnp.concatenate([vt_ref[kt * (tk // LANE) + jj, g * V_ROWS:(g + 1) * V_ROWS, :]
                                   for jj in range(tk // LANE)], axis=1)
            acc = jnp.exp2(m_i - m_n) * acc + _dot(vtb, pe)
            m_i = m_n
            yield
        box[0] = (m_i, acc)

    def last_tiles(box, k_ref, vt_ref, g, q_op, n_back, low_keep):
        kts, keeps = [], []
        for back in range(n_back + 1):
            d = t_l - ((a - back) * tk + k_s)
            kts.append(a - back)
            if back == 0:
                keeps.append(d >= 0)
            elif back == n_win and low_keep is not None:
                keeps.append(low_keep(d))
            else:
                keeps.append(None)
        return update_steps(box, k_ref, vt_ref, g, q_op, kts, keeps)

    def select_steps(g, qr, sel_out, all_selected):
        s = _dot(kc_ref[g, 0], heads_on_lanes(qt_ref, g))
        t_c = q0 + lax.broadcasted_iota(jnp.int32, (ncp, tq), 1)
        c_c = lax.broadcasted_iota(jnp.int32, (ncp, tq), 0)
        cmask = tile4((c_c * CMP_STRIDE + (CMP_BLOCK - 1)) <= t_c)
        yield
        s = jnp.where(cmask, s, NEG_INF)
        m = jnp.maximum(jnp.max(s, axis=0, keepdims=True), 0.5 * NEG_INF)
        e = jnp.exp2(s - m)
        den = jnp.sum(e, axis=0, keepdims=True)
        p = e * (1.0 / jnp.where(den > 0.0, den, 1.0))
        sel_out["o_cmp"] = _dot(vct_ref[g, 0], p.astype(BF16))
        yield
        if all_selected:
            sel_out["qr_sel"] = jnp.concatenate([qr, jnp.zeros((K_AUG - HEAD_DIM, nl), BF16)], axis=0)
            return
        psum = p[:, 0:tq]
        for r in range(1, NSA_REP):
            psum = psum + p[:, r * tq:(r + 1) * tq]
        imp = _x3_dot(mt_ref[...], psum)
        j = lax.broadcasted_iota(jnp.int32, (n_sel, tq), 0)
        t = q0 + lax.broadcasted_iota(jnp.int32, (n_sel, tq), 1)
        tb = t // SEL_BLOCK
        forced = (j == 0) | (j == tb) | (j == tb - 1)
        val = jnp.where(j <= tb, imp + jnp.where(forced, FORCE_BONUS, 0.0), -1.0)
        yield
        vals = [val[g0:g0 + SUBLANE] for g0 in range(0, n_sel, SUBLANE)]
        cnts = [jnp.zeros((SUBLANE, tq), F32) for _ in vals]
        srow = lax.broadcasted_iota(jnp.int32, (SUBLANE, tq), 0)
        for i in range(n_sel):
            vi = jnp.broadcast_to(val[i:i + 1, :], (SUBLANE, tq))
            for g, vg in enumerate(vals):
                ge = lambda: jnp.where(vi >= vg, 1.0, 0.0)
                gt = lambda: jnp.where(vi > vg, 1.0, 0.0)
                if g * SUBLANE > i:
                    beat = ge()
                elif (g + 1) * SUBLANE <= i:
                    beat = gt()
                else:
                    beat = jnp.where(srow > i - g * SUBLANE, ge(), gt())
                cnts[g] = cnts[g] + beat
            if i % SUBLANE == SUBLANE - 1:
                yield
        cnt = jnp.concatenate(cnts, axis=0)
        sel_bias = jnp.where(cnt < float(SEL_TOPK), 0.0, NEG_INF).astype(BF16)
        sel_out["qr_sel"] = jnp.concatenate(
            [qr, tile4(sel_bias), jnp.zeros((K_AUG - HEAD_DIM - n_sel, nl), BF16)], axis=0)

    init = (jnp.full((1, nl), NEG_INF, F32), jnp.zeros((V_ROWS, nl), F32))
    a = (q0 + tq - 1) // tk

    def attend(n_back):
        win_boxes, sel_boxes, sel_outs = [[init] for _ in groups], [[init] for _ in groups], [{} for _ in groups]
        all_selected = n_back < n_win and (n_back + 1) * tk <= SEL_TOPK * SEL_BLOCK
        for g in groups:
            qr = heads_on_lanes(qrt_ref, g)
            _interleave(last_tiles(win_boxes[g], kw_ref, vwt_ref, g, qr, n_back, lambda d: d < WINDOW),
                        select_steps(g, qr, sel_outs[g], all_selected))
        _interleave(*[last_tiles(sel_boxes[g], ks_ref, vst_ref, g, sel_outs[g]["qr_sel"], n_back, None)
                      for g in groups])
        c_wins = [b[0] for b in win_boxes]
        c_sels = [b[0] for b in sel_boxes]
        o_cmps = [o["o_cmp"] for o in sel_outs]
        qr_sels = [o["qr_sel"] for o in sel_outs]

        def old_tiles(kts):
            def body(carry):
                boxes = [[c] for c in carry]
                _interleave(*[update_steps(boxes[g], ks_ref, vst_ref, g, qr_sels[g], kts, [None] * len(kts))
                              for g in groups])
                return tuple(b[0] for b in boxes)
            return body

        if n_back == n_win:
            n_old = a - n_win
            c_sels = lax.fori_loop(0, n_old % 2, lambda kt, c: old_tiles([kt])(c), tuple(c_sels))
            c_sels = lax.fori_loop(0, n_old // 2,
                                   lambda i, c: old_tiles([n_old % 2 + 2 * i, n_old % 2 + 2 * i + 1])(c), c_sels)

        for g in groups:
            o_sel = c_sels[g][1][0:HEAD_DIM] * (1.0 / c_sels[g][1][HEAD_DIM:HEAD_DIM + 1])
            o_win = c_wins[g][1][0:HEAD_DIM] * (1.0 / c_wins[g][1][HEAD_DIM:HEAD_DIM + 1])
            for r in range(NSA_REP):
                ls = slice(r * tq, (r + 1) * tq)
                h = g * NSA_REP + r
                gate = lambda c: gate_ref[3 * h + c:3 * h + c + 1, :]
                o = gate(0) * o_cmps[g][:, ls] + gate(1) * o_sel[:, ls] + gate(2) * o_win[:, ls]
                rs = slice(h * HEAD_DIM, (h + 1) * HEAD_DIM)
                o_ref[rs, :] = (o * gn_ref[rs, :]).astype(BF16)

    def dispatch(n_back):
        if n_back == 0:
            attend(0)
        else:
            lax.cond(a >= n_back, lambda: attend(n_back), lambda: dispatch(n_back - 1))

    dispatch(n_win)


def _nsa(qt, qrt, kcmp, vcmpt, ksh, vst, kwh, vwt, gates, gnt, mt, *, batch, seq, tq):
    n = batch * seq
    nq = seq // tq
    ncp = kcmp.shape[2]
    qspec = pl.BlockSpec((NSA_WIDTH, tq), lambda b, i: (0, b * nq + i))
    kspec = lambda w: pl.BlockSpec((NSA_KV, seq, w), lambda b, i: (0, b, 0))
    vspec = pl.BlockSpec((seq // LANE, NSA_KV * V_ROWS, LANE), lambda b, i: (b, 0, 0))
    kern = functools.partial(_nsa_kernel, tq=tq, seq=seq)
    return pl.pallas_call(
        kern,
        grid=(batch, nq),
        in_specs=[qspec, qspec,
                  pl.BlockSpec((NSA_KV, 1, ncp, HEAD_DIM), lambda b, i: (0, b, 0, 0)),
                  pl.BlockSpec((NSA_KV, 1, HEAD_DIM, ncp), lambda b, i: (0, b, 0, 0)),
                  kspec(K_AUG), vspec, kspec(HEAD_DIM), vspec,
                  pl.BlockSpec((GATE_ROWS, tq), lambda b, i: (0, b * nq + i)),
                  qspec,
                  pl.BlockSpec(mt.shape, lambda b, i: (0, 0))],
        out_specs=qspec,
        out_shape=jax.ShapeDtypeStruct((NSA_WIDTH, n), BF16),
        compiler_params=pltpu.CompilerParams(
            dimension_semantics=("arbitrary", "arbitrary"),
            vmem_limit_bytes=VMEM_LIMIT),
        name="nsa",
    )(qt, qrt, kcmp, vcmpt, ksh, vst, kwh, vwt, gates, gnt, mt)


def _rwkv_kernel(p_ref, gr_ref, mu_ref, w0_ref, a0_ref, wab_ref, kk_ref, ka_ref, rk_ref,
                 gw_ref, gb_ref, ones_ref, tril_ref, o_ref, st_ref, carry_ref, *, tt):
    c = RWKV_CHUNK
    hd = HEAD_DIM
    nc = tt // c
    step = pl.program_id(1)

    @pl.when(step == 0)
    def _():
        st_ref[...] = jnp.zeros_like(st_ref)
        carry_ref[...] = jnp.zeros_like(carry_ref)

    p = p_ref[...]
    row = lax.broadcasted_iota(jnp.int32, p.shape, 0)
    prev = jnp.where(row == 0, carry_ref[...], pltpu.roll(p, 1, 0))
    carry_ref[...] = p[tt - 1:tt, :]
    ps = p + mu_ref[...] * (prev - p)
    r = ps[:, 0:RWKV_WIDTH]
    k = ps[:, RWKV_WIDTH:2 * RWKV_WIDTH]
    v = ps[:, 2 * RWKV_WIDTH:3 * RWKV_WIDTH]
    lora = ps[:, 3 * RWKV_WIDTH:]
    lane = lax.broadcasted_iota(jnp.int32, lora.shape, 1)
    feat = jnp.where(lane < DECAY_RANK, jnp.tanh(lora), lora).astype(BF16)
    up = _dot(feat, wab_ref[...])
    w = w0_ref[...] + up[:, 0:RWKV_WIDTH]
    lw = _sigmoid(w) * (-float(np.exp(-0.5)))
    a = _sigmoid(a0_ref[...] + up[:, RWKV_WIDTH:])
    ones_bd = ones_ref[...]
    kk = k * kk_ref[...]
    kkn = kk * lax.rsqrt(jnp.maximum(_head_sums(kk * kk, ones_bd), 1e-24))
    k2 = k * (1.0 + (a - 1.0) * ka_ref[...])
    alpha = -kkn
    beta = kkn * a
    bonus = _head_sums(r * k2 * rk_ref[...], ones_bd) * v

    cum = _x2_dot(tril_ref[...], lw)
    cend = jnp.concatenate(
        [jnp.broadcast_to(cum[(ch + 1) * c - 1:(ch + 1) * c, :], (c, RWKV_WIDTH)) for ch in range(nc)], axis=0)
    e_neg = jnp.exp(-cum)
    pc = jnp.exp(cend)
    at = alpha * jnp.exp(cum - lw)
    bt = beta * e_neg
    kt = k2 * e_neg
    rt = r * jnp.exp(cum)
    bh = bt * pc
    kh = kt * pc

    npair = RWKV_WIDTH // LANE

    def pairs(x):
        return jnp.stack([x[ch * c:(ch + 1) * c, j * LANE:(j + 1) * LANE]
                          for ch in range(nc) for j in range(npair)], axis=0)

    at_p, rt_p, bt_p, kt_p, v_p, bh_p, kh_p = (pairs(t) for t in (at, rt, bt, kt, v, bh, kh))
    pc_p = jnp.stack([pc[ch * c:ch * c + 1, j * LANE:(j + 1) * LANE]
                      for ch in range(nc) for j in range(npair)], axis=0)
    lane_c = lax.broadcasted_iota(jnp.int32, (1, c, LANE), 2)
    row_c = lax.broadcasted_iota(jnp.int32, (1, c, LANE), 1)
    even_c = lane_c < hd
    col_c = jnp.where(even_c, lane_c, lane_c - hd)
    low_s = row_c > col_c
    low_i = row_c >= col_c
    lane_2c = lax.broadcasted_iota(jnp.int32, (1, 2 * c, LANE), 2)
    row_2c = lax.broadcasted_iota(jnp.int32, (1, 2 * c, LANE), 1)
    even_2c = lane_2c < hd
    on_bd = (row_2c < hd) == even_2c
    zero_c = jnp.zeros((1, c, LANE), BF16)

    def bd(x):
        xb = x.astype(BF16)
        return jnp.concatenate([jnp.where(even_c, xb, zero_c), jnp.where(even_c, zero_c, xb)], axis=1)

    def abd(x):
        xb = x.astype(BF16)
        return jnp.concatenate([jnp.where(even_c, zero_c, xb), jnp.where(even_c, xb, zero_c)], axis=1)

    la = jnp.concatenate([at_p, rt_p], axis=1).astype(BF16)
    zero_2c = jnp.zeros((1, 2 * c, LANE), BF16)
    r_e = _bmm_nt(jnp.where(even_2c, la, zero_2c), jnp.concatenate([bt_p, kt_p], axis=1).astype(BF16))
    r_o = _bmm_nt(jnp.where(even_2c, zero_2c, la), jnp.concatenate([kt_p, bt_p], axis=1).astype(BF16))
    nab = jnp.where(low_s, jnp.where(even_c, r_e[:, 0:c], r_o[:, 0:c]), 0.0)
    aak_sw = jnp.where(low_s, jnp.where(even_c, r_o[:, 0:c], r_e[:, 0:c]), 0.0).astype(BF16)
    arb = jnp.where(low_i, jnp.where(even_c, r_e[:, c:], r_o[:, c:]), 0.0).astype(BF16)
    ark_sw = jnp.where(low_i, jnp.where(even_c, r_o[:, c:], r_e[:, c:]), 0.0).astype(BF16)
    tinv = jnp.where(row_c == col_c, 1.0, 0.0) + nab
    npow = _bmm(nab.astype(BF16), bd(nab))
    n_dbl = 5
    for it in range(n_dbl):
        nbd = bd(npow)
        if it + 1 < n_dbl:
            res = _bmm(jnp.concatenate([tinv, npow], axis=1).astype(BF16), nbd)
            tinv = tinv + res[:, 0:c]
            npow = res[:, c:]
        else:
            tinv = tinv + _bmm(tinv.astype(BF16), nbd)
    av = _bmm(jnp.concatenate([aak_sw, ark_sw], axis=1), abd(v_p))
    tx = _bmm(tinv.astype(BF16), jnp.concatenate([bd(av[:, 0:c]), bd(at_p)], axis=2))
    u0, ta = tx[:, :, 0:LANE], tx[:, :, LANE:]
    ax = _bmm(arb, jnp.concatenate([bd(ta), bd(u0)], axis=2))
    rq = rt_p + ax[:, :, 0:LANE]
    y0 = ax[:, :, LANE:] + av[:, c:]
    v_b = v_p.astype(BF16)
    w_f = jnp.concatenate([jnp.concatenate([ta, u0], axis=2).astype(BF16),
                           jnp.concatenate([jnp.zeros_like(v_b), v_b], axis=2)], axis=1)
    gh = _bmm(jnp.concatenate([jnp.swapaxes(bh_p, 1, 2), jnp.swapaxes(kh_p, 1, 2)], axis=2).astype(BF16), w_f)
    g_bd = jnp.where(on_bd, gh[:, :, 0:LANE], 0.0) + jnp.where(row_2c == lane_2c, pc_p, 0.0)
    h_bd = jnp.where(on_bd, gh[:, :, LANE:], 0.0)
    lhs = jnp.concatenate([rq, g_bd], axis=1).astype(BF16)

    st = st_ref[...]
    ys = []
    for ch in range(nc):
        sl = slice(ch * npair, (ch + 1) * npair)
        res = _bmm(lhs[sl], st.astype(BF16))
        yc = res[:, 0:c, :] + y0[sl]
        st = res[:, c:, :] + h_bd[sl]
        ys.append(jnp.concatenate([yc[j] for j in range(npair)], axis=1))
    st_ref[...] = st
    y = jnp.concatenate(ys, axis=0) if nc > 1 else ys[0]

    inv_hd = 1.0 / hd
    mean = _head_sums(y, ones_bd) * inv_hd
    ycen = y - mean
    var = _head_sums(ycen * ycen, ones_bd) * inv_hd
    yn = ycen * lax.rsqrt(var + GN_EPS) * gw_ref[...] + gb_ref[...]
    o_ref[...] = ((yn + bonus) * gr_ref[...]).astype(BF16)


def _rwkv(rw, gr, mu, w0, a0, wab, kk, ka, rk, gw, gb, ones_bd, tril, *, batch, seq, tt):
    n = batch * seq
    ns = seq // tt
    row = lambda w: pl.BlockSpec((tt, w), lambda b, i: (b * ns + i, 0))
    full = lambda a: pl.BlockSpec(a.shape, lambda b, i: (0,) * a.ndim)
    kern = functools.partial(_rwkv_kernel, tt=tt)
    consts = (mu, w0, a0, wab, kk, ka, rk, gw, gb, ones_bd, tril)
    return pl.pallas_call(
        kern,
        grid=(batch, ns),
        in_specs=[row(RWKV_SHIFT_WIDTH), row(RWKV_WIDTH)] + [full(a) for a in consts],
        out_specs=row(RWKV_WIDTH),
        out_shape=jax.ShapeDtypeStruct((n, RWKV_WIDTH), BF16),
        scratch_shapes=[pltpu.VMEM((RWKV_WIDTH // LANE, LANE, LANE), F32),
                        pltpu.VMEM((1, RWKV_SHIFT_WIDTH), F32)],
        compiler_params=pltpu.CompilerParams(dimension_semantics=("arbitrary", "arbitrary"),
                                             vmem_limit_bytes=VMEM_LIMIT),
        name="rwkv",
    )(rw, gr, *consts)


def _out_kernel(x_ref, ont_ref, or_ref, wn_ref, wr_ref, g_ref, o_ref, *, n_sub):
    ts = x_ref.shape[0] // n_sub

    def project(j):
        rows = slice(j * ts, (j + 1) * ts)
        return (x_ref[rows, :] + _dot_tn(ont_ref[:, rows], wn_ref[...])
                + _dot(or_ref[rows, :], wr_ref[...]))

    def finish(j, h):
        ms = jnp.mean(h * h, axis=-1, keepdims=True)
        o_ref[j * ts:(j + 1) * ts, :] = h * lax.rsqrt(ms + RMS_EPS) * g_ref[...]

    h = project(0)
    for j in range(1, n_sub):
        h_next = project(j)
        finish(j - 1, h)
        h = h_next
    finish(n_sub - 1, h)


def _out_proj(x2, o_nsa_t, o_rwkv, wn, wr, final_g, *, tm, n_sub):
    n = x2.shape[0]
    row = lambda w: pl.BlockSpec((tm, w), lambda i: (i, 0))
    full = lambda a: pl.BlockSpec(a.shape, lambda i: (0,) * a.ndim)
    return pl.pallas_call(
        functools.partial(_out_kernel, n_sub=n_sub),
        grid=(n // tm,),
        in_specs=[row(D_MODEL), pl.BlockSpec((NSA_WIDTH, tm), lambda i: (0, i)), row(RWKV_WIDTH),
                  full(wn), full(wr), full(final_g)],
        out_specs=row(D_MODEL),
        out_shape=jax.ShapeDtypeStruct((n, D_MODEL), F32),
        compiler_params=pltpu.CompilerParams(dimension_semantics=("arbitrary",),
                                             vmem_limit_bytes=VMEM_LIMIT),
        name="out_proj",
    )(x2, o_nsa_t, o_rwkv, wn, wr, final_g)


def _rope_tables(seq):
    inv = ROPE_THETA ** (-np.arange(ROPE_HALF, dtype=np.float64) / ROPE_HALF)
    ang = np.arange(seq, dtype=np.float64)[:, None] * inv[None, :]
    cos, sin = np.cos(ang), np.sin(ang)
    ra = np.ones((seq, HEAD_DIM)); rm = np.zeros((seq, HEAD_DIM)); rp = np.zeros((seq, HEAD_DIM))
    ra[:, :ROPE_HALF] = cos; ra[:, ROPE_HALF:ROPE_DIM] = cos
    rm[:, :ROPE_HALF] = -sin
    rp[:, ROPE_HALF:ROPE_DIM] = sin
    rep = lambda t: jnp.asarray(np.tile(t, (1, LANE // HEAD_DIM)), F32)
    assert seq // SEL_BLOCK <= K_AUG - HEAD_DIM
    onehot = np.zeros((seq, K_AUG - HEAD_DIM))
    onehot[np.arange(seq), np.arange(seq) // SEL_BLOCK] = 1.0
    return (rep(ra), rep(rm), rep(rp), jnp.asarray(cos.T, F32), jnp.asarray(sin.T, F32),
            jnp.asarray(onehot, BF16))


def _cmp_to_sel_t(n_cmp_pad, n_sel):
    n_cmp = n_cmp_pad - 1
    c0 = np.arange(n_cmp)[:, None] * CMP_STRIDE
    s0 = np.arange(n_sel)[None, :] * SEL_BLOCK
    ov = np.clip(np.minimum(c0 + CMP_BLOCK, s0 + SEL_BLOCK) - np.maximum(c0, s0), 0, None) / CMP_BLOCK
    mt = np.zeros((n_sel, n_cmp_pad))
    mt[:, :n_cmp] = ov.T
    return jnp.asarray(mt, BF16)


def _prep_w_in(w_in):
    idx = np.cumsum(IN_SIZES)[:-1].tolist()
    q, kc, vc, ks, vs, kw, vw, gl, gn, rw, gr = jnp.split(w_in, idx, axis=1)
    w_t = jnp.concatenate([kc, vc, ks, kw, rw, gr], axis=1).T.astype(BF16)
    pad = jnp.zeros((D_MODEL, GATE_ROWS - gl.shape[1]), w_in.dtype)
    w_f = jnp.concatenate([q, vs, vw, gl, pad, gn], axis=1).T.astype(BF16)
    return w_t, w_f


def _layer(x2, norm_g, w_in, cmp_pos_k, cmp_w1_k, cmp_w2_k, cmp_pos_v, cmp_w1_v, cmp_w2_v,
           shift_mu, decay_w0, decay_up, iclr_a0, iclr_up, k_k, k_a, r_k, gn_w, gn_b, w_out,
           final_g, *, batch, seq):
    tm = 256
    tq = 256
    tt = 256
    assert WINDOW % KEY_TILE == 0 and seq % KEY_TILE == 0 and KEY_TILE % tq == 0
    nch = seq // CMP_STRIDE
    n_sel = seq // SEL_BLOCK
    rowv = lambda t: t.reshape(1, -1).astype(F32)

    w_t, w_f = _prep_w_in(w_in)
    (kch, vch, ksh, kwh, rw, gr, qt, qrt, vst, vwt, gates, gnt) = _in_proj(
        x2, rowv(norm_g), w_t, w_f, _rope_tables(seq), seq=seq, tm=2 * tm)

    chunks = lambda t: t.reshape(NSA_KV, batch, nch, CMP_STRIDE * HEAD_DIM)
    kcmp, vcmpt = _compress(chunks(kch), chunks(vch),
                            rowv(cmp_pos_k), cmp_w1_k.astype(BF16), cmp_w2_k.astype(BF16),
                            rowv(cmp_pos_v), cmp_w1_v.astype(BF16), cmp_w2_v.T.astype(BF16))

    o_nsa_t = _nsa(qt, qrt, kcmp, vcmpt, ksh, vst, kwh, vwt, gates, gnt,
                   _cmp_to_sel_t(nch, n_sel), batch=batch, seq=seq, tq=tq)

    z = jnp.zeros((DECAY_RANK, RWKV_WIDTH), F32)
    wab = jnp.concatenate([jnp.concatenate([decay_up, z], axis=1),
                           jnp.concatenate([z, iclr_up], axis=1)], axis=0).astype(BF16)
    hid = np.arange(2 * LANE) // HEAD_DIM
    ones_bd = jnp.asarray(hid[:, None] == hid[None, :], BF16)
    ti = np.arange(tt)
    tril = (ti[:, None] >= ti[None, :]) & (ti[:, None] // RWKV_CHUNK == ti[None, :] // RWKV_CHUNK)
    tril = jnp.asarray(np.concatenate([tril, tril], axis=1), BF16)
    o_rwkv = _rwkv(rw, gr, rowv(shift_mu), rowv(decay_w0), rowv(iclr_a0), wab, rowv(k_k), rowv(k_a),
                   rowv(r_k), rowv(gn_w), rowv(gn_b), ones_bd, tril, batch=batch, seq=seq, tt=tt)

    w_o = w_out.astype(BF16)
    return _out_proj(x2, o_nsa_t, o_rwkv, w_o[:NSA_WIDTH], w_o[NSA_WIDTH:], rowv(final_g), tm=4 * tm, n_sub=4)


def kernel(x, norm_g, w_in, cmp_pos_k, cmp_w1_k, cmp_w2_k, cmp_pos_v, cmp_w1_v, cmp_w2_v, shift_mu, decay_w0, decay_up, iclr_a0, iclr_up, k_k, k_a, r_k, gn_w, gn_b, w_out, final_g):
    batch, seq, d = x.shape
    assert d == D_MODEL and norm_g.shape[0] == 1, "single-layer trunk"
    out = _layer(x.reshape(batch * seq, d), norm_g[0], w_in[0], cmp_pos_k[0], cmp_w1_k[0], cmp_w2_k[0],
                 cmp_pos_v[0], cmp_w1_v[0], cmp_w2_v[0], shift_mu[0], decay_w0[0], decay_up[0],
                 iclr_a0[0], iclr_up[0], k_k[0], k_a[0], r_k[0], gn_w[0], gn_b[0], w_out[0],
                 final_g, batch=batch, seq=seq)
    return out.reshape(batch, seq, d)
```

```python
import functools

import numpy as np
import jax
import jax.numpy as jnp
from jax import lax
from jax.experimental import pallas as pl
from jax.experimental.pallas import tpu as pltpu

F32 = jnp.float32
BF16 = jnp.bfloat16

D_MODEL = 1024
HEAD_DIM = 64
NSA_HEADS = 8
NSA_KV = 2
NSA_REP = NSA_HEADS // NSA_KV
RWKV_HEADS = 8
NSA_WIDTH = NSA_HEADS * HEAD_DIM
RWKV_WIDTH = RWKV_HEADS * HEAD_DIM
KV_WIDTH = NSA_KV * HEAD_DIM
ROPE_DIM = HEAD_DIM // 4
ROPE_HALF = ROPE_DIM // 2
ROPE_THETA = 500000.0
CMP_BLOCK = 32
CMP_STRIDE = 16
CMP_HIDDEN = 256
SEL_BLOCK = 64
SEL_TOPK = 8
WINDOW = 512
DECAY_RANK = 64
ICLR_RANK = 64
RWKV_SHIFT_WIDTH = 3 * RWKV_WIDTH + DECAY_RANK + ICLR_RANK
IN_SIZES = (NSA_WIDTH, KV_WIDTH, KV_WIDTH, KV_WIDTH, KV_WIDTH, KV_WIDTH, KV_WIDTH,
            3 * NSA_HEADS, NSA_WIDTH, RWKV_SHIFT_WIDTH, RWKV_WIDTH)
SCALE = HEAD_DIM ** -0.5
RMS_EPS = 1e-6
GN_EPS = 64e-5
NEG_INF = -1e30
FORCE_BONUS = 1e3

LANE = 128
SUBLANE = 8
BF16_SUBLANE = 2 * SUBLANE
GATE_ROWS = 2 * BF16_SUBLANE

T_KV = 0
T_RW = T_KV + 4 * KV_WIDTH
T_GR = T_RW + RWKV_SHIFT_WIDTH
T_END = T_GR + RWKV_WIDTH
R_Q = T_END
R_VS = R_Q + NSA_WIDTH
R_VW = R_VS + KV_WIDTH
R_GL = R_VW + KV_WIDTH
R_GN = R_GL + GATE_ROWS
R_END = R_GN + NSA_WIDTH

LOG2E = float(np.log2(np.e))
K_AUG = LANE
V_ROWS = HEAD_DIM + BF16_SUBLANE
KEY_TILE = 256
RWKV_CHUNK = 64
VMEM_LIMIT = 48 * 1024 * 1024


def _dot(a, b):
    return jnp.dot(a, b, preferred_element_type=F32)


def _dot_nt(a, b):
    return lax.dot_general(a, b, (((1,), (1,)), ((), ())), preferred_element_type=F32)


def _dot_tn(a, b):
    return lax.dot_general(a, b, (((0,), (0,)), ((), ())), preferred_element_type=F32)


def _bmm(a, b):
    return lax.dot_general(a, b, (((2,), (1,)), ((0,), (0,))), preferred_element_type=F32)


def _bmm_nt(a, b):
    return lax.dot_general(a, b, (((2,), (2,)), ((0,), (0,))), preferred_element_type=F32)


def _split3(x):
    hi = x.astype(BF16)
    r1 = x - hi.astype(F32)
    mid = r1.astype(BF16)
    lo = (r1 - mid.astype(F32)).astype(BF16)
    return hi, mid, lo


def _head_sums(x, ones_blk):
    w = ones_blk.shape[0]
    xb = x.astype(BF16)
    return jnp.concatenate([_dot(xb[:, i:i + w], ones_blk) for i in range(0, x.shape[1], w)], axis=1)


def _x2_dot(ww_bf16, x):
    hi = x.astype(BF16)
    lo = (x - hi.astype(F32)).astype(BF16)
    return _dot(ww_bf16, jnp.concatenate([hi, lo], axis=0))


def _x3_dot(w_bf16, x):
    hi, mid, lo = _split3(x)
    return _dot(w_bf16, hi) + _dot(w_bf16, mid) + _dot(w_bf16, lo)


def _sigmoid(x):
    return 0.5 * jnp.tanh(0.5 * x) + 0.5


def _interleave(*gens):
    live = list(gens)
    while live:
        for g in list(live):
            try:
                next(g)
            except StopIteration:
                live.remove(g)


def _rope128(t, ra, rm, rp):
    return t * ra + pltpu.roll(t, LANE - ROPE_HALF, 1) * rm + pltpu.roll(t, ROPE_HALF, 1) * rp


def _in_proj_kernel(x_ref, g_ref, w_ref, ra_ref, rm_ref, rp_ref, cos_ref, sin_ref, oh_ref,
                    kc_ref, vc_ref, ks_ref, kw_ref, rw_ref, gr_ref,
                    qt_ref, qrt_ref, vst_ref, vwt_ref, gate_ref, gn_ref, cmp_scr):
    x = x_ref[...]
    ms = jnp.mean(x * x, axis=-1, keepdims=True)
    y = (x * lax.rsqrt(ms + RMS_EPS) * g_ref[...]).astype(BF16)
    tm = x.shape[0]

    kv = _dot_nt(y, w_ref[T_KV:T_RW, :])
    for i, ref in enumerate((kc_ref, vc_ref)):
        cmp_scr[i] = kv[:, i * KV_WIDTH:(i + 1) * KV_WIDTH]
        for tau in range(CMP_STRIDE):
            piece = cmp_scr[i, pl.ds(tau, tm // CMP_STRIDE, stride=CMP_STRIDE), :].astype(BF16)
            for g in range(NSA_KV):
                ref[g, :, tau * HEAD_DIM:(tau + 1) * HEAD_DIM] = piece[:, g * HEAD_DIM:(g + 1) * HEAD_DIM]
    ra, rm, rp = ra_ref[...], rm_ref[...], rp_ref[...]
    for i, ref in ((2, ks_ref), (3, kw_ref)):
        t = _rope128(kv[:, i * LANE:(i + 1) * LANE], ra, rm, rp)
        for g in range(NSA_KV):
            tg = t[:, g * HEAD_DIM:(g + 1) * HEAD_DIM].astype(BF16)
            if ref is ks_ref:
                ref[g] = jnp.concatenate([tg, oh_ref[...]], axis=1)
            else:
                ref[g] = tg
    rw_ref[...] = _dot_nt(y, w_ref[T_RW:T_GR, :])
    gr = _dot_nt(y, w_ref[T_GR:T_END, :])
    gr_ref[...] = gr * _sigmoid(gr)

    def proj_t(r0, r1):
        return _dot_nt(w_ref[r0:r1, :], y)

    qt = proj_t(R_Q, R_VS) * (SCALE * LOG2E)
    cos, sin = cos_ref[...], sin_ref[...]
    qt_ref[...] = qt.astype(BF16)
    for h in range(NSA_HEADS):
        r0 = h * HEAD_DIM
        t1 = qt[r0:r0 + ROPE_HALF]
        t2 = qt[r0 + ROPE_HALF:r0 + ROPE_DIM]
        qrt_ref[r0:r0 + ROPE_DIM, :] = jnp.concatenate(
            [t1 * cos - t2 * sin, t2 * cos + t1 * sin], axis=0).astype(BF16)
        qrt_ref[r0 + ROPE_DIM:r0 + HEAD_DIM, :] = qt[r0 + ROPE_DIM:r0 + HEAD_DIM].astype(BF16)
    vt = proj_t(R_VS, R_GL).astype(BF16)
    ones = jnp.ones((V_ROWS - HEAD_DIM, LANE), BF16)
    for j in range(vt.shape[1] // LANE):
        for i, ref in enumerate((vst_ref, vwt_ref)):
            for g in range(NSA_KV):
                r0 = i * KV_WIDTH + g * HEAD_DIM
                ref[j, g * V_ROWS:(g + 1) * V_ROWS, :] = jnp.concatenate(
                    [vt[r0:r0 + HEAD_DIM, j * LANE:(j + 1) * LANE], ones], axis=0)
    gate_ref[...] = _sigmoid(proj_t(R_GL, R_GN))
    gn = proj_t(R_GN, R_END)
    gn_ref[...] = gn * _sigmoid(gn)


def _in_proj(x2, norm_g, w_all, tabs, *, seq, tm):
    n = x2.shape[0]
    spt = seq // tm
    ra, rm, rp, cos, sin, onehot = tabs
    hm = lambda w: jax.ShapeDtypeStruct((NSA_KV, n, w), BF16)
    hspec = lambda w: pl.BlockSpec((NSA_KV, tm, w), lambda i: (0, i, 0))
    row = lambda w: pl.BlockSpec((tm, w), lambda i: (i, 0))
    col = lambda r: pl.BlockSpec((r, tm), lambda i: (0, i))
    full = lambda a: pl.BlockSpec(a.shape, lambda i: (0,) * a.ndim)
    tab = lambda w: pl.BlockSpec((tm, w), lambda i: (i % spt, 0))
    tabt = pl.BlockSpec((ROPE_HALF, tm), lambda i: (0, i % spt))
    vtile = pl.BlockSpec((tm // LANE, NSA_KV * V_ROWS, LANE), lambda i: (i, 0, 0))
    vsd = jax.ShapeDtypeStruct((n // LANE, NSA_KV * V_ROWS, LANE), BF16)
    cw = CMP_STRIDE * HEAD_DIM
    cspec = pl.BlockSpec((NSA_KV, tm // CMP_STRIDE, cw), lambda i: (0, i, 0))
    csd = jax.ShapeDtypeStruct((NSA_KV, n // CMP_STRIDE, cw), BF16)
    return pl.pallas_call(
        _in_proj_kernel,
        grid=(n // tm,),
        in_specs=[row(D_MODEL), full(norm_g), full(w_all), tab(LANE), tab(LANE), tab(LANE),
                  tabt, tabt, tab(K_AUG - HEAD_DIM)],
        out_specs=[cspec, cspec, hspec(K_AUG), hspec(HEAD_DIM), row(RWKV_SHIFT_WIDTH), row(RWKV_WIDTH),
                   col(NSA_WIDTH), col(NSA_WIDTH), vtile, vtile,
                   col(GATE_ROWS), col(NSA_WIDTH)],
        out_shape=[csd, csd, hm(K_AUG), hm(HEAD_DIM),
                   jax.ShapeDtypeStruct((n, RWKV_SHIFT_WIDTH), F32),
                   jax.ShapeDtypeStruct((n, RWKV_WIDTH), F32),
                   jax.ShapeDtypeStruct((NSA_WIDTH, n), BF16),
                   jax.ShapeDtypeStruct((NSA_WIDTH, n), BF16),
                   vsd, vsd,
                   jax.ShapeDtypeStruct((GATE_ROWS, n), F32),
                   jax.ShapeDtypeStruct((NSA_WIDTH, n), F32)],
        scratch_shapes=[pltpu.VMEM((2, tm, KV_WIDTH), F32)],
        compiler_params=pltpu.CompilerParams(dimension_semantics=("arbitrary",),
                                             vmem_limit_bytes=VMEM_LIMIT),
        name="in_proj",
    )(x2, norm_g, w_all, ra, rm, rp, cos, sin, onehot)


def _compress_kernel(kc_ref, vc_ref, pk_ref, w1k_ref, w2k_ref, pv_ref, w1v_ref, w2vt_ref,
                     ko_ref, vo_ref):
    half = CMP_STRIDE * HEAD_DIM
    ng, _, nch, _ = kc_ref.shape

    def hidden(c_ref, pos_ref, w1_ref):
        c = c_ref[:, 0].reshape(ng * nch, half)
        pos = jnp.broadcast_to(pos_ref[...], (BF16_SUBLANE, 2 * half)).astype(BF16)
        za = _dot(jnp.concatenate([c, pos[:, 0:half]], axis=0), w1_ref[0:half, :])
        zb = _dot(jnp.concatenate([c, pos[:, half:]], axis=0), w1_ref[half:2 * half, :])
        pv = za[ng * nch:ng * nch + 1] + zb[ng * nch:ng * nch + 1]
        hid = jnp.concatenate(
            [za[g * nch:(g + 1) * nch] + pltpu.roll(zb[g * nch:(g + 1) * nch], nch - 1, 0) for g in range(ng)],
            axis=0) + pv
        return (hid * _sigmoid(hid)).astype(BF16)

    ko = _dot(hidden(kc_ref, pk_ref, w1k_ref), w2k_ref[...]).astype(BF16)
    hv = hidden(vc_ref, pv_ref, w1v_ref)
    for g in range(ng):
        ko_ref[g, 0] = ko[g * nch:(g + 1) * nch]
        vo_ref[g, 0] = _dot_nt(w2vt_ref[...], hv[g * nch:(g + 1) * nch]).astype(BF16)


def _compress(kc_r, vc_r, pk, w1k, w2k, pv, w1v, w2vt):
    g, b, nch, width = kc_r.shape
    blk = pl.BlockSpec((g, 1, nch, width), lambda j: (0, j, 0, 0))
    full = lambda a: pl.BlockSpec(a.shape, lambda j: (0,) * a.ndim)
    return pl.pallas_call(
        _compress_kernel,
        grid=(b,),
        in_specs=[blk, blk, full(pk), full(w1k), full(w2k), full(pv), full(w1v), full(w2vt)],
        out_specs=[pl.BlockSpec((g, 1, nch, HEAD_DIM), lambda j: (0, j, 0, 0)),
                   pl.BlockSpec((g, 1, HEAD_DIM, nch), lambda j: (0, j, 0, 0))],
        out_shape=[jax.ShapeDtypeStruct((g, b, nch, HEAD_DIM), BF16),
                   jax.ShapeDtypeStruct((g, b, HEAD_DIM, nch), BF16)],
        compiler_params=pltpu.CompilerParams(dimension_semantics=("arbitrary",),
                                             vmem_limit_bytes=VMEM_LIMIT),
        name="compress",
    )(kc_r, vc_r, pk, w1k, w2k, pv, w1v, w2vt)


def _nsa_kernel(qt_ref, qrt_ref, kc_ref, vct_ref, ks_ref, vst_ref, kw_ref, vwt_ref,
                gate_ref, gn_ref, mt_ref, o_ref, *, tq, seq):
    tk = KEY_TILE
    n_win = WINDOW // tk
    qi = pl.program_id(1)
    q0 = qi * tq
    nl = NSA_REP * tq
    n_sel = seq // SEL_BLOCK
    ncp = kc_ref.shape[2]
    groups = range(NSA_KV)

    def heads_on_lanes(ref, g):
        rows = [(g * NSA_REP + r) * HEAD_DIM for r in range(NSA_REP)]
        return jnp.concatenate([ref[r0:r0 + HEAD_DIM, :] for r0 in rows], axis=1)

    def tile4(a):
        return jnp.concatenate([a] * NSA_REP, axis=1)

    k_s = lax.broadcasted_iota(jnp.int32, (tk, tq), 0)
    t_l = q0 + lax.broadcasted_iota(jnp.int32, (tk, tq), 1)

    def update_steps(box, k_ref, vt_ref, g, q_op, kts, keeps):
        m_i, acc = box[0]

        def scores(i):
            sc = _dot(k_ref[g, pl.ds(pl.multiple_of(kts[i] * tk, tk), tk), :], q_op)
            if keeps[i] is None:
                return sc
            return jnp.concatenate([jnp.where(keeps[i], sc[:, r * tq:(r + 1) * tq], NEG_INF)
                                    for r in range(NSA_REP)], axis=1)

        sc_next = scores(0)
        yield
        for i, kt in enumerate(kts):
            sc = sc_next
            if i + 1 < len(kts):
                sc_next = scores(i + 1)
            m_n = jnp.maximum(m_i, jnp.max(sc, axis=0, keepdims=True))
            pe = jnp.exp2(sc - m_n).astype(BF16)
            yield
            vtb = jnp.concatenate([vt_ref[kt * (tk // LANE) + jj, g * V_ROWS:(g + 1) * V_ROWS, :]
                                   for jj in range(tk // LANE)], axis=1)
            acc = jnp.exp2(m_i - m_n) * acc + _dot(vtb, pe)
            m_i = m_n
            yield
        box[0] = (m_i, acc)

    def last_tiles(box, k_ref, vt_ref, g, q_op, n_back, low_keep):
        kts, keeps = [], []
        for back in range(n_back + 1):
            d = t_l - ((a - back) * tk + k_s)
            kts.append(a - back)
            if back == 0:
                keeps.append(d >= 0)
            elif back == n_win and low_keep is not None:
                keeps.append(low_keep(d))
            else:
                keeps.append(None)
        return update_steps(box, k_ref, vt_ref, g, q_op, kts, keeps)

    def select_steps(g, qr, sel_out, all_selected):
        s = _dot(kc_ref[g, 0], heads_on_lanes(qt_ref, g))
        t_c = q0 + lax.broadcasted_iota(jnp.int32, (ncp, tq), 1)
        c_c = lax.broadcasted_iota(jnp.int32, (ncp, tq), 0)
        cmask = tile4((c_c * CMP_STRIDE + (CMP_BLOCK - 1)) <= t_c)
        yield
        s = jnp.where(cmask, s, NEG_INF)
        m = jnp.maximum(jnp.max(s, axis=0, keepdims=True), 0.5 * NEG_INF)
        e = jnp.exp2(s - m)
        den = jnp.sum(e, axis=0, keepdims=True)
        p = e * (1.0 / jnp.where(den > 0.0, den, 1.0))
        sel_out["o_cmp"] = _dot(vct_ref[g, 0], p.astype(BF16))
        yield
        if all_selected:
            sel_out["qr_sel"] = jnp.concatenate([qr, jnp.zeros((K_AUG - HEAD_DIM, nl), BF16)], axis=0)
            return
        psum = p[:, 0:tq]
        for r in range(1, NSA_REP):
            psum = psum + p[:, r * tq:(r + 1) * tq]
        imp = _x3_dot(mt_ref[...], psum)
        j = lax.broadcasted_iota(jnp.int32, (n_sel, tq), 0)
        t = q0 + lax.broadcasted_iota(jnp.int32, (n_sel, tq), 1)
        tb = t // SEL_BLOCK
        forced = (j == 0) | (j == tb) | (j == tb - 1)
        val = jnp.where(j <= tb, imp + jnp.where(forced, FORCE_BONUS, 0.0), -1.0)
        yield
        vals = [val[g0:g0 + SUBLANE] for g0 in range(0, n_sel, SUBLANE)]
        cnts = [jnp.zeros((SUBLANE, tq), F32) for _ in vals]
        srow = lax.broadcasted_iota(jnp.int32, (SUBLANE, tq), 0)
        for i in range(n_sel):
            vi = jnp.broadcast_to(val[i:i + 1, :], (SUBLANE, tq))
            for g, vg in enumerate(vals):
                ge = lambda: jnp.where(vi >= vg, 1.0, 0.0)
                gt = lambda: jnp.where(vi > vg, 1.0, 0.0)
                if g * SUBLANE > i:
                    beat = ge()
                elif (g + 1) * SUBLANE <= i:
                    beat = gt()
                else:
                    beat = jnp.where(srow > i - g * SUBLANE, ge(), gt())
                cnts[g] = cnts[g] + beat
            if i % SUBLANE == SUBLANE - 1:
                yield
        cnt = jnp.concatenate(cnts, axis=0)
        sel_bias = jnp.where(cnt < float(SEL_TOPK), 0.0, NEG_INF).astype(BF16)
        sel_out["qr_sel"] = jnp.concatenate(
            [qr, tile4(sel_bias), jnp.zeros((K_AUG - HEAD_DIM - n_sel, nl), BF16)], axis=0)

    init = (jnp.full((1, nl), NEG_INF, F32), jnp.zeros((V_ROWS, nl), F32))
    a = (q0 + tq - 1) // tk

    def attend(n_back):
        win_boxes, sel_boxes, sel_outs = [[init] for _ in groups], [[init] for _ in groups], [{} for _ in groups]
        all_selected = n_back < n_win and (n_back + 1) * tk <= SEL_TOPK * SEL_BLOCK
        for g in groups:
            qr = heads_on_lanes(qrt_ref, g)
            _interleave(last_tiles(win_boxes[g], kw_ref, vwt_ref, g, qr, n_back, lambda d: d < WINDOW),
                        select_steps(g, qr, sel_outs[g], all_selected))
        _interleave(*[last_tiles(sel_boxes[g], ks_ref, vst_ref, g, sel_outs[g]["qr_sel"], n_back, None)
                      for g in groups])
        c_wins = [b[0] for b in win_boxes]
        c_sels = [b[0] for b in sel_boxes]
        o_cmps = [o["o_cmp"] for o in sel_outs]
        qr_sels = [o["qr_sel"] for o in sel_outs]

        def old_tiles(kts):
            def body(carry):
                boxes = [[c] for c in carry]
                _interleave(*[update_steps(boxes[g], ks_ref, vst_ref, g, qr_sels[g], kts, [None] * len(kts))
                              for g in groups])
                return tuple(b[0] for b in boxes)
            return body

        if n_back == n_win:
            n_old = a - n_win
            c_sels = lax.fori_loop(0, n_old % 2, lambda kt, c: old_tiles([kt])(c), tuple(c_sels))
            c_sels = lax.fori_loop(0, n_old // 2,
                                   lambda i, c: old_tiles([n_old % 2 + 2 * i, n_old % 2 + 2 * i + 1])(c), c_sels)

        for g in groups:
            o_sel = c_sels[g][1][0:HEAD_DIM] * (1.0 / c_sels[g][1][HEAD_DIM:HEAD_DIM + 1])
            o_win = c_wins[g][1][0:HEAD_DIM] * (1.0 / c_wins[g][1][HEAD_DIM:HEAD_DIM + 1])
            for r in range(NSA_REP):
                ls = slice(r * tq, (r + 1) * tq)
                h = g * NSA_REP + r
                gate = lambda c: gate_ref[3 * h + c:3 * h + c + 1, :]
                o = gate(0) * o_cmps[g][:, ls] + gate(1) * o_sel[:, ls] + gate(2) * o_win[:, ls]
                rs = slice(h * HEAD_DIM, (h + 1) * HEAD_DIM)
                o_ref[rs, :] = (o * gn_ref[rs, :]).astype(BF16)

    def dispatch(n_back):
        if n_back == 0:
            attend(0)
        else:
            lax.cond(a >= n_back, lambda: attend(n_back), lambda: dispatch(n_back - 1))

    dispatch(n_win)


def _nsa(qt, qrt, kcmp, vcmpt, ksh, vst, kwh, vwt, gates, gnt, mt, *, batch, seq, tq):
    n = batch * seq
    nq = seq // tq
    ncp = kcmp.shape[2]
    qspec = pl.BlockSpec((NSA_WIDTH, tq), lambda b, i: (0, b * nq + i))
    kspec = lambda w: pl.BlockSpec((NSA_KV, seq, w), lambda b, i: (0, b, 0))
    vspec = pl.BlockSpec((seq // LANE, NSA_KV * V_ROWS, LANE), lambda b, i: (b, 0, 0))
    kern = functools.partial(_nsa_kernel, tq=tq, seq=seq)
    return pl.pallas_call(
        kern,
        grid=(batch, nq),
        in_specs=[qspec, qspec,
                  pl.BlockSpec((NSA_KV, 1, ncp, HEAD_DIM), lambda b, i: (0, b, 0, 0)),
                  pl.BlockSpec((NSA_KV, 1, HEAD_DIM, ncp), lambda b, i: (0, b, 0, 0)),
                  kspec(K_AUG), vspec, kspec(HEAD_DIM), vspec,
                  pl.BlockSpec((GATE_ROWS, tq), lambda b, i: (0, b * nq + i)),
                  qspec,
                  pl.BlockSpec(mt.shape, lambda b, i: (0, 0))],
        out_specs=qspec,
        out_shape=jax.ShapeDtypeStruct((NSA_WIDTH, n), BF16),
        compiler_params=pltpu.CompilerParams(
            dimension_semantics=("arbitrary", "arbitrary"),
            vmem_limit_bytes=VMEM_LIMIT),
        name="nsa",
    )(qt, qrt, kcmp, vcmpt, ksh, vst, kwh, vwt, gates, gnt, mt)


def _rwkv_kernel(p_ref, gr_ref, mu_ref, w0_ref, a0_ref, wab_ref, kk_ref, ka_ref, rk_ref,
                 gw_ref, gb_ref, ones_ref, tril_ref, o_ref, st_ref, carry_ref, *, tt):
    c = RWKV_CHUNK
    hd = HEAD_DIM
    nc = tt // c
    step = pl.program_id(1)

    @pl.when(step == 0)
    def _():
        st_ref[...] = jnp.zeros_like(st_ref)
        carry_ref[...] = jnp.zeros_like(carry_ref)

    p = p_ref[...]
    row = lax.broadcasted_iota(jnp.int32, p.shape, 0)
    prev = jnp.where(row == 0, carry_ref[...], pltpu.roll(p, 1, 0))
    carry_ref[...] = p[tt - 1:tt, :]
    ps = p + mu_ref[...] * (prev - p)
    r = ps[:, 0:RWKV_WIDTH]
    k = ps[:, RWKV_WIDTH:2 * RWKV_WIDTH]
    v = ps[:, 2 * RWKV_WIDTH:3 * RWKV_WIDTH]
    lora = ps[:, 3 * RWKV_WIDTH:]
    lane = lax.broadcasted_iota(jnp.int32, lora.shape, 1)
    feat = jnp.where(lane < DECAY_RANK, jnp.tanh(lora), lora).astype(BF16)
    up = _dot(feat, wab_ref[...])
    w = w0_ref[...] + up[:, 0:RWKV_WIDTH]
    lw = _sigmoid(w) * (-float(np.exp(-0.5)))
    a = _sigmoid(a0_ref[...] + up[:, RWKV_WIDTH:])
    ones_bd = ones_ref[...]
    kk = k * kk_ref[...]
    kkn = kk * lax.rsqrt(jnp.maximum(_head_sums(kk * kk, ones_bd), 1e-24))
    k2 = k * (1.0 + (a - 1.0) * ka_ref[...])
    alpha = -kkn
    beta = kkn * a
    bonus = _head_sums(r * k2 * rk_ref[...], ones_bd) * v

    cum = _x2_dot(tril_ref[...], lw)
    cend = jnp.concatenate(
        [jnp.broadcast_to(cum[(ch + 1) * c - 1:(ch + 1) * c, :], (c, RWKV_WIDTH)) for ch in range(nc)], axis=0)
    e_neg = jnp.exp(-cum)
    pc = jnp.exp(cend)
    at = alpha * jnp.exp(cum - lw)
    bt = beta * e_neg
    kt = k2 * e_neg
    rt = r * jnp.exp(cum)
    bh = bt * pc
    kh = kt * pc

    npair = RWKV_WIDTH // LANE

    def pairs(x):
        return jnp.stack([x[ch * c:(ch + 1) * c, j * LANE:(j + 1) * LANE]
                          for ch in range(nc) for j in range(npair)], axis=0)

    at_p, rt_p, bt_p, kt_p, v_p, bh_p, kh_p = (pairs(t) for t in (at, rt, bt, kt, v, bh, kh))
    pc_p = jnp.stack([pc[ch * c:ch * c + 1, j * LANE:(j + 1) * LANE]
                      for ch in range(nc) for j in range(npair)], axis=0)
    lane_c = lax.broadcasted_iota(jnp.int32, (1, c, LANE), 2)
    row_c = lax.broadcasted_iota(jnp.int32, (1, c, LANE), 1)
    even_c = lane_c < hd
    col_c = jnp.where(even_c, lane_c, lane_c - hd)
    low_s = row_c > col_c
    low_i = row_c >= col_c
    lane_2c = lax.broadcasted_iota(jnp.int32, (1, 2 * c, LANE), 2)
    row_2c = lax.broadcasted_iota(jnp.int32, (1, 2 * c, LANE), 1)
    even_2c = lane_2c < hd
    on_bd = (row_2c < hd) == even_2c
    zero_c = jnp.zeros((1, c, LANE), BF16)

    def bd(x):
        xb = x.astype(BF16)
        return jnp.concatenate([jnp.where(even_c, xb, zero_c), jnp.where(even_c, zero_c, xb)], axis=1)

    def abd(x):
        xb = x.astype(BF16)
        return jnp.concatenate([jnp.where(even_c, zero_c, xb), jnp.where(even_c, xb, zero_c)], axis=1)

    la = jnp.concatenate([at_p, rt_p], axis=1).astype(BF16)
    zero_2c = jnp.zeros((1, 2 * c, LANE), BF16)
    r_e = _bmm_nt(jnp.where(even_2c, la, zero_2c), jnp.concatenate([bt_p, kt_p], axis=1).astype(BF16))
    r_o = _bmm_nt(jnp.where(even_2c, zero_2c, la), jnp.concatenate([kt_p, bt_p], axis=1).astype(BF16))
    nab = jnp.where(low_s, jnp.where(even_c, r_e[:, 0:c], r_o[:, 0:c]), 0.0)
    aak_sw = jnp.where(low_s, jnp.where(even_c, r_o[:, 0:c], r_e[:, 0:c]), 0.0).astype(BF16)
    arb = jnp.where(low_i, jnp.where(even_c, r_e[:, c:], r_o[:, c:]), 0.0).astype(BF16)
    ark_sw = jnp.where(low_i, jnp.where(even_c, r_o[:, c:], r_e[:, c:]), 0.0).astype(BF16)
    tinv = jnp.where(row_c == col_c, 1.0, 0.0) + nab
    npow = _bmm(nab.astype(BF16), bd(nab))
    n_dbl = 5
    for it in range(n_dbl):
        nbd = bd(npow)
        if it + 1 < n_dbl:
            res = _bmm(jnp.concatenate([tinv, npow], axis=1).astype(BF16), nbd)
            tinv = tinv + res[:, 0:c]
            npow = res[:, c:]
        else:
            tinv = tinv + _bmm(tinv.astype(BF16), nbd)
    av = _bmm(jnp.concatenate([aak_sw, ark_sw], axis=1), abd(v_p))
    tx = _bmm(tinv.astype(BF16), jnp.concatenate([bd(av[:, 0:c]), bd(at_p)], axis=2))
    u0, ta = tx[:, :, 0:LANE], tx[:, :, LANE:]
    ax = _bmm(arb, jnp.concatenate([bd(ta), bd(u0)], axis=2))
    rq = rt_p + ax[:, :, 0:LANE]
    y0 = ax[:, :, LANE:] + av[:, c:]
    v_b = v_p.astype(BF16)
    w_f = jnp.concatenate([jnp.concatenate([ta, u0], axis=2).astype(BF16),
                           jnp.concatenate([jnp.zeros_like(v_b), v_b], axis=2)], axis=1)
    gh = _bmm(jnp.concatenate([jnp.swapaxes(bh_p, 1, 2), jnp.swapaxes(kh_p, 1, 2)], axis=2).astype(BF16), w_f)
    g_bd = jnp.where(on_bd, gh[:, :, 0:LANE], 0.0) + jnp.where(row_2c == lane_2c, pc_p, 0.0)
    h_bd = jnp.where(on_bd, gh[:, :, LANE:], 0.0)
    lhs = jnp.concatenate([rq, g_bd], axis=1).astype(BF16)

    st = st_ref[...]
    ys = []
    for ch in range(nc):
        sl = slice(ch * npair, (ch + 1) * npair)
        res = _bmm(lhs[sl], st.astype(BF16))
        yc = res[:, 0:c, :] + y0[sl]
        st = res[:, c:, :] + h_bd[sl]
        ys.append(jnp.concatenate([yc[j] for j in range(npair)], axis=1))
    st_ref[...] = st
    y = jnp.concatenate(ys, axis=0) if nc > 1 else ys[0]

    inv_hd = 1.0 / hd
    mean = _head_sums(y, ones_bd) * inv_hd
    ycen = y - mean
    var = _head_sums(ycen * ycen, ones_bd) * inv_hd
    yn = ycen * lax.rsqrt(var + GN_EPS) * gw_ref[...] + gb_ref[...]
    o_ref[...] = ((yn + bonus) * gr_ref[...]).astype(BF16)


def _rwkv(rw, gr, mu, w0, a0, wab, kk, ka, rk, gw, gb, ones_bd, tril, *, batch, seq, tt):
    n = batch * seq
    ns = seq // tt
    row = lambda w: pl.BlockSpec((tt, w), lambda b, i: (b * ns + i, 0))
    full = lambda a: pl.BlockSpec(a.shape, lambda b, i: (0,) * a.ndim)
    kern = functools.partial(_rwkv_kernel, tt=tt)
    consts = (mu, w0, a0, wab, kk, ka, rk, gw, gb, ones_bd, tril)
    return pl.pallas_call(
        kern,
        grid=(batch, ns),
        in_specs=[row(RWKV_SHIFT_WIDTH), row(RWKV_WIDTH)] + [full(a) for a in consts],
        out_specs=row(RWKV_WIDTH),
        out_shape=jax.ShapeDtypeStruct((n, RWKV_WIDTH), BF16),
        scratch_shapes=[pltpu.VMEM((RWKV_WIDTH // LANE, LANE, LANE), F32),
                        pltpu.VMEM((1, RWKV_SHIFT_WIDTH), F32)],
        compiler_params=pltpu.CompilerParams(dimension_semantics=("arbitrary", "arbitrary"),
                                             vmem_limit_bytes=VMEM_LIMIT),
        name="rwkv",
    )(rw, gr, *consts)


def _out_kernel(x_ref, ont_ref, or_ref, wn_ref, wr_ref, g_ref, o_ref, *, n_sub):
    ts = x_ref.shape[0] // n_sub

    def project(j):
        rows = slice(j * ts, (j + 1) * ts)
        return (x_ref[rows, :] + _dot_tn(ont_ref[:, rows], wn_ref[...])
                + _dot(or_ref[rows, :], wr_ref[...]))

    def finish(j, h):
        ms = jnp.mean(h * h, axis=-1, keepdims=True)
        o_ref[j * ts:(j + 1) * ts, :] = h * lax.rsqrt(ms + RMS_EPS) * g_ref[...]

    h = project(0)
    for j in range(1, n_sub):
        h_next = project(j)
        finish(j - 1, h)
        h = h_next
    finish(n_sub - 1, h)


def _out_proj(x2, o_nsa_t, o_rwkv, wn, wr, final_g, *, tm, n_sub):
    n = x2.shape[0]
    row = lambda w: pl.BlockSpec((tm, w), lambda i: (i, 0))
    full = lambda a: pl.BlockSpec(a.shape, lambda i: (0,) * a.ndim)
    return pl.pallas_call(
        functools.partial(_out_kernel, n_sub=n_sub),
        grid=(n // tm,),
        in_specs=[row(D_MODEL), pl.BlockSpec((NSA_WIDTH, tm), lambda i: (0, i)), row(RWKV_WIDTH),
                  full(wn), full(wr), full(final_g)],
        out_specs=row(D_MODEL),
        out_shape=jax.ShapeDtypeStruct((n, D_MODEL), F32),
        compiler_params=pltpu.CompilerParams(dimension_semantics=("arbitrary",),
                                             vmem_limit_bytes=VMEM_LIMIT),
        name="out_proj",
    )(x2, o_nsa_t, o_rwkv, wn, wr, final_g)


def _rope_tables(seq):
    inv = ROPE_THETA ** (-np.arange(ROPE_HALF, dtype=np.float64) / ROPE_HALF)
    ang = np.arange(seq, dtype=np.float64)[:, None] * inv[None, :]
    cos, sin = np.cos(ang), np.sin(ang)
    ra = np.ones((seq, HEAD_DIM)); rm = np.zeros((seq, HEAD_DIM)); rp = np.zeros((seq, HEAD_DIM))
    ra[:, :ROPE_HALF] = cos; ra[:, ROPE_HALF:ROPE_DIM] = cos
    rm[:, :ROPE_HALF] = -sin
    rp[:, ROPE_HALF:ROPE_DIM] = sin
    rep = lambda t: jnp.asarray(np.tile(t, (1, LANE // HEAD_DIM)), F32)
    assert seq // SEL_BLOCK <= K_AUG - HEAD_DIM
    onehot = np.zeros((seq, K_AUG - HEAD_DIM))
    onehot[np.arange(seq), np.arange(seq) // SEL_BLOCK] = 1.0
    return (rep(ra), rep(rm), rep(rp), jnp.asarray(cos.T, F32), jnp.asarray(sin.T, F32),
            jnp.asarray(onehot, BF16))


def _cmp_to_sel_t(n_cmp_pad, n_sel):
    n_cmp = n_cmp_pad - 1
    c0 = np.arange(n_cmp)[:, None] * CMP_STRIDE
    s0 = np.arange(n_sel)[None, :] * SEL_BLOCK
    ov = np.clip(np.minimum(c0 + CMP_BLOCK, s0 + SEL_BLOCK) - np.maximum(c0, s0), 0, None) / CMP_BLOCK
    mt = np.zeros((n_sel, n_cmp_pad))
    mt[:, :n_cmp] = ov.T
    return jnp.asarray(mt, BF16)


def _prep_w_in(w_in):
    idx = np.cumsum(IN_SIZES)[:-1].tolist()
    q, kc, vc, ks, vs, kw, vw, gl, gn, rw, gr = jnp.split(w_in, idx, axis=1)
    pad = jnp.zeros((D_MODEL, GATE_ROWS - gl.shape[1]), w_in.dtype)
    return jnp.concatenate([kc, vc, ks, kw, rw, gr, q, vs, vw, gl, pad, gn], axis=1).T.astype(BF16)


def _layer(x2, norm_g, w_in, cmp_pos_k, cmp_w1_k, cmp_w2_k, cmp_pos_v, cmp_w1_v, cmp_w2_v,
           shift_mu, decay_w0, decay_up, iclr_a0, iclr_up, k_k, k_a, r_k, gn_w, gn_b, w_out,
           final_g, *, batch, seq):
    tm = 256
    tq = 256
    tt = 256
    assert WINDOW % KEY_TILE == 0 and seq % KEY_TILE == 0 and KEY_TILE % tq == 0
    nch = seq // CMP_STRIDE
    n_sel = seq // SEL_BLOCK
    rowv = lambda t: t.reshape(1, -1).astype(F32)

    (kch, vch, ksh, kwh, rw, gr, qt, qrt, vst, vwt, gates, gnt) = _in_proj(
        x2, rowv(norm_g), _prep_w_in(w_in), _rope_tables(seq), seq=seq, tm=2 * tm)

    chunks = lambda t: t.reshape(NSA_KV, batch, nch, CMP_STRIDE * HEAD_DIM)
    kcmp, vcmpt = _compress(chunks(kch), chunks(vch),
                            rowv(cmp_pos_k), cmp_w1_k.astype(BF16), cmp_w2_k.astype(BF16),
                            rowv(cmp_pos_v), cmp_w1_v.astype(BF16), cmp_w2_v.T.astype(BF16))

    o_nsa_t = _nsa(qt, qrt, kcmp, vcmpt, ksh, vst, kwh, vwt, gates, gnt,
                   _cmp_to_sel_t(nch, n_sel), batch=batch, seq=seq, tq=tq)

    z = jnp.zeros((DECAY_RANK, RWKV_WIDTH), F32)
    wab = jnp.concatenate([jnp.concatenate([decay_up, z], axis=1),
                           jnp.concatenate([z, iclr_up], axis=1)], axis=0).astype(BF16)
    hid = np.arange(2 * LANE) // HEAD_DIM
    ones_bd = jnp.asarray(hid[:, None] == hid[None, :], BF16)
    ti = np.arange(tt)
    tril = (ti[:, None] >= ti[None, :]) & (ti[:, None] // RWKV_CHUNK == ti[None, :] // RWKV_CHUNK)
    tril = jnp.asarray(np.concatenate([tril, tril], axis=1), BF16)
    o_rwkv = _rwkv(rw, gr, rowv(shift_mu), rowv(decay_w0), rowv(iclr_a0), wab, rowv(k_k), rowv(k_a),
                   rowv(r_k), rowv(gn_w), rowv(gn_b), ones_bd, tril, batch=batch, seq=seq, tt=tt)

    w_o = w_out.astype(BF16)
    return _out_proj(x2, o_nsa_t, o_rwkv, w_o[:NSA_WIDTH], w_o[NSA_WIDTH:], rowv(final_g), tm=4 * tm, n_sub=4)


def kernel(x, norm_g, w_in, cmp_pos_k, cmp_w1_k, cmp_w2_k, cmp_pos_v, cmp_w1_v, cmp_w2_v, shift_mu, decay_w0, decay_up, iclr_a0, iclr_up, k_k, k_a, r_k, gn_w, gn_b, w_out, final_g):
    batch, seq, d = x.shape
    assert d == D_MODEL and norm_g.shape[0] == 1, "single-layer trunk"
    out = _layer(x.reshape(batch * seq, d), norm_g[0], w_in[0], cmp_pos_k[0], cmp_w1_k[0], cmp_w2_k[0],
                 cmp_pos_v[0], cmp_w1_v[0], cmp_w2_v[0], shift_mu[0], decay_w0[0], decay_up[0],
                 iclr_a0[0], iclr_up[0], k_k[0], k_a[0], r_k[0], gn_w[0], gn_b[0], w_out[0],
                 final_g, batch=batch, seq=seq)
    return out.reshape(batch, seq, d)
```

```python
import functools

import numpy as np
import jax
import jax.numpy as jnp
from jax import lax
from jax.experimental import pallas as pl
from jax.experimental.pallas import tpu as pltpu

F32 = jnp.float32
BF16 = jnp.bfloat16

D_MODEL = 1024
HEAD_DIM = 64
NSA_HEADS = 8
NSA_KV = 2
NSA_REP = NSA_HEADS // NSA_KV
RWKV_HEADS = 8
NSA_WIDTH = NSA_HEADS * HEAD_DIM
RWKV_WIDTH = RWKV_HEADS * HEAD_DIM
KV_WIDTH = NSA_KV * HEAD_DIM
ROPE_DIM = HEAD_DIM // 4
ROPE_HALF = ROPE_DIM // 2
ROPE_THETA = 500000.0
CMP_BLOCK = 32
CMP_STRIDE = 16
CMP_HIDDEN = 256
SEL_BLOCK = 64
SEL_TOPK = 8
WINDOW = 512
DECAY_RANK = 64
ICLR_RANK = 64
RWKV_SHIFT_WIDTH = 3 * RWKV_WIDTH + DECAY_RANK + ICLR_RANK
IN_SIZES = (NSA_WIDTH, KV_WIDTH, KV_WIDTH, KV_WIDTH, KV_WIDTH, KV_WIDTH, KV_WIDTH,
            3 * NSA_HEADS, NSA_WIDTH, RWKV_SHIFT_WIDTH, RWKV_WIDTH)
SCALE = HEAD_DIM ** -0.5
RMS_EPS = 1e-6
GN_EPS = 64e-5
NEG_INF = -1e30
FORCE_BONUS = 1e3

LANE = 128
SUBLANE = 8
BF16_SUBLANE = 2 * SUBLANE
GATE_ROWS = 2 * BF16_SUBLANE

T_KV = 0
T_RW = T_KV + 4 * KV_WIDTH
T_GR = T_RW + RWKV_SHIFT_WIDTH
T_END = T_GR + RWKV_WIDTH
R_Q = T_END
R_VS = R_Q + NSA_WIDTH
R_VW = R_VS + KV_WIDTH
R_GL = R_VW + KV_WIDTH
R_GN = R_GL + GATE_ROWS
R_END = R_GN + NSA_WIDTH

LOG2E = float(np.log2(np.e))
K_AUG = LANE
V_ROWS = HEAD_DIM + BF16_SUBLANE
KEY_TILE = 256
RWKV_CHUNK = 64
VMEM_LIMIT = 48 * 1024 * 1024


def _dot(a, b):
    return jnp.dot(a, b, preferred_element_type=F32)


def _dot_nt(a, b):
    return lax.dot_general(a, b, (((1,), (1,)), ((), ())), preferred_element_type=F32)


def _dot_tn(a, b):
    return lax.dot_general(a, b, (((0,), (0,)), ((), ())), preferred_element_type=F32)


def _bmm(a, b):
    return lax.dot_general(a, b, (((2,), (1,)), ((0,), (0,))), preferred_element_type=F32)


def _bmm_nt(a, b):
    return lax.dot_general(a, b, (((2,), (2,)), ((0,), (0,))), preferred_element_type=F32)


def _split3(x):
    hi = x.astype(BF16)
    r1 = x - hi.astype(F32)
    mid = r1.astype(BF16)
    lo = (r1 - mid.astype(F32)).astype(BF16)
    return hi, mid, lo


def _head_sums(x, ones_blk):
    w = ones_blk.shape[0]
    xb = x.astype(BF16)
    return jnp.concatenate([_dot(xb[:, i:i + w], ones_blk) for i in range(0, x.shape[1], w)], axis=1)


def _x2_dot(ww_bf16, x):
    hi = x.astype(BF16)
    lo = (x - hi.astype(F32)).astype(BF16)
    return _dot(ww_bf16, jnp.concatenate([hi, lo], axis=0))


def _x3_dot(w_bf16, x):
    hi, mid, lo = _split3(x)
    return _dot(w_bf16, hi) + _dot(w_bf16, mid) + _dot(w_bf16, lo)


def _sigmoid(x):
    return 0.5 * jnp.tanh(0.5 * x) + 0.5


def _interleave(*gens):
    live = list(gens)
    while live:
        for g in list(live):
            try:
                next(g)
            except StopIteration:
                live.remove(g)


def _rope128(t, ra, rm, rp):
    return t * ra + pltpu.roll(t, LANE - ROPE_HALF, 1) * rm + pltpu.roll(t, ROPE_HALF, 1) * rp


def _in_proj_kernel(x_ref, g_ref, w_ref, ra_ref, rm_ref, rp_ref, cos_ref, sin_ref, oh_ref,
                    kc_ref, vc_ref, ks_ref, kw_ref, rw_ref, gr_ref,
                    qt_ref, qrt_ref, vst_ref, vwt_ref, gate_ref, gn_ref, cmp_scr):
    x = x_ref[...]
    ms = jnp.mean(x * x, axis=-1, keepdims=True)
    y = (x * lax.rsqrt(ms + RMS_EPS) * g_ref[...]).astype(BF16)
    tm = x.shape[0]

    kv = _dot_nt(y, w_ref[T_KV:T_RW, :])
    for i, ref in enumerate((kc_ref, vc_ref)):
        cmp_scr[i] = kv[:, i * KV_WIDTH:(i + 1) * KV_WIDTH]
        for tau in range(CMP_STRIDE):
            piece = cmp_scr[i, pl.ds(tau, tm // CMP_STRIDE, stride=CMP_STRIDE), :].astype(BF16)
            for g in range(NSA_KV):
                ref[g, :, tau * HEAD_DIM:(tau + 1) * HEAD_DIM] = piece[:, g * HEAD_DIM:(g + 1) * HEAD_DIM]
    ra, rm, rp = ra_ref[...], rm_ref[...], rp_ref[...]
    for i, ref in ((2, ks_ref), (3, kw_ref)):
        t = _rope128(kv[:, i * LANE:(i + 1) * LANE], ra, rm, rp)
        for g in range(NSA_KV):
            tg = t[:, g * HEAD_DIM:(g + 1) * HEAD_DIM].astype(BF16)
            if ref is ks_ref:
                ref[g] = jnp.concatenate([tg, oh_ref[...]], axis=1)
            else:
                ref[g] = tg
    rw_ref[...] = _dot_nt(y, w_ref[T_RW:T_GR, :])
    gr = _dot_nt(y, w_ref[T_GR:T_END, :])
    gr_ref[...] = gr * _sigmoid(gr)

    def proj_t(r0, r1):
        return _dot_nt(w_ref[r0:r1, :], y)

    qt = proj_t(R_Q, R_VS) * (SCALE * LOG2E)
    cos, sin = cos_ref[...], sin_ref[...]
    qt_ref[...] = qt.astype(BF16)
    for h in range(NSA_HEADS):
        r0 = h * HEAD_DIM
        t1 = qt[r0:r0 + ROPE_HALF]
        t2 = qt[r0 + ROPE_HALF:r0 + ROPE_DIM]
        qrt_ref[r0:r0 + ROPE_DIM, :] = jnp.concatenate(
            [t1 * cos - t2 * sin, t2 * cos + t1 * sin], axis=0).astype(BF16)
        qrt_ref[r0 + ROPE_DIM:r0 + HEAD_DIM, :] = qt[r0 + ROPE_DIM:r0 + HEAD_DIM].astype(BF16)
    vt = proj_t(R_VS, R_GL).astype(BF16)
    ones = jnp.ones((V_ROWS - HEAD_DIM, LANE), BF16)
    for j in range(vt.shape[1] // LANE):
        for i, ref in enumerate((vst_ref, vwt_ref)):
            for g in range(NSA_KV):
                r0 = i * KV_WIDTH + g * HEAD_DIM
                ref[j, g * V_ROWS:(g + 1) * V_ROWS, :] = jnp.concatenate(
                    [vt[r0:r0 + HEAD_DIM, j * LANE:(j + 1) * LANE], ones], axis=0)
    gate_ref[...] = _sigmoid(proj_t(R_GL, R_GN))
    gn = proj_t(R_GN, R_END)
    gn_ref[...] = gn * _sigmoid(gn)


def _in_proj(x2, norm_g, w_all, tabs, *, seq, tm):
    n = x2.shape[0]
    spt = seq // tm
    ra, rm, rp, cos, sin, onehot = tabs
    hm = lambda w: jax.ShapeDtypeStruct((NSA_KV, n, w), BF16)
    hspec = lambda w: pl.BlockSpec((NSA_KV, tm, w), lambda i: (0, i, 0))
    row = lambda w: pl.BlockSpec((tm, w), lambda i: (i, 0))
    col = lambda r: pl.BlockSpec((r, tm), lambda i: (0, i))
    full = lambda a: pl.BlockSpec(a.shape, lambda i: (0,) * a.ndim)
    tab = lambda w: pl.BlockSpec((tm, w), lambda i: (i % spt, 0))
    tabt = pl.BlockSpec((ROPE_HALF, tm), lambda i: (0, i % spt))
    vtile = pl.BlockSpec((tm // LANE, NSA_KV * V_ROWS, LANE), lambda i: (i, 0, 0))
    vsd = jax.ShapeDtypeStruct((n // LANE, NSA_KV * V_ROWS, LANE), BF16)
    cw = CMP_STRIDE * HEAD_DIM
    cspec = pl.BlockSpec((NSA_KV, tm // CMP_STRIDE, cw), lambda i: (0, i, 0))
    csd = jax.ShapeDtypeStruct((NSA_KV, n // CMP_STRIDE, cw), BF16)
    return pl.pallas_call(
        _in_proj_kernel,
        grid=(n // tm,),
        in_specs=[row(D_MODEL), full(norm_g), full(w_all), tab(LANE), tab(LANE), tab(LANE),
                  tabt, tabt, tab(K_AUG - HEAD_DIM)],
        out_specs=[cspec, cspec, hspec(K_AUG), hspec(HEAD_DIM), row(RWKV_SHIFT_WIDTH), row(RWKV_WIDTH),
                   col(NSA_WIDTH), col(NSA_WIDTH), vtile, vtile,
                   col(GATE_ROWS), col(NSA_WIDTH)],
        out_shape=[csd, csd, hm(K_AUG), hm(HEAD_DIM),
                   jax.ShapeDtypeStruct((n, RWKV_SHIFT_WIDTH), F32),
                   jax.ShapeDtypeStruct((n, RWKV_WIDTH), F32),
                   jax.ShapeDtypeStruct((NSA_WIDTH, n), BF16),
                   jax.ShapeDtypeStruct((NSA_WIDTH, n), BF16),
                   vsd, vsd,
                   jax.ShapeDtypeStruct((GATE_ROWS, n), F32),
                   jax.ShapeDtypeStruct((NSA_WIDTH, n), F32)],
        scratch_shapes=[pltpu.VMEM((2, tm, KV_WIDTH), F32)],
        compiler_params=pltpu.CompilerParams(dimension_semantics=("arbitrary",),
                                             vmem_limit_bytes=VMEM_LIMIT),
        name="in_proj",
    )(x2, norm_g, w_all, ra, rm, rp, cos, sin, onehot)


def _compress_kernel(kc_ref, vc_ref, pk_ref, w1k_ref, w2k_ref, pv_ref, w1v_ref, w2vt_ref,
                     ko_ref, vo_ref):
    half = CMP_STRIDE * HEAD_DIM
    ng, _, nch, _ = kc_ref.shape

    def hidden(c_ref, pos_ref, w1_ref):
        c = c_ref[:, 0].reshape(ng * nch, half)
        pos = jnp.broadcast_to(pos_ref[...], (BF16_SUBLANE, 2 * half)).astype(BF16)
        za = _dot(jnp.concatenate([c, pos[:, 0:half]], axis=0), w1_ref[0:half, :])
        zb = _dot(jnp.concatenate([c, pos[:, half:]], axis=0), w1_ref[half:2 * half, :])
        pv = za[ng * nch:ng * nch + 1] + zb[ng * nch:ng * nch + 1]
        hid = jnp.concatenate(
            [za[g * nch:(g + 1) * nch] + pltpu.roll(zb[g * nch:(g + 1) * nch], nch - 1, 0) for g in range(ng)],
            axis=0) + pv
        return (hid * _sigmoid(hid)).astype(BF16)

    ko = _dot(hidden(kc_ref, pk_ref, w1k_ref), w2k_ref[...]).astype(BF16)
    hv = hidden(vc_ref, pv_ref, w1v_ref)
    for g in range(ng):
        ko_ref[g, 0] = ko[g * nch:(g + 1) * nch]
        vo_ref[g, 0] = _dot_nt(w2vt_ref[...], hv[g * nch:(g + 1) * nch]).astype(BF16)


def _compress(kc_r, vc_r, pk, w1k, w2k, pv, w1v, w2vt):
    g, b, nch, width = kc_r.shape
    blk = pl.BlockSpec((g, 1, nch, width), lambda j: (0, j, 0, 0))
    full = lambda a: pl.BlockSpec(a.shape, lambda j: (0,) * a.ndim)
    return pl.pallas_call(
        _compress_kernel,
        grid=(b,),
        in_specs=[blk, blk, full(pk), full(w1k), full(w2k), full(pv), full(w1v), full(w2vt)],
        out_specs=[pl.BlockSpec((g, 1, nch, HEAD_DIM), lambda j: (0, j, 0, 0)),
                   pl.BlockSpec((g, 1, HEAD_DIM, nch), lambda j: (0, j, 0, 0))],
        out_shape=[jax.ShapeDtypeStruct((g, b, nch, HEAD_DIM), BF16),
                   jax.ShapeDtypeStruct((g, b, HEAD_DIM, nch), BF16)],
        compiler_params=pltpu.CompilerParams(dimension_semantics=("arbitrary",),
                                             vmem_limit_bytes=VMEM_LIMIT),
        name="compress",
    )(kc_r, vc_r, pk, w1k, w2k, pv, w1v, w2vt)


def _nsa_kernel(qt_ref, qrt_ref, kc_ref, vct_ref, ks_ref, vst_ref, kw_ref, vwt_ref,
                gate_ref, gn_ref, mt_ref, o_ref, *, tq, seq):
    tk = KEY_TILE
    n_win = WINDOW // tk
    qi = pl.program_id(1)
    q0 = qi * tq
    nl = NSA_REP * tq
    n_sel = seq // SEL_BLOCK
    ncp = kc_ref.shape[2]
    groups = range(NSA_KV)

    def heads_on_lanes(ref, g):
        rows = [(g * NSA_REP + r) * HEAD_DIM for r in range(NSA_REP)]
        return jnp.concatenate([ref[r0:r0 + HEAD_DIM, :] for r0 in rows], axis=1)

    def tile4(a):
        return jnp.concatenate([a] * NSA_REP, axis=1)

    k_s = lax.broadcasted_iota(jnp.int32, (tk, tq), 0)
    t_l = q0 + lax.broadcasted_iota(jnp.int32, (tk, tq), 1)

    def update_steps(box, k_ref, vt_ref, g, q_op, kts, keeps):
        m_i, acc = box[0]

        def scores(i):
            sc = _dot(k_ref[g, pl.ds(pl.multiple_of(kts[i] * tk, tk), tk), :], q_op)
            if keeps[i] is None:
                return sc
            return jnp.concatenate([jnp.where(keeps[i], sc[:, r * tq:(r + 1) * tq], NEG_INF)
                                    for r in range(NSA_REP)], axis=1)

        sc_next = scores(0)
        yield
        for i, kt in enumerate(kts):
            sc = sc_next
            if i + 1 < len(kts):
                sc_next = scores(i + 1)
            m_n = jnp.maximum(m_i, jnp.max(sc, axis=0, keepdims=True))
            pe = jnp.exp2(sc - m_n).astype(BF16)
            yield
            vtb = jnp.concatenate([vt_ref[kt * (tk // LANE) + jj, g * V_ROWS:(g + 1) * V_ROWS, :]
                                   for jj in range(tk // LANE)], axis=1)
            acc = jnp.exp2(m_i - m_n) * acc + _dot(vtb, pe)
            m_i = m_n
            yield
        box[0] = (m_i, acc)

    def last_tiles(box, k_ref, vt_ref, g, q_op, n_back, low_keep):
        kts, keeps = [], []
        for back in range(n_back + 1):
            d = t_l - ((a - back) * tk + k_s)
            kts.append(a - back)
            if back == 0:
                keeps.append(d >= 0)
            elif back == n_win and low_keep is not None:
                keeps.append(low_keep(d))
            else:
                keeps.append(None)
        return update_steps(box, k_ref, vt_ref, g, q_op, kts, keeps)

    def select_steps(g, qr, sel_out, all_selected):
        s = _dot(kc_ref[g, 0], heads_on_lanes(qt_ref, g))
        t_c = q0 + lax.broadcasted_iota(jnp.int32, (ncp, tq), 1)
        c_c = lax.broadcasted_iota(jnp.int32, (ncp, tq), 0)
        cmask = tile4((c_c * CMP_STRIDE + (CMP_BLOCK - 1)) <= t_c)
        yield
        s = jnp.where(cmask, s, NEG_INF)
        m = jnp.maximum(jnp.max(s, axis=0, keepdims=True), 0.5 * NEG_INF)
        e = jnp.exp2(s - m)
        den = jnp.sum(e, axis=0, keepdims=True)
        p = e * (1.0 / jnp.where(den > 0.0, den, 1.0))
        sel_out["o_cmp"] = _dot(vct_ref[g, 0], p.astype(BF16))
        yield
        if all_selected:
            sel_out["qr_sel"] = jnp.concatenate([qr, jnp.zeros((K_AUG - HEAD_DIM, nl), BF16)], axis=0)
            return
        psum = p[:, 0:tq]
        for r in range(1, NSA_REP):
            psum = psum + p[:, r * tq:(r + 1) * tq]
        imp = _x3_dot(mt_ref[...], psum)
        j = lax.broadcasted_iota(jnp.int32, (n_sel, tq), 0)
        t = q0 + lax.broadcasted_iota(jnp.int32, (n_sel, tq), 1)
        tb = t // SEL_BLOCK
        forced = (j == 0) | (j == tb) | (j == tb - 1)
        val = jnp.where(j <= tb, imp + jnp.where(forced, FORCE_BONUS, 0.0), -1.0)
        yield
        vals = [val[g0:g0 + SUBLANE] for g0 in range(0, n_sel, SUBLANE)]
        cnts = [jnp.zeros((SUBLANE, tq), F32) for _ in vals]
        srow = lax.broadcasted_iota(jnp.int32, (SUBLANE, tq), 0)
        for i in range(n_sel):
            vi = jnp.broadcast_to(val[i:i + 1, :], (SUBLANE, tq))
            for g, vg in enumerate(vals):
                ge = lambda: jnp.where(vi >= vg, 1.0, 0.0)
                gt = lambda: jnp.where(vi > vg, 1.0, 0.0)
                if g * SUBLANE > i:
                    beat = ge()
                elif (g + 1) * SUBLANE <= i:
                    beat = gt()
                else:
                    beat = jnp.where(srow > i - g * SUBLANE, ge(), gt())
                cnts[g] = cnts[g] + beat
            if i % SUBLANE == SUBLANE - 1:
                yield
        cnt = jnp.concatenate(cnts, axis=0)
        sel_bias = jnp.where(cnt < float(SEL_TOPK), 0.0, NEG_INF).astype(BF16)
        sel_out["qr_sel"] = jnp.concatenate(
            [qr, tile4(sel_bias), jnp.zeros((K_AUG - HEAD_DIM - n_sel, nl), BF16)], axis=0)

    init = (jnp.full((1, nl), NEG_INF, F32), jnp.zeros((V_ROWS, nl), F32))
    a = (q0 + tq - 1) // tk

    def attend(n_back):
        win_boxes, sel_boxes, sel_outs = [[init] for _ in groups], [[init] for _ in groups], [{} for _ in groups]
        all_selected = n_back < n_win and (n_back + 1) * tk <= SEL_TOPK * SEL_BLOCK
        for g in groups:
            qr = heads_on_lanes(qrt_ref, g)
            _interleave(last_tiles(win_boxes[g], kw_ref, vwt_ref, g, qr, n_back, lambda d: d < WINDOW),
                        select_steps(g, qr, sel_outs[g], all_selected))
        _interleave(*[last_tiles(sel_boxes[g], ks_ref, vst_ref, g, sel_outs[g]["qr_sel"], n_back, None)
                      for g in groups])
        c_wins = [b[0] for b in win_boxes]
        c_sels = [b[0] for b in sel_boxes]
        o_cmps = [o["o_cmp"] for o in sel_outs]
        qr_sels = [o["qr_sel"] for o in sel_outs]

        def old_tiles(kts):
            def body(carry):
                boxes = [[c] for c in carry]
                _interleave(*[update_steps(boxes[g], ks_ref, vst_ref, g, qr_sels[g], kts, [None] * len(kts))
                              for g in groups])
                return tuple(b[0] for b in boxes)
            return body

        if n_back == n_win:
            n_old = a - n_win
            c_sels = lax.fori_loop(0, n_old % 2, lambda kt, c: old_tiles([kt])(c), tuple(c_sels))
            c_sels = lax.fori_loop(0, n_old // 2,
                                   lambda i, c: old_tiles([n_old % 2 + 2 * i, n_old % 2 + 2 * i + 1])(c), c_sels)

        for g in groups:
            o_sel = c_sels[g][1][0:HEAD_DIM] * (1.0 / c_sels[g][1][HEAD_DIM:HEAD_DIM + 1])
            o_win = c_wins[g][1][0:HEAD_DIM] * (1.0 / c_wins[g][1][HEAD_DIM:HEAD_DIM + 1])
            for r in range(NSA_REP):
                ls = slice(r * tq, (r + 1) * tq)
                h = g * NSA_REP + r
                gate = lambda c: gate_ref[3 * h + c:3 * h + c + 1, :]
                o = gate(0) * o_cmps[g][:, ls] + gate(1) * o_sel[:, ls] + gate(2) * o_win[:, ls]
                rs = slice(h * HEAD_DIM, (h + 1) * HEAD_DIM)
                o_ref[rs, :] = (o * gn_ref[rs, :]).astype(BF16)

    def dispatch(n_back):
        if n_back == 0:
            attend(0)
        else:
            lax.cond(a >= n_back, lambda: attend(n_back), lambda: dispatch(n_back - 1))

    dispatch(n_win)


def _nsa(qt, qrt, kcmp, vcmpt, ksh, vst, kwh, vwt, gates, gnt, mt, *, batch, seq, tq):
    n = batch * seq
    nq = seq // tq
    ncp = kcmp.shape[2]
    qspec = pl.BlockSpec((NSA_WIDTH, tq), lambda b, i: (0, b * nq + i))
    kspec = lambda w: pl.BlockSpec((NSA_KV, seq, w), lambda b, i: (0, b, 0))
    vspec = pl.BlockSpec((seq // LANE, NSA_KV * V_ROWS, LANE), lambda b, i: (b, 0, 0))
    kern = functools.partial(_nsa_kernel, tq=tq, seq=seq)
    return pl.pallas_call(
        kern,
        grid=(batch, nq),
        in_specs=[qspec, qspec,
                  pl.BlockSpec((NSA_KV, 1, ncp, HEAD_DIM), lambda b, i: (0, b, 0, 0)),
                  pl.BlockSpec((NSA_KV, 1, HEAD_DIM, ncp), lambda b, i: (0, b, 0, 0)),
                  kspec(K_AUG), vspec, kspec(HEAD_DIM), vspec,
                  pl.BlockSpec((GATE_ROWS, tq), lambda b, i: (0, b * nq + i)),
                  qspec,
                  pl.BlockSpec(mt.shape, lambda b, i: (0, 0))],
        out_specs=qspec,
        out_shape=jax.ShapeDtypeStruct((NSA_WIDTH, n), BF16),
        compiler_params=pltpu.CompilerParams(
            dimension_semantics=("arbitrary", "arbitrary"),
            vmem_limit_bytes=VMEM_LIMIT),
        name="nsa",
    )(qt, qrt, kcmp, vcmpt, ksh, vst, kwh, vwt, gates, gnt, mt)


def _rwkv_kernel(p_ref, gr_ref, mu_ref, w0_ref, a0_ref, wab_ref, kk_ref, ka_ref, rk_ref,
                 gw_ref, gb_ref, ones_ref, tril_ref, o_ref, st_ref, carry_ref, *, tt):
    c = RWKV_CHUNK
    hd = HEAD_DIM
    nc = tt // c
    step = pl.program_id(1)

    @pl.when(step == 0)
    def _():
        st_ref[...] = jnp.zeros_like(st_ref)
        carry_ref[...] = jnp.zeros_like(carry_ref)

    p = p_ref[...]
    row = lax.broadcasted_iota(jnp.int32, p.shape, 0)
    prev = jnp.where(row == 0, carry_ref[...], pltpu.roll(p, 1, 0))
    carry_ref[...] = p[tt - 1:tt, :]
    ps = p + mu_ref[...] * (prev - p)
    r = ps[:, 0:RWKV_WIDTH]
    k = ps[:, RWKV_WIDTH:2 * RWKV_WIDTH]
    v = ps[:, 2 * RWKV_WIDTH:3 * RWKV_WIDTH]
    lora = ps[:, 3 * RWKV_WIDTH:]
    lane = lax.broadcasted_iota(jnp.int32, lora.shape, 1)
    feat = jnp.where(lane < DECAY_RANK, jnp.tanh(lora), lora).astype(BF16)
    up = _dot(feat, wab_ref[...])
    w = w0_ref[...] + up[:, 0:RWKV_WIDTH]
    lw = _sigmoid(w) * (-float(np.exp(-0.5)))
    a = _sigmoid(a0_ref[...] + up[:, RWKV_WIDTH:])
    ones_bd = ones_ref[...]
    kk = k * kk_ref[...]
    kkn = kk * lax.rsqrt(jnp.maximum(_head_sums(kk * kk, ones_bd), 1e-24))
    k2 = k * (1.0 + (a - 1.0) * ka_ref[...])
    alpha = -kkn
    beta = kkn * a
    bonus = _head_sums(r * k2 * rk_ref[...], ones_bd) * v

    cum = _x2_dot(tril_ref[...], lw)
    pc_rows = [jnp.exp(cum[(ch + 1) * c - 1:(ch + 1) * c, :]) for ch in range(nc)]
    pc = jnp.concatenate([jnp.broadcast_to(p, (c, RWKV_WIDTH)) for p in pc_rows], axis=0)
    e_neg = jnp.exp(-cum)
    at = alpha * jnp.exp(cum - lw)
    bt = beta * e_neg
    kt = k2 * e_neg
    rt = r * jnp.exp(cum)
    bh = bt * pc
    kh = kt * pc

    npair = RWKV_WIDTH // LANE

    def pairs(x):
        return jnp.stack([x[ch * c:(ch + 1) * c, j * LANE:(j + 1) * LANE]
                          for ch in range(nc) for j in range(npair)], axis=0)

    at_p, rt_p, bt_p, kt_p, v_p, bh_p, kh_p = (pairs(t) for t in (at, rt, bt, kt, v, bh, kh))
    pc_p = jnp.stack([pc_rows[ch][:, j * LANE:(j + 1) * LANE]
                      for ch in range(nc) for j in range(npair)], axis=0)
    lane_c = lax.broadcasted_iota(jnp.int32, (1, c, LANE), 2)
    row_c = lax.broadcasted_iota(jnp.int32, (1, c, LANE), 1)
    even_c = lane_c < hd
    col_c = jnp.where(even_c, lane_c, lane_c - hd)
    low_s = row_c > col_c
    low_i = row_c >= col_c
    lane_2c = lax.broadcasted_iota(jnp.int32, (1, 2 * c, LANE), 2)
    row_2c = lax.broadcasted_iota(jnp.int32, (1, 2 * c, LANE), 1)
    even_2c = lane_2c < hd
    on_bd = (row_2c < hd) == even_2c
    zero_c = jnp.zeros((1, c, LANE), BF16)

    def bd(x):
        xb = x.astype(BF16)
        return jnp.concatenate([jnp.where(even_c, xb, zero_c), jnp.where(even_c, zero_c, xb)], axis=1)

    def abd(x):
        xb = x.astype(BF16)
        return jnp.concatenate([jnp.where(even_c, zero_c, xb), jnp.where(even_c, xb, zero_c)], axis=1)

    la = jnp.concatenate([at_p, rt_p], axis=1).astype(BF16)
    zero_2c = jnp.zeros((1, 2 * c, LANE), BF16)
    r_e = _bmm_nt(jnp.where(even_2c, la, zero_2c), jnp.concatenate([bt_p, kt_p], axis=1).astype(BF16))
    r_o = _bmm_nt(jnp.where(even_2c, zero_2c, la), jnp.concatenate([kt_p, bt_p], axis=1).astype(BF16))
    nab = jnp.where(low_s, jnp.where(even_c, r_e[:, 0:c], r_o[:, 0:c]), 0.0)
    aak_sw = jnp.where(low_s, jnp.where(even_c, r_o[:, 0:c], r_e[:, 0:c]), 0.0).astype(BF16)
    arb = jnp.where(low_i, jnp.where(even_c, r_e[:, c:], r_o[:, c:]), 0.0).astype(BF16)
    ark_sw = jnp.where(low_i, jnp.where(even_c, r_o[:, c:], r_e[:, c:]), 0.0).astype(BF16)
    tinv = jnp.where(row_c == col_c, 1.0, 0.0) + nab
    npow = _bmm(nab.astype(BF16), bd(nab))
    n_dbl = 5
    for it in range(n_dbl):
        nbd = bd(npow)
        if it + 1 < n_dbl:
            res = _bmm(jnp.concatenate([tinv, npow], axis=1).astype(BF16), nbd)
            tinv = tinv + res[:, 0:c]
            npow = res[:, c:]
        else:
            tinv = tinv + _bmm(tinv.astype(BF16), nbd)
    av = _bmm(jnp.concatenate([aak_sw, ark_sw], axis=1), abd(v_p))
    tx = _bmm(tinv.astype(BF16), jnp.concatenate([bd(av[:, 0:c]), bd(at_p)], axis=2))
    u0, ta = tx[:, :, 0:LANE], tx[:, :, LANE:]
    ax = _bmm(arb, jnp.concatenate([bd(ta), bd(u0)], axis=2))
    rq = rt_p + ax[:, :, 0:LANE]
    y0 = ax[:, :, LANE:] + av[:, c:]
    v_b = v_p.astype(BF16)
    w_f = jnp.concatenate([jnp.concatenate([ta, u0], axis=2).astype(BF16),
                           jnp.concatenate([jnp.zeros_like(v_b), v_b], axis=2)], axis=1)
    gh = _bmm(jnp.concatenate([jnp.swapaxes(bh_p, 1, 2), jnp.swapaxes(kh_p, 1, 2)], axis=2).astype(BF16), w_f)
    g_bd = jnp.where(on_bd, gh[:, :, 0:LANE], 0.0) + jnp.where(row_2c == lane_2c, pc_p, 0.0)
    h_bd = jnp.where(on_bd, gh[:, :, LANE:], 0.0)
    lhs = jnp.concatenate([g_bd, rq], axis=1).astype(BF16)

    st = st_ref[...]
    ys = []
    for ch in range(nc):
        sl = slice(ch * npair, (ch + 1) * npair)
        res = _bmm(lhs[sl], st.astype(BF16))
        st = res[:, 0:2 * hd, :] + h_bd[sl]
        yc = res[:, 2 * hd:, :] + y0[sl]
        ys.append(jnp.concatenate([yc[j] for j in range(npair)], axis=1))
    st_ref[...] = st
    y = jnp.concatenate(ys, axis=0) if nc > 1 else ys[0]

    inv_hd = 1.0 / hd
    mean = _head_sums(y, ones_bd) * inv_hd
    ycen = y - mean
    var = _head_sums(ycen * ycen, ones_bd) * inv_hd
    yn = ycen * lax.rsqrt(var + GN_EPS) * gw_ref[...] + gb_ref[...]
    o_ref[...] = ((yn + bonus) * gr_ref[...]).astype(BF16)


def _rwkv(rw, gr, mu, w0, a0, wab, kk, ka, rk, gw, gb, ones_bd, tril, *, batch, seq, tt):
    n = batch * seq
    ns = seq // tt
    row = lambda w: pl.BlockSpec((tt, w), lambda b, i: (b * ns + i, 0))
    full = lambda a: pl.BlockSpec(a.shape, lambda b, i: (0,) * a.ndim)
    kern = functools.partial(_rwkv_kernel, tt=tt)
    consts = (mu, w0, a0, wab, kk, ka, rk, gw, gb, ones_bd, tril)
    return pl.pallas_call(
        kern,
        grid=(batch, ns),
        in_specs=[row(RWKV_SHIFT_WIDTH), row(RWKV_WIDTH)] + [full(a) for a in consts],
        out_specs=row(RWKV_WIDTH),
        out_shape=jax.ShapeDtypeStruct((n, RWKV_WIDTH), BF16),
        scratch_shapes=[pltpu.VMEM((RWKV_WIDTH // LANE, LANE, LANE), F32),
                        pltpu.VMEM((1, RWKV_SHIFT_WIDTH), F32)],
        compiler_params=pltpu.CompilerParams(dimension_semantics=("arbitrary", "arbitrary"),
                                             vmem_limit_bytes=VMEM_LIMIT),
        name="rwkv",
    )(rw, gr, *consts)


def _out_kernel(x_ref, ont_ref, or_ref, wn_ref, wr_ref, g_ref, o_ref, *, n_sub):
    ts = x_ref.shape[0] // n_sub

    def project(j):
        rows = slice(j * ts, (j + 1) * ts)
        return (x_ref[rows, :] + _dot_tn(ont_ref[:, rows], wn_ref[...])
                + _dot(or_ref[rows, :], wr_ref[...]))

    def finish(j, h):
        ms = jnp.mean(h * h, axis=-1, keepdims=True)
        o_ref[j * ts:(j + 1) * ts, :] = h * lax.rsqrt(ms + RMS_EPS) * g_ref[...]

    h = project(0)
    for j in range(1, n_sub):
        h_next = project(j)
        finish(j - 1, h)
        h = h_next
    finish(n_sub - 1, h)


def _out_proj(x2, o_nsa_t, o_rwkv, wn, wr, final_g, *, tm, n_sub):
    n = x2.shape[0]
    row = lambda w: pl.BlockSpec((tm, w), lambda i: (i, 0))
    full = lambda a: pl.BlockSpec(a.shape, lambda i: (0,) * a.ndim)
    return pl.pallas_call(
        functools.partial(_out_kernel, n_sub=n_sub),
        grid=(n // tm,),
        in_specs=[row(D_MODEL), pl.BlockSpec((NSA_WIDTH, tm), lambda i: (0, i)), row(RWKV_WIDTH),
                  full(wn), full(wr), full(final_g)],
        out_specs=row(D_MODEL),
        out_shape=jax.ShapeDtypeStruct((n, D_MODEL), F32),
        compiler_params=pltpu.CompilerParams(dimension_semantics=("arbitrary",),
                                             vmem_limit_bytes=VMEM_LIMIT),
        name="out_proj",
    )(x2, o_nsa_t, o_rwkv, wn, wr, final_g)


def _rope_tables(seq):
    inv = ROPE_THETA ** (-np.arange(ROPE_HALF, dtype=np.float64) / ROPE_HALF)
    ang = np.arange(seq, dtype=np.float64)[:, None] * inv[None, :]
    cos, sin = np.cos(ang), np.sin(ang)
    ra = np.ones((seq, HEAD_DIM)); rm = np.zeros((seq, HEAD_DIM)); rp = np.zeros((seq, HEAD_DIM))
    ra[:, :ROPE_HALF] = cos; ra[:, ROPE_HALF:ROPE_DIM] = cos
    rm[:, :ROPE_HALF] = -sin
    rp[:, ROPE_HALF:ROPE_DIM] = sin
    rep = lambda t: jnp.asarray(np.tile(t, (1, LANE // HEAD_DIM)), F32)
    assert seq // SEL_BLOCK <= K_AUG - HEAD_DIM
    onehot = np.zeros((seq, K_AUG - HEAD_DIM))
    onehot[np.arange(seq), np.arange(seq) // SEL_BLOCK] = 1.0
    return (rep(ra), rep(rm), rep(rp), jnp.asarray(cos.T, F32), jnp.asarray(sin.T, F32),
            jnp.asarray(onehot, BF16))


def _cmp_to_sel_t(n_cmp_pad, n_sel):
    n_cmp = n_cmp_pad - 1
    c0 = np.arange(n_cmp)[:, None] * CMP_STRIDE
    s0 = np.arange(n_sel)[None, :] * SEL_BLOCK
    ov = np.clip(np.minimum(c0 + CMP_BLOCK, s0 + SEL_BLOCK) - np.maximum(c0, s0), 0, None) / CMP_BLOCK
    mt = np.zeros((n_sel, n_cmp_pad))
    mt[:, :n_cmp] = ov.T
    return jnp.asarray(mt, BF16)


def _prep_w_in(w_in):
    idx = np.cumsum(IN_SIZES)[:-1].tolist()
    q, kc, vc, ks, vs, kw, vw, gl, gn, rw, gr = jnp.split(w_in, idx, axis=1)
    pad = jnp.zeros((D_MODEL, GATE_ROWS - gl.shape[1]), w_in.dtype)
    return jnp.concatenate([kc, vc, ks, kw, rw, gr, q, vs, vw, gl, pad, gn], axis=1).T.astype(BF16)


def _layer(x2, norm_g, w_in, cmp_pos_k, cmp_w1_k, cmp_w2_k, cmp_pos_v, cmp_w1_v, cmp_w2_v,
           shift_mu, decay_w0, decay_up, iclr_a0, iclr_up, k_k, k_a, r_k, gn_w, gn_b, w_out,
           final_g, *, batch, seq):
    tm = 256
    tq = 256
    tt = 256
    assert WINDOW % KEY_TILE == 0 and seq % KEY_TILE == 0 and KEY_TILE % tq == 0
    nch = seq // CMP_STRIDE
    n_sel = seq // SEL_BLOCK
    rowv = lambda t: t.reshape(1, -1).astype(F32)

    (kch, vch, ksh, kwh, rw, gr, qt, qrt, vst, vwt, gates, gnt) = _in_proj(
        x2, rowv(norm_g), _prep_w_in(w_in), _rope_tables(seq), seq=seq, tm=2 * tm)

    chunks = lambda t: t.reshape(NSA_KV, batch, nch, CMP_STRIDE * HEAD_DIM)
    kcmp, vcmpt = _compress(chunks(kch), chunks(vch),
                            rowv(cmp_pos_k), cmp_w1_k.astype(BF16), cmp_w2_k.astype(BF16),
                            rowv(cmp_pos_v), cmp_w1_v.astype(BF16), cmp_w2_v.T.astype(BF16))

    o_nsa_t = _nsa(qt, qrt, kcmp, vcmpt, ksh, vst, kwh, vwt, gates, gnt,
                   _cmp_to_sel_t(nch, n_sel), batch=batch, seq=seq, tq=tq)

    z = jnp.zeros((DECAY_RANK, RWKV_WIDTH), F32)
    wab = jnp.concatenate([jnp.concatenate([decay_up, z], axis=1),
                           jnp.concatenate([z, iclr_up], axis=1)], axis=0).astype(BF16)
    hid = np.arange(2 * LANE) // HEAD_DIM
    ones_bd = jnp.asarray(hid[:, None] == hid[None, :], BF16)
    ti = np.arange(tt)
    tril = (ti[:, None] >= ti[None, :]) & (ti[:, None] // RWKV_CHUNK == ti[None, :] // RWKV_CHUNK)
    tril = jnp.asarray(np.concatenate([tril, tril], axis=1), BF16)
    o_rwkv = _rwkv(rw, gr, rowv(shift_mu), rowv(decay_w0), rowv(iclr_a0), wab, rowv(k_k), rowv(k_a),
                   rowv(r_k), rowv(gn_w), rowv(gn_b), ones_bd, tril, batch=batch, seq=seq, tt=tt)

    w_o = w_out.astype(BF16)
    return _out_proj(x2, o_nsa_t, o_rwkv, w_o[:NSA_WIDTH], w_o[NSA_WIDTH:], rowv(final_g), tm=4 * tm, n_sub=4)


def kernel(x, norm_g, w_in, cmp_pos_k, cmp_w1_k, cmp_w2_k, cmp_pos_v, cmp_w1_v, cmp_w2_v, shift_mu, decay_w0, decay_up, iclr_a0, iclr_up, k_k, k_a, r_k, gn_w, gn_b, w_out, final_g):
    batch, seq, d = x.shape
    assert d == D_MODEL and norm_g.shape[0] == 1, "single-layer trunk"
    out = _layer(x.reshape(batch * seq, d), norm_g[0], w_in[0], cmp_pos_k[0], cmp_w1_k[0], cmp_w2_k[0],
                 cmp_pos_v[0], cmp_w1_v[0], cmp_w2_v[0], shift_mu[0], decay_w0[0], decay_up[0],
                 iclr_a0[0], iclr_up[0], k_k[0], k_a[0], r_k[0], gn_w[0], gn_b[0], w_out[0],
                 final_g, batch=batch, seq=seq)
    return out.reshape(batch, seq, d)
```

```python
import functools

import numpy as np
import jax
import jax.numpy as jnp
from jax import lax
from jax.experimental import pallas as pl
from jax.experimental.pallas import tpu as pltpu

F32 = jnp.float32
BF16 = jnp.bfloat16

D_MODEL = 1024
HEAD_DIM = 64
NSA_HEADS = 8
NSA_KV = 2
NSA_REP = NSA_HEADS // NSA_KV
RWKV_HEADS = 8
NSA_WIDTH = NSA_HEADS * HEAD_DIM
RWKV_WIDTH = RWKV_HEADS * HEAD_DIM
KV_WIDTH = NSA_KV * HEAD_DIM
ROPE_DIM = HEAD_DIM // 4
ROPE_HALF = ROPE_DIM // 2
ROPE_THETA = 500000.0
CMP_BLOCK = 32
CMP_STRIDE = 16
CMP_HIDDEN = 256
SEL_BLOCK = 64
SEL_TOPK = 8
WINDOW = 512
DECAY_RANK = 64
ICLR_RANK = 64
RWKV_SHIFT_WIDTH = 3 * RWKV_WIDTH + DECAY_RANK + ICLR_RANK
IN_SIZES = (NSA_WIDTH, KV_WIDTH, KV_WIDTH, KV_WIDTH, KV_WIDTH, KV_WIDTH, KV_WIDTH,
            3 * NSA_HEADS, NSA_WIDTH, RWKV_SHIFT_WIDTH, RWKV_WIDTH)
SCALE = HEAD_DIM ** -0.5
RMS_EPS = 1e-6
GN_EPS = 64e-5
NEG_INF = -1e30
FORCE_BONUS = 1e3

LANE = 128
SUBLANE = 8
BF16_SUBLANE = 2 * SUBLANE
GATE_ROWS = 2 * BF16_SUBLANE

T_KV = 0
T_RW = T_KV + 4 * KV_WIDTH
T_GR = T_RW + RWKV_SHIFT_WIDTH
T_END = T_GR + RWKV_WIDTH
R_Q = T_END
R_VS = R_Q + NSA_WIDTH
R_VW = R_VS + KV_WIDTH
R_GL = R_VW + KV_WIDTH
R_GN = R_GL + GATE_ROWS
R_END = R_GN + NSA_WIDTH

LOG2E = float(np.log2(np.e))
K_AUG = LANE
V_ROWS = HEAD_DIM + BF16_SUBLANE
KEY_TILE = 256
RWKV_CHUNK = 64
VMEM_LIMIT = 48 * 1024 * 1024


def _dot(a, b):
    return jnp.dot(a, b, preferred_element_type=F32)


def _dot_nt(a, b):
    return lax.dot_general(a, b, (((1,), (1,)), ((), ())), preferred_element_type=F32)


def _dot_tn(a, b):
    return lax.dot_general(a, b, (((0,), (0,)), ((), ())), preferred_element_type=F32)


def _bmm(a, b):
    return lax.dot_general(a, b, (((2,), (1,)), ((0,), (0,))), preferred_element_type=F32)


def _bmm_nt(a, b):
    return lax.dot_general(a, b, (((2,), (2,)), ((0,), (0,))), preferred_element_type=F32)


def _split3(x):
    hi = x.astype(BF16)
    r1 = x - hi.astype(F32)
    mid = r1.astype(BF16)
    lo = (r1 - mid.astype(F32)).astype(BF16)
    return hi, mid, lo


def _head_sums(x, ones_blk):
    w = ones_blk.shape[0]
    xb = x.astype(BF16)
    return jnp.concatenate([_dot(xb[:, i:i + w], ones_blk) for i in range(0, x.shape[1], w)], axis=1)


def _x2_dot(ww_bf16, x):
    hi = x.astype(BF16)
    lo = (x - hi.astype(F32)).astype(BF16)
    return _dot(ww_bf16, jnp.concatenate([hi, lo], axis=0))


def _x3_dot(w_bf16, x):
    hi, mid, lo = _split3(x)
    return _dot(w_bf16, hi) + _dot(w_bf16, mid) + _dot(w_bf16, lo)


def _sigmoid(x):
    return 0.5 * jnp.tanh(0.5 * x) + 0.5


def _interleave(*gens):
    live = list(gens)
    while live:
        for g in list(live):
            try:
                next(g)
            except StopIteration:
                live.remove(g)


def _rope128(t, ra, rm, rp):
    return t * ra + pltpu.roll(t, LANE - ROPE_HALF, 1) * rm + pltpu.roll(t, ROPE_HALF, 1) * rp


def _in_proj_kernel(x_ref, g_ref, w_ref, ra_ref, rm_ref, rp_ref, cos_ref, sin_ref, oh_ref,
                    kc_ref, vc_ref, ks_ref, kw_ref, rw_ref, gr_ref,
                    qt_ref, qrt_ref, vst_ref, vwt_ref, gate_ref, gn_ref, cmp_scr):
    x = x_ref[...]
    ms = jnp.mean(x * x, axis=-1, keepdims=True)
    y = (x * lax.rsqrt(ms + RMS_EPS) * g_ref[...]).astype(BF16)
    tm = x.shape[0]

    kv = _dot_nt(y, w_ref[T_KV:T_RW, :])
    for i, ref in enumerate((kc_ref, vc_ref)):
        cmp_scr[i] = kv[:, i * KV_WIDTH:(i + 1) * KV_WIDTH]
        for tau in range(CMP_STRIDE):
            piece = cmp_scr[i, pl.ds(tau, tm // CMP_STRIDE, stride=CMP_STRIDE), :].astype(BF16)
            for g in range(NSA_KV):
                ref[g, :, tau * HEAD_DIM:(tau + 1) * HEAD_DIM] = piece[:, g * HEAD_DIM:(g + 1) * HEAD_DIM]
    ra, rm, rp = ra_ref[...], rm_ref[...], rp_ref[...]
    for i, ref in ((2, ks_ref), (3, kw_ref)):
        t = _rope128(kv[:, i * LANE:(i + 1) * LANE], ra, rm, rp)
        for g in range(NSA_KV):
            tg = t[:, g * HEAD_DIM:(g + 1) * HEAD_DIM].astype(BF16)
            if ref is ks_ref:
                ref[g] = jnp.concatenate([tg, oh_ref[...]], axis=1)
            else:
                ref[g] = tg
    rw_ref[...] = _dot_nt(y, w_ref[T_RW:T_GR, :])
    gr = _dot_nt(y, w_ref[T_GR:T_END, :])
    gr_ref[...] = gr * _sigmoid(gr)

    def proj_t(r0, r1):
        return _dot_nt(w_ref[r0:r1, :], y)

    ft = proj_t(R_Q, R_END)
    qt = ft[0:R_VS - R_Q] * (SCALE * LOG2E)
    cos, sin = cos_ref[...], sin_ref[...]
    qt_ref[...] = qt.astype(BF16)
    for h in range(NSA_HEADS):
        r0 = h * HEAD_DIM
        t1 = qt[r0:r0 + ROPE_HALF]
        t2 = qt[r0 + ROPE_HALF:r0 + ROPE_DIM]
        qrt_ref[r0:r0 + ROPE_DIM, :] = jnp.concatenate(
            [t1 * cos - t2 * sin, t2 * cos + t1 * sin], axis=0).astype(BF16)
        qrt_ref[r0 + ROPE_DIM:r0 + HEAD_DIM, :] = qt[r0 + ROPE_DIM:r0 + HEAD_DIM].astype(BF16)
    vg = ft[R_VS - R_Q:R_GN - R_Q]
    vt = vg[0:R_GL - R_VS].astype(BF16)
    ones = jnp.ones((V_ROWS - HEAD_DIM, LANE), BF16)
    for j in range(vt.shape[1] // LANE):
        for i, ref in enumerate((vst_ref, vwt_ref)):
            for g in range(NSA_KV):
                r0 = i * KV_WIDTH + g * HEAD_DIM
                ref[j, g * V_ROWS:(g + 1) * V_ROWS, :] = jnp.concatenate(
                    [vt[r0:r0 + HEAD_DIM, j * LANE:(j + 1) * LANE], ones], axis=0)
    gate_ref[...] = _sigmoid(vg[R_GL - R_VS:])
    gn = ft[R_GN - R_Q:]
    gn_ref[...] = gn * _sigmoid(gn)


def _in_proj(x2, norm_g, w_all, tabs, *, seq, tm):
    n = x2.shape[0]
    spt = seq // tm
    ra, rm, rp, cos, sin, onehot = tabs
    hm = lambda w: jax.ShapeDtypeStruct((NSA_KV, n, w), BF16)
    hspec = lambda w: pl.BlockSpec((NSA_KV, tm, w), lambda i: (0, i, 0))
    row = lambda w: pl.BlockSpec((tm, w), lambda i: (i, 0))
    col = lambda r: pl.BlockSpec((r, tm), lambda i: (0, i))
    full = lambda a: pl.BlockSpec(a.shape, lambda i: (0,) * a.ndim)
    tab = lambda w: pl.BlockSpec((tm, w), lambda i: (i % spt, 0))
    tabt = pl.BlockSpec((ROPE_HALF, tm), lambda i: (0, i % spt))
    vtile = pl.BlockSpec((tm // LANE, NSA_KV * V_ROWS, LANE), lambda i: (i, 0, 0))
    vsd = jax.ShapeDtypeStruct((n // LANE, NSA_KV * V_ROWS, LANE), BF16)
    cw = CMP_STRIDE * HEAD_DIM
    cspec = pl.BlockSpec((NSA_KV, tm // CMP_STRIDE, cw), lambda i: (0, i, 0))
    csd = jax.ShapeDtypeStruct((NSA_KV, n // CMP_STRIDE, cw), BF16)
    return pl.pallas_call(
        _in_proj_kernel,
        grid=(n // tm,),
        in_specs=[row(D_MODEL), full(norm_g), full(w_all), tab(LANE), tab(LANE), tab(LANE),
                  tabt, tabt, tab(K_AUG - HEAD_DIM)],
        out_specs=[cspec, cspec, hspec(K_AUG), hspec(HEAD_DIM), row(RWKV_SHIFT_WIDTH), row(RWKV_WIDTH),
                   col(NSA_WIDTH), col(NSA_WIDTH), vtile, vtile,
                   col(GATE_ROWS), col(NSA_WIDTH)],
        out_shape=[csd, csd, hm(K_AUG), hm(HEAD_DIM),
                   jax.ShapeDtypeStruct((n, RWKV_SHIFT_WIDTH), F32),
                   jax.ShapeDtypeStruct((n, RWKV_WIDTH), F32),
                   jax.ShapeDtypeStruct((NSA_WIDTH, n), BF16),
                   jax.ShapeDtypeStruct((NSA_WIDTH, n), BF16),
                   vsd, vsd,
                   jax.ShapeDtypeStruct((GATE_ROWS, n), F32),
                   jax.ShapeDtypeStruct((NSA_WIDTH, n), F32)],
        scratch_shapes=[pltpu.VMEM((2, tm, KV_WIDTH), F32)],
        compiler_params=pltpu.CompilerParams(dimension_semantics=("arbitrary",),
                                             vmem_limit_bytes=VMEM_LIMIT),
        name="in_proj",
    )(x2, norm_g, w_all, ra, rm, rp, cos, sin, onehot)


def _compress_kernel(kc_ref, vc_ref, pk_ref, w1k_ref, w2k_ref, pv_ref, w1v_ref, w2vt_ref,
                     ko_ref, vo_ref):
    half = CMP_STRIDE * HEAD_DIM
    ng, _, nch, _ = kc_ref.shape

    def hidden(c_ref, pos_ref, w1_ref):
        c = c_ref[:, 0].reshape(ng * nch, half)
        pos = jnp.broadcast_to(pos_ref[...], (BF16_SUBLANE, 2 * half)).astype(BF16)
        za = _dot(jnp.concatenate([c, pos[:, 0:half]], axis=0), w1_ref[0:half, :])
        zb = _dot(jnp.concatenate([c, pos[:, half:]], axis=0), w1_ref[half:2 * half, :])
        pv = za[ng * nch:ng * nch + 1] + zb[ng * nch:ng * nch + 1]
        hid = jnp.concatenate(
            [za[g * nch:(g + 1) * nch] + pltpu.roll(zb[g * nch:(g + 1) * nch], nch - 1, 0) for g in range(ng)],
            axis=0) + pv
        return (hid * _sigmoid(hid)).astype(BF16)

    ko = _dot(hidden(kc_ref, pk_ref, w1k_ref), w2k_ref[...]).astype(BF16)
    hv = hidden(vc_ref, pv_ref, w1v_ref)
    for g in range(ng):
        ko_ref[g, 0] = ko[g * nch:(g + 1) * nch]
        vo_ref[g, 0] = _dot_nt(w2vt_ref[...], hv[g * nch:(g + 1) * nch]).astype(BF16)


def _compress(kc_r, vc_r, pk, w1k, w2k, pv, w1v, w2vt):
    g, b, nch, width = kc_r.shape
    blk = pl.BlockSpec((g, 1, nch, width), lambda j: (0, j, 0, 0))
    full = lambda a: pl.BlockSpec(a.shape, lambda j: (0,) * a.ndim)
    return pl.pallas_call(
        _compress_kernel,
        grid=(b,),
        in_specs=[blk, blk, full(pk), full(w1k), full(w2k), full(pv), full(w1v), full(w2vt)],
        out_specs=[pl.BlockSpec((g, 1, nch, HEAD_DIM), lambda j: (0, j, 0, 0)),
                   pl.BlockSpec((g, 1, HEAD_DIM, nch), lambda j: (0, j, 0, 0))],
        out_shape=[jax.ShapeDtypeStruct((g, b, nch, HEAD_DIM), BF16),
                   jax.ShapeDtypeStruct((g, b, HEAD_DIM, nch), BF16)],
        compiler_params=pltpu.CompilerParams(dimension_semantics=("arbitrary",),
                                             vmem_limit_bytes=VMEM_LIMIT),
        name="compress",
    )(kc_r, vc_r, pk, w1k, w2k, pv, w1v, w2vt)


def _nsa_kernel(qt_ref, qrt_ref, kc_ref, vct_ref, ks_ref, vst_ref, kw_ref, vwt_ref,
                gate_ref, gn_ref, mt_ref, o_ref, *, tq, seq):
    tk = KEY_TILE
    n_win = WINDOW // tk
    qi = pl.program_id(1)
    q0 = qi * tq
    nl = NSA_REP * tq
    n_sel = seq // SEL_BLOCK
    ncp = kc_ref.shape[2]
    groups = range(NSA_KV)

    def heads_on_lanes(ref, g):
        rows = [(g * NSA_REP + r) * HEAD_DIM for r in range(NSA_REP)]
        return jnp.concatenate([ref[r0:r0 + HEAD_DIM, :] for r0 in rows], axis=1)

    def tile4(a):
        return jnp.concatenate([a] * NSA_REP, axis=1)

    k_s = lax.broadcasted_iota(jnp.int32, (tk, tq), 0)
    t_l = q0 + lax.broadcasted_iota(jnp.int32, (tk, tq), 1)

    def update_steps(box, k_ref, vt_ref, g, q_op, kts, keeps):
        m_i, acc = box[0]

        def scores(i):
            sc = _dot(k_ref[g, pl.ds(pl.multiple_of(kts[i] * tk, tk), tk), :], q_op)
            if keeps[i] is None:
                return sc
            return jnp.concatenate([jnp.where(keeps[i], sc[:, r * tq:(r + 1) * tq], NEG_INF)
                                    for r in range(NSA_REP)], axis=1)

        sc_next = scores(0)
        yield
        for i, kt in enumerate(kts):
            sc = sc_next
            if i + 1 < len(kts):
                sc_next = scores(i + 1)
            m_n = jnp.maximum(m_i, jnp.max(sc, axis=0, keepdims=True))
            pe = jnp.exp2(sc - m_n).astype(BF16)
            yield
            vtb = jnp.concatenate([vt_ref[kt * (tk // LANE) + jj, g * V_ROWS:(g + 1) * V_ROWS, :]
                                   for jj in range(tk // LANE)], axis=1)
            acc = jnp.exp2(m_i - m_n) * acc + _dot(vtb, pe)
            m_i = m_n
            yield
        box[0] = (m_i, acc)

    def last_tiles(box, k_ref, vt_ref, g, q_op, n_back, low_keep):
        kts, keeps = [], []
        for back in range(n_back + 1):
            d = t_l - ((a - back) * tk + k_s)
            kts.append(a - back)
            if back == 0:
                keeps.append(d >= 0)
            elif back == n_win and low_keep is not None:
                keeps.append(low_keep(d))
            else:
                keeps.append(None)
        return update_steps(box, k_ref, vt_ref, g, q_op, kts, keeps)

    def select_steps(g, qr, sel_out, all_selected):
        s = _dot(kc_ref[g, 0], heads_on_lanes(qt_ref, g))
        t_c = q0 + lax.broadcasted_iota(jnp.int32, (ncp, tq), 1)
        c_c = lax.broadcasted_iota(jnp.int32, (ncp, tq), 0)
        cmask = tile4((c_c * CMP_STRIDE + (CMP_BLOCK - 1)) <= t_c)
        yield
        s = jnp.where(cmask, s, NEG_INF)
        m = jnp.maximum(jnp.max(s, axis=0, keepdims=True), 0.5 * NEG_INF)
        e = jnp.exp2(s - m)
        den = jnp.sum(e, axis=0, keepdims=True)
        p = e * (1.0 / jnp.where(den > 0.0, den, 1.0))
        sel_out["o_cmp"] = _dot(vct_ref[g, 0], p.astype(BF16))
        yield
        if all_selected:
            sel_out["qr_sel"] = jnp.concatenate([qr, jnp.zeros((K_AUG - HEAD_DIM, nl), BF16)], axis=0)
            return
        psum = p[:, 0:tq]
        for r in range(1, NSA_REP):
            psum = psum + p[:, r * tq:(r + 1) * tq]
        imp = _x3_dot(mt_ref[...], psum)
        j = lax.broadcasted_iota(jnp.int32, (n_sel, tq), 0)
        t = q0 + lax.broadcasted_iota(jnp.int32, (n_sel, tq), 1)
        tb = t // SEL_BLOCK
        forced = (j == 0) | (j == tb) | (j == tb - 1)
        val = jnp.where(j <= tb, imp + jnp.where(forced, FORCE_BONUS, 0.0), -1.0)
        yield
        vals = [val[g0:g0 + SUBLANE] for g0 in range(0, n_sel, SUBLANE)]
        cnts = [jnp.zeros((SUBLANE, tq), F32) for _ in vals]
        srow = lax.broadcasted_iota(jnp.int32, (SUBLANE, tq), 0)
        for i in range(n_sel):
            vi = jnp.broadcast_to(val[i:i + 1, :], (SUBLANE, tq))
            for g, vg in enumerate(vals):
                ge = lambda: jnp.where(vi >= vg, 1.0, 0.0)
                gt = lambda: jnp.where(vi > vg, 1.0, 0.0)
                if g * SUBLANE > i:
                    beat = ge()
                elif (g + 1) * SUBLANE <= i:
                    beat = gt()
                else:
                    beat = jnp.where(srow > i - g * SUBLANE, ge(), gt())
                cnts[g] = cnts[g] + beat
            if i % SUBLANE == SUBLANE - 1:
                yield
        cnt = jnp.concatenate(cnts, axis=0)
        sel_bias = jnp.where(cnt < float(SEL_TOPK), 0.0, NEG_INF).astype(BF16)
        sel_out["qr_sel"] = jnp.concatenate(
            [qr, tile4(sel_bias), jnp.zeros((K_AUG - HEAD_DIM - n_sel, nl), BF16)], axis=0)

    init = (jnp.full((1, nl), NEG_INF, F32), jnp.zeros((V_ROWS, nl), F32))
    a = (q0 + tq - 1) // tk

    def attend(n_back):
        win_boxes, sel_boxes, sel_outs = [[init] for _ in groups], [[init] for _ in groups], [{} for _ in groups]
        all_selected = n_back < n_win and (n_back + 1) * tk <= SEL_TOPK * SEL_BLOCK
        for g in groups:
            qr = heads_on_lanes(qrt_ref, g)
            _interleave(last_tiles(win_boxes[g], kw_ref, vwt_ref, g, qr, n_back, lambda d: d < WINDOW),
                        select_steps(g, qr, sel_outs[g], all_selected))
        _interleave(*[last_tiles(sel_boxes[g], ks_ref, vst_ref, g, sel_outs[g]["qr_sel"], n_back, None)
                      for g in groups])
        c_wins = [b[0] for b in win_boxes]
        c_sels = [b[0] for b in sel_boxes]
        o_cmps = [o["o_cmp"] for o in sel_outs]
        qr_sels = [o["qr_sel"] for o in sel_outs]

        def old_tiles(kts):
            def body(carry):
                boxes = [[c] for c in carry]
                _interleave(*[update_steps(boxes[g], ks_ref, vst_ref, g, qr_sels[g], kts, [None] * len(kts))
                              for g in groups])
                return tuple(b[0] for b in boxes)
            return body

        if n_back == n_win:
            n_old = a - n_win
            c_sels = lax.fori_loop(0, n_old % 2, lambda kt, c: old_tiles([kt])(c), tuple(c_sels))
            c_sels = lax.fori_loop(0, n_old // 2,
                                   lambda i, c: old_tiles([n_old % 2 + 2 * i, n_old % 2 + 2 * i + 1])(c), c_sels)

        for g in groups:
            o_sel = c_sels[g][1][0:HEAD_DIM] * (1.0 / c_sels[g][1][HEAD_DIM:HEAD_DIM + 1])
            o_win = c_wins[g][1][0:HEAD_DIM] * (1.0 / c_wins[g][1][HEAD_DIM:HEAD_DIM + 1])
            for r in range(NSA_REP):
                ls = slice(r * tq, (r + 1) * tq)
                h = g * NSA_REP + r
                gate = lambda c: gate_ref[3 * h + c:3 * h + c + 1, :]
                o = gate(0) * o_cmps[g][:, ls] + gate(1) * o_sel[:, ls] + gate(2) * o_win[:, ls]
                rs = slice(h * HEAD_DIM, (h + 1) * HEAD_DIM)
                o_ref[rs, :] = (o * gn_ref[rs, :]).astype(BF16)

    def dispatch(n_back):
        if n_back == 0:
            attend(0)
        else:
            lax.cond(a >= n_back, lambda: attend(n_back), lambda: dispatch(n_back - 1))

    dispatch(n_win)


def _nsa(qt, qrt, kcmp, vcmpt, ksh, vst, kwh, vwt, gates, gnt, mt, *, batch, seq, tq):
    n = batch * seq
    nq = seq // tq
    ncp = kcmp.shape[2]
    qspec = pl.BlockSpec((NSA_WIDTH, tq), lambda b, i: (0, b * nq + i))
    kspec = lambda w: pl.BlockSpec((NSA_KV, seq, w), lambda b, i: (0, b, 0))
    vspec = pl.BlockSpec((seq // LANE, NSA_KV * V_ROWS, LANE), lambda b, i: (b, 0, 0))
    kern = functools.partial(_nsa_kernel, tq=tq, seq=seq)
    return pl.pallas_call(
        kern,
        grid=(batch, nq),
        in_specs=[qspec, qspec,
                  pl.BlockSpec((NSA_KV, 1, ncp, HEAD_DIM), lambda b, i: (0, b, 0, 0)),
                  pl.BlockSpec((NSA_KV, 1, HEAD_DIM, ncp), lambda b, i: (0, b, 0, 0)),
                  kspec(K_AUG), vspec, kspec(HEAD_DIM), vspec,
                  pl.BlockSpec((GATE_ROWS, tq), lambda b, i: (0, b * nq + i)),
                  qspec,
                  pl.BlockSpec(mt.shape, lambda b, i: (0, 0))],
        out_specs=qspec,
        out_shape=jax.ShapeDtypeStruct((NSA_WIDTH, n), BF16),
        compiler_params=pltpu.CompilerParams(
            dimension_semantics=("arbitrary", "arbitrary"),
            vmem_limit_bytes=VMEM_LIMIT),
        name="nsa",
    )(qt, qrt, kcmp, vcmpt, ksh, vst, kwh, vwt, gates, gnt, mt)


def _rwkv_kernel(p_ref, gr_ref, mu_ref, w0_ref, a0_ref, wab_ref, kk_ref, ka_ref, rk_ref,
                 gw_ref, gb_ref, ones_ref, tril_ref, o_ref, st_ref, carry_ref, *, tt):
    c = RWKV_CHUNK
    hd = HEAD_DIM
    nc = tt // c
    step = pl.program_id(1)

    @pl.when(step == 0)
    def _():
        st_ref[...] = jnp.zeros_like(st_ref)
        carry_ref[...] = jnp.zeros_like(carry_ref)

    p = p_ref[...]
    row = lax.broadcasted_iota(jnp.int32, p.shape, 0)
    prev = jnp.where(row == 0, carry_ref[...], pltpu.roll(p, 1, 0))
    carry_ref[...] = p[tt - 1:tt, :]
    ps = p + mu_ref[...] * (prev - p)
    r = ps[:, 0:RWKV_WIDTH]
    k = ps[:, RWKV_WIDTH:2 * RWKV_WIDTH]
    v = ps[:, 2 * RWKV_WIDTH:3 * RWKV_WIDTH]
    lora = ps[:, 3 * RWKV_WIDTH:]
    lane = lax.broadcasted_iota(jnp.int32, lora.shape, 1)
    feat = jnp.where(lane < DECAY_RANK, jnp.tanh(lora), lora).astype(BF16)
    up = _dot(feat, wab_ref[...])
    w = w0_ref[...] + up[:, 0:RWKV_WIDTH]
    lw = _sigmoid(w) * (-float(np.exp(-0.5)))
    a = _sigmoid(a0_ref[...] + up[:, RWKV_WIDTH:])
    ones_bd = ones_ref[...]
    kk = k * kk_ref[...]
    kkn = kk * lax.rsqrt(jnp.maximum(_head_sums(kk * kk, ones_bd), 1e-24))
    k2 = k * (1.0 + (a - 1.0) * ka_ref[...])
    alpha = -kkn
    beta = kkn * a
    bonus = _head_sums(r * k2 * rk_ref[...], ones_bd) * v

    cum = _x2_dot(tril_ref[...], lw)
    cend = jnp.concatenate(
        [jnp.broadcast_to(cum[(ch + 1) * c - 1:(ch + 1) * c, :], (c, RWKV_WIDTH)) for ch in range(nc)], axis=0)
    e_neg = jnp.exp(-cum)
    pc = jnp.exp(cend)
    at = alpha * jnp.exp(cum - lw)
    bt = beta * e_neg
    kt = k2 * e_neg
    rt = r * jnp.exp(cum)
    bh = bt * pc
    kh = kt * pc

    npair = RWKV_WIDTH // LANE

    def pairs(x):
        return jnp.stack([x[ch * c:(ch + 1) * c, j * LANE:(j + 1) * LANE]
                          for ch in range(nc) for j in range(npair)], axis=0)

    at_p, rt_p, bt_p, kt_p, v_p, bh_p, kh_p = (pairs(t) for t in (at, rt, bt, kt, v, bh, kh))
    pc_p = jnp.stack([pc[ch * c:ch * c + 1, j * LANE:(j + 1) * LANE]
                      for ch in range(nc) for j in range(npair)], axis=0)
    lane_c = lax.broadcasted_iota(jnp.int32, (1, c, LANE), 2)
    row_c = lax.broadcasted_iota(jnp.int32, (1, c, LANE), 1)
    even_c = lane_c < hd
    col_c = jnp.where(even_c, lane_c, lane_c - hd)
    low_s = row_c > col_c
    low_i = row_c >= col_c
    lane_2c = lax.broadcasted_iota(jnp.int32, (1, 2 * c, LANE), 2)
    row_2c = lax.broadcasted_iota(jnp.int32, (1, 2 * c, LANE), 1)
    even_2c = lane_2c < hd
    on_bd = (row_2c < hd) == even_2c
    zero_c = jnp.zeros((1, c, LANE), BF16)

    def bd(x):
        xb = x.astype(BF16)
        return jnp.concatenate([jnp.where(even_c, xb, zero_c), jnp.where(even_c, zero_c, xb)], axis=1)

    def abd(x):
        xb = x.astype(BF16)
        return jnp.concatenate([jnp.where(even_c, zero_c, xb), jnp.where(even_c, xb, zero_c)], axis=1)

    la = jnp.concatenate([at_p, rt_p], axis=1).astype(BF16)
    zero_2c = jnp.zeros((1, 2 * c, LANE), BF16)
    r_e = _bmm_nt(jnp.where(even_2c, la, zero_2c), jnp.concatenate([bt_p, kt_p], axis=1).astype(BF16))
    r_o = _bmm_nt(jnp.where(even_2c, zero_2c, la), jnp.concatenate([kt_p, bt_p], axis=1).astype(BF16))
    nab = jnp.where(low_s, jnp.where(even_c, r_e[:, 0:c], r_o[:, 0:c]), 0.0)
    aak_sw = jnp.where(low_s, jnp.where(even_c, r_o[:, 0:c], r_e[:, 0:c]), 0.0).astype(BF16)
    arb = jnp.where(low_i, jnp.where(even_c, r_e[:, c:], r_o[:, c:]), 0.0).astype(BF16)
    ark_sw = jnp.where(low_i, jnp.where(even_c, r_o[:, c:], r_e[:, c:]), 0.0).astype(BF16)
    tinv = jnp.where(row_c == col_c, 1.0, 0.0) + nab
    npow = _bmm(nab.astype(BF16), bd(nab))
    n_dbl = 5
    for it in range(n_dbl):
        nbd = bd(npow)
        if it + 1 < n_dbl:
            res = _bmm(jnp.concatenate([tinv, npow], axis=1).astype(BF16), nbd)
            tinv = tinv + res[:, 0:c]
            npow = res[:, c:]
        else:
            tinv = tinv + _bmm(tinv.astype(BF16), nbd)
    av = _bmm(jnp.concatenate([aak_sw, ark_sw], axis=1), abd(v_p))
    tx = _bmm(tinv.astype(BF16), jnp.concatenate([bd(av[:, 0:c]), bd(at_p)], axis=2))
    u0, ta = tx[:, :, 0:LANE], tx[:, :, LANE:]
    ax = _bmm(arb, jnp.concatenate([bd(ta), bd(u0)], axis=2))
    rq = rt_p + ax[:, :, 0:LANE]
    y0 = ax[:, :, LANE:] + av[:, c:]
    v_b = v_p.astype(BF16)
    w_f = jnp.concatenate([jnp.concatenate([ta, u0], axis=2).astype(BF16),
                           jnp.concatenate([jnp.zeros_like(v_b), v_b], axis=2)], axis=1)
    gh = _bmm(jnp.concatenate([jnp.swapaxes(bh_p, 1, 2), jnp.swapaxes(kh_p, 1, 2)], axis=2).astype(BF16), w_f)
    g_bd = jnp.where(on_bd, gh[:, :, 0:LANE], 0.0) + jnp.where(row_2c == lane_2c, pc_p, 0.0)
    h_bd = jnp.where(on_bd, gh[:, :, LANE:], 0.0)
    lhs = jnp.concatenate([rq, g_bd], axis=1).astype(BF16)

    st = st_ref[...]
    ys = []
    for ch in range(nc):
        sl = slice(ch * npair, (ch + 1) * npair)
        res = _bmm(lhs[sl], st.astype(BF16))
        yc = res[:, 0:c, :] + y0[sl]
        st = res[:, c:, :] + h_bd[sl]
        ys.append(jnp.concatenate([yc[j] for j in range(npair)], axis=1))
    st_ref[...] = st
    y = jnp.concatenate(ys, axis=0) if nc > 1 else ys[0]

    inv_hd = 1.0 / hd
    mean = _head_sums(y, ones_bd) * inv_hd
    ycen = y - mean
    var = _head_sums(ycen * ycen, ones_bd) * inv_hd
    yn = ycen * lax.rsqrt(var + GN_EPS) * gw_ref[...] + gb_ref[...]
    o_ref[...] = ((yn + bonus) * gr_ref[...]).astype(BF16)


def _rwkv(rw, gr, mu, w0, a0, wab, kk, ka, rk, gw, gb, ones_bd, tril, *, batch, seq, tt):
    n = batch * seq
    ns = seq // tt
    row = lambda w: pl.BlockSpec((tt, w), lambda b, i: (b * ns + i, 0))
    full = lambda a: pl.BlockSpec(a.shape, lambda b, i: (0,) * a.ndim)
    kern = functools.partial(_rwkv_kernel, tt=tt)
    consts = (mu, w0, a0, wab, kk, ka, rk, gw, gb, ones_bd, tril)
    return pl.pallas_call(
        kern,
        grid=(batch, ns),
        in_specs=[row(RWKV_SHIFT_WIDTH), row(RWKV_WIDTH)] + [full(a) for a in consts],
        out_specs=row(RWKV_WIDTH),
        out_shape=jax.ShapeDtypeStruct((n, RWKV_WIDTH), BF16),
        scratch_shapes=[pltpu.VMEM((RWKV_WIDTH // LANE, LANE, LANE), F32),
                        pltpu.VMEM((1, RWKV_SHIFT_WIDTH), F32)],
        compiler_params=pltpu.CompilerParams(dimension_semantics=("arbitrary", "arbitrary"),
                                             vmem_limit_bytes=VMEM_LIMIT),
        name="rwkv",
    )(rw, gr, *consts)


def _out_kernel(x_ref, ont_ref, or_ref, wn_ref, wr_ref, g_ref, o_ref, *, n_sub):
    ts = x_ref.shape[0] // n_sub

    def project(j):
        rows = slice(j * ts, (j + 1) * ts)
        return (x_ref[rows, :] + _dot_tn(ont_ref[:, rows], wn_ref[...])
                + _dot(or_ref[rows, :], wr_ref[...]))

    def finish(j, h):
        ms = jnp.mean(h * h, axis=-1, keepdims=True)
        o_ref[j * ts:(j + 1) * ts, :] = h * lax.rsqrt(ms + RMS_EPS) * g_ref[...]

    h = project(0)
    for j in range(1, n_sub):
        h_next = project(j)
        finish(j - 1, h)
        h = h_next
    finish(n_sub - 1, h)


def _out_proj(x2, o_nsa_t, o_rwkv, wn, wr, final_g, *, tm, n_sub):
    n = x2.shape[0]
    row = lambda w: pl.BlockSpec((tm, w), lambda i: (i, 0))
    full = lambda a: pl.BlockSpec(a.shape, lambda i: (0,) * a.ndim)
    return pl.pallas_call(
        functools.partial(_out_kernel, n_sub=n_sub),
        grid=(n // tm,),
        in_specs=[row(D_MODEL), pl.BlockSpec((NSA_WIDTH, tm), lambda i: (0, i)), row(RWKV_WIDTH),
                  full(wn), full(wr), full(final_g)],
        out_specs=row(D_MODEL),
        out_shape=jax.ShapeDtypeStruct((n, D_MODEL), F32),
        compiler_params=pltpu.CompilerParams(dimension_semantics=("arbitrary",),
                                             vmem_limit_bytes=VMEM_LIMIT),
        name="out_proj",
    )(x2, o_nsa_t, o_rwkv, wn, wr, final_g)


def _rope_tables(seq):
    inv = ROPE_THETA ** (-np.arange(ROPE_HALF, dtype=np.float64) / ROPE_HALF)
    ang = np.arange(seq, dtype=np.float64)[:, None] * inv[None, :]
    cos, sin = np.cos(ang), np.sin(ang)
    ra = np.ones((seq, HEAD_DIM)); rm = np.zeros((seq, HEAD_DIM)); rp = np.zeros((seq, HEAD_DIM))
    ra[:, :ROPE_HALF] = cos; ra[:, ROPE_HALF:ROPE_DIM] = cos
    rm[:, :ROPE_HALF] = -sin
    rp[:, ROPE_HALF:ROPE_DIM] = sin
    rep = lambda t: jnp.asarray(np.tile(t, (1, LANE // HEAD_DIM)), F32)
    assert seq // SEL_BLOCK <= K_AUG - HEAD_DIM
    onehot = np.zeros((seq, K_AUG - HEAD_DIM))
    onehot[np.arange(seq), np.arange(seq) // SEL_BLOCK] = 1.0
    return (rep(ra), rep(rm), rep(rp), jnp.asarray(cos.T, F32), jnp.asarray(sin.T, F32),
            jnp.asarray(onehot, BF16))


def _cmp_to_sel_t(n_cmp_pad, n_sel):
    n_cmp = n_cmp_pad - 1
    c0 = np.arange(n_cmp)[:, None] * CMP_STRIDE
    s0 = np.arange(n_sel)[None, :] * SEL_BLOCK
    ov = np.clip(np.minimum(c0 + CMP_BLOCK, s0 + SEL_BLOCK) - np.maximum(c0, s0), 0, None) / CMP_BLOCK
    mt = np.zeros((n_sel, n_cmp_pad))
    mt[:, :n_cmp] = ov.T
    return jnp.asarray(mt, BF16)


def _prep_w_in(w_in):
    idx = np.cumsum(IN_SIZES)[:-1].tolist()
    q, kc, vc, ks, vs, kw, vw, gl, gn, rw, gr = jnp.split(w_in, idx, axis=1)
    pad = jnp.zeros((D_MODEL, GATE_ROWS - gl.shape[1]), w_in.dtype)
    return jnp.concatenate([kc, vc, ks, kw, rw, gr, q, vs, vw, gl, pad, gn], axis=1).T.astype(BF16)


def _layer(x2, norm_g, w_in, cmp_pos_k, cmp_w1_k, cmp_w2_k, cmp_pos_v, cmp_w1_v, cmp_w2_v,
           shift_mu, decay_w0, decay_up, iclr_a0, iclr_up, k_k, k_a, r_k, gn_w, gn_b, w_out,
           final_g, *, batch, seq):
    tm = 256
    tq = 256
    tt = 256
    assert WINDOW % KEY_TILE == 0 and seq % KEY_TILE == 0 and KEY_TILE % tq == 0
    nch = seq // CMP_STRIDE
    n_sel = seq // SEL_BLOCK
    rowv = lambda t: t.reshape(1, -1).astype(F32)

    (kch, vch, ksh, kwh, rw, gr, qt, qrt, vst, vwt, gates, gnt) = _in_proj(
        x2, rowv(norm_g), _prep_w_in(w_in), _rope_tables(seq), seq=seq, tm=2 * tm)

    chunks = lambda t: t.reshape(NSA_KV, batch, nch, CMP_STRIDE * HEAD_DIM)
    kcmp, vcmpt = _compress(chunks(kch), chunks(vch),
                            rowv(cmp_pos_k), cmp_w1_k.astype(BF16), cmp_w2_k.astype(BF16),
                            rowv(cmp_pos_v), cmp_w1_v.astype(BF16), cmp_w2_v.T.astype(BF16))

    o_nsa_t = _nsa(qt, qrt, kcmp, vcmpt, ksh, vst, kwh, vwt, gates, gnt,
                   _cmp_to_sel_t(nch, n_sel), batch=batch, seq=seq, tq=tq)

    z = jnp.zeros((DECAY_RANK, RWKV_WIDTH), F32)
    wab = jnp.concatenate([jnp.concatenate([decay_up, z], axis=1),
                           jnp.concatenate([z, iclr_up], axis=1)], axis=0).astype(BF16)
    hid = np.arange(2 * LANE) // HEAD_DIM
    ones_bd = jnp.asarray(hid[:, None] == hid[None, :], BF16)
    ti = np.arange(tt)
    tril = (ti[:, None] >= ti[None, :]) & (ti[:, None] // RWKV_CHUNK == ti[None, :] // RWKV_CHUNK)
    tril = jnp.asarray(np.concatenate([tril, tril], axis=1), BF16)
    o_rwkv = _rwkv(rw, gr, rowv(shift_mu), rowv(decay_w0), rowv(iclr_a0), wab, rowv(k_k), rowv(k_a),
                   rowv(r_k), rowv(gn_w), rowv(gn_b), ones_bd, tril, batch=batch, seq=seq, tt=tt)

    w_o = w_out.astype(BF16)
    return _out_proj(x2, o_nsa_t, o_rwkv, w_o[:NSA_WIDTH], w_o[NSA_WIDTH:], rowv(final_g), tm=4 * tm, n_sub=4)


def kernel(x, norm_g, w_in, cmp_pos_k, cmp_w1_k, cmp_w2_k, cmp_pos_v, cmp_w1_v, cmp_w2_v, shift_mu, decay_w0, decay_up, iclr_a0, iclr_up, k_k, k_a, r_k, gn_w, gn_b, w_out, final_g):
    batch, seq, d = x.shape
    assert d == D_MODEL and norm_g.shape[0] == 1, "single-layer trunk"
    out = _layer(x.reshape(batch * seq, d), norm_g[0], w_in[0], cmp_pos_k[0], cmp_w1_k[0], cmp_w2_k[0],
                 cmp_pos_v[0], cmp_w1_v[0], cmp_w2_v[0], shift_mu[0], decay_w0[0], decay_up[0],
                 iclr_a0[0], iclr_up[0], k_k[0], k_a[0], r_k[0], gn_w[0], gn_b[0], w_out[0],
                 final_g, batch=batch, seq=seq)
    return out.reshape(batch, seq, d)
```

```python
import functools

import numpy as np
import jax
import jax.numpy as jnp
from jax import lax
from jax.experimental import pallas as pl
from jax.experimental.pallas import tpu as pltpu

F32 = jnp.float32
BF16 = jnp.bfloat16

D_MODEL = 1024
HEAD_DIM = 64
NSA_HEADS = 8
NSA_KV = 2
NSA_REP = NSA_HEADS // NSA_KV
RWKV_HEADS = 8
NSA_WIDTH = NSA_HEADS * HEAD_DIM
RWKV_WIDTH = RWKV_HEADS * HEAD_DIM
KV_WIDTH = NSA_KV * HEAD_DIM
ROPE_DIM = HEAD_DIM // 4
ROPE_HALF = ROPE_DIM // 2
ROPE_THETA = 500000.0
CMP_BLOCK = 32
CMP_STRIDE = 16
CMP_HIDDEN = 256
SEL_BLOCK = 64
SEL_TOPK = 8
WINDOW = 512
DECAY_RANK = 64
ICLR_RANK = 64
RWKV_SHIFT_WIDTH = 3 * RWKV_WIDTH + DECAY_RANK + ICLR_RANK
IN_SIZES = (NSA_WIDTH, KV_WIDTH, KV_WIDTH, KV_WIDTH, KV_WIDTH, KV_WIDTH, KV_WIDTH,
            3 * NSA_HEADS, NSA_WIDTH, RWKV_SHIFT_WIDTH, RWKV_WIDTH)
SCALE = HEAD_DIM ** -0.5
RMS_EPS = 1e-6
GN_EPS = 64e-5
NEG_INF = -1e30
FORCE_BONUS = 1e3

LANE = 128
SUBLANE = 8
BF16_SUBLANE = 2 * SUBLANE
GATE_ROWS = 2 * BF16_SUBLANE

T_KV = 0
T_RW = T_KV + 4 * KV_WIDTH
T_GR = T_RW + RWKV_SHIFT_WIDTH
T_END = T_GR + RWKV_WIDTH
R_Q = T_END
R_VS = R_Q + NSA_WIDTH
R_VW = R_VS + KV_WIDTH
R_GL = R_VW + KV_WIDTH
R_GN = R_GL + GATE_ROWS
R_END = R_GN + NSA_WIDTH

LOG2E = float(np.log2(np.e))
K_AUG = LANE
V_ROWS = HEAD_DIM + BF16_SUBLANE
KEY_TILE = 256
RWKV_CHUNK = 64
VMEM_LIMIT = 48 * 1024 * 1024


def _dot(a, b):
    return jnp.dot(a, b, preferred_element_type=F32)


def _dot_nt(a, b):
    return lax.dot_general(a, b, (((1,), (1,)), ((), ())), preferred_element_type=F32)


def _dot_tn(a, b):
    return lax.dot_general(a, b, (((0,), (0,)), ((), ())), preferred_element_type=F32)


def _bmm(a, b):
    return lax.dot_general(a, b, (((2,), (1,)), ((0,), (0,))), preferred_element_type=F32)


def _bmm_nt(a, b):
    return lax.dot_general(a, b, (((2,), (2,)), ((0,), (0,))), preferred_element_type=F32)


def _split3(x):
    hi = x.astype(BF16)
    r1 = x - hi.astype(F32)
    mid = r1.astype(BF16)
    lo = (r1 - mid.astype(F32)).astype(BF16)
    return hi, mid, lo


def _head_sums(x, ones_blk):
    w = ones_blk.shape[0]
    xb = x.astype(BF16)
    return jnp.concatenate([_dot(xb[:, i:i + w], ones_blk) for i in range(0, x.shape[1], w)], axis=1)


def _x2_dot(ww_bf16, x):
    hi = x.astype(BF16)
    lo = (x - hi.astype(F32)).astype(BF16)
    return _dot(ww_bf16, jnp.concatenate([hi, lo], axis=0))


def _x3_dot(w_bf16, x):
    hi, mid, lo = _split3(x)
    return _dot(w_bf16, hi) + _dot(w_bf16, mid) + _dot(w_bf16, lo)


def _sigmoid(x):
    return 0.5 * jnp.tanh(0.5 * x) + 0.5


def _interleave(*gens):
    live = list(gens)
    while live:
        for g in list(live):
            try:
                next(g)
            except StopIteration:
                live.remove(g)


def _rope128(t, ra, rm, rp):
    return t * ra + pltpu.roll(t, LANE - ROPE_HALF, 1) * rm + pltpu.roll(t, ROPE_HALF, 1) * rp


def _in_proj_kernel(x_ref, g_ref, w_ref, ra_ref, rm_ref, rp_ref, cos_ref, sin_ref, oh_ref,
                    kc_ref, vc_ref, ks_ref, kw_ref, rw_ref, gr_ref,
                    qt_ref, qrt_ref, vst_ref, vwt_ref, gate_ref, gn_ref, cmp_scr):
    x = x_ref[...]
    ms = jnp.mean(x * x, axis=-1, keepdims=True)
    y = (x * lax.rsqrt(ms + RMS_EPS) * g_ref[...]).astype(BF16)
    tm = x.shape[0]

    kv = _dot_nt(y, w_ref[T_KV:T_RW, :])
    for i, ref in enumerate((kc_ref, vc_ref)):
        cmp_scr[i] = kv[:, i * KV_WIDTH:(i + 1) * KV_WIDTH]
        for tau in range(CMP_STRIDE):
            piece = cmp_scr[i, pl.ds(tau, tm // CMP_STRIDE, stride=CMP_STRIDE), :].astype(BF16)
            for g in range(NSA_KV):
                ref[g, :, tau * HEAD_DIM:(tau + 1) * HEAD_DIM] = piece[:, g * HEAD_DIM:(g + 1) * HEAD_DIM]
    ra, rm, rp = ra_ref[...], rm_ref[...], rp_ref[...]
    for i, ref in ((2, ks_ref), (3, kw_ref)):
        t = _rope128(kv[:, i * LANE:(i + 1) * LANE], ra, rm, rp)
        for g in range(NSA_KV):
            tg = t[:, g * HEAD_DIM:(g + 1) * HEAD_DIM].astype(BF16)
            if ref is ks_ref:
                ref[g] = jnp.concatenate([tg, oh_ref[...]], axis=1)
            else:
                ref[g] = tg
    rw_ref[...] = _dot_nt(y, w_ref[T_RW:T_GR, :])
    gr = _dot_nt(y, w_ref[T_GR:T_END, :])
    gr_ref[...] = gr * _sigmoid(gr)

    def proj_t(r0, r1):
        return _dot_nt(w_ref[r0:r1, :], y)

    ft = proj_t(R_Q, R_END)
    qt = ft[0:R_VS - R_Q] * (SCALE * LOG2E)
    cos, sin = cos_ref[...], sin_ref[...]
    qt_ref[...] = qt.astype(BF16)
    for h in range(NSA_HEADS):
        r0 = h * HEAD_DIM
        t1 = qt[r0:r0 + ROPE_HALF]
        t2 = qt[r0 + ROPE_HALF:r0 + ROPE_DIM]
        qrt_ref[r0:r0 + ROPE_DIM, :] = jnp.concatenate(
            [t1 * cos - t2 * sin, t2 * cos + t1 * sin], axis=0).astype(BF16)
        qrt_ref[r0 + ROPE_DIM:r0 + HEAD_DIM, :] = qt[r0 + ROPE_DIM:r0 + HEAD_DIM].astype(BF16)
    vg = ft[R_VS - R_Q:R_GN - R_Q]
    vt = vg[0:R_GL - R_VS].astype(BF16)
    ones = jnp.ones((V_ROWS - HEAD_DIM, LANE), BF16)
    for j in range(vt.shape[1] // LANE):
        for i, ref in enumerate((vst_ref, vwt_ref)):
            for g in range(NSA_KV):
                r0 = i * KV_WIDTH + g * HEAD_DIM
                ref[j, g * V_ROWS:(g + 1) * V_ROWS, :] = jnp.concatenate(
                    [vt[r0:r0 + HEAD_DIM, j * LANE:(j + 1) * LANE], ones], axis=0)
    gate_ref[...] = _sigmoid(vg[R_GL - R_VS:])
    gn = ft[R_GN - R_Q:]
    gn_ref[...] = gn * _sigmoid(gn)


def _in_proj(x2, norm_g, w_all, tabs, *, seq, tm):
    n = x2.shape[0]
    spt = seq // tm
    ra, rm, rp, cos, sin, onehot = tabs
    hm = lambda w: jax.ShapeDtypeStruct((NSA_KV, n, w), BF16)
    hspec = lambda w: pl.BlockSpec((NSA_KV, tm, w), lambda i: (0, i, 0))
    row = lambda w: pl.BlockSpec((tm, w), lambda i: (i, 0))
    col = lambda r: pl.BlockSpec((r, tm), lambda i: (0, i))
    full = lambda a: pl.BlockSpec(a.shape, lambda i: (0,) * a.ndim)
    tab = lambda w: pl.BlockSpec((tm, w), lambda i: (i % spt, 0))
    tabt = pl.BlockSpec((ROPE_HALF, tm), lambda i: (0, i % spt))
    vtile = pl.BlockSpec((tm // LANE, NSA_KV * V_ROWS, LANE), lambda i: (i, 0, 0))
    vsd = jax.ShapeDtypeStruct((n // LANE, NSA_KV * V_ROWS, LANE), BF16)
    cw = CMP_STRIDE * HEAD_DIM
    cspec = pl.BlockSpec((NSA_KV, tm // CMP_STRIDE, cw), lambda i: (0, i, 0))
    csd = jax.ShapeDtypeStruct((NSA_KV, n // CMP_STRIDE, cw), BF16)
    return pl.pallas_call(
        _in_proj_kernel,
        grid=(n // tm,),
        in_specs=[row(D_MODEL), full(norm_g), full(w_all), tab(LANE), tab(LANE), tab(LANE),
                  tabt, tabt, tab(K_AUG - HEAD_DIM)],
        out_specs=[cspec, cspec, hspec(K_AUG), hspec(HEAD_DIM), row(RWKV_SHIFT_WIDTH), row(RWKV_WIDTH),
                   col(NSA_WIDTH), col(NSA_WIDTH), vtile, vtile,
                   col(GATE_ROWS), col(NSA_WIDTH)],
        out_shape=[csd, csd, hm(K_AUG), hm(HEAD_DIM),
                   jax.ShapeDtypeStruct((n, RWKV_SHIFT_WIDTH), F32),
                   jax.ShapeDtypeStruct((n, RWKV_WIDTH), F32),
                   jax.ShapeDtypeStruct((NSA_WIDTH, n), BF16),
                   jax.ShapeDtypeStruct((NSA_WIDTH, n), BF16),
                   vsd, vsd,
                   jax.ShapeDtypeStruct((GATE_ROWS, n), F32),
                   jax.ShapeDtypeStruct((NSA_WIDTH, n), F32)],
        scratch_shapes=[pltpu.VMEM((2, tm, KV_WIDTH), F32)],
        compiler_params=pltpu.CompilerParams(dimension_semantics=("arbitrary",),
                                             vmem_limit_bytes=VMEM_LIMIT),
        name="in_proj",
    )(x2, norm_g, w_all, ra, rm, rp, cos, sin, onehot)


def _compress_kernel(kc_ref, vc_ref, pk_ref, w1k_ref, w2k_ref, pv_ref, w1v_ref, w2vt_ref,
                     ko_ref, vo_ref):
    half = CMP_STRIDE * HEAD_DIM
    ng, nb, nch, _ = kc_ref.shape
    nseg = ng * nb

    def hidden(c_ref, pos_ref, w1_ref):
        c = c_ref[...].reshape(nseg * nch, half)
        pos = jnp.broadcast_to(pos_ref[...], (BF16_SUBLANE, 2 * half)).astype(BF16)
        za = _dot(jnp.concatenate([c, pos[:, 0:half]], axis=0), w1_ref[0:half, :])
        zb = _dot(jnp.concatenate([c, pos[:, half:]], axis=0), w1_ref[half:2 * half, :])
        pv = za[nseg * nch:nseg * nch + 1] + zb[nseg * nch:nseg * nch + 1]
        hid = jnp.concatenate(
            [za[s * nch:(s + 1) * nch] + pltpu.roll(zb[s * nch:(s + 1) * nch], nch - 1, 0) for s in range(nseg)],
            axis=0) + pv
        return (hid * _sigmoid(hid)).astype(BF16)

    ko = _dot(hidden(kc_ref, pk_ref, w1k_ref), w2k_ref[...]).astype(BF16)
    hv = hidden(vc_ref, pv_ref, w1v_ref)
    for s in range(nseg):
        ko_ref[s // nb, s % nb] = ko[s * nch:(s + 1) * nch]
        vo_ref[s // nb, s % nb] = _dot_nt(w2vt_ref[...], hv[s * nch:(s + 1) * nch]).astype(BF16)


def _compress(kc_r, vc_r, pk, w1k, w2k, pv, w1v, w2vt, *, nb):
    g, b, nch, width = kc_r.shape
    blk = pl.BlockSpec((g, nb, nch, width), lambda j: (0, j, 0, 0))
    full = lambda a: pl.BlockSpec(a.shape, lambda j: (0,) * a.ndim)
    return pl.pallas_call(
        _compress_kernel,
        grid=(b // nb,),
        in_specs=[blk, blk, full(pk), full(w1k), full(w2k), full(pv), full(w1v), full(w2vt)],
        out_specs=[pl.BlockSpec((g, nb, nch, HEAD_DIM), lambda j: (0, j, 0, 0)),
                   pl.BlockSpec((g, nb, HEAD_DIM, nch), lambda j: (0, j, 0, 0))],
        out_shape=[jax.ShapeDtypeStruct((g, b, nch, HEAD_DIM), BF16),
                   jax.ShapeDtypeStruct((g, b, HEAD_DIM, nch), BF16)],
        compiler_params=pltpu.CompilerParams(dimension_semantics=("arbitrary",),
                                             vmem_limit_bytes=VMEM_LIMIT),
        name="compress",
    )(kc_r, vc_r, pk, w1k, w2k, pv, w1v, w2vt)


def _nsa_kernel(qt_ref, qrt_ref, kc_ref, vct_ref, ks_ref, vst_ref, kw_ref, vwt_ref,
                gate_ref, gn_ref, mt_ref, o_ref, *, tq, seq):
    tk = KEY_TILE
    n_win = WINDOW // tk
    qi = pl.program_id(1)
    q0 = qi * tq
    nl = NSA_REP * tq
    n_sel = seq // SEL_BLOCK
    ncp = kc_ref.shape[2]
    groups = range(NSA_KV)

    def heads_on_lanes(ref, g):
        rows = [(g * NSA_REP + r) * HEAD_DIM for r in range(NSA_REP)]
        return jnp.concatenate([ref[r0:r0 + HEAD_DIM, :] for r0 in rows], axis=1)

    def tile4(a):
        return jnp.concatenate([a] * NSA_REP, axis=1)

    k_s = lax.broadcasted_iota(jnp.int32, (tk, tq), 0)
    t_l = q0 + lax.broadcasted_iota(jnp.int32, (tk, tq), 1)

    def update_steps(box, k_ref, vt_ref, g, q_op, kts, keeps):
        m_i, acc = box[0]

        def scores(i):
            sc = _dot(k_ref[g, pl.ds(pl.multiple_of(kts[i] * tk, tk), tk), :], q_op)
            if keeps[i] is None:
                return sc
            return jnp.concatenate([jnp.where(keeps[i], sc[:, r * tq:(r + 1) * tq], NEG_INF)
                                    for r in range(NSA_REP)], axis=1)

        sc_next = scores(0)
        yield
        for i, kt in enumerate(kts):
            sc = sc_next
            if i + 1 < len(kts):
                sc_next = scores(i + 1)
            m_n = jnp.maximum(m_i, jnp.max(sc, axis=0, keepdims=True))
            pe = jnp.exp2(sc - m_n).astype(BF16)
            yield
            vtb = jnp.concatenate([vt_ref[kt * (tk // LANE) + jj, g * V_ROWS:(g + 1) * V_ROWS, :]
                                   for jj in range(tk // LANE)], axis=1)
            acc = jnp.exp2(m_i - m_n) * acc + _dot(vtb, pe)
            m_i = m_n
            yield
        box[0] = (m_i, acc)

    def last_tiles(box, k_ref, vt_ref, g, q_op, n_back, low_keep):
        kts, keeps = [], []
        for back in range(n_back + 1):
            d = t_l - ((a - back) * tk + k_s)
            kts.append(a - back)
            if back == 0:
                keeps.append(d >= 0)
            elif back == n_win and low_keep is not None:
                keeps.append(low_keep(d))
            else:
                keeps.append(None)
        return update_steps(box, k_ref, vt_ref, g, q_op, kts, keeps)

    def select_steps(g, qr, sel_out, all_selected):
        s = _dot(kc_ref[g, 0], heads_on_lanes(qt_ref, g))
        t_c = q0 + lax.broadcasted_iota(jnp.int32, (ncp, tq), 1)
        c_c = lax.broadcasted_iota(jnp.int32, (ncp, tq), 0)
        cmask = tile4((c_c * CMP_STRIDE + (CMP_BLOCK - 1)) <= t_c)
        yield
        s = jnp.where(cmask, s, NEG_INF)
        m = jnp.maximum(jnp.max(s, axis=0, keepdims=True), 0.5 * NEG_INF)
        e = jnp.exp2(s - m)
        den = jnp.sum(e, axis=0, keepdims=True)
        p = e * (1.0 / jnp.where(den > 0.0, den, 1.0))
        sel_out["o_cmp"] = _dot(vct_ref[g, 0], p.astype(BF16))
        yield
        if all_selected:
            sel_out["qr_sel"] = jnp.concatenate([qr, jnp.zeros((K_AUG - HEAD_DIM, nl), BF16)], axis=0)
            return
        psum = p[:, 0:tq]
        for r in range(1, NSA_REP):
            psum = psum + p[:, r * tq:(r + 1) * tq]
        imp = _x3_dot(mt_ref[...], psum)
        j = lax.broadcasted_iota(jnp.int32, (n_sel, tq), 0)
        t = q0 + lax.broadcasted_iota(jnp.int32, (n_sel, tq), 1)
        tb = t // SEL_BLOCK
        forced = (j == 0) | (j == tb) | (j == tb - 1)
        val = jnp.where(j <= tb, imp + jnp.where(forced, FORCE_BONUS, 0.0), -1.0)
        yield
        vals = [val[g0:g0 + SUBLANE] for g0 in range(0, n_sel, SUBLANE)]
        cnts = [jnp.zeros((SUBLANE, tq), F32) for _ in vals]
        srow = lax.broadcasted_iota(jnp.int32, (SUBLANE, tq), 0)
        for i in range(n_sel):
            vi = jnp.broadcast_to(val[i:i + 1, :], (SUBLANE, tq))
            for g, vg in enumerate(vals):
                ge = lambda: jnp.where(vi >= vg, 1.0, 0.0)
                gt = lambda: jnp.where(vi > vg, 1.0, 0.0)
                if g * SUBLANE > i:
                    beat = ge()
                elif (g + 1) * SUBLANE <= i:
                    beat = gt()
                else:
                    beat = jnp.where(srow > i - g * SUBLANE, ge(), gt())
                cnts[g] = cnts[g] + beat
            if i % SUBLANE == SUBLANE - 1:
                yield
        cnt = jnp.concatenate(cnts, axis=0)
        sel_bias = jnp.where(cnt < float(SEL_TOPK), 0.0, NEG_INF).astype(BF16)
        sel_out["qr_sel"] = jnp.concatenate(
            [qr, tile4(sel_bias), jnp.zeros((K_AUG - HEAD_DIM - n_sel, nl), BF16)], axis=0)

    init = (jnp.full((1, nl), NEG_INF, F32), jnp.zeros((V_ROWS, nl), F32))
    a = (q0 + tq - 1) // tk

    def attend(n_back):
        win_boxes, sel_boxes, sel_outs = [[init] for _ in groups], [[init] for _ in groups], [{} for _ in groups]
        all_selected = n_back < n_win and (n_back + 1) * tk <= SEL_TOPK * SEL_BLOCK
        for g in groups:
            qr = heads_on_lanes(qrt_ref, g)
            _interleave(last_tiles(win_boxes[g], kw_ref, vwt_ref, g, qr, n_back, lambda d: d < WINDOW),
                        select_steps(g, qr, sel_outs[g], all_selected))
        _interleave(*[last_tiles(sel_boxes[g], ks_ref, vst_ref, g, sel_outs[g]["qr_sel"], n_back, None)
                      for g in groups])
        c_wins = [b[0] for b in win_boxes]
        c_sels = [b[0] for b in sel_boxes]
        o_cmps = [o["o_cmp"] for o in sel_outs]
        qr_sels = [o["qr_sel"] for o in sel_outs]

        def old_tiles(kts):
            def body(carry):
                boxes = [[c] for c in carry]
                _interleave(*[update_steps(boxes[g], ks_ref, vst_ref, g, qr_sels[g], kts, [None] * len(kts))
                              for g in groups])
                return tuple(b[0] for b in boxes)
            return body

        if n_back == n_win:
            n_old = a - n_win
            c_sels = lax.fori_loop(0, n_old % 2, lambda kt, c: old_tiles([kt])(c), tuple(c_sels))
            c_sels = lax.fori_loop(0, n_old // 2,
                                   lambda i, c: old_tiles([n_old % 2 + 2 * i, n_old % 2 + 2 * i + 1])(c), c_sels)

        for g in groups:
            o_sel = c_sels[g][1][0:HEAD_DIM] * (1.0 / c_sels[g][1][HEAD_DIM:HEAD_DIM + 1])
            o_win = c_wins[g][1][0:HEAD_DIM] * (1.0 / c_wins[g][1][HEAD_DIM:HEAD_DIM + 1])
            for r in range(NSA_REP):
                ls = slice(r * tq, (r + 1) * tq)
                h = g * NSA_REP + r
                gate = lambda c: gate_ref[3 * h + c:3 * h + c + 1, :]
                o = gate(0) * o_cmps[g][:, ls] + gate(1) * o_sel[:, ls] + gate(2) * o_win[:, ls]
                rs = slice(h * HEAD_DIM, (h + 1) * HEAD_DIM)
                o_ref[rs, :] = (o * gn_ref[rs, :]).astype(BF16)

    def dispatch(n_back):
        if n_back == 0:
            attend(0)
        else:
            lax.cond(a >= n_back, lambda: attend(n_back), lambda: dispatch(n_back - 1))

    dispatch(n_win)


def _nsa(qt, qrt, kcmp, vcmpt, ksh, vst, kwh, vwt, gates, gnt, mt, *, batch, seq, tq):
    n = batch * seq
    nq = seq // tq
    ncp = kcmp.shape[2]
    qspec = pl.BlockSpec((NSA_WIDTH, tq), lambda b, i: (0, b * nq + i))
    kspec = lambda w: pl.BlockSpec((NSA_KV, seq, w), lambda b, i: (0, b, 0))
    vspec = pl.BlockSpec((seq // LANE, NSA_KV * V_ROWS, LANE), lambda b, i: (b, 0, 0))
    kern = functools.partial(_nsa_kernel, tq=tq, seq=seq)
    return pl.pallas_call(
        kern,
        grid=(batch, nq),
        in_specs=[qspec, qspec,
                  pl.BlockSpec((NSA_KV, 1, ncp, HEAD_DIM), lambda b, i: (0, b, 0, 0)),
                  pl.BlockSpec((NSA_KV, 1, HEAD_DIM, ncp), lambda b, i: (0, b, 0, 0)),
                  kspec(K_AUG), vspec, kspec(HEAD_DIM), vspec,
                  pl.BlockSpec((GATE_ROWS, tq), lambda b, i: (0, b * nq + i)),
                  qspec,
                  pl.BlockSpec(mt.shape, lambda b, i: (0, 0))],
        out_specs=qspec,
        out_shape=jax.ShapeDtypeStruct((NSA_WIDTH, n), BF16),
        compiler_params=pltpu.CompilerParams(
            dimension_semantics=("arbitrary", "arbitrary"),
            vmem_limit_bytes=VMEM_LIMIT),
        name="nsa",
    )(qt, qrt, kcmp, vcmpt, ksh, vst, kwh, vwt, gates, gnt, mt)


def _rwkv_kernel(p_ref, gr_ref, mu_ref, w0_ref, a0_ref, wab_ref, kk_ref, ka_ref, rk_ref,
                 gw_ref, gb_ref, ones_ref, tril_ref, o_ref, st_ref, carry_ref, *, tt):
    c = RWKV_CHUNK
    hd = HEAD_DIM
    nc = tt // c
    step = pl.program_id(1)

    @pl.when(step == 0)
    def _():
        st_ref[...] = jnp.zeros_like(st_ref)
        carry_ref[...] = jnp.zeros_like(carry_ref)

    p = p_ref[...]
    row = lax.broadcasted_iota(jnp.int32, p.shape, 0)
    prev = jnp.where(row == 0, carry_ref[...], pltpu.roll(p, 1, 0))
    carry_ref[...] = p[tt - 1:tt, :]
    ps = p + mu_ref[...] * (prev - p)
    r = ps[:, 0:RWKV_WIDTH]
    k = ps[:, RWKV_WIDTH:2 * RWKV_WIDTH]
    v = ps[:, 2 * RWKV_WIDTH:3 * RWKV_WIDTH]
    lora = ps[:, 3 * RWKV_WIDTH:]
    lane = lax.broadcasted_iota(jnp.int32, lora.shape, 1)
    feat = jnp.where(lane < DECAY_RANK, jnp.tanh(lora), lora).astype(BF16)
    up = _dot(feat, wab_ref[...])
    w = w0_ref[...] + up[:, 0:RWKV_WIDTH]
    lw = _sigmoid(w) * (-float(np.exp(-0.5)))
    a = _sigmoid(a0_ref[...] + up[:, RWKV_WIDTH:])
    ones_bd = ones_ref[...]
    kk = k * kk_ref[...]
    kkn = kk * lax.rsqrt(jnp.maximum(_head_sums(kk * kk, ones_bd), 1e-24))
    k2 = k * (1.0 + (a - 1.0) * ka_ref[...])
    alpha = -kkn
    beta = kkn * a
    bonus = _head_sums(r * k2 * rk_ref[...], ones_bd) * v

    cum = _x2_dot(tril_ref[...], lw)
    cend = jnp.concatenate(
        [jnp.broadcast_to(cum[(ch + 1) * c - 1:(ch + 1) * c, :], (c, RWKV_WIDTH)) for ch in range(nc)], axis=0)
    e_neg = jnp.exp(-cum)
    pc = jnp.exp(cend)
    at = alpha * jnp.exp(cum - lw)
    bt = beta * e_neg
    kt = k2 * e_neg
    rt = r * jnp.exp(cum)
    bh = bt * pc
    kh = kt * pc

    npair = RWKV_WIDTH // LANE

    def pairs(x):
        return jnp.stack([x[ch * c:(ch + 1) * c, j * LANE:(j + 1) * LANE]
                          for ch in range(nc) for j in range(npair)], axis=0)

    at_p, rt_p, bt_p, kt_p, v_p, bh_p, kh_p = (pairs(t) for t in (at, rt, bt, kt, v, bh, kh))
    pc_p = jnp.stack([pc[ch * c:ch * c + 1, j * LANE:(j + 1) * LANE]
                      for ch in range(nc) for j in range(npair)], axis=0)
    lane_c = lax.broadcasted_iota(jnp.int32, (1, c, LANE), 2)
    row_c = lax.broadcasted_iota(jnp.int32, (1, c, LANE), 1)
    even_c = lane_c < hd
    col_c = jnp.where(even_c, lane_c, lane_c - hd)
    low_s = row_c > col_c
    low_i = row_c >= col_c
    lane_2c = lax.broadcasted_iota(jnp.int32, (1, 2 * c, LANE), 2)
    row_2c = lax.broadcasted_iota(jnp.int32, (1, 2 * c, LANE), 1)
    even_2c = lane_2c < hd
    on_bd = (row_2c < hd) == even_2c
    zero_c = jnp.zeros((1, c, LANE), BF16)

    def bd(x):
        xb = x.astype(BF16)
        return jnp.concatenate([jnp.where(even_c, xb, zero_c), jnp.where(even_c, zero_c, xb)], axis=1)

    def abd(x):
        xb = x.astype(BF16)
        return jnp.concatenate([jnp.where(even_c, zero_c, xb), jnp.where(even_c, xb, zero_c)], axis=1)

    la = jnp.concatenate([at_p, rt_p], axis=1).astype(BF16)
    zero_2c = jnp.zeros((1, 2 * c, LANE), BF16)
    r_e = _bmm_nt(jnp.where(even_2c, la, zero_2c), jnp.concatenate([bt_p, kt_p], axis=1).astype(BF16))
    r_o = _bmm_nt(jnp.where(even_2c, zero_2c, la), jnp.concatenate([kt_p, bt_p], axis=1).astype(BF16))
    nab = jnp.where(low_s, jnp.where(even_c, r_e[:, 0:c], r_o[:, 0:c]), 0.0)
    aak_sw = jnp.where(low_s, jnp.where(even_c, r_o[:, 0:c], r_e[:, 0:c]), 0.0).astype(BF16)
    arb = jnp.where(low_i, jnp.where(even_c, r_e[:, c:], r_o[:, c:]), 0.0).astype(BF16)
    ark_sw = jnp.where(low_i, jnp.where(even_c, r_o[:, c:], r_e[:, c:]), 0.0).astype(BF16)
    tinv = jnp.where(row_c == col_c, 1.0, 0.0) + nab
    npow = _bmm(nab.astype(BF16), bd(nab))
    n_dbl = 5
    for it in range(n_dbl):
        nbd = bd(npow)
        if it + 1 < n_dbl:
            res = _bmm(jnp.concatenate([tinv, npow], axis=1).astype(BF16), nbd)
            tinv = tinv + res[:, 0:c]
            npow = res[:, c:]
        else:
            tinv = tinv + _bmm(tinv.astype(BF16), nbd)
    av = _bmm(jnp.concatenate([aak_sw, ark_sw], axis=1), abd(v_p))
    tx = _bmm(tinv.astype(BF16), jnp.concatenate([bd(av[:, 0:c]), bd(at_p)], axis=2))
    u0, ta = tx[:, :, 0:LANE], tx[:, :, LANE:]
    ax = _bmm(arb, jnp.concatenate([bd(ta), bd(u0)], axis=2))
    rq = rt_p + ax[:, :, 0:LANE]
    y0 = ax[:, :, LANE:] + av[:, c:]
    v_b = v_p.astype(BF16)
    w_f = jnp.concatenate([jnp.concatenate([ta, u0], axis=2).astype(BF16),
                           jnp.concatenate([jnp.zeros_like(v_b), v_b], axis=2)], axis=1)
    gh = _bmm(jnp.concatenate([jnp.swapaxes(bh_p, 1, 2), jnp.swapaxes(kh_p, 1, 2)], axis=2).astype(BF16), w_f)
    g_bd = jnp.where(on_bd, gh[:, :, 0:LANE], 0.0) + jnp.where(row_2c == lane_2c, pc_p, 0.0)
    h_bd = jnp.where(on_bd, gh[:, :, LANE:], 0.0)
    lhs = jnp.concatenate([rq, g_bd], axis=1).astype(BF16)

    st = st_ref[...]
    ys = []
    for ch in range(nc):
        sl = slice(ch * npair, (ch + 1) * npair)
        res = _bmm(lhs[sl], st.astype(BF16))
        yc = res[:, 0:c, :] + y0[sl]
        st = res[:, c:, :] + h_bd[sl]
        ys.append(jnp.concatenate([yc[j] for j in range(npair)], axis=1))
    st_ref[...] = st
    y = jnp.concatenate(ys, axis=0) if nc > 1 else ys[0]

    inv_hd = 1.0 / hd
    mean = _head_sums(y, ones_bd) * inv_hd
    ycen = y - mean
    var = _head_sums(ycen * ycen, ones_bd) * inv_hd
    yn = ycen * lax.rsqrt(var + GN_EPS) * gw_ref[...] + gb_ref[...]
    o_ref[...] = ((yn + bonus) * gr_ref[...]).astype(BF16)


def _rwkv(rw, gr, mu, w0, a0, wab, kk, ka, rk, gw, gb, ones_bd, tril, *, batch, seq, tt):
    n = batch * seq
    ns = seq // tt
    row = lambda w: pl.BlockSpec((tt, w), lambda b, i: (b * ns + i, 0))
    full = lambda a: pl.BlockSpec(a.shape, lambda b, i: (0,) * a.ndim)
    kern = functools.partial(_rwkv_kernel, tt=tt)
    consts = (mu, w0, a0, wab, kk, ka, rk, gw, gb, ones_bd, tril)
    return pl.pallas_call(
        kern,
        grid=(batch, ns),
        in_specs=[row(RWKV_SHIFT_WIDTH), row(RWKV_WIDTH)] + [full(a) for a in consts],
        out_specs=row(RWKV_WIDTH),
        out_shape=jax.ShapeDtypeStruct((n, RWKV_WIDTH), BF16),
        scratch_shapes=[pltpu.VMEM((RWKV_WIDTH // LANE, LANE, LANE), F32),
                        pltpu.VMEM((1, RWKV_SHIFT_WIDTH), F32)],
        compiler_params=pltpu.CompilerParams(dimension_semantics=("arbitrary", "arbitrary"),
                                             vmem_limit_bytes=VMEM_LIMIT),
        name="rwkv",
    )(rw, gr, *consts)


def _out_kernel(x_ref, ont_ref, or_ref, wn_ref, wr_ref, g_ref, o_ref, *, n_sub):
    ts = x_ref.shape[0] // n_sub

    def project(j):
        rows = slice(j * ts, (j + 1) * ts)
        return (x_ref[rows, :] + _dot_tn(ont_ref[:, rows], wn_ref[...])
                + _dot(or_ref[rows, :], wr_ref[...]))

    def finish(j, h):
        ms = jnp.mean(h * h, axis=-1, keepdims=True)
        o_ref[j * ts:(j + 1) * ts, :] = h * lax.rsqrt(ms + RMS_EPS) * g_ref[...]

    h = project(0)
    for j in range(1, n_sub):
        h_next = project(j)
        finish(j - 1, h)
        h = h_next
    finish(n_sub - 1, h)


def _out_proj(x2, o_nsa_t, o_rwkv, wn, wr, final_g, *, tm, n_sub):
    n = x2.shape[0]
    row = lambda w: pl.BlockSpec((tm, w), lambda i: (i, 0))
    full = lambda a: pl.BlockSpec(a.shape, lambda i: (0,) * a.ndim)
    return pl.pallas_call(
        functools.partial(_out_kernel, n_sub=n_sub),
        grid=(n // tm,),
        in_specs=[row(D_MODEL), pl.BlockSpec((NSA_WIDTH, tm), lambda i: (0, i)), row(RWKV_WIDTH),
                  full(wn), full(wr), full(final_g)],
        out_specs=row(D_MODEL),
        out_shape=jax.ShapeDtypeStruct((n, D_MODEL), F32),
        compiler_params=pltpu.CompilerParams(dimension_semantics=("arbitrary",),
                                             vmem_limit_bytes=VMEM_LIMIT),
        name="out_proj",
    )(x2, o_nsa_t, o_rwkv, wn, wr, final_g)


def _rope_tables(seq):
    inv = ROPE_THETA ** (-np.arange(ROPE_HALF, dtype=np.float64) / ROPE_HALF)
    ang = np.arange(seq, dtype=np.float64)[:, None] * inv[None, :]
    cos, sin = np.cos(ang), np.sin(ang)
    ra = np.ones((seq, HEAD_DIM)); rm = np.zeros((seq, HEAD_DIM)); rp = np.zeros((seq, HEAD_DIM))
    ra[:, :ROPE_HALF] = cos; ra[:, ROPE_HALF:ROPE_DIM] = cos
    rm[:, :ROPE_HALF] = -sin
    rp[:, ROPE_HALF:ROPE_DIM] = sin
    rep = lambda t: jnp.asarray(np.tile(t, (1, LANE // HEAD_DIM)), F32)
    assert seq // SEL_BLOCK <= K_AUG - HEAD_DIM
    onehot = np.zeros((seq, K_AUG - HEAD_DIM))
    onehot[np.arange(seq), np.arange(seq) // SEL_BLOCK] = 1.0
    return (rep(ra), rep(rm), rep(rp), jnp.asarray(cos.T, F32), jnp.asarray(sin.T, F32),
            jnp.asarray(onehot, BF16))


def _cmp_to_sel_t(n_cmp_pad, n_sel):
    n_cmp = n_cmp_pad - 1
    c0 = np.arange(n_cmp)[:, None] * CMP_STRIDE
    s0 = np.arange(n_sel)[None, :] * SEL_BLOCK
    ov = np.clip(np.minimum(c0 + CMP_BLOCK, s0 + SEL_BLOCK) - np.maximum(c0, s0), 0, None) / CMP_BLOCK
    mt = np.zeros((n_sel, n_cmp_pad))
    mt[:, :n_cmp] = ov.T
    return jnp.asarray(mt, BF16)


def _prep_w_in(w_in):
    idx = np.cumsum(IN_SIZES)[:-1].tolist()
    q, kc, vc, ks, vs, kw, vw, gl, gn, rw, gr = jnp.split(w_in, idx, axis=1)
    pad = jnp.zeros((D_MODEL, GATE_ROWS - gl.shape[1]), w_in.dtype)
    return jnp.concatenate([kc, vc, ks, kw, rw, gr, q, vs, vw, gl, pad, gn], axis=1).T.astype(BF16)


def _layer(x2, norm_g, w_in, cmp_pos_k, cmp_w1_k, cmp_w2_k, cmp_pos_v, cmp_w1_v, cmp_w2_v,
           shift_mu, decay_w0, decay_up, iclr_a0, iclr_up, k_k, k_a, r_k, gn_w, gn_b, w_out,
           final_g, *, batch, seq):
    tm = 256
    tq = 256
    tt = 256
    assert WINDOW % KEY_TILE == 0 and seq % KEY_TILE == 0 and KEY_TILE % tq == 0
    nch = seq // CMP_STRIDE
    n_sel = seq // SEL_BLOCK
    rowv = lambda t: t.reshape(1, -1).astype(F32)

    (kch, vch, ksh, kwh, rw, gr, qt, qrt, vst, vwt, gates, gnt) = _in_proj(
        x2, rowv(norm_g), _prep_w_in(w_in), _rope_tables(seq), seq=seq, tm=2 * tm)

    chunks = lambda t: t.reshape(NSA_KV, batch, nch, CMP_STRIDE * HEAD_DIM)
    kcmp, vcmpt = _compress(chunks(kch), chunks(vch),
                            rowv(cmp_pos_k), cmp_w1_k.astype(BF16), cmp_w2_k.astype(BF16),
                            rowv(cmp_pos_v), cmp_w1_v.astype(BF16), cmp_w2_v.T.astype(BF16),
                            nb=4 if batch % 4 == 0 else 1)

    o_nsa_t = _nsa(qt, qrt, kcmp, vcmpt, ksh, vst, kwh, vwt, gates, gnt,
                   _cmp_to_sel_t(nch, n_sel), batch=batch, seq=seq, tq=tq)

    z = jnp.zeros((DECAY_RANK, RWKV_WIDTH), F32)
    wab = jnp.concatenate([jnp.concatenate([decay_up, z], axis=1),
                           jnp.concatenate([z, iclr_up], axis=1)], axis=0).astype(BF16)
    hid = np.arange(2 * LANE) // HEAD_DIM
    ones_bd = jnp.asarray(hid[:, None] == hid[None, :], BF16)
    ti = np.arange(tt)
    tril = (ti[:, None] >= ti[None, :]) & (ti[:, None] // RWKV_CHUNK == ti[None, :] // RWKV_CHUNK)
    tril = jnp.asarray(np.concatenate([tril, tril], axis=1), BF16)
    o_rwkv = _rwkv(rw, gr, rowv(shift_mu), rowv(decay_w0), rowv(iclr_a0), wab, rowv(k_k), rowv(k_a),
                   rowv(r_k), rowv(gn_w), rowv(gn_b), ones_bd, tril, batch=batch, seq=seq, tt=tt)

    w_o = w_out.astype(BF16)
    return _out_proj(x2, o_nsa_t, o_rwkv, w_o[:NSA_WIDTH], w_o[NSA_WIDTH:], rowv(final_g), tm=4 * tm, n_sub=4)


def kernel(x, norm_g, w_in, cmp_pos_k, cmp_w1_k, cmp_w2_k, cmp_pos_v, cmp_w1_v, cmp_w2_v, shift_mu, decay_w0, decay_up, iclr_a0, iclr_up, k_k, k_a, r_k, gn_w, gn_b, w_out, final_g):
    batch, seq, d = x.shape
    assert d == D_MODEL and norm_g.shape[0] == 1, "single-layer trunk"
    out = _layer(x.reshape(batch * seq, d), norm_g[0], w_in[0], cmp_pos_k[0], cmp_w1_k[0], cmp_w2_k[0],
                 cmp_pos_v[0], cmp_w1_v[0], cmp_w2_v[0], shift_mu[0], decay_w0[0], decay_up[0],
                 iclr_a0[0], iclr_up[0], k_k[0], k_a[0], r_k[0], gn_w[0], gn_b[0], w_out[0],
                 final_g, batch=batch, seq=seq)
    return out.reshape(batch, seq, d)
```

```python
import functools

import numpy as np
import jax
import jax.numpy as jnp
from jax import lax
from jax.experimental import pallas as pl
from jax.experimental.pallas import tpu as pltpu

F32 = jnp.float32
BF16 = jnp.bfloat16

D_MODEL = 1024
HEAD_DIM = 64
NSA_HEADS = 8
NSA_KV = 2
NSA_REP = NSA_HEADS // NSA_KV
RWKV_HEADS = 8
NSA_WIDTH = NSA_HEADS * HEAD_DIM
RWKV_WIDTH = RWKV_HEADS * HEAD_DIM
KV_WIDTH = NSA_KV * HEAD_DIM
ROPE_DIM = HEAD_DIM // 4
ROPE_HALF = ROPE_DIM // 2
ROPE_THETA = 500000.0
CMP_BLOCK = 32
CMP_STRIDE = 16
CMP_HIDDEN = 256
SEL_BLOCK = 64
SEL_TOPK = 8
WINDOW = 512
DECAY_RANK = 64
ICLR_RANK = 64
RWKV_SHIFT_WIDTH = 3 * RWKV_WIDTH + DECAY_RANK + ICLR_RANK
IN_SIZES = (NSA_WIDTH, KV_WIDTH, KV_WIDTH, KV_WIDTH, KV_WIDTH, KV_WIDTH, KV_WIDTH,
            3 * NSA_HEADS, NSA_WIDTH, RWKV_SHIFT_WIDTH, RWKV_WIDTH)
SCALE = HEAD_DIM ** -0.5
RMS_EPS = 1e-6
GN_EPS = 64e-5
NEG_INF = -1e30
FORCE_BONUS = 1e3

LANE = 128
SUBLANE = 8
BF16_SUBLANE = 2 * SUBLANE
GATE_ROWS = 2 * BF16_SUBLANE

T_KV = 0
T_RW = T_KV + 4 * KV_WIDTH
T_GR = T_RW + RWKV_SHIFT_WIDTH
T_END = T_GR + RWKV_WIDTH
R_Q = T_END
R_VS = R_Q + NSA_WIDTH
R_VW = R_VS + KV_WIDTH
R_GL = R_VW + KV_WIDTH
R_GN = R_GL + GATE_ROWS
R_END = R_GN + NSA_WIDTH

LOG2E = float(np.log2(np.e))
K_AUG = LANE
V_ROWS = HEAD_DIM + BF16_SUBLANE
KEY_TILE = 256
RWKV_CHUNK = 64
VMEM_LIMIT = 48 * 1024 * 1024


def _dot(a, b):
    return jnp.dot(a, b, preferred_element_type=F32)


def _dot_nt(a, b):
    return lax.dot_general(a, b, (((1,), (1,)), ((), ())), preferred_element_type=F32)


def _dot_tn(a, b):
    return lax.dot_general(a, b, (((0,), (0,)), ((), ())), preferred_element_type=F32)


def _bmm(a, b):
    return lax.dot_general(a, b, (((2,), (1,)), ((0,), (0,))), preferred_element_type=F32)


def _bmm_nt(a, b):
    return lax.dot_general(a, b, (((2,), (2,)), ((0,), (0,))), preferred_element_type=F32)


def _split3(x):
    hi = x.astype(BF16)
    r1 = x - hi.astype(F32)
    mid = r1.astype(BF16)
    lo = (r1 - mid.astype(F32)).astype(BF16)
    return hi, mid, lo


def _head_sums(x, ones_blk):
    w = ones_blk.shape[0]
    xb = x.astype(BF16)
    return jnp.concatenate([_dot(xb[:, i:i + w], ones_blk) for i in range(0, x.shape[1], w)], axis=1)


def _x2_dot(ww_bf16, x):
    hi = x.astype(BF16)
    lo = (x - hi.astype(F32)).astype(BF16)
    return _dot(ww_bf16, jnp.concatenate([hi, lo], axis=0))


def _x3_dot(w_bf16, x):
    hi, mid, lo = _split3(x)
    return _dot(w_bf16, hi) + _dot(w_bf16, mid) + _dot(w_bf16, lo)


def _sigmoid(x):
    return 0.5 * jnp.tanh(0.5 * x) + 0.5


def _interleave(*gens):
    live = list(gens)
    while live:
        for g in list(live):
            try:
                next(g)
            except StopIteration:
                live.remove(g)


def _rope128(t, ra, rm, rp):
    return t * ra + pltpu.roll(t, LANE - ROPE_HALF, 1) * rm + pltpu.roll(t, ROPE_HALF, 1) * rp


def _in_proj_kernel(x_ref, g_ref, w_ref, ra_ref, rm_ref, rp_ref, cos_ref, sin_ref, oh_ref,
                    kc_ref, vc_ref, ks_ref, kw_ref, rw_ref, gr_ref,
                    qt_ref, qrt_ref, vst_ref, vwt_ref, gate_ref, gn_ref, cmp_scr):
    x = x_ref[...]
    ms = jnp.mean(x * x, axis=-1, keepdims=True)
    y = (x * lax.rsqrt(ms + RMS_EPS) * g_ref[...]).astype(BF16)
    tm = x.shape[0]

    kv = _dot_nt(y, w_ref[T_KV:T_RW, :])
    for i, ref in enumerate((kc_ref, vc_ref)):
        cmp_scr[i] = kv[:, i * KV_WIDTH:(i + 1) * KV_WIDTH]
        for tau in range(CMP_STRIDE):
            piece = cmp_scr[i, pl.ds(tau, tm // CMP_STRIDE, stride=CMP_STRIDE), :].astype(BF16)
            for g in range(NSA_KV):
                ref[g, :, tau * HEAD_DIM:(tau + 1) * HEAD_DIM] = piece[:, g * HEAD_DIM:(g + 1) * HEAD_DIM]
    ra, rm, rp = ra_ref[...], rm_ref[...], rp_ref[...]
    for i, ref in ((2, ks_ref), (3, kw_ref)):
        t = _rope128(kv[:, i * LANE:(i + 1) * LANE], ra, rm, rp)
        for g in range(NSA_KV):
            tg = t[:, g * HEAD_DIM:(g + 1) * HEAD_DIM].astype(BF16)
            if ref is ks_ref:
                ref[g] = jnp.concatenate([tg, oh_ref[...]], axis=1)
            else:
                ref[g] = tg
    rw_ref[...] = _dot_nt(y, w_ref[T_RW:T_GR, :])
    gr = _dot_nt(y, w_ref[T_GR:T_END, :])
    gr_ref[...] = gr * _sigmoid(gr)

    def proj_t(r0, r1):
        return _dot_nt(w_ref[r0:r1, :], y)

    ft = proj_t(R_Q, R_END)
    qt = ft[0:R_VS - R_Q] * (SCALE * LOG2E)
    cos, sin = cos_ref[...], sin_ref[...]
    qt_ref[...] = qt.astype(BF16)
    for h in range(NSA_HEADS):
        r0 = h * HEAD_DIM
        t1 = qt[r0:r0 + ROPE_HALF]
        t2 = qt[r0 + ROPE_HALF:r0 + ROPE_DIM]
        qrt_ref[r0:r0 + ROPE_DIM, :] = jnp.concatenate(
            [t1 * cos - t2 * sin, t2 * cos + t1 * sin], axis=0).astype(BF16)
        qrt_ref[r0 + ROPE_DIM:r0 + HEAD_DIM, :] = qt[r0 + ROPE_DIM:r0 + HEAD_DIM].astype(BF16)
    vg = ft[R_VS - R_Q:R_GN - R_Q]
    vt = vg[0:R_GL - R_VS].astype(BF16)
    ones = jnp.ones((V_ROWS - HEAD_DIM, LANE), BF16)
    for j in range(vt.shape[1] // LANE):
        for i, ref in enumerate((vst_ref, vwt_ref)):
            for g in range(NSA_KV):
                r0 = i * KV_WIDTH + g * HEAD_DIM
                ref[j, g * V_ROWS:(g + 1) * V_ROWS, :] = jnp.concatenate(
                    [vt[r0:r0 + HEAD_DIM, j * LANE:(j + 1) * LANE], ones], axis=0)
    gate_ref[...] = _sigmoid(vg[R_GL - R_VS:])
    gn = ft[R_GN - R_Q:]
    gn_ref[...] = gn * _sigmoid(gn)


def _in_proj(x2, norm_g, w_all, tabs, *, seq, tm):
    n = x2.shape[0]
    spt = seq // tm
    ra, rm, rp, cos, sin, onehot = tabs
    hm = lambda w: jax.ShapeDtypeStruct((NSA_KV, n, w), BF16)
    hspec = lambda w: pl.BlockSpec((NSA_KV, tm, w), lambda i: (0, i, 0))
    row = lambda w: pl.BlockSpec((tm, w), lambda i: (i, 0))
    col = lambda r: pl.BlockSpec((r, tm), lambda i: (0, i))
    full = lambda a: pl.BlockSpec(a.shape, lambda i: (0,) * a.ndim)
    tab = lambda w: pl.BlockSpec((tm, w), lambda i: (i % spt, 0))
    tabt = pl.BlockSpec((ROPE_HALF, tm), lambda i: (0, i % spt))
    vtile = pl.BlockSpec((tm // LANE, NSA_KV * V_ROWS, LANE), lambda i: (i, 0, 0))
    vsd = jax.ShapeDtypeStruct((n // LANE, NSA_KV * V_ROWS, LANE), BF16)
    cw = CMP_STRIDE * HEAD_DIM
    cspec = pl.BlockSpec((NSA_KV, tm // CMP_STRIDE, cw), lambda i: (0, i, 0))
    csd = jax.ShapeDtypeStruct((NSA_KV, n // CMP_STRIDE, cw), BF16)
    return pl.pallas_call(
        _in_proj_kernel,
        grid=(n // tm,),
        in_specs=[row(D_MODEL), full(norm_g), full(w_all), tab(LANE), tab(LANE), tab(LANE),
                  tabt, tabt, tab(K_AUG - HEAD_DIM)],
        out_specs=[cspec, cspec, hspec(K_AUG), hspec(HEAD_DIM), row(RWKV_SHIFT_WIDTH), row(RWKV_WIDTH),
                   col(NSA_WIDTH), col(NSA_WIDTH), vtile, vtile,
                   col(GATE_ROWS), col(NSA_WIDTH)],
        out_shape=[csd, csd, hm(K_AUG), hm(HEAD_DIM),
                   jax.ShapeDtypeStruct((n, RWKV_SHIFT_WIDTH), F32),
                   jax.ShapeDtypeStruct((n, RWKV_WIDTH), F32),
                   jax.ShapeDtypeStruct((NSA_WIDTH, n), BF16),
                   jax.ShapeDtypeStruct((NSA_WIDTH, n), BF16),
                   vsd, vsd,
                   jax.ShapeDtypeStruct((GATE_ROWS, n), F32),
                   jax.ShapeDtypeStruct((NSA_WIDTH, n), F32)],
        scratch_shapes=[pltpu.VMEM((2, tm, KV_WIDTH), F32)],
        compiler_params=pltpu.CompilerParams(dimension_semantics=("arbitrary",),
                                             vmem_limit_bytes=VMEM_LIMIT),
        name="in_proj",
    )(x2, norm_g, w_all, ra, rm, rp, cos, sin, onehot)


def _compress_kernel(kc_ref, vc_ref, pk_ref, w1k_ref, w2k_ref, pv_ref, w1v_ref, w2vt_ref,
                     ko_ref, vo_ref):
    half = CMP_STRIDE * HEAD_DIM
    ng, nb, nch, _ = kc_ref.shape
    nseg = ng * nb

    def hidden(c_ref, pos_ref, w1_ref):
        c = c_ref[...].reshape(nseg * nch, half)
        pos = jnp.broadcast_to(pos_ref[...], (BF16_SUBLANE, 2 * half)).astype(BF16)
        za = _dot(jnp.concatenate([c, pos[:, 0:half]], axis=0), w1_ref[0:half, :])
        zb = _dot(jnp.concatenate([c, pos[:, half:]], axis=0), w1_ref[half:2 * half, :])
        pv = za[nseg * nch:nseg * nch + 1] + zb[nseg * nch:nseg * nch + 1]
        hid = jnp.concatenate(
            [za[s * nch:(s + 1) * nch] + pltpu.roll(zb[s * nch:(s + 1) * nch], nch - 1, 0) for s in range(nseg)],
            axis=0) + pv
        return (hid * _sigmoid(hid)).astype(BF16)

    ko = _dot(hidden(kc_ref, pk_ref, w1k_ref), w2k_ref[...]).astype(BF16)
    hv = hidden(vc_ref, pv_ref, w1v_ref)
    for s in range(nseg):
        ko_ref[s // nb, s % nb] = ko[s * nch:(s + 1) * nch]
        vo_ref[s // nb, s % nb] = _dot_nt(w2vt_ref[...], hv[s * nch:(s + 1) * nch]).astype(BF16)


def _compress(kc_r, vc_r, pk, w1k, w2k, pv, w1v, w2vt, *, nb):
    g, b, nch, width = kc_r.shape
    blk = pl.BlockSpec((g, nb, nch, width), lambda j: (0, j, 0, 0))
    full = lambda a: pl.BlockSpec(a.shape, lambda j: (0,) * a.ndim)
    return pl.pallas_call(
        _compress_kernel,
        grid=(b // nb,),
        in_specs=[blk, blk, full(pk), full(w1k), full(w2k), full(pv), full(w1v), full(w2vt)],
        out_specs=[pl.BlockSpec((g, nb, nch, HEAD_DIM), lambda j: (0, j, 0, 0)),
                   pl.BlockSpec((g, nb, HEAD_DIM, nch), lambda j: (0, j, 0, 0))],
        out_shape=[jax.ShapeDtypeStruct((g, b, nch, HEAD_DIM), BF16),
                   jax.ShapeDtypeStruct((g, b, HEAD_DIM, nch), BF16)],
        compiler_params=pltpu.CompilerParams(dimension_semantics=("arbitrary",),
                                             vmem_limit_bytes=VMEM_LIMIT),
        name="compress",
    )(kc_r, vc_r, pk, w1k, w2k, pv, w1v, w2vt)


def _nsa_kernel(qt_ref, qrt_ref, kc_ref, vct_ref, ks_ref, vst_ref, kw_ref, vwt_ref,
                gate_ref, gn_ref, mt_ref, o_ref, *, tq, seq):
    tk = KEY_TILE
    n_win = WINDOW // tk
    qi = pl.program_id(1)
    q0 = qi * tq
    nl = NSA_REP * tq
    n_sel = seq // SEL_BLOCK
    ncp = kc_ref.shape[2]
    groups = range(NSA_KV)

    def heads_on_lanes(ref, g):
        rows = [(g * NSA_REP + r) * HEAD_DIM for r in range(NSA_REP)]
        return jnp.concatenate([ref[r0:r0 + HEAD_DIM, :] for r0 in rows], axis=1)

    def tile4(a):
        return jnp.concatenate([a] * NSA_REP, axis=1)

    k_s = lax.broadcasted_iota(jnp.int32, (tk, tq), 0)
    t_l = q0 + lax.broadcasted_iota(jnp.int32, (tk, tq), 1)

    def update_steps(box, k_ref, vt_ref, g, q_op, kts, keeps):
        m_i, acc = box[0]

        def scores(i):
            sc = _dot(k_ref[g, pl.ds(pl.multiple_of(kts[i] * tk, tk), tk), :], q_op)
            if keeps[i] is None:
                return sc
            return jnp.concatenate([jnp.where(keeps[i], sc[:, r * tq:(r + 1) * tq], NEG_INF)
                                    for r in range(NSA_REP)], axis=1)

        sc_next = scores(0)
        yield
        for i, kt in enumerate(kts):
            sc = sc_next
            if i + 1 < len(kts):
                sc_next = scores(i + 1)
            m_n = jnp.maximum(m_i, jnp.max(sc, axis=0, keepdims=True))
            pe = jnp.exp2(sc - m_n).astype(BF16)
            yield
            vtb = jnp.concatenate([vt_ref[kt * (tk // LANE) + jj, g * V_ROWS:(g + 1) * V_ROWS, :]
                                   for jj in range(tk // LANE)], axis=1)
            acc = jnp.exp2(m_i - m_n) * acc + _dot(vtb, pe)
            m_i = m_n
            yield
        box[0] = (m_i, acc)

    def last_tiles(box, k_ref, vt_ref, g, q_op, n_back, low_keep):
        kts, keeps = [], []
        for back in range(n_back + 1):
            d = t_l - ((a - back) * tk + k_s)
            kts.append(a - back)
            if back == 0:
                keeps.append(d >= 0)
            elif back == n_win and low_keep is not None:
                keeps.append(low_keep(d))
            else:
                keeps.append(None)
        return update_steps(box, k_ref, vt_ref, g, q_op, kts, keeps)

    def select_steps(g, qr, sel_out, all_selected):
        s = _dot(kc_ref[g, 0], heads_on_lanes(qt_ref, g))
        t_c = q0 + lax.broadcasted_iota(jnp.int32, (ncp, tq), 1)
        c_c = lax.broadcasted_iota(jnp.int32, (ncp, tq), 0)
        cmask = tile4((c_c * CMP_STRIDE + (CMP_BLOCK - 1)) <= t_c)
        yield
        s = jnp.where(cmask, s, NEG_INF)
        m = jnp.maximum(jnp.max(s, axis=0, keepdims=True), 0.5 * NEG_INF)
        e = jnp.exp2(s - m)
        den = jnp.sum(e, axis=0, keepdims=True)
        p = e * (1.0 / jnp.where(den > 0.0, den, 1.0))
        sel_out["o_cmp"] = _dot(vct_ref[g, 0], p.astype(BF16))
        yield
        if all_selected:
            sel_out["qr_sel"] = jnp.concatenate([qr, jnp.zeros((K_AUG - HEAD_DIM, nl), BF16)], axis=0)
            return
        psum = p[:, 0:tq]
        for r in range(1, NSA_REP):
            psum = psum + p[:, r * tq:(r + 1) * tq]
        imp = _x3_dot(mt_ref[...], psum)
        j = lax.broadcasted_iota(jnp.int32, (n_sel, tq), 0)
        t = q0 + lax.broadcasted_iota(jnp.int32, (n_sel, tq), 1)
        tb = t // SEL_BLOCK
        forced = (j == 0) | (j == tb) | (j == tb - 1)
        val = jnp.where(j <= tb, imp + jnp.where(forced, FORCE_BONUS, 0.0), -1.0)
        yield
        vals = [val[g0:g0 + SUBLANE] for g0 in range(0, n_sel, SUBLANE)]
        cnts = [jnp.zeros((SUBLANE, tq), F32) for _ in vals]
        srow = lax.broadcasted_iota(jnp.int32, (SUBLANE, tq), 0)
        for i in range(n_sel):
            vi = jnp.broadcast_to(val[i:i + 1, :], (SUBLANE, tq))
            for g, vg in enumerate(vals):
                ge = lambda: jnp.where(vi >= vg, 1.0, 0.0)
                gt = lambda: jnp.where(vi > vg, 1.0, 0.0)
                if g * SUBLANE > i:
                    beat = ge()
                elif (g + 1) * SUBLANE <= i:
                    beat = gt()
                else:
                    beat = jnp.where(srow > i - g * SUBLANE, ge(), gt())
                cnts[g] = cnts[g] + beat
            if i % SUBLANE == SUBLANE - 1:
                yield
        cnt = jnp.concatenate(cnts, axis=0)
        sel_bias = jnp.where(cnt < float(SEL_TOPK), 0.0, NEG_INF).astype(BF16)
        sel_out["qr_sel"] = jnp.concatenate(
            [qr, tile4(sel_bias), jnp.zeros((K_AUG - HEAD_DIM - n_sel, nl), BF16)], axis=0)

    init = (jnp.full((1, nl), NEG_INF, F32), jnp.zeros((V_ROWS, nl), F32))
    a = (q0 + tq - 1) // tk

    def attend(n_back):
        win_boxes, sel_boxes, sel_outs = [[init] for _ in groups], [[init] for _ in groups], [{} for _ in groups]
        all_selected = n_back < n_win and (n_back + 1) * tk <= SEL_TOPK * SEL_BLOCK
        for g in groups:
            qr = heads_on_lanes(qrt_ref, g)
            _interleave(last_tiles(win_boxes[g], kw_ref, vwt_ref, g, qr, n_back, lambda d: d < WINDOW),
                        select_steps(g, qr, sel_outs[g], all_selected))
        _interleave(*[last_tiles(sel_boxes[g], ks_ref, vst_ref, g, sel_outs[g]["qr_sel"], n_back, None)
                      for g in groups])
        c_wins = [b[0] for b in win_boxes]
        c_sels = [b[0] for b in sel_boxes]
        o_cmps = [o["o_cmp"] for o in sel_outs]
        qr_sels = [o["qr_sel"] for o in sel_outs]

        def old_tiles(kts):
            def body(carry):
                boxes = [[c] for c in carry]
                _interleave(*[update_steps(boxes[g], ks_ref, vst_ref, g, qr_sels[g], kts, [None] * len(kts))
                              for g in groups])
                return tuple(b[0] for b in boxes)
            return body

        if n_back == n_win:
            n_old = a - n_win
            c_sels = lax.fori_loop(0, n_old % 2, lambda kt, c: old_tiles([kt])(c), tuple(c_sels))
            c_sels = lax.fori_loop(0, n_old // 2,
                                   lambda i, c: old_tiles([n_old % 2 + 2 * i, n_old % 2 + 2 * i + 1])(c), c_sels)

        for g in groups:
            o_sel = c_sels[g][1][0:HEAD_DIM] * (1.0 / c_sels[g][1][HEAD_DIM:HEAD_DIM + 1])
            o_win = c_wins[g][1][0:HEAD_DIM] * (1.0 / c_wins[g][1][HEAD_DIM:HEAD_DIM + 1])
            for r in range(NSA_REP):
                ls = slice(r * tq, (r + 1) * tq)
                h = g * NSA_REP + r
                gate = lambda c: gate_ref[3 * h + c:3 * h + c + 1, :]
                o = gate(0) * o_cmps[g][:, ls] + gate(1) * o_sel[:, ls] + gate(2) * o_win[:, ls]
                rs = slice(h * HEAD_DIM, (h + 1) * HEAD_DIM)
                o_ref[rs, :] = (o * gn_ref[rs, :]).astype(BF16)

    def dispatch(n_back):
        if n_back == 0:
            attend(0)
        else:
            lax.cond(a >= n_back, lambda: attend(n_back), lambda: dispatch(n_back - 1))

    dispatch(n_win)


def _nsa(qt, qrt, kcmp, vcmpt, ksh, vst, kwh, vwt, gates, gnt, mt, *, batch, seq, tq):
    n = batch * seq
    nq = seq // tq
    ncp = kcmp.shape[2]
    qspec = pl.BlockSpec((NSA_WIDTH, tq), lambda b, i: (0, b * nq + i))
    kspec = lambda w: pl.BlockSpec((NSA_KV, seq, w), lambda b, i: (0, b, 0))
    vspec = pl.BlockSpec((seq // LANE, NSA_KV * V_ROWS, LANE), lambda b, i: (b, 0, 0))
    kern = functools.partial(_nsa_kernel, tq=tq, seq=seq)
    return pl.pallas_call(
        kern,
        grid=(batch, nq),
        in_specs=[qspec, qspec,
                  pl.BlockSpec((NSA_KV, 1, ncp, HEAD_DIM), lambda b, i: (0, b, 0, 0)),
                  pl.BlockSpec((NSA_KV, 1, HEAD_DIM, ncp), lambda b, i: (0, b, 0, 0)),
                  kspec(K_AUG), vspec, kspec(HEAD_DIM), vspec,
                  pl.BlockSpec((GATE_ROWS, tq), lambda b, i: (0, b * nq + i)),
                  qspec,
                  pl.BlockSpec(mt.shape, lambda b, i: (0, 0))],
        out_specs=qspec,
        out_shape=jax.ShapeDtypeStruct((NSA_WIDTH, n), BF16),
        compiler_params=pltpu.CompilerParams(
            dimension_semantics=("arbitrary", "arbitrary"),
            vmem_limit_bytes=VMEM_LIMIT),
        name="nsa",
    )(qt, qrt, kcmp, vcmpt, ksh, vst, kwh, vwt, gates, gnt, mt)


def _rwkv_kernel(p_ref, gr_ref, mu_ref, w0_ref, a0_ref, wab_ref, kk_ref, ka_ref, rk_ref,
                 gw_ref, gb_ref, ones_ref, tril_ref, o_ref, st_ref, carry_ref, *, tt):
    c = RWKV_CHUNK
    hd = HEAD_DIM
    nc = tt // c
    step = pl.program_id(1)

    @pl.when(step == 0)
    def _():
        st_ref[...] = jnp.zeros_like(st_ref)
        carry_ref[...] = jnp.zeros_like(carry_ref)

    p = p_ref[...]
    row = lax.broadcasted_iota(jnp.int32, p.shape, 0)
    prev = jnp.where(row == 0, carry_ref[...], pltpu.roll(p, 1, 0))
    carry_ref[...] = p[tt - 1:tt, :]
    ps = p + mu_ref[...] * (prev - p)
    r = ps[:, 0:RWKV_WIDTH]
    k = ps[:, RWKV_WIDTH:2 * RWKV_WIDTH]
    v = ps[:, 2 * RWKV_WIDTH:3 * RWKV_WIDTH]
    lora = ps[:, 3 * RWKV_WIDTH:]
    lane = lax.broadcasted_iota(jnp.int32, lora.shape, 1)
    feat = jnp.where(lane < DECAY_RANK, jnp.tanh(lora), lora).astype(BF16)
    up = _dot(feat, wab_ref[...])
    w = w0_ref[...] + up[:, 0:RWKV_WIDTH]
    lw = _sigmoid(w) * (-float(np.exp(-0.5)))
    a = _sigmoid(a0_ref[...] + up[:, RWKV_WIDTH:])
    ones_bd = ones_ref[...]
    kk = k * kk_ref[...]
    kkn = kk * lax.rsqrt(jnp.maximum(_head_sums(kk * kk, ones_bd), 1e-24))
    k2 = k * (1.0 + (a - 1.0) * ka_ref[...])
    alpha = -kkn
    beta = kkn * a
    bonus = _head_sums(r * k2 * rk_ref[...], ones_bd) * v

    cum = _x2_dot(tril_ref[...], lw)
    cend = jnp.concatenate(
        [jnp.broadcast_to(cum[(ch + 1) * c - 1:(ch + 1) * c, :], (c, RWKV_WIDTH)) for ch in range(nc)], axis=0)
    e_neg = jnp.exp(-cum)
    pc = jnp.exp(cend)
    at = alpha * jnp.exp(cum - lw)
    bt = beta * e_neg
    kt = k2 * e_neg
    rt = r * jnp.exp(cum)
    bh = bt * pc
    kh = kt * pc

    npair = RWKV_WIDTH // LANE

    def pairs(x):
        return jnp.stack([x[ch * c:(ch + 1) * c, j * LANE:(j + 1) * LANE]
                          for ch in range(nc) for j in range(npair)], axis=0)

    at_p, rt_p, bt_p, kt_p, v_p, bh_p, kh_p = (pairs(t) for t in (at, rt, bt, kt, v, bh, kh))
    pc_p = jnp.stack([pc[ch * c:ch * c + 1, j * LANE:(j + 1) * LANE]
                      for ch in range(nc) for j in range(npair)], axis=0)
    lane_c = lax.broadcasted_iota(jnp.int32, (1, c, LANE), 2)
    row_c = lax.broadcasted_iota(jnp.int32, (1, c, LANE), 1)
    even_c = lane_c < hd
    col_c = jnp.where(even_c, lane_c, lane_c - hd)
    low_s = row_c > col_c
    low_i = row_c >= col_c
    lane_2c = lax.broadcasted_iota(jnp.int32, (1, 2 * c, LANE), 2)
    row_2c = lax.broadcasted_iota(jnp.int32, (1, 2 * c, LANE), 1)
    even_2c = lane_2c < hd
    on_bd = (row_2c < hd) == even_2c
    zero_c = jnp.zeros((1, c, LANE), BF16)

    def bd(x):
        xb = x.astype(BF16)
        return jnp.concatenate([jnp.where(even_c, xb, zero_c), jnp.where(even_c, zero_c, xb)], axis=1)

    def abd(x):
        xb = x.astype(BF16)
        return jnp.concatenate([jnp.where(even_c, zero_c, xb), jnp.where(even_c, xb, zero_c)], axis=1)

    la = jnp.concatenate([at_p, rt_p], axis=1).astype(BF16)
    zero_2c = jnp.zeros((1, 2 * c, LANE), BF16)
    r_e = _bmm_nt(jnp.where(even_2c, la, zero_2c), jnp.concatenate([bt_p, kt_p], axis=1).astype(BF16))
    r_o = _bmm_nt(jnp.where(even_2c, zero_2c, la), jnp.concatenate([kt_p, bt_p], axis=1).astype(BF16))
    nab = jnp.where(low_s, jnp.where(even_c, r_e[:, 0:c], r_o[:, 0:c]), 0.0)
    aak_sw = jnp.where(low_s, jnp.where(even_c, r_o[:, 0:c], r_e[:, 0:c]), 0.0).astype(BF16)
    arb = jnp.where(low_i, jnp.where(even_c, r_e[:, c:], r_o[:, c:]), 0.0).astype(BF16)
    ark_sw = jnp.where(low_i, jnp.where(even_c, r_o[:, c:], r_e[:, c:]), 0.0).astype(BF16)
    tinv = jnp.where(row_c == col_c, 1.0, 0.0) + nab
    npow = _bmm(nab.astype(BF16), bd(nab))
    n_dbl = 5
    for it in range(n_dbl):
        nbd = bd(npow)
        if it + 1 < n_dbl:
            res = _bmm(jnp.concatenate([tinv, npow], axis=1).astype(BF16), nbd)
            tinv = tinv + res[:, 0:c]
            npow = res[:, c:]
        else:
            tinv = tinv + _bmm(tinv.astype(BF16), nbd)
    av = _bmm(jnp.concatenate([aak_sw, ark_sw], axis=1), abd(v_p))
    tx = _bmm(tinv.astype(BF16), jnp.concatenate([bd(av[:, 0:c]), bd(at_p)], axis=2))
    u0, ta = tx[:, :, 0:LANE], tx[:, :, LANE:]
    ax = _bmm(arb, jnp.concatenate([bd(ta), bd(u0)], axis=2))
    rq = rt_p + ax[:, :, 0:LANE]
    y0 = ax[:, :, LANE:] + av[:, c:]
    v_b = v_p.astype(BF16)
    w_f = jnp.concatenate([jnp.concatenate([ta, u0], axis=2).astype(BF16),
                           jnp.concatenate([jnp.zeros_like(v_b), v_b], axis=2)], axis=1)
    gh = _bmm(jnp.concatenate([jnp.swapaxes(bh_p, 1, 2), jnp.swapaxes(kh_p, 1, 2)], axis=2).astype(BF16), w_f)
    g_bd = jnp.where(on_bd, gh[:, :, 0:LANE], 0.0) + jnp.where(row_2c == lane_2c, pc_p, 0.0)
    h_bd = jnp.where(on_bd, gh[:, :, LANE:], 0.0)
    lhs = jnp.concatenate([rq, g_bd], axis=1).astype(BF16)

    st = st_ref[...]
    ys = []
    for ch in range(nc):
        sl = slice(ch * npair, (ch + 1) * npair)
        res = _bmm(lhs[sl], st.astype(BF16))
        yc = res[:, 0:c, :] + y0[sl]
        st = res[:, c:, :] + h_bd[sl]
        ys.append(jnp.concatenate([yc[j] for j in range(npair)], axis=1))
    st_ref[...] = st
    y = jnp.concatenate(ys, axis=0) if nc > 1 else ys[0]

    inv_hd = 1.0 / hd
    mean = _head_sums(y, ones_bd) * inv_hd
    ycen = y - mean
    var = _head_sums(ycen * ycen, ones_bd) * inv_hd
    yn = ycen * lax.rsqrt(var + GN_EPS) * gw_ref[...] + gb_ref[...]
    o_ref[...] = ((yn + bonus) * gr_ref[...]).astype(BF16)


def _rwkv(rw, gr, mu, w0, a0, wab, kk, ka, rk, gw, gb, ones_bd, tril, *, batch, seq, tt):
    n = batch * seq
    ns = seq // tt
    row = lambda w: pl.BlockSpec((tt, w), lambda b, i: (b * ns + i, 0))
    full = lambda a: pl.BlockSpec(a.shape, lambda b, i: (0,) * a.ndim)
    kern = functools.partial(_rwkv_kernel, tt=tt)
    consts = (mu, w0, a0, wab, kk, ka, rk, gw, gb, ones_bd, tril)
    return pl.pallas_call(
        kern,
        grid=(batch, ns),
        in_specs=[row(RWKV_SHIFT_WIDTH), row(RWKV_WIDTH)] + [full(a) for a in consts],
        out_specs=row(RWKV_WIDTH),
        out_shape=jax.ShapeDtypeStruct((n, RWKV_WIDTH), BF16),
        scratch_shapes=[pltpu.VMEM((RWKV_WIDTH // LANE, LANE, LANE), F32),
                        pltpu.VMEM((1, RWKV_SHIFT_WIDTH), F32)],
        compiler_params=pltpu.CompilerParams(dimension_semantics=("arbitrary", "arbitrary"),
                                             vmem_limit_bytes=VMEM_LIMIT),
        name="rwkv",
    )(rw, gr, *consts)


def _out_kernel(x_ref, ont_ref, or_ref, wn_ref, wr_ref, g_ref, o_ref, *, n_sub):
    ts = x_ref.shape[0] // n_sub

    def project(j):
        rows = slice(j * ts, (j + 1) * ts)
        return (x_ref[rows, :] + _dot_tn(ont_ref[:, rows], wn_ref[...])
                + _dot(or_ref[rows, :], wr_ref[...]))

    def finish(j, h):
        ms = jnp.mean(h * h, axis=-1, keepdims=True)
        o_ref[j * ts:(j + 1) * ts, :] = h * lax.rsqrt(ms + RMS_EPS) * g_ref[...]

    h = project(0)
    for j in range(1, n_sub):
        h_next = project(j)
        finish(j - 1, h)
        h = h_next
    finish(n_sub - 1, h)


def _out_proj(x2, o_nsa_t, o_rwkv, wn, wr, final_g, *, tm, n_sub):
    n = x2.shape[0]
    row = lambda w: pl.BlockSpec((tm, w), lambda i: (i, 0))
    full = lambda a: pl.BlockSpec(a.shape, lambda i: (0,) * a.ndim)
    return pl.pallas_call(
        functools.partial(_out_kernel, n_sub=n_sub),
        grid=(n // tm,),
        in_specs=[row(D_MODEL), pl.BlockSpec((NSA_WIDTH, tm), lambda i: (0, i)), row(RWKV_WIDTH),
                  full(wn), full(wr), full(final_g)],
        out_specs=row(D_MODEL),
        out_shape=jax.ShapeDtypeStruct((n, D_MODEL), F32),
        compiler_params=pltpu.CompilerParams(dimension_semantics=("arbitrary",),
                                             vmem_limit_bytes=VMEM_LIMIT),
        name="out_proj",
    )(x2, o_nsa_t, o_rwkv, wn, wr, final_g)


def _rope_tables(seq):
    inv = ROPE_THETA ** (-np.arange(ROPE_HALF, dtype=np.float64) / ROPE_HALF)
    ang = np.arange(seq, dtype=np.float64)[:, None] * inv[None, :]
    cos, sin = np.cos(ang), np.sin(ang)
    ra = np.ones((seq, HEAD_DIM)); rm = np.zeros((seq, HEAD_DIM)); rp = np.zeros((seq, HEAD_DIM))
    ra[:, :ROPE_HALF] = cos; ra[:, ROPE_HALF:ROPE_DIM] = cos
    rm[:, :ROPE_HALF] = -sin
    rp[:, ROPE_HALF:ROPE_DIM] = sin
    rep = lambda t: jnp.asarray(np.tile(t, (1, LANE // HEAD_DIM)), F32)
    assert seq // SEL_BLOCK <= K_AUG - HEAD_DIM
    onehot = np.zeros((seq, K_AUG - HEAD_DIM))
    onehot[np.arange(seq), np.arange(seq) // SEL_BLOCK] = 1.0
    return (rep(ra), rep(rm), rep(rp), jnp.asarray(cos.T, F32), jnp.asarray(sin.T, F32),
            jnp.asarray(onehot, BF16))


def _cmp_to_sel_t(n_cmp_pad, n_sel):
    n_cmp = n_cmp_pad - 1
    c0 = np.arange(n_cmp)[:, None] * CMP_STRIDE
    s0 = np.arange(n_sel)[None, :] * SEL_BLOCK
    ov = np.clip(np.minimum(c0 + CMP_BLOCK, s0 + SEL_BLOCK) - np.maximum(c0, s0), 0, None) / CMP_BLOCK
    mt = np.zeros((n_sel, n_cmp_pad))
    mt[:, :n_cmp] = ov.T
    return jnp.asarray(mt, BF16)


def _prep_w_in(w_in):
    idx = np.cumsum(IN_SIZES)[:-1].tolist()
    q, kc, vc, ks, vs, kw, vw, gl, gn, rw, gr = jnp.split(w_in, idx, axis=1)
    pad = jnp.zeros((D_MODEL, GATE_ROWS - gl.shape[1]), w_in.dtype)
    return jnp.concatenate([kc, vc, ks, kw, rw, gr, q, vs, vw, gl, pad, gn], axis=1).T.astype(BF16)


def _layer(x2, norm_g, w_in, cmp_pos_k, cmp_w1_k, cmp_w2_k, cmp_pos_v, cmp_w1_v, cmp_w2_v,
           shift_mu, decay_w0, decay_up, iclr_a0, iclr_up, k_k, k_a, r_k, gn_w, gn_b, w_out,
           final_g, *, batch, seq):
    tm = 256
    tq = 256
    tt = 256
    assert WINDOW % KEY_TILE == 0 and seq % KEY_TILE == 0 and KEY_TILE % tq == 0
    nch = seq // CMP_STRIDE
    n_sel = seq // SEL_BLOCK
    rowv = lambda t: t.reshape(1, -1).astype(F32)

    (kch, vch, ksh, kwh, rw, gr, qt, qrt, vst, vwt, gates, gnt) = _in_proj(
        x2, rowv(norm_g), _prep_w_in(w_in), _rope_tables(seq), seq=seq, tm=2 * tm)

    chunks = lambda t: t.reshape(NSA_KV, batch, nch, CMP_STRIDE * HEAD_DIM)
    kcmp, vcmpt = _compress(chunks(kch), chunks(vch),
                            rowv(cmp_pos_k), cmp_w1_k.astype(BF16), cmp_w2_k.astype(BF16),
                            rowv(cmp_pos_v), cmp_w1_v.astype(BF16), cmp_w2_v.T.astype(BF16),
                            nb=4 if batch % 4 == 0 else 1)

    o_nsa_t = _nsa(qt, qrt, kcmp, vcmpt, ksh, vst, kwh, vwt, gates, gnt,
                   _cmp_to_sel_t(nch, n_sel), batch=batch, seq=seq, tq=tq)

    z = jnp.zeros((DECAY_RANK, RWKV_WIDTH), F32)
    wab = jnp.concatenate([jnp.concatenate([decay_up, z], axis=1),
                           jnp.concatenate([z, iclr_up], axis=1)], axis=0).astype(BF16)
    hid = np.arange(2 * LANE) // HEAD_DIM
    ones_bd = jnp.asarray(hid[:, None] == hid[None, :], BF16)
    ti = np.arange(tt)
    tril = (ti[:, None] >= ti[None, :]) & (ti[:, None] // RWKV_CHUNK == ti[None, :] // RWKV_CHUNK)
    tril = jnp.asarray(np.concatenate([tril, tril], axis=1), BF16)
    o_rwkv = _rwkv(rw, gr, rowv(shift_mu), rowv(decay_w0), rowv(iclr_a0), wab, rowv(k_k), rowv(k_a),
                   rowv(r_k), rowv(gn_w), rowv(gn_b), ones_bd, tril, batch=batch, seq=seq, tt=tt)

    w_o = w_out.astype(BF16)
    return _out_proj(x2, o_nsa_t, o_rwkv, w_o[:NSA_WIDTH], w_o[NSA_WIDTH:], rowv(final_g), tm=4 * tm, n_sub=2)


def kernel(x, norm_g, w_in, cmp_pos_k, cmp_w1_k, cmp_w2_k, cmp_pos_v, cmp_w1_v, cmp_w2_v, shift_mu, decay_w0, decay_up, iclr_a0, iclr_up, k_k, k_a, r_k, gn_w, gn_b, w_out, final_g):
    batch, seq, d = x.shape
    assert d == D_MODEL and norm_g.shape[0] == 1, "single-layer trunk"
    out = _layer(x.reshape(batch * seq, d), norm_g[0], w_in[0], cmp_pos_k[0], cmp_w1_k[0], cmp_w2_k[0],
                 cmp_pos_v[0], cmp_w1_v[0], cmp_w2_v[0], shift_mu[0], decay_w0[0], decay_up[0],
                 iclr_a0[0], iclr_up[0], k_k[0], k_a[0], r_k[0], gn_w[0], gn_b[0], w_out[0],
                 final_g, batch=batch, seq=seq)
    return out.reshape(batch, seq, d)
```
